```python
import jax, jax.numpy as jnp
from jax import lax
import numpy as np

D_MODEL = 1024
BATCH = 8
SEQ = 4096
DEPTH = 2
DEC_BATCH = 2
DEC_SEQ = 8192
PAST_LEN = 128

N_META = 16
GRID_W = 64
D_FF = 2816
EPS = 1e-6
Q_BLOCK = 128
NEG_INF = -1e30
MLA_HEADS = 8
MLA_Q_LORA = 256
MLA_KV_LORA = 128
MLA_NOPE = 64
MLA_ROPE = 32
MLA_V = 64
MLA_THETA = 10000.0
GQA_HEADS = 8
GQA_KV_HEADS = 2
GQA_HEAD_DIM = 64
AXIAL_THETA = 10000.0
NA_HEADS = 16
NA_HEAD_DIM = D_MODEL // NA_HEADS
NA_WIN_R = 8
NA_WIN_C = 16
N_EVEN = (DEPTH + 1) // 2
N_ODD = DEPTH // 2
IN_SPLITS = (MLA_Q_LORA, MLA_KV_LORA, MLA_ROPE, GQA_HEADS * GQA_HEAD_DIM,
             GQA_KV_HEADS * GQA_HEAD_DIM, GQA_KV_HEADS * GQA_HEAD_DIM)
IN_COLS = sum(IN_SPLITS)
MIX_WIDTH = MLA_HEADS * MLA_V + GQA_HEADS * GQA_HEAD_DIM

kernel_name = "hybrid_mla_gqa_natten_macaron_encoder"


def rmsnorm(x, g):
    x32 = x.astype(jnp.float32)
    y = x32 * lax.rsqrt(jnp.mean(x32 * x32, axis=-1, keepdims=True) + EPS)
    return (y * g.astype(jnp.float32)).astype(x.dtype)


def swiglu(x, wg, wu, wd):
    return (jax.nn.silu(x @ wg) * (x @ wu)) @ wd


def rope(x, pos, theta):
    half = x.shape[-1] // 2
    inv = 1.0 / (theta ** (jnp.arange(half, dtype=jnp.float32) / half))
    ang = pos.astype(jnp.float32)[:, None] * inv[None, :]
    cos = jnp.cos(ang)[:, None, :]
    sin = jnp.sin(ang)[:, None, :]
    x32 = x.astype(jnp.float32)
    x1, x2 = x32[..., :half], x32[..., half:]
    return jnp.concatenate([x1 * cos - x2 * sin, x2 * cos + x1 * sin], axis=-1).astype(x.dtype)


def blocked_attention(q, k, v, scale):
    b, L, hk, g, dk = q.shape
    nb = -(-L // Q_BLOCK)
    lp = nb * Q_BLOCK
    qp = jnp.pad(q, ((0, 0), (0, lp - L), (0, 0), (0, 0), (0, 0)))
    qb = jnp.moveaxis(qp.reshape(b, nb, Q_BLOCK, hk, g, dk), 1, 0)

    def one_block(qi):
        s = jnp.einsum('bqhgd,bkhd->bhgqk', qi, k).astype(jnp.float32) * scale
        p = jax.nn.softmax(s, axis=-1).astype(v.dtype)
        return jnp.einsum('bhgqk,bkhe->bqhge', p, v)

    o = lax.map(one_block, qb)
    return jnp.moveaxis(o, 0, 1).reshape(b, lp, hk, g, v.shape[-1])[:, :L]


def mla_gqa_mixer(h, w_in, q_norm, w_uq, kv_norm, w_ukv, gq_norm, gk_norm, w_out):
    b, L, _ = h.shape
    n_tok = L - N_META
    proj = h @ w_in
    cuts = np.cumsum(IN_SPLITS)[:-1].tolist()
    cq, ckv, kr, qb_, kb_, vb_ = jnp.split(proj, cuts, axis=-1)
    pos = jnp.arange(L, dtype=jnp.float32)
    qa = (rmsnorm(cq, q_norm) @ w_uq).reshape(b, L, MLA_HEADS, MLA_NOPE + MLA_ROPE)
    qa = jnp.concatenate([qa[..., :MLA_NOPE], rope(qa[..., MLA_NOPE:], pos, MLA_THETA)], axis=-1)
    kv = (rmsnorm(ckv, kv_norm) @ w_ukv).reshape(b, L, MLA_HEADS, MLA_NOPE + MLA_V)
    k_rope = jnp.broadcast_to(rope(kr[:, :, None, :], pos, MLA_THETA), (b, L, MLA_HEADS, MLA_ROPE))
    ka = jnp.concatenate([kv[..., :MLA_NOPE], k_rope], axis=-1)
    o_a = blocked_attention(qa[:, :, :, None, :], ka, kv[..., MLA_NOPE:],
                            (MLA_NOPE + MLA_ROPE) ** -0.5)
    tok = jnp.arange(n_tok)
    row = jnp.concatenate([jnp.full((N_META,), -1.0, jnp.float32), (tok // GRID_W).astype(jnp.float32)])
    col = jnp.concatenate([jnp.arange(N_META, dtype=jnp.float32), (tok % GRID_W).astype(jnp.float32)])
    half = GQA_HEAD_DIM // 2

    def axial(x):
        return jnp.concatenate([rope(x[..., :half], row, AXIAL_THETA),
                                rope(x[..., half:], col, AXIAL_THETA)], axis=-1)

    qg = axial(rmsnorm(qb_.reshape(b, L, GQA_HEADS, GQA_HEAD_DIM), gq_norm))
    qg = qg.reshape(b, L, GQA_KV_HEADS, GQA_HEADS // GQA_KV_HEADS, GQA_HEAD_DIM)
    kg = axial(rmsnorm(kb_.reshape(b, L, GQA_KV_HEADS, GQA_HEAD_DIM), gk_norm))
    vg = vb_.reshape(b, L, GQA_KV_HEADS, GQA_HEAD_DIM)
    o_b = blocked_attention(qg, kg, vg, GQA_HEAD_DIM ** -0.5)
    o = jnp.concatenate([o_a.reshape(b, L, -1), o_b.reshape(b, L, -1)], axis=-1)
    return o @ w_out


def neighbourhood_mixer(h, w_qkv, rpb, meta_bias, w_out):
    b, L, _ = h.shape
    n_tok = L - N_META
    rows = n_tok // GRID_W
    win_r = min(NA_WIN_R, rows)
    kblk_len = win_r * GRID_W
    qkv = (h @ w_qkv).reshape(b, L, 3, NA_HEADS, NA_HEAD_DIM)
    q = qkv[:, :, 0] * (NA_HEAD_DIM ** -0.5)
    k = qkv[:, :, 1]
    v = qkv[:, :, 2]
    qm, km, vm = q[:, :N_META], k[:, :N_META], v[:, :N_META]
    grid = (b, rows, GRID_W, NA_HEADS, NA_HEAD_DIM)
    qg = q[:, N_META:].reshape(grid)
    kg = k[:, N_META:].reshape(grid)
    vg = v[:, N_META:].reshape(grid)
    r_idx = jnp.arange(rows)
    r_start = jnp.clip(r_idx - win_r // 2, 0, rows - win_r)
    c_idx = jnp.arange(GRID_W)
    c_start = jnp.clip(c_idx - NA_WIN_C // 2, 0, GRID_W - NA_WIN_C)
    col_mask = (c_idx[None, :] >= c_start[:, None]) & (c_idx[None, :] < c_start[:, None] + NA_WIN_C)
    blk_mask = jnp.tile(col_mask, (1, win_r))
    col_off = jnp.clip(c_idx[None, :] - c_idx[:, None] + NA_WIN_C - 1, 0, 2 * NA_WIN_C - 2)
    rpb_c = rpb[:, :, col_off]
    mb = meta_bias[:, None, :].astype(jnp.float32)

    def one_row(args):
        q_r, r = args
        rs = r_start[r]
        k_blk = lax.dynamic_slice_in_dim(kg, rs, win_r, axis=1).reshape(b, kblk_len, NA_HEADS, NA_HEAD_DIM)
        v_blk = lax.dynamic_slice_in_dim(vg, rs, win_r, axis=1).reshape(b, kblk_len, NA_HEADS, NA_HEAD_DIM)
        row_off = rs + jnp.arange(win_r) - r + NA_WIN_R - 1
        bias = jnp.take(rpb_c, row_off, axis=1).transpose(0, 2, 1, 3).reshape(NA_HEADS, GRID_W, kblk_len)
        s_g = jnp.einsum('bqhd,bkhd->bhqk', q_r, k_blk).astype(jnp.float32) + bias.astype(jnp.float32)
        s_g = jnp.where(blk_mask, s_g, NEG_INF)
        s_m = jnp.einsum('bqhd,bmhd->bhqm', q_r, km).astype(jnp.float32) + mb
        p = jax.nn.softmax(jnp.concatenate([s_g, s_m], axis=-1), axis=-1).astype(v.dtype)
        return (jnp.einsum('bhqk,bkhd->bqhd', p[..., :kblk_len], v_blk)
                + jnp.einsum('bhqm,bmhd->bqhd', p[..., kblk_len:], vm))

    o_g = lax.map(one_row, (jnp.moveaxis(qg, 1, 0), r_idx))
    o_g = jnp.moveaxis(o_g, 0, 1).reshape(b, n_tok, NA_HEADS * NA_HEAD_DIM)
    s_mm = jnp.einsum('bqhd,bmhd->bhqm', qm, km).astype(jnp.float32) + mb
    p_mm = jax.nn.softmax(s_mm, axis=-1).astype(v.dtype)
    o_m = jnp.einsum('bhqm,bmhd->bqhd', p_mm, vm).reshape(b, N_META, NA_HEADS * NA_HEAD_DIM)
    return jnp.concatenate([o_m, o_g], axis=1) @ w_out


def encoder(x, meta, norm_gains, ffn1_w_gate, ffn1_w_up, ffn1_w_down, ffn2_w_gate, ffn2_w_up,
            ffn2_w_down, attn_w_in, mla_q_norm, mla_w_uq, mla_kv_norm, mla_w_ukv, gqa_q_norm,
            gqa_k_norm, attn_w_out, na_w_qkv, na_rpb, na_meta_bias, na_w_out):
    b = x.shape[0]
    h = jnp.concatenate([jnp.broadcast_to(meta.astype(x.dtype)[None], (b, N_META, D_MODEL)), x], axis=1)
    for i in range(DEPTH):
        g = norm_gains[i]
        h = h + 0.5 * rmsnorm(swiglu(rmsnorm(h, g[0]), ffn1_w_gate[i], ffn1_w_up[i], ffn1_w_down[i]), g[1])
        a = rmsnorm(h, g[2])
        j = i // 2
        if i % 2 == 0:
            m = mla_gqa_mixer(a, attn_w_in[j], mla_q_norm[j], mla_w_uq[j], mla_kv_norm[j], mla_w_ukv[j],
                              gqa_q_norm[j], gqa_k_norm[j], attn_w_out[j])
        else:
            m = neighbourhood_mixer(a, na_w_qkv[j], na_rpb[j], na_meta_bias[j], na_w_out[j])
        h = h + rmsnorm(m, g[3])
        h = h + 0.5 * rmsnorm(swiglu(rmsnorm(h, g[4]), ffn2_w_gate[i], ffn2_w_up[i], ffn2_w_down[i]), g[5])
    return h[:, N_META:]


def setup_inputs(seed: int = 0) -> dict:
    key = jax.random.key(seed)
    ks = jax.random.split(key, 24)

    def nrm(k, shape, scale):
        return jax.random.normal(k, shape, jnp.float32) * scale

    def gain(k, shape):
        return 1.0 + 0.05 * jax.random.normal(k, shape, jnp.float32)

    D, F = D_MODEL, D_FF
    return {
        "x_prompt": nrm(ks[0], (BATCH, SEQ, D), 1.0),
        "x_sample": nrm(ks[1], (DEC_BATCH, DEC_SEQ, D), 1.0),
        "meta": nrm(ks[2], (N_META, D), 1.0),
        "norm_gains": gain(ks[3], (DEPTH, 6, D)),
        "ffn1_w_gate": nrm(ks[4], (DEPTH, D, F), D ** -0.5),
        "ffn1_w_up": nrm(ks[5], (DEPTH, D, F), D ** -0.5),
        "ffn1_w_down": nrm(ks[6], (DEPTH, F, D), F ** -0.5),
        "ffn2_w_gate": nrm(ks[7], (DEPTH, D, F), D ** -0.5),
        "ffn2_w_up": nrm(ks[8], (DEPTH, D, F), D ** -0.5),
        "ffn2_w_down": nrm(ks[9], (DEPTH, F, D), F ** -0.5),
        "attn_w_in": nrm(ks[10], (N_EVEN, D, IN_COLS), D ** -0.5),
        "mla_q_norm": gain(ks[11], (N_EVEN, MLA_Q_LORA)),
        "mla_w_uq": nrm(ks[12], (N_EVEN, MLA_Q_LORA, MLA_HEADS * (MLA_NOPE + MLA_ROPE)), MLA_Q_LORA ** -0.5),
        "mla_kv_norm": gain(ks[13], (N_EVEN, MLA_KV_LORA)),
        "mla_w_ukv": nrm(ks[14], (N_EVEN, MLA_KV_LORA, MLA_HEADS * (MLA_NOPE + MLA_V)), MLA_KV_LORA ** -0.5),
        "gqa_q_norm": gain(ks[15], (N_EVEN, GQA_HEAD_DIM)),
        "gqa_k_norm": gain(ks[16], (N_EVEN, GQA_HEAD_DIM)),
        "attn_w_out": nrm(ks[17], (N_EVEN, MIX_WIDTH, D), MIX_WIDTH ** -0.5),
        "na_w_qkv": nrm(ks[18], (N_ODD, D, 3 * NA_HEADS * NA_HEAD_DIM), D ** -0.5),
        "na_rpb": nrm(ks[19], (N_ODD, NA_HEADS, 2 * NA_WIN_R - 1, 2 * NA_WIN_C - 1), 0.1),
        "na_meta_bias": nrm(ks[20], (N_ODD, NA_HEADS, N_META), 0.1),
        "na_w_out": nrm(ks[21], (N_ODD, NA_HEADS * NA_HEAD_DIM, D), (NA_HEADS * NA_HEAD_DIM) ** -0.5),
    }


def reference(x_prompt, x_sample, meta, norm_gains, ffn1_w_gate, ffn1_w_up, ffn1_w_down, ffn2_w_gate,
              ffn2_w_up, ffn2_w_down, attn_w_in, mla_q_norm, mla_w_uq, mla_kv_norm, mla_w_ukv,
              gqa_q_norm, gqa_k_norm, attn_w_out, na_w_qkv, na_rpb, na_meta_bias, na_w_out):
    y_prompt = encoder(x_prompt, meta, norm_gains, ffn1_w_gate, ffn1_w_up, ffn1_w_down, ffn2_w_gate,
                       ffn2_w_up, ffn2_w_down, attn_w_in, mla_q_norm, mla_w_uq, mla_kv_norm, mla_w_ukv,
                       gqa_q_norm, gqa_k_norm, attn_w_out, na_w_qkv, na_rpb, na_meta_bias, na_w_out)
    y_sample = encoder(x_sample, meta, norm_gains, ffn1_w_gate, ffn1_w_up, ffn1_w_down, ffn2_w_gate,
                       ffn2_w_up, ffn2_w_down, attn_w_in, mla_q_norm, mla_w_uq, mla_kv_norm, mla_w_ukv,
                       gqa_q_norm, gqa_k_norm, attn_w_out, na_w_qkv, na_rpb, na_meta_bias, na_w_out)
    return (y_prompt, y_sample)
```

```python
import functools
import math

import jax
import jax.numpy as jnp
import numpy as np
from jax import lax
from jax.experimental import pallas as pl
from jax.experimental.pallas import tpu as pltpu

F32 = jnp.float32
BF16 = jnp.bfloat16

D_MODEL = 1024
N_META = 16
GRID_W = 64
D_FF = 2816
EPS = 1e-6
NEG_INF = -1e30
LOG2E = math.log2(math.e)

MLA_HEADS = 8
MLA_Q_LORA = 256
MLA_KV_LORA = 128
MLA_NOPE = 64
MLA_ROPE = 32
MLA_V = 64
GQA_HEADS = 8
GQA_KV_HEADS = 2
GQA_HEAD_DIM = 64
ROPE_THETA = 10000.0
NA_HEADS = 16
NA_HEAD_DIM = 64
NA_WIN_R = 8
NA_WIN_C = 16

LANE = 128
HEAD_SLOTS = MLA_HEADS + GQA_HEADS
K_SLOTS = MLA_HEADS + GQA_KV_HEADS
V_SLOTS = MLA_HEADS // 2 + 1
VMEM_LIMIT = 56 * 1024 * 1024

_C_CQ = 0
_C_CKV = _C_CQ + MLA_Q_LORA
_C_KRA = _C_CKV + MLA_KV_LORA
_C_KRB = _C_KRA + LANE
_C_GQA = _C_KRB + LANE
_C_GQB = _C_GQA + GQA_HEADS * LANE
_C_GKA = _C_GQB + GQA_HEADS * LANE
_C_GKB = _C_GKA + GQA_KV_HEADS * LANE
_C_GV = _C_GKB + GQA_KV_HEADS * LANE
_C_END = _C_GV + LANE


def _const_spec(shape):
    nd = len(shape)
    return pl.BlockSpec(shape, lambda *_: (0,) * nd, pipeline_mode=pl.Buffered(1))


def _rms(x, g):
    ms = jnp.mean(x * x, axis=-1, keepdims=True)
    return x * lax.rsqrt(ms + EPS) * g


def _dot(a, b):
    return jnp.dot(a, b, preferred_element_type=F32)


def _dot_nt(a, b):
    return lax.dot_general(a, b, (((1,), (1,)), ((), ())), preferred_element_type=F32)


def _row_tile(rows, want):
    t = min(rows, want)
    while rows % t:
        t //= 2
    return t


def _ffn_body(h, g_ref, pre, post, wg_ref, wu_ref, wd_ref):
    xn = _rms(h, g_ref[pre:pre + 1, :]).astype(BF16)
    gate = _dot(xn, wg_ref[...])
    up = _dot(xn, wu_ref[...])
    act = (gate * jax.nn.sigmoid(gate) * up).astype(BF16)
    y = _dot(act, wd_ref[...])
    return h + 0.5 * _rms(y, g_ref[post:post + 1, :])


def _ffn1_kernel(h_ref, g_ref, wg_ref, wu_ref, wd_ref, out_ref):
    out_ref[...] = _ffn_body(h_ref[...], g_ref, 0, 1, wg_ref, wu_ref, wd_ref)


def _mix_ffn2_kernel(h_ref, o_ref, wo_ref, g_ref, wg_ref, wu_ref, wd_ref, out_ref):
    mixed = _dot(o_ref[...], wo_ref[...])
    h = h_ref[...] + _rms(mixed, g_ref[3:4, :])
    out_ref[...] = _ffn_body(h, g_ref, 4, 5, wg_ref, wu_ref, wd_ref)


def _ffn1(h, gains, wg, wu, wd):
    rows = h.shape[0]
    tm = _row_tile(rows, 512)
    return pl.pallas_call(
        _ffn1_kernel,
        grid=(rows // tm,),
        in_specs=[
            pl.BlockSpec((tm, D_MODEL), lambda i: (i, 0)),
            _const_spec(gains.shape),
            _const_spec(wg.shape), _const_spec(wu.shape), _const_spec(wd.shape),
        ],
        out_specs=pl.BlockSpec((tm, D_MODEL), lambda i: (i, 0)),
        out_shape=jax.ShapeDtypeStruct((rows, D_MODEL), F32),
        compiler_params=pltpu.CompilerParams(
            dimension_semantics=("arbitrary",), vmem_limit_bytes=VMEM_LIMIT),
        name="ffn1",
    )(h, gains, wg, wu, wd)


def _mix_ffn2(h, o, wo, gains, wg, wu, wd):
    rows = h.shape[0]
    tm = _row_tile(rows, 512)
    return pl.pallas_call(
        _mix_ffn2_kernel,
        grid=(rows // tm,),
        in_specs=[
            pl.BlockSpec((tm, D_MODEL), lambda i: (i, 0)),
            pl.BlockSpec((tm, o.shape[1]), lambda i: (i, 0)),
            _const_spec(wo.shape),
            _const_spec(gains.shape),
            _const_spec(wg.shape), _const_spec(wu.shape), _const_spec(wd.shape),
        ],
        out_specs=pl.BlockSpec((tm, D_MODEL), lambda i: (i, 0)),
        out_shape=jax.ShapeDtypeStruct((rows, D_MODEL), F32),
        compiler_params=pltpu.CompilerParams(
            dimension_semantics=("arbitrary",), vmem_limit_bytes=VMEM_LIMIT),
        name="mix_ffn2",
    )(h, o, wo, gains, wg, wu, wd)


def _proj_dense_kernel(h_ref, g_ref, win_ref, qn_ref, wuq_ref, kvn_ref, wukv_ref,
                       gqa_ref, gqb_ref, gka_ref, gkb_ref, tab_ref, q_ref, k_ref, v_ref):
    a = _rms(h_ref[...], g_ref[2:3, :]).astype(BF16)
    proj = _dot(a, win_ref[...])
    tab = tab_ref[...]
    cos_q, sin_q = tab[:, 0:LANE], tab[:, LANE:2 * LANE]
    cos_k, sin_k = tab[:, 2 * LANE:3 * LANE], tab[:, 3 * LANE:4 * LANE]
    cos_g, sin_g = tab[:, 4 * LANE:5 * LANE], tab[:, 5 * LANE:6 * LANE]

    cqn = _rms(proj[:, _C_CQ:_C_CQ + MLA_Q_LORA], qn_ref[...]).astype(BF16)
    qab = _dot(cqn, wuq_ref[...])
    nq = MLA_HEADS * LANE
    for h in range(MLA_HEADS):
        qa = qab[:, h * LANE:(h + 1) * LANE]
        qb = qab[:, nq + h * LANE:nq + (h + 1) * LANE]
        q_ref[:, h * LANE:(h + 1) * LANE] = (qa * cos_q + qb * sin_q).astype(BF16)

    ckvn = _rms(proj[:, _C_CKV:_C_CKV + MLA_KV_LORA], kvn_ref[...]).astype(BF16)
    kv = _dot(ckvn, wukv_ref[...])
    k_rope = (proj[:, _C_KRA:_C_KRA + LANE] * cos_k + proj[:, _C_KRB:_C_KRB + LANE] * sin_k)
    for h in range(MLA_HEADS):
        k_ref[:, h * LANE:(h + 1) * LANE] = (kv[:, h * LANE:(h + 1) * LANE] + k_rope).astype(BF16)
    nv = MLA_HEADS * MLA_V
    v_ref[:, 0:nv] = kv[:, nq:nq + nv].astype(BF16)

    gq_scale = GQA_HEAD_DIM ** -0.5 * LOG2E
    cq_g = cos_g * (gqa_ref[...] * gq_scale)
    sq_g = sin_g * (gqb_ref[...] * gq_scale)
    for h in range(GQA_HEADS):
        xa = proj[:, _C_GQA + h * LANE:_C_GQA + (h + 1) * LANE]
        xb = proj[:, _C_GQB + h * LANE:_C_GQB + (h + 1) * LANE]
        r = lax.rsqrt(jnp.sum(xa * xa, axis=-1, keepdims=True) * (1.0 / GQA_HEAD_DIM) + EPS)
        q_ref[:, nq + h * LANE:nq + (h + 1) * LANE] = ((xa * cq_g + xb * sq_g) * r).astype(BF16)
    ck_g = cos_g * gka_ref[...]
    sk_g = sin_g * gkb_ref[...]
    for h in range(GQA_KV_HEADS):
        xa = proj[:, _C_GKA + h * LANE:_C_GKA + (h + 1) * LANE]
        xb = proj[:, _C_GKB + h * LANE:_C_GKB + (h + 1) * LANE]
        r = lax.rsqrt(jnp.sum(xa * xa, axis=-1, keepdims=True) * (1.0 / GQA_HEAD_DIM) + EPS)
        k_ref[:, nq + h * LANE:nq + (h + 1) * LANE] = ((xa * ck_g + xb * sk_g) * r).astype(BF16)
    v_ref[:, nv:nv + LANE] = proj[:, _C_GV:_C_GV + LANE].astype(BF16)


def _proj_dense(h, gains, w, tab, seq):
    rows = h.shape[0]
    tm = _row_tile(seq if seq else rows, 512)
    nblk = (seq // tm) if seq else 1
    ntab = tab.shape[1]
    consts = [w["w_in"], w["q_norm"], w["w_uq"], w["kv_norm"], w["w_ukv"],
              w["gq_a"], w["gq_b"], w["gk_a"], w["gk_b"]]
    return pl.pallas_call(
        _proj_dense_kernel,
        grid=(rows // tm,),
        in_specs=[pl.BlockSpec((tm, D_MODEL), lambda i: (i, 0)), _const_spec(gains.shape)]
        + [_const_spec(c.shape) for c in consts]
        + [pl.BlockSpec((tm, ntab), lambda i: (i % nblk, 0))],
        out_specs=[
            pl.BlockSpec((tm, HEAD_SLOTS * LANE), lambda i: (i, 0)),
            pl.BlockSpec((tm, K_SLOTS * LANE), lambda i: (i, 0)),
            pl.BlockSpec((tm, V_SLOTS * LANE), lambda i: (i, 0)),
        ],
        out_shape=[
            jax.ShapeDtypeStruct((rows, HEAD_SLOTS * LANE), BF16),
            jax.ShapeDtypeStruct((rows, K_SLOTS * LANE), BF16),
            jax.ShapeDtypeStruct((rows, V_SLOTS * LANE), BF16),
        ],
        compiler_params=pltpu.CompilerParams(
            dimension_semantics=("arbitrary",), vmem_limit_bytes=VMEM_LIMIT),
        name="proj_dense",
    )(h, gains, *consts, tab)


def _head_slots(h):
    if h < MLA_HEADS:
        return h, h // 2, h % 2
    g = h - MLA_HEADS
    kvh = g // (GQA_HEADS // GQA_KV_HEADS)
    return MLA_HEADS + kvh, MLA_HEADS // 2, kvh


def _dense_attn_kernel(q_ref, k_ref, v_ref, km_ref, vm_ref, o_ref, m_ref, l_ref, acc_ref):
    kv = pl.program_id(2)
    tq = q_ref.shape[0]

    @pl.when(kv == 0)
    def _():
        lane = lax.broadcasted_iota(jnp.int32, (tq, LANE), 1)
        for h in range(HEAD_SLOTS):
            ks, vs, _ = _head_slots(h)
            q = q_ref[:, h * LANE:(h + 1) * LANE]
            s = _dot_nt(q, km_ref[:, ks * LANE:(ks + 1) * LANE])
            s = jnp.where(lane < N_META, s, NEG_INF)
            m = jnp.max(s, axis=-1, keepdims=True)
            p = jnp.exp2(s - m)
            m_ref[h] = m
            l_ref[h] = jnp.sum(p, axis=-1, keepdims=True)
            acc_ref[h] = _dot(p.astype(BF16), vm_ref[:, vs * LANE:(vs + 1) * LANE])

    for h in range(HEAD_SLOTS):
        ks, vs, _ = _head_slots(h)
        q = q_ref[:, h * LANE:(h + 1) * LANE]
        s = _dot_nt(q, k_ref[:, ks * LANE:(ks + 1) * LANE])
        m_prev = m_ref[h]
        m_new = jnp.maximum(m_prev, jnp.max(s, axis=-1, keepdims=True))
        alpha = jnp.exp2(m_prev - m_new)
        p = jnp.exp2(s - m_new)
        l_ref[h] = alpha * l_ref[h] + jnp.sum(p, axis=-1, keepdims=True)
        acc_ref[h] = alpha * acc_ref[h] + _dot(p.astype(BF16), v_ref[:, vs * LANE:(vs + 1) * LANE])
        m_ref[h] = m_new

    @pl.when(kv == pl.num_programs(2) - 1)
    def _():
        lane = lax.broadcasted_iota(jnp.int32, (tq, LANE), 1)
        first = lane < (LANE // 2)
        pairs = [(2 * j, 2 * j + 1) for j in range(MLA_HEADS // 2)]
        per_kv = GQA_HEADS // GQA_KV_HEADS
        pairs += [(MLA_HEADS + i, MLA_HEADS + per_kv + i) for i in range(per_kv)]
        for j, (ha, hb) in enumerate(pairs):
            oa = acc_ref[ha] / l_ref[ha]
            ob = acc_ref[hb] / l_ref[hb]
            o_ref[:, j * LANE:(j + 1) * LANE] = jnp.where(first, oa, ob).astype(BF16)


def _dense_attn(q, k, v, km, vm, *, n_seq, seq, tq, q_base, meta_base):
    nq = (q.shape[0] - q_base) // (n_seq * tq) if tq != N_META else 1
    tk = _row_tile(seq, 512)
    nk = seq // tk
    qb0 = q_base // tq
    out_rows = n_seq * nq * tq
    return pl.pallas_call(
        _dense_attn_kernel,
        grid=(n_seq, nq, nk),
        in_specs=[
            pl.BlockSpec((tq, HEAD_SLOTS * LANE), lambda b, i, j: (qb0 + b * nq + i, 0)),
            pl.BlockSpec((tk, K_SLOTS * LANE), lambda b, i, j: (b * nk + j, 0)),
            pl.BlockSpec((tk, V_SLOTS * LANE), lambda b, i, j: (b * nk + j, 0)),
            pl.BlockSpec((None, LANE, K_SLOTS * LANE), lambda b, i, j: (meta_base + b, 0, 0)),
            pl.BlockSpec((None, LANE, V_SLOTS * LANE), lambda b, i, j: (meta_base + b, 0, 0)),
        ],
        out_specs=pl.BlockSpec((tq, D_MODEL), lambda b, i, j: (b * nq + i, 0)),
        out_shape=jax.ShapeDtypeStruct((out_rows, D_MODEL), BF16),
        scratch_shapes=[
            pltpu.VMEM((HEAD_SLOTS, tq, 1), F32),
            pltpu.VMEM((HEAD_SLOTS, tq, 1), F32),
            pltpu.VMEM((HEAD_SLOTS, tq, LANE), F32),
        ],
        compiler_params=pltpu.CompilerParams(
            dimension_semantics=("arbitrary", "arbitrary", "arbitrary"),
            vmem_limit_bytes=VMEM_LIMIT),
        name="dense_attn",
    )(q, k, v, km, vm)


def _proj_na_kernel(h_ref, g_ref, w_ref, q_ref, k_ref, v_ref):
    a = _rms(h_ref[...], g_ref[2:3, :]).astype(BF16)
    qkv = _dot(a, w_ref[...])
    n = NA_HEADS * NA_HEAD_DIM
    q_ref[...] = (qkv[:, 0:n] * (NA_HEAD_DIM ** -0.5 * LOG2E)).astype(BF16)
    k_ref[...] = qkv[:, n:2 * n].astype(BF16)
    v_ref[...] = qkv[:, 2 * n:3 * n].astype(BF16)


def _proj_na(h, gains, w):
    rows = h.shape[0]
    tm = _row_tile(rows, 512)
    n = NA_HEADS * NA_HEAD_DIM
    return pl.pallas_call(
        _proj_na_kernel,
        grid=(rows // tm,),
        in_specs=[pl.BlockSpec((tm, D_MODEL), lambda i: (i, 0)), _const_spec(gains.shape),
                  _const_spec(w.shape)],
        out_specs=[pl.BlockSpec((tm, n), lambda i: (i, 0))] * 3,
        out_shape=[jax.ShapeDtypeStruct((rows, n), BF16)] * 3,
        compiler_params=pltpu.CompilerParams(
            dimension_semantics=("arbitrary",), vmem_limit_bytes=VMEM_LIMIT),
        name="proj_na",
    )(h, gains, w)


def _na_kernel(q_ref, k_ref, v_ref, km_ref, vm_ref, bias_ref, mb_ref, o_ref, *, rows, rows_per_step):
    step = pl.program_id(1)
    win_keys = NA_WIN_R * GRID_W
    lane = lax.broadcasted_iota(jnp.int32, (GRID_W, LANE), 1)
    first = lane < (LANE // 2)

    def one_row(rl, carry):
        r = step * rows_per_step + rl
        rs = jnp.clip(r - NA_WIN_R // 2, 0, rows - NA_WIN_R)
        off = rs - r + NA_WIN_R - 1
        q0 = pl.multiple_of(rl * GRID_W, GRID_W)
        k0 = pl.multiple_of(rs * GRID_W, GRID_W)
        for j in range(NA_HEADS // 2):
            cols = slice(j * LANE, (j + 1) * LANE)
            qp = q_ref[pl.ds(q0, GRID_W), cols]
            kw = k_ref[pl.ds(k0, win_keys), cols]
            vw = v_ref[pl.ds(k0, win_keys), cols]
            km = km_ref[:, cols]
            vm = vm_ref[:, cols]
            outs = []
            for half in range(2):
                h = 2 * j + half
                qh = jnp.where(first if half == 0 else jnp.logical_not(first), qp, jnp.zeros_like(qp))
                s = _dot_nt(qh, kw)
                b = jnp.concatenate([bias_ref[h, off + 2 * t] for t in range(NA_WIN_R // 2)], axis=1)
                s = jnp.where(b > 0.5 * NEG_INF, s + b, NEG_INF)
                sm = _dot_nt(qh, km)
                sm = jnp.where(lane < N_META, sm + mb_ref[h:h + 1, :], NEG_INF)
                m = jnp.maximum(jnp.max(s, axis=-1, keepdims=True), jnp.max(sm, axis=-1, keepdims=True))
                p = jnp.exp2(s - m)
                pm = jnp.exp2(sm - m)
                l = jnp.sum(p, axis=-1, keepdims=True) + jnp.sum(pm, axis=-1, keepdims=True)
                o = _dot(p.astype(BF16), vw) + _dot(pm.astype(BF16), vm)
                outs.append(o / l)
            o_ref[pl.ds(q0, GRID_W), cols] = jnp.where(first, outs[0], outs[1]).astype(BF16)
        return carry

    lax.fori_loop(0, rows_per_step, one_row, 0)


def _na_attn(q, k, v, km, vm, bias, mb, *, n_seq, seq, meta_base):
    rows = seq // GRID_W
    rps = 8
    nsteps = rows // rps
    n = NA_HEADS * NA_HEAD_DIM
    return pl.pallas_call(
        functools.partial(_na_kernel, rows=rows, rows_per_step=rps),
        grid=(n_seq, nsteps),
        in_specs=[
            pl.BlockSpec((rps * GRID_W, n), lambda b, i: (b * nsteps + i, 0)),
            pl.BlockSpec((seq, n), lambda b, i: (b, 0), pipeline_mode=pl.Buffered(1)),
            pl.BlockSpec((seq, n), lambda b, i: (b, 0), pipeline_mode=pl.Buffered(1)),
            pl.BlockSpec((None, LANE, n), lambda b, i: (meta_base + b, 0, 0)),
            pl.BlockSpec((None, LANE, n), lambda b, i: (meta_base + b, 0, 0)),
            _const_spec(bias.shape),
            _const_spec(mb.shape),
        ],
        out_specs=pl.BlockSpec((rps * GRID_W, n), lambda b, i: (b * nsteps + i, 0)),
        out_shape=jax.ShapeDtypeStruct((n_seq * seq, n), BF16),
        compiler_params=pltpu.CompilerParams(
            dimension_semantics=("arbitrary", "arbitrary"), vmem_limit_bytes=VMEM_LIMIT),
        name="na_attn",
    )(q, k, v, km, vm, bias, mb)


def _na_meta_kernel(q_ref, km_ref, vm_ref, mb_ref, o_ref):
    lane = lax.broadcasted_iota(jnp.int32, (N_META, LANE), 1)
    first = lane < (LANE // 2)
    for j in range(NA_HEADS // 2):
        cols = slice(j * LANE, (j + 1) * LANE)
        qp = q_ref[:, cols]
        km = km_ref[:, cols]
        vm = vm_ref[:, cols]
        outs = []
        for half in range(2):
            h = 2 * j + half
            qh = jnp.where(first if half == 0 else jnp.logical_not(first), qp, jnp.zeros_like(qp))
            sm = _dot_nt(qh, km)
            sm = jnp.where(lane < N_META, sm + mb_ref[h:h + 1, :], NEG_INF)
            m = jnp.max(sm, axis=-1, keepdims=True)
            pm = jnp.exp2(sm - m)
            l = jnp.sum(pm, axis=-1, keepdims=True)
            outs.append(_dot(pm.astype(BF16), vm) / l)
        o_ref[:, cols] = jnp.where(first, outs[0], outs[1]).astype(BF16)


def _na_meta(qm, km, vm, mb):
    n_seq = km.shape[0]
    n = NA_HEADS * NA_HEAD_DIM
    return pl.pallas_call(
        _na_meta_kernel,
        grid=(n_seq,),
        in_specs=[
            pl.BlockSpec((N_META, n), lambda b: (b, 0)),
            pl.BlockSpec((None, LANE, n), lambda b: (b, 0, 0)),
            pl.BlockSpec((None, LANE, n), lambda b: (b, 0, 0)),
            _const_spec(mb.shape),
        ],
        out_specs=pl.BlockSpec((N_META, n), lambda b: (b, 0)),
        out_shape=jax.ShapeDtypeStruct((n_seq * N_META, n), BF16),
        compiler_params=pltpu.CompilerParams(dimension_semantics=("arbitrary",)),
        name="na_meta",
    )(qm, km, vm, mb)


def _take_cols(w, idx):
    idx = np.asarray(idx)
    cols = jnp.take(w, jnp.asarray(np.maximum(idx, 0)), axis=1)
    return jnp.where(jnp.asarray(idx >= 0)[None, :], cols, 0.0)


def _swap_halves(n):
    half = n // 2
    return np.concatenate([np.arange(half, n), np.arange(0, half)])


def _dense_weights(w_in, q_norm, w_uq, kv_norm, w_ukv, gq_norm, gk_norm, w_out):
    pad = lambda k: -np.ones(k, np.int64)
    o_kr = MLA_Q_LORA + MLA_KV_LORA
    o_gq = o_kr + MLA_ROPE
    o_gk = o_gq + GQA_HEADS * GQA_HEAD_DIM
    o_gv = o_gk + GQA_KV_HEADS * GQA_HEAD_DIM
    axial = np.concatenate([_swap_halves(GQA_HEAD_DIM // 2),
                            GQA_HEAD_DIM // 2 + _swap_halves(GQA_HEAD_DIM // 2)])
    idx = [np.arange(0, o_kr)]
    idx += [pad(MLA_NOPE), o_kr + np.arange(MLA_ROPE), pad(LANE - MLA_NOPE - MLA_ROPE)]
    idx += [pad(MLA_NOPE), o_kr + _swap_halves(MLA_ROPE), pad(LANE - MLA_NOPE - MLA_ROPE)]
    for h in range(GQA_HEADS):
        idx += [o_gq + h * GQA_HEAD_DIM + np.arange(GQA_HEAD_DIM), pad(LANE - GQA_HEAD_DIM)]
    for h in range(GQA_HEADS):
        idx += [o_gq + h * GQA_HEAD_DIM + axial, pad(LANE - GQA_HEAD_DIM)]
    for h in range(GQA_KV_HEADS):
        idx += [o_gk + h * GQA_HEAD_DIM + np.arange(GQA_HEAD_DIM), pad(LANE - GQA_HEAD_DIM)]
    for h in range(GQA_KV_HEADS):
        idx += [o_gk + h * GQA_HEAD_DIM + axial, pad(LANE - GQA_HEAD_DIM)]
    idx += [o_gv + np.arange(GQA_KV_HEADS * GQA_HEAD_DIM)]
    idx = np.concatenate(idx)
    assert idx.shape[0] == _C_END
    w_in2 = _take_cols(w_in, idx).astype(BF16)

    hd = MLA_NOPE + MLA_ROPE
    ia, ib = [], []
    for h in range(MLA_HEADS):
        ia += [h * hd + np.arange(hd), pad(LANE - hd)]
        ib += [pad(MLA_NOPE), h * hd + MLA_NOPE + _swap_halves(MLA_ROPE), pad(LANE - hd)]
    w_uq2 = _take_cols(w_uq, np.concatenate(ia + ib)).astype(BF16)

    kvd = MLA_NOPE + MLA_V
    ik, iv = [], []
    for h in range(MLA_HEADS):
        ik += [h * kvd + np.arange(MLA_NOPE), pad(LANE - MLA_NOPE)]
        iv += [h * kvd + MLA_NOPE + np.arange(MLA_V)]
    w_ukv2 = _take_cols(w_ukv, np.concatenate(ik + iv)).astype(BF16)

    def gain_pair(g):
        ga = jnp.concatenate([g, jnp.zeros((LANE - GQA_HEAD_DIM,), F32)])[None, :]
        gb = jnp.concatenate([g[jnp.asarray(axial)], jnp.zeros((LANE - GQA_HEAD_DIM,), F32)])[None, :]
        return ga, gb

    gq_a, gq_b = gain_pair(gq_norm)
    gk_a, gk_b = gain_pair(gk_norm)

    per_kv = GQA_HEADS // GQA_KV_HEADS
    rows = [np.arange(MLA_HEADS * MLA_V)]
    base = MLA_HEADS * MLA_V
    for i in range(per_kv):
        rows += [base + i * GQA_HEAD_DIM + np.arange(GQA_HEAD_DIM),
                 base + (per_kv + i) * GQA_HEAD_DIM + np.arange(GQA_HEAD_DIM)]
    w_out2 = jnp.take(w_out, jnp.asarray(np.concatenate(rows)), axis=0).astype(BF16)

    return dict(w_in=w_in2, q_norm=q_norm[None, :], w_uq=w_uq2, kv_norm=kv_norm[None, :],
                w_ukv=w_ukv2, gq_a=gq_a, gq_b=gq_b, gk_a=gk_a, gk_b=gk_b), w_out2


def _rope_tables(pos, row, col):
    half = MLA_ROPE // 2
    inv = 1.0 / (ROPE_THETA ** (jnp.arange(half, dtype=F32) / half))
    n = pos.shape[0]

    def cs(p):
        ang = p.astype(F32)[:, None] * inv[None, :]
        return jnp.cos(ang), jnp.sin(ang)

    c1, s1 = cs(pos)
    cr, sr = cs(row)
    cc, sc = cs(col)
    z = lambda k: jnp.zeros((n, k), F32)
    qs = (MLA_NOPE + MLA_ROPE) ** -0.5 * LOG2E
    tail = LANE - MLA_NOPE - MLA_ROPE
    cos_q = jnp.concatenate([jnp.full((n, MLA_NOPE), qs, F32), qs * c1, qs * c1, z(tail)], axis=1)
    sin_q = jnp.concatenate([z(MLA_NOPE), -qs * s1, qs * s1, z(tail)], axis=1)
    cos_k = jnp.concatenate([z(MLA_NOPE), c1, c1, z(tail)], axis=1)
    sin_k = jnp.concatenate([z(MLA_NOPE), -s1, s1, z(tail)], axis=1)
    cos_g = jnp.concatenate([cr, cr, cc, cc, z(LANE - GQA_HEAD_DIM)], axis=1)
    sin_g = jnp.concatenate([-sr, sr, -sc, sc, z(LANE - GQA_HEAD_DIM)], axis=1)
    return jnp.concatenate([cos_q, sin_q, cos_k, sin_k, cos_g, sin_g], axis=1)


def _na_bias_tables(rpb, meta_bias):
    c_idx = np.arange(GRID_W)
    c_start = np.clip(c_idx - NA_WIN_C // 2, 0, GRID_W - NA_WIN_C)
    col_mask = (c_idx[None, :] >= c_start[:, None]) & (c_idx[None, :] < c_start[:, None] + NA_WIN_C)
    col_off = np.clip(c_idx[None, :] - c_idx[:, None] + NA_WIN_C - 1, 0, 2 * NA_WIN_C - 2)
    t = rpb[:, :, jnp.asarray(col_off)] * LOG2E
    t = jnp.where(jnp.asarray(col_mask)[None, None], t, NEG_INF)
    bias = jnp.concatenate([t[:, :-1], t[:, 1:]], axis=-1)
    mb = jnp.pad(meta_bias * LOG2E, ((0, 0), (0, LANE - N_META)))
    return bias, mb


def _pad_meta(x, n_seq):
    c = x.shape[1]
    return jnp.pad(x.reshape(n_seq, N_META, c), ((0, 0), (0, LANE - N_META), (0, 0)))


def kernel(x_prompt, x_sample, meta, norm_gains, ffn1_w_gate, ffn1_w_up, ffn1_w_down, ffn2_w_gate, ffn2_w_up, ffn2_w_down, attn_w_in, mla_q_norm, mla_w_uq, mla_kv_norm, mla_w_ukv, gqa_q_norm, gqa_k_norm, attn_w_out, na_w_qkv, na_rpb, na_meta_bias, na_w_out):
    bp, sp, _ = x_prompt.shape
    bs, ss, _ = x_sample.shape
    n_seq = bp + bs
    depth = norm_gains.shape[0]
    groups = [(bp, sp, 0), (bs, ss, bp)]

    h_tok = [x_prompt.reshape(bp * sp, D_MODEL), x_sample.reshape(bs * ss, D_MODEL)]
    h_meta = jnp.tile(meta.astype(F32), (n_seq, 1))

    smax = max(sp, ss)
    t = jnp.arange(smax)
    tab_tok = _rope_tables(t + N_META, t // GRID_W, t % GRID_W)
    mi = jnp.tile(jnp.arange(N_META), n_seq)
    tab_meta = _rope_tables(mi, jnp.full_like(mi, -1), mi)

    for i in range(depth):
        gains = jnp.pad(norm_gains[i], ((0, 2), (0, 0)))
        w1 = (ffn1_w_gate[i].astype(BF16), ffn1_w_up[i].astype(BF16), ffn1_w_down[i].astype(BF16))
        w2 = (ffn2_w_gate[i].astype(BF16), ffn2_w_up[i].astype(BF16), ffn2_w_down[i].astype(BF16))
        j = i // 2
        h_tok = [_ffn1(h, gains, *w1) for h in h_tok]
        h_meta = _ffn1(h_meta, gains, *w1)
        if i % 2 == 0:
            w, w_out = _dense_weights(attn_w_in[j], mla_q_norm[j], mla_w_uq[j], mla_kv_norm[j],
                                      mla_w_ukv[j], gqa_q_norm[j], gqa_k_norm[j], attn_w_out[j])
            qkv_tok = [_proj_dense(h, gains, w, tab_tok, s) for h, (_, s, _) in zip(h_tok, groups)]
            qm, km, vm = _proj_dense(h_meta, gains, w, tab_meta, 0)
            kmp, vmp = _pad_meta(km, n_seq), _pad_meta(vm, n_seq)
            o_tok, o_meta = [], []
            for (q, k, v), (nb, s, b0) in zip(qkv_tok, groups):
                o_tok.append(_dense_attn(q, k, v, kmp, vmp, n_seq=nb, seq=s,
                                         tq=_row_tile(s, 512), q_base=0, meta_base=b0))
                o_meta.append(_dense_attn(qm, k, v, kmp, vmp, n_seq=nb, seq=s,
                                          tq=N_META, q_base=b0 * N_META, meta_base=b0))
            o_meta = jnp.concatenate(o_meta, axis=0)
        else:
            w_qkv = na_w_qkv[j].astype(BF16)
            w_out = na_w_out[j].astype(BF16)
            bias, mb = _na_bias_tables(na_rpb[j], na_meta_bias[j])
            qkv_tok = [_proj_na(h, gains, w_qkv) for h in h_tok]
            qm, km, vm = _proj_na(h_meta, gains, w_qkv)
            kmp, vmp = _pad_meta(km, n_seq), _pad_meta(vm, n_seq)
            o_tok = [_na_attn(q, k, v, kmp, vmp, bias, mb, n_seq=nb, seq=s, meta_base=b0)
                     for (q, k, v), (nb, s, b0) in zip(qkv_tok, groups)]
            o_meta = _na_meta(qm, kmp, vmp, mb)
        h_tok = [_mix_ffn2(h, o, w_out, gains, *w2) for h, o in zip(h_tok, o_tok)]
        h_meta = _mix_ffn2(h_meta, o_meta, w_out, gains, *w2)

    return (h_tok[0].reshape(bp, sp, D_MODEL), h_tok[1].reshape(bs, ss, D_MODEL))
```

```python
import functools
import math

import jax
import jax.numpy as jnp
import numpy as np
from jax import lax
from jax.experimental import pallas as pl
from jax.experimental.pallas import tpu as pltpu

F32 = jnp.float32
BF16 = jnp.bfloat16

D_MODEL = 1024
N_META = 16
GRID_W = 64
D_FF = 2816
EPS = 1e-6
NEG_INF = -1e30
LOG2E = math.log2(math.e)

MLA_HEADS = 8
MLA_Q_LORA = 256
MLA_KV_LORA = 128
MLA_NOPE = 64
MLA_ROPE = 32
MLA_V = 64
GQA_HEADS = 8
GQA_KV_HEADS = 2
GQA_HEAD_DIM = 64
ROPE_THETA = 10000.0
NA_HEADS = 16
NA_HEAD_DIM = 64
NA_WIN_R = 8
NA_WIN_C = 16

LANE = 128
HEAD_SLOTS = MLA_HEADS + GQA_HEADS
K_SLOTS = MLA_HEADS + GQA_KV_HEADS
V_ROWS = (MLA_HEADS + GQA_KV_HEADS) * MLA_V
SUBLANE = 8
VMEM_LIMIT = 56 * 1024 * 1024

_C_CQ = 0
_C_CKV = _C_CQ + MLA_Q_LORA
_C_KRA = _C_CKV + MLA_KV_LORA
_C_KRB = _C_KRA + LANE
_C_GQA = _C_KRB + LANE
_C_GQB = _C_GQA + GQA_HEADS * LANE
_C_GKA = _C_GQB + GQA_HEADS * LANE
_C_GKB = _C_GKA + GQA_KV_HEADS * LANE
_C_GV = _C_GKB + GQA_KV_HEADS * LANE
_C_END = _C_GV + LANE


def _const_spec(shape):
    nd = len(shape)
    return pl.BlockSpec(shape, lambda *_: (0,) * nd, pipeline_mode=pl.Buffered(1))


def _rms(x, g):
    ms = jnp.mean(x * x, axis=-1, keepdims=True)
    return x * lax.rsqrt(ms + EPS) * g


def _dot(a, b):
    return jnp.dot(a, b, preferred_element_type=F32)


def _dot_nt(a, b):
    return lax.dot_general(a, b, (((1,), (1,)), ((), ())), preferred_element_type=F32)


def _row_tile(rows, want):
    t = min(rows, want)
    while rows % t:
        t //= 2
    return t


def _ffn_body(h, g_ref, pre, post, wg_ref, wu_ref, wd_ref):
    xn = _rms(h, g_ref[pre:pre + 1, :]).astype(BF16)
    gate = _dot(xn, wg_ref[...])
    up = _dot(xn, wu_ref[...])
    act = (gate * jax.nn.sigmoid(gate) * up).astype(BF16)
    y = _dot(act, wd_ref[...])
    return h + 0.5 * _rms(y, g_ref[post:post + 1, :])


def _ffn1_kernel(h_ref, g_ref, wg_ref, wu_ref, wd_ref, out_ref):
    out_ref[...] = _ffn_body(h_ref[...], g_ref, 0, 1, wg_ref, wu_ref, wd_ref)


def _mix_ffn2_kernel(h_ref, o_ref, wo_ref, g_ref, wg_ref, wu_ref, wd_ref, out_ref):
    mixed = _dot(o_ref[...], wo_ref[...])
    h = h_ref[...] + _rms(mixed, g_ref[3:4, :])
    out_ref[...] = _ffn_body(h, g_ref, 4, 5, wg_ref, wu_ref, wd_ref)


def _ffn1(h, gains, wg, wu, wd):
    rows = h.shape[0]
    tm = _row_tile(rows, 512)
    return pl.pallas_call(
        _ffn1_kernel,
        grid=(rows // tm,),
        in_specs=[
            pl.BlockSpec((tm, D_MODEL), lambda i: (i, 0)),
            _const_spec(gains.shape),
            _const_spec(wg.shape), _const_spec(wu.shape), _const_spec(wd.shape),
        ],
        out_specs=pl.BlockSpec((tm, D_MODEL), lambda i: (i, 0)),
        out_shape=jax.ShapeDtypeStruct((rows, D_MODEL), F32),
        compiler_params=pltpu.CompilerParams(
            dimension_semantics=("arbitrary",), vmem_limit_bytes=VMEM_LIMIT),
        name="ffn1",
    )(h, gains, wg, wu, wd)


def _mix_ffn2(h, o, wo, gains, wg, wu, wd):
    rows = h.shape[0]
    tm = _row_tile(rows, 512)
    return pl.pallas_call(
        _mix_ffn2_kernel,
        grid=(rows // tm,),
        in_specs=[
            pl.BlockSpec((tm, D_MODEL), lambda i: (i, 0)),
            pl.BlockSpec((tm, o.shape[1]), lambda i: (i, 0)),
            _const_spec(wo.shape),
            _const_spec(gains.shape),
            _const_spec(wg.shape), _const_spec(wu.shape), _const_spec(wd.shape),
        ],
        out_specs=pl.BlockSpec((tm, D_MODEL), lambda i: (i, 0)),
        out_shape=jax.ShapeDtypeStruct((rows, D_MODEL), F32),
        compiler_params=pltpu.CompilerParams(
            dimension_semantics=("arbitrary",), vmem_limit_bytes=VMEM_LIMIT),
        name="mix_ffn2",
    )(h, o, wo, gains, wg, wu, wd)


def _proj_dense_kernel(h_ref, g_ref, win_ref, qn_ref, wuq_ref, kvn_ref, wukv_ref,
                       gqa_ref, gqb_ref, gka_ref, gkb_ref, tab_ref, q_ref, k_ref, vt_ref):
    a = _rms(h_ref[...], g_ref[2:3, :]).astype(BF16)
    proj = _dot(a, win_ref[...])
    tab = tab_ref[...]
    cos_q, sin_q = tab[:, 0:LANE], tab[:, LANE:2 * LANE]
    cos_k, sin_k = tab[:, 2 * LANE:3 * LANE], tab[:, 3 * LANE:4 * LANE]
    cos_g, sin_g = tab[:, 4 * LANE:5 * LANE], tab[:, 5 * LANE:6 * LANE]

    cqn = _rms(proj[:, _C_CQ:_C_CQ + MLA_Q_LORA], qn_ref[...]).astype(BF16)
    qab = _dot(cqn, wuq_ref[...])
    nq = MLA_HEADS * LANE
    for h in range(MLA_HEADS):
        qa = qab[:, h * LANE:(h + 1) * LANE]
        qb = qab[:, nq + h * LANE:nq + (h + 1) * LANE]
        q_ref[:, h * LANE:(h + 1) * LANE] = (qa * cos_q + qb * sin_q).astype(BF16)

    ckvn = _rms(proj[:, _C_CKV:_C_CKV + MLA_KV_LORA], kvn_ref[...]).astype(BF16)
    kv = _dot(ckvn, wukv_ref[...])
    k_rope = (proj[:, _C_KRA:_C_KRA + LANE] * cos_k + proj[:, _C_KRB:_C_KRB + LANE] * sin_k)
    for h in range(MLA_HEADS):
        k_ref[:, h * LANE:(h + 1) * LANE] = (kv[:, h * LANE:(h + 1) * LANE] + k_rope).astype(BF16)
    nv = MLA_HEADS * MLA_V

    gq_scale = GQA_HEAD_DIM ** -0.5 * LOG2E
    cq_g = cos_g * (gqa_ref[...] * gq_scale)
    sq_g = sin_g * (gqb_ref[...] * gq_scale)
    for h in range(GQA_HEADS):
        xa = proj[:, _C_GQA + h * LANE:_C_GQA + (h + 1) * LANE]
        xb = proj[:, _C_GQB + h * LANE:_C_GQB + (h + 1) * LANE]
        r = lax.rsqrt(jnp.sum(xa * xa, axis=-1, keepdims=True) * (1.0 / GQA_HEAD_DIM) + EPS)
        q_ref[:, nq + h * LANE:nq + (h + 1) * LANE] = ((xa * cq_g + xb * sq_g) * r).astype(BF16)
    ck_g = cos_g * gka_ref[...]
    sk_g = sin_g * gkb_ref[...]
    for h in range(GQA_KV_HEADS):
        xa = proj[:, _C_GKA + h * LANE:_C_GKA + (h + 1) * LANE]
        xb = proj[:, _C_GKB + h * LANE:_C_GKB + (h + 1) * LANE]
        r = lax.rsqrt(jnp.sum(xa * xa, axis=-1, keepdims=True) * (1.0 / GQA_HEAD_DIM) + EPS)
        k_ref[:, nq + h * LANE:nq + (h + 1) * LANE] = ((xa * ck_g + xb * sk_g) * r).astype(BF16)
    v = jnp.concatenate([kv[:, nq:nq + nv], proj[:, _C_GV:_C_GV + LANE]], axis=1)
    vt_ref[...] = v.T.astype(BF16)


def _proj_dense(h, gains, w, tab, seq):
    rows = h.shape[0]
    tm = _row_tile(seq if seq else rows, 512)
    nblk = (seq // tm) if seq else 1
    ntab = tab.shape[1]
    consts = [w["w_in"], w["q_norm"], w["w_uq"], w["kv_norm"], w["w_ukv"],
              w["gq_a"], w["gq_b"], w["gk_a"], w["gk_b"]]
    return pl.pallas_call(
        _proj_dense_kernel,
        grid=(rows // tm,),
        in_specs=[pl.BlockSpec((tm, D_MODEL), lambda i: (i, 0)), _const_spec(gains.shape)]
        + [_const_spec(c.shape) for c in consts]
        + [pl.BlockSpec((tm, ntab), lambda i: (i % nblk, 0))],
        out_specs=[
            pl.BlockSpec((tm, HEAD_SLOTS * LANE), lambda i: (i, 0)),
            pl.BlockSpec((tm, K_SLOTS * LANE), lambda i: (i, 0)),
            pl.BlockSpec((V_ROWS, tm), lambda i: (0, i)),
        ],
        out_shape=[
            jax.ShapeDtypeStruct((rows, HEAD_SLOTS * LANE), BF16),
            jax.ShapeDtypeStruct((rows, K_SLOTS * LANE), BF16),
            jax.ShapeDtypeStruct((V_ROWS, rows), BF16),
        ],
        compiler_params=pltpu.CompilerParams(
            dimension_semantics=("arbitrary",), vmem_limit_bytes=VMEM_LIMIT),
        name="proj_dense",
    )(h, gains, *consts, tab)


def _head_slots(h):
    if h < MLA_HEADS:
        return h, h
    kvh = (h - MLA_HEADS) // (GQA_HEADS // GQA_KV_HEADS)
    return MLA_HEADS + kvh, MLA_HEADS + kvh


def _sublane_bcast_max(x):
    return jnp.broadcast_to(jnp.max(x, axis=0, keepdims=True), x.shape)


def _dense_attn_kernel(q_ref, k_ref, vt_ref, km_ref, vmt_ref, o_ref, m_ref, l_ref, acc_ref):
    kv = pl.program_id(2)
    tq = q_ref.shape[0]
    tk = k_ref.shape[0]
    hd = MLA_V

    @pl.when(kv == 0)
    def _():
        key = lax.broadcasted_iota(jnp.int32, (LANE, tq), 0)
        for h in range(HEAD_SLOTS):
            ks, vh = _head_slots(h)
            q = q_ref[:, h * LANE:(h + 1) * LANE]
            s = _dot_nt(km_ref[:, ks * LANE:(ks + 1) * LANE], q)
            s = jnp.where(key < N_META, s, NEG_INF)
            s3 = s.reshape(LANE // SUBLANE, SUBLANE, tq)
            m = _sublane_bcast_max(jnp.max(s3, axis=0))
            p3 = jnp.exp2(s3 - m[None])
            m_ref[h] = m
            l_ref[h] = jnp.sum(p3, axis=0)
            p = p3.reshape(LANE, tq).astype(BF16)
            acc_ref[h] = _dot(vmt_ref[vh * hd:(vh + 1) * hd, :], p)

    for h in range(HEAD_SLOTS):
        ks, vh = _head_slots(h)
        q = q_ref[:, h * LANE:(h + 1) * LANE]
        s = _dot_nt(k_ref[:, ks * LANE:(ks + 1) * LANE], q)
        s3 = s.reshape(tk // SUBLANE, SUBLANE, tq)
        m_prev = m_ref[h]
        m_new = jnp.maximum(m_prev, _sublane_bcast_max(jnp.max(s3, axis=0)))
        alpha = jnp.exp2(m_prev - m_new)
        p3 = jnp.exp2(s3 - m_new[None])
        l_ref[h] = alpha * l_ref[h] + jnp.sum(p3, axis=0)
        p = p3.reshape(tk, tq).astype(BF16)
        pv = _dot(vt_ref[vh * hd:(vh + 1) * hd, :], p)
        acc = acc_ref[h].reshape(hd // SUBLANE, SUBLANE, tq) * alpha[None]
        acc_ref[h] = acc.reshape(hd, tq) + pv
        m_ref[h] = m_new

    @pl.when(kv == pl.num_programs(2) - 1)
    def _():
        for j in range(HEAD_SLOTS // 2):
            outs = []
            for h in (2 * j, 2 * j + 1):
                l = jnp.sum(l_ref[h], axis=0, keepdims=True)
                outs.append(acc_ref[h] / l)
            o_t = jnp.concatenate(outs, axis=0)
            o_ref[:, j * LANE:(j + 1) * LANE] = o_t.T.astype(BF16)


def _dense_attn(q, k, vt, km, vmt, *, n_seq, seq, tq, q_base, meta_base):
    nq = (q.shape[0] - q_base) // (n_seq * tq)
    tk = _row_tile(seq, 512)
    nk = seq // tk
    qb0 = q_base // tq
    out_rows = n_seq * nq * tq
    return pl.pallas_call(
        _dense_attn_kernel,
        grid=(n_seq, nq, nk),
        in_specs=[
            pl.BlockSpec((tq, HEAD_SLOTS * LANE), lambda b, i, j: (qb0 + b * nq + i, 0)),
            pl.BlockSpec((tk, K_SLOTS * LANE), lambda b, i, j: (b * nk + j, 0)),
            pl.BlockSpec((V_ROWS, tk), lambda b, i, j: (0, b * nk + j)),
            pl.BlockSpec((None, LANE, K_SLOTS * LANE), lambda b, i, j: (meta_base + b, 0, 0)),
            pl.BlockSpec((None, V_ROWS, LANE), lambda b, i, j: (meta_base + b, 0, 0)),
        ],
        out_specs=pl.BlockSpec((tq, D_MODEL), lambda b, i, j: (b * nq + i, 0)),
        out_shape=jax.ShapeDtypeStruct((out_rows, D_MODEL), BF16),
        scratch_shapes=[
            pltpu.VMEM((HEAD_SLOTS, SUBLANE, tq), F32),
            pltpu.VMEM((HEAD_SLOTS, SUBLANE, tq), F32),
            pltpu.VMEM((HEAD_SLOTS, MLA_V, tq), F32),
        ],
        compiler_params=pltpu.CompilerParams(
            dimension_semantics=("arbitrary", "arbitrary", "arbitrary"),
            vmem_limit_bytes=VMEM_LIMIT),
        name="dense_attn",
    )(q, k, vt, km, vmt)


def _proj_na_kernel(h_ref, g_ref, w_ref, q_ref, k_ref, v_ref):
    a = _rms(h_ref[...], g_ref[2:3, :]).astype(BF16)
    qkv = _dot(a, w_ref[...])
    n = NA_HEADS * NA_HEAD_DIM
    q_ref[...] = (qkv[:, 0:n] * (NA_HEAD_DIM ** -0.5 * LOG2E)).astype(BF16)
    k_ref[...] = qkv[:, n:2 * n].astype(BF16)
    v_ref[...] = qkv[:, 2 * n:3 * n].astype(BF16)


def _proj_na(h, gains, w):
    rows = h.shape[0]
    tm = _row_tile(rows, 512)
    n = NA_HEADS * NA_HEAD_DIM
    return pl.pallas_call(
        _proj_na_kernel,
        grid=(rows // tm,),
        in_specs=[pl.BlockSpec((tm, D_MODEL), lambda i: (i, 0)), _const_spec(gains.shape),
                  _const_spec(w.shape)],
        out_specs=[pl.BlockSpec((tm, n), lambda i: (i, 0))] * 3,
        out_shape=[jax.ShapeDtypeStruct((rows, n), BF16)] * 3,
        compiler_params=pltpu.CompilerParams(
            dimension_semantics=("arbitrary",), vmem_limit_bytes=VMEM_LIMIT),
        name="proj_na",
    )(h, gains, w)


def _na_kernel(q_ref, k_ref, v_ref, km_ref, vm_ref, bias_ref, mb_ref, o_ref, *, rows, rows_per_step):
    step = pl.program_id(1)
    win_keys = NA_WIN_R * GRID_W
    lane = lax.broadcasted_iota(jnp.int32, (GRID_W, LANE), 1)
    first = lane < (LANE // 2)

    def one_row(rl, carry):
        r = step * rows_per_step + rl
        rs = jnp.clip(r - NA_WIN_R // 2, 0, rows - NA_WIN_R)
        off = rs - r + NA_WIN_R - 1
        q0 = pl.multiple_of(rl * GRID_W, GRID_W)
        k0 = pl.multiple_of(rs * GRID_W, GRID_W)
        for j in range(NA_HEADS // 2):
            cols = slice(j * LANE, (j + 1) * LANE)
            qp = q_ref[pl.ds(q0, GRID_W), cols]
            kw = k_ref[pl.ds(k0, win_keys), cols]
            vw = v_ref[pl.ds(k0, win_keys), cols]
            km = km_ref[:, cols]
            vm = vm_ref[:, cols]
            outs = []
            for half in range(2):
                h = 2 * j + half
                qh = jnp.where(first if half == 0 else jnp.logical_not(first), qp, jnp.zeros_like(qp))
                s = _dot_nt(qh, kw)
                b = jnp.concatenate([bias_ref[h, off + 2 * t] for t in range(NA_WIN_R // 2)], axis=1)
                s = jnp.where(b > 0.5 * NEG_INF, s + b, NEG_INF)
                sm = _dot_nt(qh, km)
                sm = jnp.where(lane < N_META, sm + mb_ref[h:h + 1, :], NEG_INF)
                m = jnp.maximum(jnp.max(s, axis=-1, keepdims=True), jnp.max(sm, axis=-1, keepdims=True))
                p = jnp.exp2(s - m)
                pm = jnp.exp2(sm - m)
                l = jnp.sum(p, axis=-1, keepdims=True) + jnp.sum(pm, axis=-1, keepdims=True)
                o = _dot(p.astype(BF16), vw) + _dot(pm.astype(BF16), vm)
                outs.append(o / l)
            o_ref[pl.ds(q0, GRID_W), cols] = jnp.where(first, outs[0], outs[1]).astype(BF16)
        return carry

    lax.fori_loop(0, rows_per_step, one_row, 0)


def _na_attn(q, k, v, km, vm, bias, mb, *, n_seq, seq, meta_base):
    rows = seq // GRID_W
    rps = 8
    nsteps = rows // rps
    n = NA_HEADS * NA_HEAD_DIM
    return pl.pallas_call(
        functools.partial(_na_kernel, rows=rows, rows_per_step=rps),
        grid=(n_seq, nsteps),
        in_specs=[
            pl.BlockSpec((rps * GRID_W, n), lambda b, i: (b * nsteps + i, 0)),
            pl.BlockSpec((seq, n), lambda b, i: (b, 0), pipeline_mode=pl.Buffered(1)),
            pl.BlockSpec((seq, n), lambda b, i: (b, 0), pipeline_mode=pl.Buffered(1)),
            pl.BlockSpec((None, LANE, n), lambda b, i: (meta_base + b, 0, 0)),
            pl.BlockSpec((None, LANE, n), lambda b, i: (meta_base + b, 0, 0)),
            _const_spec(bias.shape),
            _const_spec(mb.shape),
        ],
        out_specs=pl.BlockSpec((rps * GRID_W, n), lambda b, i: (b * nsteps + i, 0)),
        out_shape=jax.ShapeDtypeStruct((n_seq * seq, n), BF16),
        compiler_params=pltpu.CompilerParams(
            dimension_semantics=("arbitrary", "arbitrary"), vmem_limit_bytes=VMEM_LIMIT),
        name="na_attn",
    )(q, k, v, km, vm, bias, mb)


def _na_meta_kernel(q_ref, km_ref, vm_ref, mb_ref, o_ref):
    lane = lax.broadcasted_iota(jnp.int32, (N_META, LANE), 1)
    first = lane < (LANE // 2)
    for j in range(NA_HEADS // 2):
        cols = slice(j * LANE, (j + 1) * LANE)
        qp = q_ref[:, cols]
        km = km_ref[:, cols]
        vm = vm_ref[:, cols]
        outs = []
        for half in range(2):
            h = 2 * j + half
            qh = jnp.where(first if half == 0 else jnp.logical_not(first), qp, jnp.zeros_like(qp))
            sm = _dot_nt(qh, km)
            sm = jnp.where(lane < N_META, sm + mb_ref[h:h + 1, :], NEG_INF)
            m = jnp.max(sm, axis=-1, keepdims=True)
            pm = jnp.exp2(sm - m)
            l = jnp.sum(pm, axis=-1, keepdims=True)
            outs.append(_dot(pm.astype(BF16), vm) / l)
        o_ref[:, cols] = jnp.where(first, outs[0], outs[1]).astype(BF16)


def _na_meta(qm, km, vm, mb):
    n_seq = km.shape[0]
    n = NA_HEADS * NA_HEAD_DIM
    return pl.pallas_call(
        _na_meta_kernel,
        grid=(n_seq,),
        in_specs=[
            pl.BlockSpec((N_META, n), lambda b: (b, 0)),
            pl.BlockSpec((None, LANE, n), lambda b: (b, 0, 0)),
            pl.BlockSpec((None, LANE, n), lambda b: (b, 0, 0)),
            _const_spec(mb.shape),
        ],
        out_specs=pl.BlockSpec((N_META, n), lambda b: (b, 0)),
        out_shape=jax.ShapeDtypeStruct((n_seq * N_META, n), BF16),
        compiler_params=pltpu.CompilerParams(dimension_semantics=("arbitrary",)),
        name="na_meta",
    )(qm, km, vm, mb)


def _take_cols(w, idx):
    idx = np.asarray(idx)
    cols = jnp.take(w, jnp.asarray(np.maximum(idx, 0)), axis=1)
    return jnp.where(jnp.asarray(idx >= 0)[None, :], cols, 0.0)


def _swap_halves(n):
    half = n // 2
    return np.concatenate([np.arange(half, n), np.arange(0, half)])


def _dense_weights(w_in, q_norm, w_uq, kv_norm, w_ukv, gq_norm, gk_norm, w_out):
    pad = lambda k: -np.ones(k, np.int64)
    o_kr = MLA_Q_LORA + MLA_KV_LORA
    o_gq = o_kr + MLA_ROPE
    o_gk = o_gq + GQA_HEADS * GQA_HEAD_DIM
    o_gv = o_gk + GQA_KV_HEADS * GQA_HEAD_DIM
    axial = np.concatenate([_swap_halves(GQA_HEAD_DIM // 2),
                            GQA_HEAD_DIM // 2 + _swap_halves(GQA_HEAD_DIM // 2)])
    idx = [np.arange(0, o_kr)]
    idx += [pad(MLA_NOPE), o_kr + np.arange(MLA_ROPE), pad(LANE - MLA_NOPE - MLA_ROPE)]
    idx += [pad(MLA_NOPE), o_kr + _swap_halves(MLA_ROPE), pad(LANE - MLA_NOPE - MLA_ROPE)]
    for h in range(GQA_HEADS):
        idx += [o_gq + h * GQA_HEAD_DIM + np.arange(GQA_HEAD_DIM), pad(LANE - GQA_HEAD_DIM)]
    for h in range(GQA_HEADS):
        idx += [o_gq + h * GQA_HEAD_DIM + axial, pad(LANE - GQA_HEAD_DIM)]
    for h in range(GQA_KV_HEADS):
        idx += [o_gk + h * GQA_HEAD_DIM + np.arange(GQA_HEAD_DIM), pad(LANE - GQA_HEAD_DIM)]
    for h in range(GQA_KV_HEADS):
        idx += [o_gk + h * GQA_HEAD_DIM + axial, pad(LANE - GQA_HEAD_DIM)]
    idx += [o_gv + np.arange(GQA_KV_HEADS * GQA_HEAD_DIM)]
    idx = np.concatenate(idx)
    assert idx.shape[0] == _C_END
    w_in2 = _take_cols(w_in, idx).astype(BF16)

    hd = MLA_NOPE + MLA_ROPE
    ia, ib = [], []
    for h in range(MLA_HEADS):
        ia += [h * hd + np.arange(hd), pad(LANE - hd)]
        ib += [pad(MLA_NOPE), h * hd + MLA_NOPE + _swap_halves(MLA_ROPE), pad(LANE - hd)]
    w_uq2 = _take_cols(w_uq, np.concatenate(ia + ib)).astype(BF16)

    kvd = MLA_NOPE + MLA_V
    ik, iv = [], []
    for h in range(MLA_HEADS):
        ik += [h * kvd + np.arange(MLA_NOPE), pad(LANE - MLA_NOPE)]
        iv += [h * kvd + MLA_NOPE + np.arange(MLA_V)]
    w_ukv2 = _take_cols(w_ukv, np.concatenate(ik + iv)).astype(BF16)

    def gain_pair(g):
        ga = jnp.concatenate([g, jnp.zeros((LANE - GQA_HEAD_DIM,), F32)])[None, :]
        gb = jnp.concatenate([g[jnp.asarray(axial)], jnp.zeros((LANE - GQA_HEAD_DIM,), F32)])[None, :]
        return ga, gb

    gq_a, gq_b = gain_pair(gq_norm)
    gk_a, gk_b = gain_pair(gk_norm)

    w_out2 = w_out.astype(BF16)

    return dict(w_in=w_in2, q_norm=q_norm[None, :], w_uq=w_uq2, kv_norm=kv_norm[None, :],
                w_ukv=w_ukv2, gq_a=gq_a, gq_b=gq_b, gk_a=gk_a, gk_b=gk_b), w_out2


def _rope_tables(pos, row, col):
    half = MLA_ROPE // 2
    inv = 1.0 / (ROPE_THETA ** (jnp.arange(half, dtype=F32) / half))
    n = pos.shape[0]

    def cs(p):
        ang = p.astype(F32)[:, None] * inv[None, :]
        return jnp.cos(ang), jnp.sin(ang)

    c1, s1 = cs(pos)
    cr, sr = cs(row)
    cc, sc = cs(col)
    z = lambda k: jnp.zeros((n, k), F32)
    qs = (MLA_NOPE + MLA_ROPE) ** -0.5 * LOG2E
    tail = LANE - MLA_NOPE - MLA_ROPE
    cos_q = jnp.concatenate([jnp.full((n, MLA_NOPE), qs, F32), qs * c1, qs * c1, z(tail)], axis=1)
    sin_q = jnp.concatenate([z(MLA_NOPE), -qs * s1, qs * s1, z(tail)], axis=1)
    cos_k = jnp.concatenate([z(MLA_NOPE), c1, c1, z(tail)], axis=1)
    sin_k = jnp.concatenate([z(MLA_NOPE), -s1, s1, z(tail)], axis=1)
    cos_g = jnp.concatenate([cr, cr, cc, cc, z(LANE - GQA_HEAD_DIM)], axis=1)
    sin_g = jnp.concatenate([-sr, sr, -sc, sc, z(LANE - GQA_HEAD_DIM)], axis=1)
    return jnp.concatenate([cos_q, sin_q, cos_k, sin_k, cos_g, sin_g], axis=1)


def _na_bias_tables(rpb, meta_bias):
    c_idx = np.arange(GRID_W)
    c_start = np.clip(c_idx - NA_WIN_C // 2, 0, GRID_W - NA_WIN_C)
    col_mask = (c_idx[None, :] >= c_start[:, None]) & (c_idx[None, :] < c_start[:, None] + NA_WIN_C)
    col_off = np.clip(c_idx[None, :] - c_idx[:, None] + NA_WIN_C - 1, 0, 2 * NA_WIN_C - 2)
    t = rpb[:, :, jnp.asarray(col_off)] * LOG2E
    t = jnp.where(jnp.asarray(col_mask)[None, None], t, NEG_INF)
    bias = jnp.concatenate([t[:, :-1], t[:, 1:]], axis=-1)
    mb = jnp.pad(meta_bias * LOG2E, ((0, 0), (0, LANE - N_META)))
    return bias, mb


def _pad_meta(x, n_seq):
    c = x.shape[1]
    return jnp.pad(x.reshape(n_seq, N_META, c), ((0, 0), (0, LANE - N_META), (0, 0)))


def kernel(x_prompt, x_sample, meta, norm_gains, ffn1_w_gate, ffn1_w_up, ffn1_w_down, ffn2_w_gate, ffn2_w_up, ffn2_w_down, attn_w_in, mla_q_norm, mla_w_uq, mla_kv_norm, mla_w_ukv, gqa_q_norm, gqa_k_norm, attn_w_out, na_w_qkv, na_rpb, na_meta_bias, na_w_out):
    bp, sp, _ = x_prompt.shape
    bs, ss, _ = x_sample.shape
    n_seq = bp + bs
    depth = norm_gains.shape[0]
    groups = [(bp, sp, 0), (bs, ss, bp)]

    n_meta = n_seq * N_META
    meta_rows = -(-n_meta // LANE) * LANE
    pad_rows = lambda x: jnp.pad(x, ((0, meta_rows - x.shape[0]), (0, 0)))
    h_tok = [x_prompt.reshape(bp * sp, D_MODEL), x_sample.reshape(bs * ss, D_MODEL)]
    h_meta = pad_rows(jnp.tile(meta.astype(F32), (n_seq, 1)))

    smax = max(sp, ss)
    t = jnp.arange(smax)
    tab_tok = _rope_tables(t + N_META, t // GRID_W, t % GRID_W)
    mi = jnp.arange(meta_rows) % N_META
    tab_meta = _rope_tables(mi, jnp.full_like(mi, -1), mi)

    for i in range(depth):
        gains = jnp.pad(norm_gains[i], ((0, 2), (0, 0)))
        w1 = (ffn1_w_gate[i].astype(BF16), ffn1_w_up[i].astype(BF16), ffn1_w_down[i].astype(BF16))
        w2 = (ffn2_w_gate[i].astype(BF16), ffn2_w_up[i].astype(BF16), ffn2_w_down[i].astype(BF16))
        j = i // 2
        h_tok = [_ffn1(h, gains, *w1) for h in h_tok]
        h_meta = _ffn1(h_meta, gains, *w1)
        if i % 2 == 0:
            w, w_out = _dense_weights(attn_w_in[j], mla_q_norm[j], mla_w_uq[j], mla_kv_norm[j],
                                      mla_w_ukv[j], gqa_q_norm[j], gqa_k_norm[j], attn_w_out[j])
            qkv_tok = [_proj_dense(h, gains, w, tab_tok, s) for h, (_, s, _) in zip(h_tok, groups)]
            qm, km, vmt = _proj_dense(h_meta, gains, w, tab_meta, 0)
            kmp = _pad_meta(km[:n_meta], n_seq)
            vmtp = vmt[:, :n_meta].reshape(V_ROWS, n_seq, N_META).transpose(1, 0, 2)
            vmtp = jnp.pad(vmtp, ((0, 0), (0, 0), (0, LANE - N_META)))
            qmp = _pad_meta(qm[:n_meta], n_seq).reshape(n_seq * LANE, HEAD_SLOTS * LANE)
            o_tok, o_meta = [], []
            for (q, k, vt), (nb, s, b0) in zip(qkv_tok, groups):
                o_tok.append(_dense_attn(q, k, vt, kmp, vmtp, n_seq=nb, seq=s,
                                         tq=_row_tile(s, 512), q_base=0, meta_base=b0))
                om = _dense_attn(qmp, k, vt, kmp, vmtp, n_seq=nb, seq=s,
                                 tq=LANE, q_base=b0 * LANE, meta_base=b0)
                o_meta.append(om.reshape(nb, LANE, D_MODEL)[:, :N_META].reshape(nb * N_META, D_MODEL))
            o_meta = pad_rows(jnp.concatenate(o_meta, axis=0))
        else:
            w_qkv = na_w_qkv[j].astype(BF16)
            w_out = na_w_out[j].astype(BF16)
            bias, mb = _na_bias_tables(na_rpb[j], na_meta_bias[j])
            qkv_tok = [_proj_na(h, gains, w_qkv) for h in h_tok]
            qm, km, vm = _proj_na(h_meta, gains, w_qkv)
            kmp, vmp = _pad_meta(km[:n_meta], n_seq), _pad_meta(vm[:n_meta], n_seq)
            o_tok = [_na_attn(q, k, v, kmp, vmp, bias, mb, n_seq=nb, seq=s, meta_base=b0)
                     for (q, k, v), (nb, s, b0) in zip(qkv_tok, groups)]
            o_meta = pad_rows(_na_meta(qm[:n_meta], kmp, vmp, mb))
        h_tok = [_mix_ffn2(h, o, w_out, gains, *w2) for h, o in zip(h_tok, o_tok)]
        h_meta = _mix_ffn2(h_meta, o_meta, w_out, gains, *w2)

    return (h_tok[0].reshape(bp, sp, D_MODEL), h_tok[1].reshape(bs, ss, D_MODEL))
```

```python
import functools
import math

import jax
import jax.numpy as jnp
import numpy as np
from jax import lax
from jax.experimental import pallas as pl
from jax.experimental.pallas import tpu as pltpu

F32 = jnp.float32
BF16 = jnp.bfloat16

D_MODEL = 1024
N_META = 16
GRID_W = 64
D_FF = 2816
EPS = 1e-6
NEG_INF = -1e30
LOG2E = math.log2(math.e)

MLA_HEADS = 8
MLA_Q_LORA = 256
MLA_KV_LORA = 128
MLA_NOPE = 64
MLA_ROPE = 32
MLA_V = 64
GQA_HEADS = 8
GQA_KV_HEADS = 2
GQA_HEAD_DIM = 64
ROPE_THETA = 10000.0
NA_HEADS = 16
NA_HEAD_DIM = 64
NA_WIN_R = 8
NA_WIN_C = 16

LANE = 128
HEAD_SLOTS = MLA_HEADS + GQA_HEADS
K_SLOTS = MLA_HEADS + GQA_KV_HEADS
V_ROWS = (MLA_HEADS + GQA_KV_HEADS) * MLA_V
SUBLANE = 8
VMEM_LIMIT = 56 * 1024 * 1024

_C_CQ = 0
_C_CKV = _C_CQ + MLA_Q_LORA
_C_KRA = _C_CKV + MLA_KV_LORA
_C_KRB = _C_KRA + LANE
_C_GQA = _C_KRB + LANE
_C_GQB = _C_GQA + GQA_HEADS * LANE
_C_GKA = _C_GQB + GQA_HEADS * LANE
_C_GKB = _C_GKA + GQA_KV_HEADS * LANE
_C_GV = _C_GKB + GQA_KV_HEADS * LANE
_C_END = _C_GV + LANE


def _const_spec(shape):
    nd = len(shape)
    return pl.BlockSpec(shape, lambda *_: (0,) * nd, pipeline_mode=pl.Buffered(1))


def _rms(x, g):
    ms = jnp.mean(x * x, axis=-1, keepdims=True)
    return x * lax.rsqrt(ms + EPS) * g


def _dot(a, b):
    return jnp.dot(a, b, preferred_element_type=F32)


def _dot_nt(a, b):
    return lax.dot_general(a, b, (((1,), (1,)), ((), ())), preferred_element_type=F32)


def _row_tile(rows, want):
    t = min(rows, want)
    while rows % t:
        t //= 2
    return t


def _ffn_body(h, g_ref, pre, post, wg_ref, wu_ref, wd_ref):
    xn = _rms(h, g_ref[pre:pre + 1, :]).astype(BF16)
    gate = _dot(xn, wg_ref[...])
    up = _dot(xn, wu_ref[...])
    act = (gate * jax.nn.sigmoid(gate) * up).astype(BF16)
    y = _dot(act, wd_ref[...])
    return h + 0.5 * _rms(y, g_ref[post:post + 1, :])


def _ffn1_kernel(h_ref, g_ref, wg_ref, wu_ref, wd_ref, out_ref):
    out_ref[...] = _ffn_body(h_ref[...], g_ref, 0, 1, wg_ref, wu_ref, wd_ref)


def _mix_ffn2_kernel(h_ref, o_ref, wo_ref, g_ref, wg_ref, wu_ref, wd_ref, out_ref):
    mixed = _dot(o_ref[...], wo_ref[...])
    h = h_ref[...] + _rms(mixed, g_ref[3:4, :])
    out_ref[...] = _ffn_body(h, g_ref, 4, 5, wg_ref, wu_ref, wd_ref)


def _ffn1(h, gains, wg, wu, wd):
    rows = h.shape[0]
    tm = _row_tile(rows, 512)
    return pl.pallas_call(
        _ffn1_kernel,
        grid=(rows // tm,),
        in_specs=[
            pl.BlockSpec((tm, D_MODEL), lambda i: (i, 0)),
            _const_spec(gains.shape),
            _const_spec(wg.shape), _const_spec(wu.shape), _const_spec(wd.shape),
        ],
        out_specs=pl.BlockSpec((tm, D_MODEL), lambda i: (i, 0)),
        out_shape=jax.ShapeDtypeStruct((rows, D_MODEL), F32),
        compiler_params=pltpu.CompilerParams(
            dimension_semantics=("arbitrary",), vmem_limit_bytes=VMEM_LIMIT),
        name="ffn1",
    )(h, gains, wg, wu, wd)


def _mix_ffn2(h, o, wo, gains, wg, wu, wd):
    rows = h.shape[0]
    tm = _row_tile(rows, 512)
    return pl.pallas_call(
        _mix_ffn2_kernel,
        grid=(rows // tm,),
        in_specs=[
            pl.BlockSpec((tm, D_MODEL), lambda i: (i, 0)),
            pl.BlockSpec((tm, o.shape[1]), lambda i: (i, 0)),
            _const_spec(wo.shape),
            _const_spec(gains.shape),
            _const_spec(wg.shape), _const_spec(wu.shape), _const_spec(wd.shape),
        ],
        out_specs=pl.BlockSpec((tm, D_MODEL), lambda i: (i, 0)),
        out_shape=jax.ShapeDtypeStruct((rows, D_MODEL), F32),
        compiler_params=pltpu.CompilerParams(
            dimension_semantics=("arbitrary",), vmem_limit_bytes=VMEM_LIMIT),
        name="mix_ffn2",
    )(h, o, wo, gains, wg, wu, wd)


def _proj_dense_kernel(h_ref, g_ref, win_ref, qn_ref, wuq_ref, kvn_ref, wukv_ref,
                       gqa_ref, gqb_ref, gka_ref, gkb_ref, tab_ref, q_ref, k_ref, vt_ref):
    a = _rms(h_ref[...], g_ref[2:3, :]).astype(BF16)
    proj = _dot(a, win_ref[...])
    tab = tab_ref[...]
    cos_q, sin_q = tab[:, 0:LANE], tab[:, LANE:2 * LANE]
    cos_k, sin_k = tab[:, 2 * LANE:3 * LANE], tab[:, 3 * LANE:4 * LANE]
    cos_g, sin_g = tab[:, 4 * LANE:5 * LANE], tab[:, 5 * LANE:6 * LANE]

    cqn = _rms(proj[:, _C_CQ:_C_CQ + MLA_Q_LORA], qn_ref[...]).astype(BF16)
    qab = _dot(cqn, wuq_ref[...])
    nq = MLA_HEADS * LANE
    for h in range(MLA_HEADS):
        qa = qab[:, h * LANE:(h + 1) * LANE]
        qb = qab[:, nq + h * LANE:nq + (h + 1) * LANE]
        q_ref[:, h * LANE:(h + 1) * LANE] = (qa * cos_q + qb * sin_q).astype(BF16)

    ckvn = _rms(proj[:, _C_CKV:_C_CKV + MLA_KV_LORA], kvn_ref[...]).astype(BF16)
    kv = _dot(ckvn, wukv_ref[...])
    k_rope = (proj[:, _C_KRA:_C_KRA + LANE] * cos_k + proj[:, _C_KRB:_C_KRB + LANE] * sin_k)
    for h in range(MLA_HEADS):
        k_ref[:, h * LANE:(h + 1) * LANE] = (kv[:, h * LANE:(h + 1) * LANE] + k_rope).astype(BF16)
    nv = MLA_HEADS * MLA_V

    gq_scale = GQA_HEAD_DIM ** -0.5 * LOG2E
    cq_g = cos_g * (gqa_ref[...] * gq_scale)
    sq_g = sin_g * (gqb_ref[...] * gq_scale)
    for h in range(GQA_HEADS):
        xa = proj[:, _C_GQA + h * LANE:_C_GQA + (h + 1) * LANE]
        xb = proj[:, _C_GQB + h * LANE:_C_GQB + (h + 1) * LANE]
        r = lax.rsqrt(jnp.sum(xa * xa, axis=-1, keepdims=True) * (1.0 / GQA_HEAD_DIM) + EPS)
        q_ref[:, nq + h * LANE:nq + (h + 1) * LANE] = ((xa * cq_g + xb * sq_g) * r).astype(BF16)
    ck_g = cos_g * gka_ref[...]
    sk_g = sin_g * gkb_ref[...]
    for h in range(GQA_KV_HEADS):
        xa = proj[:, _C_GKA + h * LANE:_C_GKA + (h + 1) * LANE]
        xb = proj[:, _C_GKB + h * LANE:_C_GKB + (h + 1) * LANE]
        r = lax.rsqrt(jnp.sum(xa * xa, axis=-1, keepdims=True) * (1.0 / GQA_HEAD_DIM) + EPS)
        k_ref[:, nq + h * LANE:nq + (h + 1) * LANE] = ((xa * ck_g + xb * sk_g) * r).astype(BF16)
    v = jnp.concatenate([kv[:, nq:nq + nv], proj[:, _C_GV:_C_GV + LANE]], axis=1)
    vt_ref[0] = v.T.astype(BF16)


def _proj_dense(h, gains, w, tab, seq):
    rows = h.shape[0]
    tm = _row_tile(seq if seq else rows, 512)
    nblk = (seq // tm) if seq else 1
    ntab = tab.shape[1]
    consts = [w["w_in"], w["q_norm"], w["w_uq"], w["kv_norm"], w["w_ukv"],
              w["gq_a"], w["gq_b"], w["gk_a"], w["gk_b"]]
    return pl.pallas_call(
        _proj_dense_kernel,
        grid=(rows // tm,),
        in_specs=[pl.BlockSpec((tm, D_MODEL), lambda i: (i, 0)), _const_spec(gains.shape)]
        + [_const_spec(c.shape) for c in consts]
        + [pl.BlockSpec((tm, ntab), lambda i: (i % nblk, 0))],
        out_specs=[
            pl.BlockSpec((tm, HEAD_SLOTS * LANE), lambda i: (i, 0)),
            pl.BlockSpec((tm, K_SLOTS * LANE), lambda i: (i, 0)),
            pl.BlockSpec((1, V_ROWS, tm), lambda i: (i, 0, 0)),
        ],
        out_shape=[
            jax.ShapeDtypeStruct((rows, HEAD_SLOTS * LANE), BF16),
            jax.ShapeDtypeStruct((rows, K_SLOTS * LANE), BF16),
            jax.ShapeDtypeStruct((rows // tm, V_ROWS, tm), BF16),
        ],
        compiler_params=pltpu.CompilerParams(
            dimension_semantics=("arbitrary",), vmem_limit_bytes=VMEM_LIMIT),
        name="proj_dense",
    )(h, gains, *consts, tab)


def _head_slots(h):
    if h < MLA_HEADS:
        return h, h
    kvh = (h - MLA_HEADS) // (GQA_HEADS // GQA_KV_HEADS)
    return MLA_HEADS + kvh, MLA_HEADS + kvh


def _sublane_bcast_max(x):
    return jnp.broadcast_to(jnp.max(x, axis=0, keepdims=True), x.shape)


def _dense_attn_kernel(q_ref, k_ref, vt_ref, km_ref, vmt_ref, o_ref, m_ref, l_ref, acc_ref, s_ref):
    kv = pl.program_id(2)
    tq = q_ref.shape[0]
    n_sub, _, tk = vt_ref.shape
    hd = MLA_V

    @pl.when(kv == 0)
    def _():
        zeros = jnp.zeros((LANE - N_META, tq), F32)
        for h in range(HEAD_SLOTS):
            ks, vh = _head_slots(h)
            q = q_ref[:, h * LANE:(h + 1) * LANE]
            s = _dot_nt(km_ref[0:N_META, ks * LANE:(ks + 1) * LANE], q)
            s3 = s.reshape(N_META // SUBLANE, SUBLANE, tq)
            m = _sublane_bcast_max(jnp.max(s3, axis=0))
            p3 = jnp.exp2(s3 - m[None])
            m_ref[h] = m
            l_ref[h] = jnp.sum(p3, axis=0)
            p = jnp.concatenate([p3.reshape(N_META, tq), zeros], axis=0).astype(BF16)
            acc_ref[h] = _dot(vmt_ref[vh * hd:(vh + 1) * hd, :], p)

    def scores(t, h, slot):
        ks, _ = _head_slots(h)
        k0 = pl.multiple_of(t * tk, tk)
        k = k_ref[pl.ds(k0, tk), ks * LANE:(ks + 1) * LANE]
        s_ref[slot] = _dot_nt(k, q_ref[:, h * LANE:(h + 1) * LANE])

    scores(0, 0, 0)

    def sub_tile(t, carry):
        for h in range(HEAD_SLOTS):
            slot = h % 2
            if h + 1 < HEAD_SLOTS:
                scores(t, h + 1, 1 - slot)
            else:
                scores(jnp.minimum(t + 1, n_sub - 1), 0, 1 - slot)
            _, vh = _head_slots(h)
            s3 = s_ref[slot].reshape(tk // SUBLANE, SUBLANE, tq)
            m_prev = m_ref[h]
            m_new = jnp.maximum(m_prev, _sublane_bcast_max(jnp.max(s3, axis=0)))
            alpha = jnp.exp2(m_prev - m_new)
            p3 = jnp.exp2(s3 - m_new[None])
            l_ref[h] = alpha * l_ref[h] + jnp.sum(p3, axis=0)
            p = p3.reshape(tk, tq).astype(BF16)
            pv = _dot(vt_ref[t, vh * hd:(vh + 1) * hd, :], p)
            acc = acc_ref[h].reshape(hd // SUBLANE, SUBLANE, tq) * alpha[None]
            acc_ref[h] = acc.reshape(hd, tq) + pv
            m_ref[h] = m_new
        return carry

    lax.fori_loop(0, n_sub, sub_tile, 0)

    @pl.when(kv == pl.num_programs(2) - 1)
    def _():
        for j in range(HEAD_SLOTS // 2):
            outs = []
            for h in (2 * j, 2 * j + 1):
                l = jnp.sum(l_ref[h], axis=0, keepdims=True)
                outs.append(acc_ref[h] / l)
            o_t = jnp.concatenate(outs, axis=0)
            o_ref[:, j * LANE:(j + 1) * LANE] = o_t.T.astype(BF16)


def _dense_attn(q, k, vt, km, vmt, *, n_seq, seq, tq, q_base, meta_base):
    nq = (q.shape[0] - q_base) // (n_seq * tq)
    tk = vt.shape[2]
    n_sub = _row_tile(seq // tk, 4)
    nk = seq // (tk * n_sub)
    qb0 = q_base // tq
    out_rows = n_seq * nq * tq
    return pl.pallas_call(
        _dense_attn_kernel,
        grid=(n_seq, nq, nk),
        in_specs=[
            pl.BlockSpec((tq, HEAD_SLOTS * LANE), lambda b, i, j: (qb0 + b * nq + i, 0)),
            pl.BlockSpec((n_sub * tk, K_SLOTS * LANE), lambda b, i, j: (b * nk + j, 0)),
            pl.BlockSpec((n_sub, V_ROWS, tk), lambda b, i, j: (b * nk + j, 0, 0)),
            pl.BlockSpec((None, LANE, K_SLOTS * LANE), lambda b, i, j: (meta_base + b, 0, 0)),
            pl.BlockSpec((None, V_ROWS, LANE), lambda b, i, j: (meta_base + b, 0, 0)),
        ],
        out_specs=pl.BlockSpec((tq, D_MODEL), lambda b, i, j: (b * nq + i, 0)),
        out_shape=jax.ShapeDtypeStruct((out_rows, D_MODEL), BF16),
        scratch_shapes=[
            pltpu.VMEM((HEAD_SLOTS, SUBLANE, tq), F32),
            pltpu.VMEM((HEAD_SLOTS, SUBLANE, tq), F32),
            pltpu.VMEM((HEAD_SLOTS, MLA_V, tq), F32),
            pltpu.VMEM((2, tk, tq), F32),
        ],
        compiler_params=pltpu.CompilerParams(
            dimension_semantics=("arbitrary", "arbitrary", "arbitrary"),
            vmem_limit_bytes=VMEM_LIMIT),
        name="dense_attn",
    )(q, k, vt, km, vmt)


def _proj_na_kernel(h_ref, g_ref, w_ref, q_ref, k_ref, v_ref, *, transpose_v):
    a = _rms(h_ref[...], g_ref[2:3, :]).astype(BF16)
    qkv = _dot(a, w_ref[...])
    n = NA_HEADS * NA_HEAD_DIM
    q_ref[...] = (qkv[:, 0:n] * (NA_HEAD_DIM ** -0.5 * LOG2E)).astype(BF16)
    k_ref[...] = qkv[:, n:2 * n].astype(BF16)
    v = qkv[:, 2 * n:3 * n]
    if transpose_v:
        vt = v.T.astype(BF16)
        for t in range(v_ref.shape[0]):
            v_ref[t] = vt[:, t * LANE:(t + 1) * LANE]
    else:
        v_ref[...] = v.astype(BF16)


def _proj_na(h, gains, w, transpose_v):
    rows = h.shape[0]
    tm = _row_tile(rows, 512)
    n = NA_HEADS * NA_HEAD_DIM
    if transpose_v:
        v_spec = pl.BlockSpec((tm // LANE, n, LANE), lambda i: (i, 0, 0))
        v_shape = jax.ShapeDtypeStruct((rows // LANE, n, LANE), BF16)
    else:
        v_spec = pl.BlockSpec((tm, n), lambda i: (i, 0))
        v_shape = jax.ShapeDtypeStruct((rows, n), BF16)
    return pl.pallas_call(
        functools.partial(_proj_na_kernel, transpose_v=transpose_v),
        grid=(rows // tm,),
        in_specs=[pl.BlockSpec((tm, D_MODEL), lambda i: (i, 0)), _const_spec(gains.shape),
                  _const_spec(w.shape)],
        out_specs=[pl.BlockSpec((tm, n), lambda i: (i, 0))] * 2 + [v_spec],
        out_shape=[jax.ShapeDtypeStruct((rows, n), BF16)] * 2 + [v_shape],
        compiler_params=pltpu.CompilerParams(
            dimension_semantics=("arbitrary",), vmem_limit_bytes=VMEM_LIMIT),
        name="proj_na",
    )(h, gains, w)


NA_SPAN_R = NA_WIN_R + 2
NA_MASKED = 2 * NA_WIN_R - 1


def _na_kernel(q_ref, k_ref, vt_ref, km_ref, vmt_ref, bias_ref, mb_ref, o_ref, s_ref,
               *, rows, rows_per_step):
    step = pl.program_id(1)
    n_pairs = rows_per_step // 2
    n_hp = NA_HEADS // 2
    span = NA_SPAN_R * GRID_W
    lane = lax.broadcasted_iota(jnp.int32, (GRID_W, LANE), 1)
    first = lane < (LANE // 2)
    zeros_m = jnp.zeros((LANE - N_META, 2 * LANE), F32)

    def geometry(rp):
        ra = step * rows_per_step + 2 * rp
        rs = [jnp.clip(ra + x - NA_WIN_R // 2, 0, rows - NA_WIN_R) for x in range(2)]
        ws = jnp.minimum((rs[0] // 2) * 2, rows - NA_SPAN_R)
        return ra, rs, ws

    def scores(rp, hp, slot):
        _, _, ws = geometry(rp)
        cols = slice(hp * LANE, (hp + 1) * LANE)
        parts = []
        for x in range(2):
            q0 = pl.multiple_of((2 * rp + x) * GRID_W, GRID_W)
            qx = q_ref[pl.ds(q0, GRID_W), cols]
            parts += [jnp.where(first, qx, jnp.zeros_like(qx)), jnp.where(first, jnp.zeros_like(qx), qx)]
        qblk = jnp.concatenate(parts, axis=0)
        k0 = pl.multiple_of(ws * GRID_W, 2 * GRID_W)
        s_ref[slot, 0:span, :] = _dot_nt(k_ref[pl.ds(k0, span), cols], qblk)
        s_ref[slot, span:span + N_META, :] = _dot_nt(km_ref[:, cols], qblk)

    scores(0, 0, 0)

    def row_pair(rp, carry):
        ra, rs, ws = geometry(rp)
        idx = []
        for jj in range(NA_SPAN_R):
            kr = ws + jj
            idx.append([jnp.where((kr >= rs[x]) & (kr < rs[x] + NA_WIN_R),
                                  kr - (ra + x) + NA_WIN_R - 1, NA_MASKED) for x in range(2)])
        t0 = ws // 2
        for hp in range(n_hp):
            slot = hp % 2
            if hp + 1 < n_hp:
                scores(rp, hp + 1, 1 - slot)
            else:
                scores(jnp.minimum(rp + 1, n_pairs - 1), 0, 1 - slot)
            cols = slice(hp * LANE, (hp + 1) * LANE)
            b = jnp.concatenate(
                [jnp.concatenate([bias_ref[hp, idx[jj][0]], bias_ref[hp, idx[jj][1]]], axis=1)
                 for jj in range(NA_SPAN_R)], axis=0)
            s = s_ref[slot, 0:span, :]
            s = jnp.where(b > 0.5 * NEG_INF, s + b, NEG_INF)
            mb = mb_ref[hp]
            sm = s_ref[slot, span:span + N_META, :] + jnp.concatenate([mb, mb], axis=1)
            s3 = s.reshape(span // SUBLANE, SUBLANE, 2 * LANE)
            sm3 = sm.reshape(N_META // SUBLANE, SUBLANE, 2 * LANE)
            m = _sublane_bcast_max(jnp.maximum(jnp.max(s3, axis=0), jnp.max(sm3, axis=0)))
            p3 = jnp.exp2(s3 - m[None])
            pm3 = jnp.exp2(sm3 - m[None])
            l = jnp.sum(jnp.sum(p3, axis=0) + jnp.sum(pm3, axis=0), axis=0, keepdims=True)
            p = p3.reshape(span, 2 * LANE).astype(BF16)
            pm = jnp.concatenate([pm3.reshape(N_META, 2 * LANE), zeros_m], axis=0).astype(BF16)
            o_t = _dot(vmt_ref[cols, :], pm)
            for t in range(span // LANE):
                o_t = o_t + _dot(vt_ref[t0 + t, cols, :], p[t * LANE:(t + 1) * LANE, :])
            o_t = o_t / l
            for x in range(2):
                blk = o_t[:, x * LANE:(x + 1) * LANE].T
                q0 = pl.multiple_of((2 * rp + x) * GRID_W, GRID_W)
                o_ref[pl.ds(q0, GRID_W), cols] = jnp.where(
                    first, blk[0:GRID_W], blk[GRID_W:2 * GRID_W]).astype(BF16)
        return carry

    lax.fori_loop(0, n_pairs, row_pair, 0)


def _na_attn(q, k, vt, km, vmt, bias, mb, *, n_seq, seq, meta_base):
    rows = seq // GRID_W
    assert rows >= NA_SPAN_R and rows % 2 == 0
    rps = 8
    nsteps = rows // rps
    n = NA_HEADS * NA_HEAD_DIM
    span = NA_SPAN_R * GRID_W
    return pl.pallas_call(
        functools.partial(_na_kernel, rows=rows, rows_per_step=rps),
        grid=(n_seq, nsteps),
        in_specs=[
            pl.BlockSpec((rps * GRID_W, n), lambda b, i: (b * nsteps + i, 0)),
            pl.BlockSpec((seq, n), lambda b, i: (b, 0), pipeline_mode=pl.Buffered(1)),
            pl.BlockSpec((seq // LANE, n, LANE), lambda b, i: (b, 0, 0), pipeline_mode=pl.Buffered(1)),
            pl.BlockSpec((None, N_META, n), lambda b, i: (meta_base + b, 0, 0)),
            pl.BlockSpec((None, n, LANE), lambda b, i: (meta_base + b, 0, 0)),
            _const_spec(bias.shape),
            _const_spec(mb.shape),
        ],
        out_specs=pl.BlockSpec((rps * GRID_W, n), lambda b, i: (b * nsteps + i, 0)),
        out_shape=jax.ShapeDtypeStruct((n_seq * seq, n), BF16),
        scratch_shapes=[pltpu.VMEM((2, span + N_META, 2 * LANE), F32)],
        compiler_params=pltpu.CompilerParams(
            dimension_semantics=("arbitrary", "arbitrary"), vmem_limit_bytes=VMEM_LIMIT),
        name="na_attn",
    )(q, k, vt, km, vmt, bias, mb)


def _na_meta_kernel(q_ref, km_ref, vm_ref, mb_ref, o_ref):
    lane = lax.broadcasted_iota(jnp.int32, (N_META, LANE), 1)
    first = lane < (LANE // 2)
    for j in range(NA_HEADS // 2):
        cols = slice(j * LANE, (j + 1) * LANE)
        qp = q_ref[:, cols]
        km = km_ref[:, cols]
        vm = vm_ref[:, cols]
        outs = []
        for half in range(2):
            h = 2 * j + half
            qh = jnp.where(first if half == 0 else jnp.logical_not(first), qp, jnp.zeros_like(qp))
            sm = _dot_nt(qh, km)
            sm = jnp.where(lane < N_META, sm + mb_ref[h:h + 1, :], NEG_INF)
            m = jnp.max(sm, axis=-1, keepdims=True)
            pm = jnp.exp2(sm - m)
            l = jnp.sum(pm, axis=-1, keepdims=True)
            outs.append(_dot(pm.astype(BF16), vm) / l)
        o_ref[:, cols] = jnp.where(first, outs[0], outs[1]).astype(BF16)


def _na_meta(qm, km, vm, mb):
    n_seq = km.shape[0]
    n = NA_HEADS * NA_HEAD_DIM
    return pl.pallas_call(
        _na_meta_kernel,
        grid=(n_seq,),
        in_specs=[
            pl.BlockSpec((N_META, n), lambda b: (b, 0)),
            pl.BlockSpec((None, LANE, n), lambda b: (b, 0, 0)),
            pl.BlockSpec((None, LANE, n), lambda b: (b, 0, 0)),
            _const_spec(mb.shape),
        ],
        out_specs=pl.BlockSpec((N_META, n), lambda b: (b, 0)),
        out_shape=jax.ShapeDtypeStruct((n_seq * N_META, n), BF16),
        compiler_params=pltpu.CompilerParams(dimension_semantics=("arbitrary",)),
        name="na_meta",
    )(qm, km, vm, mb)


def _take_cols(w, idx):
    idx = np.asarray(idx)
    cols = jnp.take(w, jnp.asarray(np.maximum(idx, 0)), axis=1)
    return jnp.where(jnp.asarray(idx >= 0)[None, :], cols, 0.0)


def _swap_halves(n):
    half = n // 2
    return np.concatenate([np.arange(half, n), np.arange(0, half)])


def _dense_weights(w_in, q_norm, w_uq, kv_norm, w_ukv, gq_norm, gk_norm, w_out):
    pad = lambda k: -np.ones(k, np.int64)
    o_kr = MLA_Q_LORA + MLA_KV_LORA
    o_gq = o_kr + MLA_ROPE
    o_gk = o_gq + GQA_HEADS * GQA_HEAD_DIM
    o_gv = o_gk + GQA_KV_HEADS * GQA_HEAD_DIM
    axial = np.concatenate([_swap_halves(GQA_HEAD_DIM // 2),
                            GQA_HEAD_DIM // 2 + _swap_halves(GQA_HEAD_DIM // 2)])
    idx = [np.arange(0, o_kr)]
    idx += [pad(MLA_NOPE), o_kr + np.arange(MLA_ROPE), pad(LANE - MLA_NOPE - MLA_ROPE)]
    idx += [pad(MLA_NOPE), o_kr + _swap_halves(MLA_ROPE), pad(LANE - MLA_NOPE - MLA_ROPE)]
    for h in range(GQA_HEADS):
        idx += [o_gq + h * GQA_HEAD_DIM + np.arange(GQA_HEAD_DIM), pad(LANE - GQA_HEAD_DIM)]
    for h in range(GQA_HEADS):
        idx += [o_gq + h * GQA_HEAD_DIM + axial, pad(LANE - GQA_HEAD_DIM)]
    for h in range(GQA_KV_HEADS):
        idx += [o_gk + h * GQA_HEAD_DIM + np.arange(GQA_HEAD_DIM), pad(LANE - GQA_HEAD_DIM)]
    for h in range(GQA_KV_HEADS):
        idx += [o_gk + h * GQA_HEAD_DIM + axial, pad(LANE - GQA_HEAD_DIM)]
    idx += [o_gv + np.arange(GQA_KV_HEADS * GQA_HEAD_DIM)]
    idx = np.concatenate(idx)
    assert idx.shape[0] == _C_END
    w_in2 = _take_cols(w_in, idx).astype(BF16)

    hd = MLA_NOPE + MLA_ROPE
    ia, ib = [], []
    for h in range(MLA_HEADS):
        ia += [h * hd + np.arange(hd), pad(LANE - hd)]
        ib += [pad(MLA_NOPE), h * hd + MLA_NOPE + _swap_halves(MLA_ROPE), pad(LANE - hd)]
    w_uq2 = _take_cols(w_uq, np.concatenate(ia + ib)).astype(BF16)

    kvd = MLA_NOPE + MLA_V
    ik, iv = [], []
    for h in range(MLA_HEADS):
        ik += [h * kvd + np.arange(MLA_NOPE), pad(LANE - MLA_NOPE)]
        iv += [h * kvd + MLA_NOPE + np.arange(MLA_V)]
    w_ukv2 = _take_cols(w_ukv, np.concatenate(ik + iv)).astype(BF16)

    def gain_pair(g):
        ga = jnp.concatenate([g, jnp.zeros((LANE - GQA_HEAD_DIM,), F32)])[None, :]
        gb = jnp.concatenate([g[jnp.asarray(axial)], jnp.zeros((LANE - GQA_HEAD_DIM,), F32)])[None, :]
        return ga, gb

    gq_a, gq_b = gain_pair(gq_norm)
    gk_a, gk_b = gain_pair(gk_norm)

    w_out2 = w_out.astype(BF16)

    return dict(w_in=w_in2, q_norm=q_norm[None, :], w_uq=w_uq2, kv_norm=kv_norm[None, :],
                w_ukv=w_ukv2, gq_a=gq_a, gq_b=gq_b, gk_a=gk_a, gk_b=gk_b), w_out2


def _rope_tables(pos, row, col):
    half = MLA_ROPE // 2
    inv = 1.0 / (ROPE_THETA ** (jnp.arange(half, dtype=F32) / half))
    n = pos.shape[0]

    def cs(p):
        ang = p.astype(F32)[:, None] * inv[None, :]
        return jnp.cos(ang), jnp.sin(ang)

    c1, s1 = cs(pos)
    cr, sr = cs(row)
    cc, sc = cs(col)
    z = lambda k: jnp.zeros((n, k), F32)
    qs = (MLA_NOPE + MLA_ROPE) ** -0.5 * LOG2E
    tail = LANE - MLA_NOPE - MLA_ROPE
    cos_q = jnp.concatenate([jnp.full((n, MLA_NOPE), qs, F32), qs * c1, qs * c1, z(tail)], axis=1)
    sin_q = jnp.concatenate([z(MLA_NOPE), -qs * s1, qs * s1, z(tail)], axis=1)
    cos_k = jnp.concatenate([z(MLA_NOPE), c1, c1, z(tail)], axis=1)
    sin_k = jnp.concatenate([z(MLA_NOPE), -s1, s1, z(tail)], axis=1)
    cos_g = jnp.concatenate([cr, cr, cc, cc, z(LANE - GQA_HEAD_DIM)], axis=1)
    sin_g = jnp.concatenate([-sr, sr, -sc, sc, z(LANE - GQA_HEAD_DIM)], axis=1)
    return jnp.concatenate([cos_q, sin_q, cos_k, sin_k, cos_g, sin_g], axis=1)


def _na_bias_tables(rpb, meta_bias):
    c_idx = np.arange(GRID_W)
    c_start = np.clip(c_idx - NA_WIN_C // 2, 0, GRID_W - NA_WIN_C)
    col_mask = (c_idx[None, :] >= c_start[:, None]) & (c_idx[None, :] < c_start[:, None] + NA_WIN_C)
    col_off = np.clip(c_idx[None, :] - c_idx[:, None] + NA_WIN_C - 1, 0, 2 * NA_WIN_C - 2)
    t = rpb[:, :, jnp.asarray(col_off)] * LOG2E
    t = jnp.where(jnp.asarray(col_mask)[None, None], t, NEG_INF)
    t = jnp.concatenate([t, jnp.full_like(t[:, :1], NEG_INF)], axis=1)
    t = t.transpose(0, 1, 3, 2)
    hp = NA_HEADS // 2
    bias = t.reshape(hp, 2, NA_MASKED + 1, GRID_W, GRID_W).transpose(0, 2, 3, 1, 4)
    bias = bias.reshape(hp, NA_MASKED + 1, GRID_W, LANE)
    mbl = meta_bias * LOG2E
    mb_t = jnp.repeat(mbl.reshape(hp, 2, N_META).transpose(0, 2, 1), GRID_W, axis=2)
    mb = jnp.pad(mbl, ((0, 0), (0, LANE - N_META)))
    return bias, mb_t, mb


def _pad_meta(x, n_seq):
    c = x.shape[1]
    return jnp.pad(x.reshape(n_seq, N_META, c), ((0, 0), (0, LANE - N_META), (0, 0)))


def kernel(x_prompt, x_sample, meta, norm_gains, ffn1_w_gate, ffn1_w_up, ffn1_w_down, ffn2_w_gate, ffn2_w_up, ffn2_w_down, attn_w_in, mla_q_norm, mla_w_uq, mla_kv_norm, mla_w_ukv, gqa_q_norm, gqa_k_norm, attn_w_out, na_w_qkv, na_rpb, na_meta_bias, na_w_out):
    bp, sp, _ = x_prompt.shape
    bs, ss, _ = x_sample.shape
    n_seq = bp + bs
    depth = norm_gains.shape[0]
    groups = [(bp, sp, 0), (bs, ss, bp)]

    n_meta = n_seq * N_META
    meta_rows = -(-n_meta // LANE) * LANE
    pad_rows = lambda x: jnp.pad(x, ((0, meta_rows - x.shape[0]), (0, 0)))
    h_tok = [x_prompt.reshape(bp * sp, D_MODEL), x_sample.reshape(bs * ss, D_MODEL)]
    h_meta = pad_rows(jnp.tile(meta.astype(F32), (n_seq, 1)))

    smax = max(sp, ss)
    t = jnp.arange(smax)
    tab_tok = _rope_tables(t + N_META, t // GRID_W, t % GRID_W)
    mi = jnp.arange(meta_rows) % N_META
    tab_meta = _rope_tables(mi, jnp.full_like(mi, -1), mi)

    for i in range(depth):
        gains = jnp.pad(norm_gains[i], ((0, 2), (0, 0)))
        w1 = (ffn1_w_gate[i].astype(BF16), ffn1_w_up[i].astype(BF16), ffn1_w_down[i].astype(BF16))
        w2 = (ffn2_w_gate[i].astype(BF16), ffn2_w_up[i].astype(BF16), ffn2_w_down[i].astype(BF16))
        j = i // 2
        h_tok = [_ffn1(h, gains, *w1) for h in h_tok]
        h_meta = _ffn1(h_meta, gains, *w1)
        if i % 2 == 0:
            w, w_out = _dense_weights(attn_w_in[j], mla_q_norm[j], mla_w_uq[j], mla_kv_norm[j],
                                      mla_w_ukv[j], gqa_q_norm[j], gqa_k_norm[j], attn_w_out[j])
            qkv_tok = [_proj_dense(h, gains, w, tab_tok, s) for h, (_, s, _) in zip(h_tok, groups)]
            qm, km, vmt = _proj_dense(h_meta, gains, w, tab_meta, 0)
            kmp = _pad_meta(km[:n_meta], n_seq)
            vmt = vmt.transpose(1, 0, 2).reshape(V_ROWS, meta_rows)
            vmtp = vmt[:, :n_meta].reshape(V_ROWS, n_seq, N_META).transpose(1, 0, 2)
            vmtp = jnp.pad(vmtp, ((0, 0), (0, 0), (0, LANE - N_META)))
            qmp = _pad_meta(qm[:n_meta], n_seq).reshape(n_seq * LANE, HEAD_SLOTS * LANE)
            o_tok, o_meta = [], []
            for (q, k, vt), (nb, s, b0) in zip(qkv_tok, groups):
                o_tok.append(_dense_attn(q, k, vt, kmp, vmtp, n_seq=nb, seq=s,
                                         tq=_row_tile(s, 512), q_base=0, meta_base=b0))
                om = _dense_attn(qmp, k, vt, kmp, vmtp, n_seq=nb, seq=s,
                                 tq=LANE, q_base=b0 * LANE, meta_base=b0)
                o_meta.append(om.reshape(nb, LANE, D_MODEL)[:, :N_META].reshape(nb * N_META, D_MODEL))
            o_meta = pad_rows(jnp.concatenate(o_meta, axis=0))
        else:
            w_qkv = na_w_qkv[j].astype(BF16)
            w_out = na_w_out[j].astype(BF16)
            bias, mb_t, mb = _na_bias_tables(na_rpb[j], na_meta_bias[j])
            qkv_tok = [_proj_na(h, gains, w_qkv, True) for h in h_tok]
            qm, km, vm = _proj_na(h_meta, gains, w_qkv, False)
            kmp, vmp = _pad_meta(km[:n_meta], n_seq), _pad_meta(vm[:n_meta], n_seq)
            km16 = km[:n_meta].reshape(n_seq, N_META, NA_HEADS * NA_HEAD_DIM)
            vmtp = vmp.transpose(0, 2, 1)
            o_tok = [_na_attn(q, k, vt, km16, vmtp, bias, mb_t, n_seq=nb, seq=s, meta_base=b0)
                     for (q, k, vt), (nb, s, b0) in zip(qkv_tok, groups)]
            o_meta = pad_rows(_na_meta(qm[:n_meta], kmp, vmp, mb))
        h_tok = [_mix_ffn2(h, o, w_out, gains, *w2) for h, o in zip(h_tok, o_tok)]
        h_meta = _mix_ffn2(h_meta, o_meta, w_out, gains, *w2)

    return (h_tok[0].reshape(bp, sp, D_MODEL), h_tok[1].reshape(bs, ss, D_MODEL))
```

```python
import functools
import math

import jax
import jax.numpy as jnp
import numpy as np
from jax import lax
from jax.experimental import pallas as pl
from jax.experimental.pallas import tpu as pltpu

F32 = jnp.float32
BF16 = jnp.bfloat16

D_MODEL = 1024
N_META = 16
GRID_W = 64
D_FF = 2816
EPS = 1e-6
NEG_INF = -1e30
LOG2E = math.log2(math.e)

MLA_HEADS = 8
MLA_Q_LORA = 256
MLA_KV_LORA = 128
MLA_NOPE = 64
MLA_ROPE = 32
MLA_V = 64
GQA_HEADS = 8
GQA_KV_HEADS = 2
GQA_HEAD_DIM = 64
ROPE_THETA = 10000.0
NA_HEADS = 16
NA_HEAD_DIM = 64
NA_WIN_R = 8
NA_WIN_C = 16

LANE = 128
HEAD_SLOTS = MLA_HEADS + GQA_HEADS
K_SLOTS = MLA_HEADS + GQA_KV_HEADS
V_ROWS = (MLA_HEADS + GQA_KV_HEADS) * MLA_V
SUBLANE = 8
VMEM_LIMIT = 56 * 1024 * 1024

_C_CQ = 0
_C_CKV = _C_CQ + MLA_Q_LORA
_C_KRA = _C_CKV + MLA_KV_LORA
_C_KRB = _C_KRA + LANE
_C_GQA = _C_KRB + LANE
_C_GQB = _C_GQA + GQA_HEADS * LANE
_C_GKA = _C_GQB + GQA_HEADS * LANE
_C_GKB = _C_GKA + GQA_KV_HEADS * LANE
_C_GV = _C_GKB + GQA_KV_HEADS * LANE
_C_END = _C_GV + LANE


def _const_spec(shape):
    nd = len(shape)
    return pl.BlockSpec(shape, lambda *_: (0,) * nd, pipeline_mode=pl.Buffered(1))


def _rms(x, g):
    ms = jnp.mean(x * x, axis=-1, keepdims=True)
    return x * lax.rsqrt(ms + EPS) * g


def _dot(a, b):
    return jnp.dot(a, b, preferred_element_type=F32)


def _dot_nt(a, b):
    return lax.dot_general(a, b, (((1,), (1,)), ((), ())), preferred_element_type=F32)


def _row_tile(rows, want):
    t = min(rows, want)
    while rows % t:
        t //= 2
    return t


def _ffn_body(h, g_ref, pre, post, wg_ref, wu_ref, wd_ref):
    xn = _rms(h, g_ref[pre:pre + 1, :]).astype(BF16)
    gate = _dot(xn, wg_ref[...])
    up = _dot(xn, wu_ref[...])
    act = (gate * jax.nn.sigmoid(gate) * up).astype(BF16)
    y = _dot(act, wd_ref[...])
    return h + 0.5 * _rms(y, g_ref[post:post + 1, :])


def _ffn1_kernel(h_ref, g_ref, wg_ref, wu_ref, wd_ref, out_ref):
    out_ref[...] = _ffn_body(h_ref[...], g_ref, 0, 1, wg_ref, wu_ref, wd_ref)


def _mix_ffn2_kernel(h_ref, o_ref, wo_ref, g_ref, wg_ref, wu_ref, wd_ref, out_ref):
    mixed = _dot(o_ref[...], wo_ref[...])
    h = h_ref[...] + _rms(mixed, g_ref[3:4, :])
    out_ref[...] = _ffn_body(h, g_ref, 4, 5, wg_ref, wu_ref, wd_ref)


def _ffn1(h, gains, wg, wu, wd):
    rows = h.shape[0]
    tm = _row_tile(rows, 512)
    return pl.pallas_call(
        _ffn1_kernel,
        grid=(rows // tm,),
        in_specs=[
            pl.BlockSpec((tm, D_MODEL), lambda i: (i, 0)),
            _const_spec(gains.shape),
            _const_spec(wg.shape), _const_spec(wu.shape), _const_spec(wd.shape),
        ],
        out_specs=pl.BlockSpec((tm, D_MODEL), lambda i: (i, 0)),
        out_shape=jax.ShapeDtypeStruct((rows, D_MODEL), F32),
        compiler_params=pltpu.CompilerParams(
            dimension_semantics=("arbitrary",), vmem_limit_bytes=VMEM_LIMIT),
        name="ffn1",
    )(h, gains, wg, wu, wd)


def _mix_ffn2(h, o, wo, gains, wg, wu, wd):
    rows = h.shape[0]
    tm = _row_tile(rows, 512)
    return pl.pallas_call(
        _mix_ffn2_kernel,
        grid=(rows // tm,),
        in_specs=[
            pl.BlockSpec((tm, D_MODEL), lambda i: (i, 0)),
            pl.BlockSpec((tm, o.shape[1]), lambda i: (i, 0)),
            _const_spec(wo.shape),
            _const_spec(gains.shape),
            _const_spec(wg.shape), _const_spec(wu.shape), _const_spec(wd.shape),
        ],
        out_specs=pl.BlockSpec((tm, D_MODEL), lambda i: (i, 0)),
        out_shape=jax.ShapeDtypeStruct((rows, D_MODEL), F32),
        compiler_params=pltpu.CompilerParams(
            dimension_semantics=("arbitrary",), vmem_limit_bytes=VMEM_LIMIT),
        name="mix_ffn2",
    )(h, o, wo, gains, wg, wu, wd)


def _proj_dense_kernel(h_ref, g_ref, win_ref, qn_ref, wuq_ref, kvn_ref, wukv_ref,
                       gqa_ref, gqb_ref, gka_ref, gkb_ref, tab_ref, q_ref, k_ref, vt_ref):
    a = _rms(h_ref[...], g_ref[2:3, :]).astype(BF16)
    proj = _dot(a, win_ref[...])
    tab = tab_ref[...]
    cos_k, sin_k = tab[:, 0:LANE], tab[:, LANE:2 * LANE]
    cos_g, sin_g = tab[:, 2 * LANE:3 * LANE], tab[:, 3 * LANE:4 * LANE]
    qs = (MLA_NOPE + MLA_ROPE) ** -0.5 * LOG2E
    lane = lax.broadcasted_iota(jnp.int32, cos_k.shape, 1)
    cos_q = jnp.where(lane < MLA_NOPE, qs, cos_k * qs)
    sin_q = sin_k * qs

    cqn = _rms(proj[:, _C_CQ:_C_CQ + MLA_Q_LORA], qn_ref[...]).astype(BF16)
    qab = _dot(cqn, wuq_ref[...])
    nq = MLA_HEADS * LANE
    for h in range(MLA_HEADS):
        qa = qab[:, h * LANE:(h + 1) * LANE]
        qb = qab[:, nq + h * LANE:nq + (h + 1) * LANE]
        q_ref[:, h * LANE:(h + 1) * LANE] = (qa * cos_q + qb * sin_q).astype(BF16)

    ckvn = _rms(proj[:, _C_CKV:_C_CKV + MLA_KV_LORA], kvn_ref[...]).astype(BF16)
    kv = _dot(ckvn, wukv_ref[...])
    k_rope = (proj[:, _C_KRA:_C_KRA + LANE] * cos_k + proj[:, _C_KRB:_C_KRB + LANE] * sin_k)
    for h in range(MLA_HEADS):
        k_ref[:, h * LANE:(h + 1) * LANE] = (kv[:, h * LANE:(h + 1) * LANE] + k_rope).astype(BF16)
    nv = MLA_HEADS * MLA_V

    gq_scale = GQA_HEAD_DIM ** -0.5 * LOG2E
    cq_g = cos_g * (gqa_ref[...] * gq_scale)
    sq_g = sin_g * (gqb_ref[...] * gq_scale)
    for h in range(GQA_HEADS):
        xa = proj[:, _C_GQA + h * LANE:_C_GQA + (h + 1) * LANE]
        xb = proj[:, _C_GQB + h * LANE:_C_GQB + (h + 1) * LANE]
        r = lax.rsqrt(jnp.sum(xa * xa, axis=-1, keepdims=True) * (1.0 / GQA_HEAD_DIM) + EPS)
        q_ref[:, nq + h * LANE:nq + (h + 1) * LANE] = ((xa * cq_g + xb * sq_g) * r).astype(BF16)
    ck_g = cos_g * gka_ref[...]
    sk_g = sin_g * gkb_ref[...]
    for h in range(GQA_KV_HEADS):
        xa = proj[:, _C_GKA + h * LANE:_C_GKA + (h + 1) * LANE]
        xb = proj[:, _C_GKB + h * LANE:_C_GKB + (h + 1) * LANE]
        r = lax.rsqrt(jnp.sum(xa * xa, axis=-1, keepdims=True) * (1.0 / GQA_HEAD_DIM) + EPS)
        k_ref[:, nq + h * LANE:nq + (h + 1) * LANE] = ((xa * ck_g + xb * sk_g) * r).astype(BF16)
    v = jnp.concatenate([kv[:, nq:nq + nv], proj[:, _C_GV:_C_GV + LANE]], axis=1)
    vt_ref[0] = v.T.astype(BF16)


def _proj_dense(h, gains, w, tab, seq):
    rows = h.shape[0]
    tm = _row_tile(seq if seq else rows, 512)
    nblk = (seq // tm) if seq else 1
    ntab = tab.shape[1]
    consts = [w["w_in"], w["q_norm"], w["w_uq"], w["kv_norm"], w["w_ukv"],
              w["gq_a"], w["gq_b"], w["gk_a"], w["gk_b"]]
    return pl.pallas_call(
        _proj_dense_kernel,
        grid=(rows // tm,),
        in_specs=[pl.BlockSpec((tm, D_MODEL), lambda i: (i, 0)), _const_spec(gains.shape)]
        + [_const_spec(c.shape) for c in consts]
        + [pl.BlockSpec((tm, ntab), lambda i: (i % nblk, 0))],
        out_specs=[
            pl.BlockSpec((tm, HEAD_SLOTS * LANE), lambda i: (i, 0)),
            pl.BlockSpec((tm, K_SLOTS * LANE), lambda i: (i, 0)),
            pl.BlockSpec((1, V_ROWS, tm), lambda i: (i, 0, 0)),
        ],
        out_shape=[
            jax.ShapeDtypeStruct((rows, HEAD_SLOTS * LANE), BF16),
            jax.ShapeDtypeStruct((rows, K_SLOTS * LANE), BF16),
            jax.ShapeDtypeStruct((rows // tm, V_ROWS, tm), BF16),
        ],
        compiler_params=pltpu.CompilerParams(
            dimension_semantics=("arbitrary",), vmem_limit_bytes=VMEM_LIMIT),
        name="proj_dense",
    )(h, gains, *consts, tab)


def _head_slots(h):
    if h < MLA_HEADS:
        return h, h
    kvh = (h - MLA_HEADS) // (GQA_HEADS // GQA_KV_HEADS)
    return MLA_HEADS + kvh, MLA_HEADS + kvh


def _sublane_bcast_max(x):
    return jnp.broadcast_to(jnp.max(x, axis=0, keepdims=True), x.shape)


def _dense_attn_kernel(q_ref, k_ref, vt_ref, km_ref, vmt_ref, o_ref, m_ref, acc_ref, s_ref, sm_ref):
    kv = pl.program_id(2)
    tq = q_ref.shape[0]
    n_sub, _, tk = vt_ref.shape
    hd = MLA_V
    acc_rows = acc_ref.shape[1]

    def with_ones(vt):
        return jnp.concatenate([vt, jnp.ones((acc_rows - hd, vt.shape[1]), BF16)], axis=0)

    @pl.when(kv == 0)
    def _():
        for h in range(HEAD_SLOTS):
            ks, _ = _head_slots(h)
            q = q_ref[:, h * LANE:(h + 1) * LANE]
            sm_ref[h] = _dot_nt(km_ref[0:N_META, ks * LANE:(ks + 1) * LANE], q)
        zeros = jnp.zeros((LANE - N_META, tq), F32)
        for h in range(HEAD_SLOTS):
            _, vh = _head_slots(h)
            s3 = sm_ref[h].reshape(N_META // SUBLANE, SUBLANE, tq)
            m = _sublane_bcast_max(jnp.max(s3, axis=0))
            p3 = jnp.exp2(s3 - m[None])
            m_ref[h] = m
            p = jnp.concatenate([p3.reshape(N_META, tq), zeros], axis=0).astype(BF16)
            acc_ref[h] = _dot(with_ones(vmt_ref[vh * hd:(vh + 1) * hd, :]), p)

    def scores(t, h, slot):
        ks, _ = _head_slots(h)
        k0 = pl.multiple_of(t * tk, tk)
        k = k_ref[pl.ds(k0, tk), ks * LANE:(ks + 1) * LANE]
        s_ref[slot] = _dot_nt(k, q_ref[:, h * LANE:(h + 1) * LANE])

    scores(0, 0, 0)

    def sub_tile(t, carry):
        for h in range(HEAD_SLOTS):
            slot = h % 2
            if h + 1 < HEAD_SLOTS:
                scores(t, h + 1, 1 - slot)
            else:
                scores(jnp.minimum(t + 1, n_sub - 1), 0, 1 - slot)
            _, vh = _head_slots(h)
            s3 = s_ref[slot].reshape(tk // SUBLANE, SUBLANE, tq)
            m_prev = m_ref[h]
            m_new = jnp.maximum(m_prev, _sublane_bcast_max(jnp.max(s3, axis=0)))
            alpha = jnp.exp2(m_prev - m_new)
            p = jnp.exp2(s3 - m_new[None]).reshape(tk, tq).astype(BF16)
            pv = _dot(with_ones(vt_ref[t, vh * hd:(vh + 1) * hd, :]), p)
            acc = acc_ref[h].reshape(acc_rows // SUBLANE, SUBLANE, tq) * alpha[None]
            acc_ref[h] = acc.reshape(acc_rows, tq) + pv
            m_ref[h] = m_new
        return carry

    lax.fori_loop(0, n_sub, sub_tile, 0)

    @pl.when(kv == pl.num_programs(2) - 1)
    def _():
        for j in range(HEAD_SLOTS // 2):
            outs = []
            for h in (2 * j, 2 * j + 1):
                outs.append(acc_ref[h, 0:hd, :] / acc_ref[h, hd:hd + 1, :])
            o_t = jnp.concatenate(outs, axis=0)
            o_ref[:, j * LANE:(j + 1) * LANE] = o_t.T.astype(BF16)


def _dense_attn(q, k, vt, km, vmt, *, n_seq, seq, tq, q_base, meta_base):
    nq = (q.shape[0] - q_base) // (n_seq * tq)
    tk = vt.shape[2]
    n_sub = _row_tile(seq // tk, 8)
    nk = seq // (tk * n_sub)
    qb0 = q_base // tq
    out_rows = n_seq * nq * tq
    return pl.pallas_call(
        _dense_attn_kernel,
        grid=(n_seq, nq, nk),
        in_specs=[
            pl.BlockSpec((tq, HEAD_SLOTS * LANE), lambda b, i, j: (qb0 + b * nq + i, 0)),
            pl.BlockSpec((n_sub * tk, K_SLOTS * LANE), lambda b, i, j: (b * nk + j, 0)),
            pl.BlockSpec((n_sub, V_ROWS, tk), lambda b, i, j: (b * nk + j, 0, 0)),
            pl.BlockSpec((None, LANE, K_SLOTS * LANE), lambda b, i, j: (meta_base + b, 0, 0)),
            pl.BlockSpec((None, V_ROWS, LANE), lambda b, i, j: (meta_base + b, 0, 0)),
        ],
        out_specs=pl.BlockSpec((tq, D_MODEL), lambda b, i, j: (b * nq + i, 0)),
        out_shape=jax.ShapeDtypeStruct((out_rows, D_MODEL), BF16),
        scratch_shapes=[
            pltpu.VMEM((HEAD_SLOTS, SUBLANE, tq), F32),
            pltpu.VMEM((HEAD_SLOTS, MLA_V + 2 * SUBLANE, tq), F32),
            pltpu.VMEM((2, tk, tq), F32),
            pltpu.VMEM((HEAD_SLOTS, N_META, tq), F32),
        ],
        compiler_params=pltpu.CompilerParams(
            dimension_semantics=("arbitrary", "arbitrary", "arbitrary"),
            vmem_limit_bytes=VMEM_LIMIT),
        name="dense_attn",
    )(q, k, vt, km, vmt)


def _proj_na_kernel(h_ref, g_ref, w_ref, q_ref, k_ref, v_ref, *, transpose_v):
    a = _rms(h_ref[...], g_ref[2:3, :]).astype(BF16)
    qkv = _dot(a, w_ref[...])
    n = NA_HEADS * NA_HEAD_DIM
    q_ref[...] = (qkv[:, 0:n] * (NA_HEAD_DIM ** -0.5 * LOG2E)).astype(BF16)
    k_ref[...] = qkv[:, n:2 * n].astype(BF16)
    v = qkv[:, 2 * n:3 * n]
    if transpose_v:
        vt = v.T.astype(BF16)
        for t in range(v_ref.shape[0]):
            v_ref[t] = vt[:, t * LANE:(t + 1) * LANE]
    else:
        v_ref[...] = v.astype(BF16)


def _proj_na(h, gains, w, transpose_v):
    rows = h.shape[0]
    tm = _row_tile(rows, 512)
    n = NA_HEADS * NA_HEAD_DIM
    if transpose_v:
        v_spec = pl.BlockSpec((tm // LANE, n, LANE), lambda i: (i, 0, 0))
        v_shape = jax.ShapeDtypeStruct((rows // LANE, n, LANE), BF16)
    else:
        v_spec = pl.BlockSpec((tm, n), lambda i: (i, 0))
        v_shape = jax.ShapeDtypeStruct((rows, n), BF16)
    return pl.pallas_call(
        functools.partial(_proj_na_kernel, transpose_v=transpose_v),
        grid=(rows // tm,),
        in_specs=[pl.BlockSpec((tm, D_MODEL), lambda i: (i, 0)), _const_spec(gains.shape),
                  _const_spec(w.shape)],
        out_specs=[pl.BlockSpec((tm, n), lambda i: (i, 0))] * 2 + [v_spec],
        out_shape=[jax.ShapeDtypeStruct((rows, n), BF16)] * 2 + [v_shape],
        compiler_params=pltpu.CompilerParams(
            dimension_semantics=("arbitrary",), vmem_limit_bytes=VMEM_LIMIT),
        name="proj_na",
    )(h, gains, w)


NA_SPAN_R = NA_WIN_R + 2
NA_MASKED = 2 * NA_WIN_R - 1


def _na_kernel(q_ref, k_ref, vt_ref, km_ref, vmt_ref, bias_ref, mb_ref, o_ref, s_ref,
               *, rows, rows_per_step):
    step = pl.program_id(1)
    n_pairs = rows_per_step // 2
    n_hp = NA_HEADS // 2
    span = NA_SPAN_R * GRID_W
    lane = lax.broadcasted_iota(jnp.int32, (GRID_W, LANE), 1)
    first = lane < (LANE // 2)
    zeros_m = jnp.zeros((LANE - N_META, 2 * LANE), F32)

    def geometry(rp):
        ra = step * rows_per_step + 2 * rp
        rs = [jnp.clip(ra + x - NA_WIN_R // 2, 0, rows - NA_WIN_R) for x in range(2)]
        ws = jnp.minimum((rs[0] // 2) * 2, rows - NA_SPAN_R)
        return ra, rs, ws

    def scores(rp, hp, slot):
        _, _, ws = geometry(rp)
        cols = slice(hp * LANE, (hp + 1) * LANE)
        parts = []
        for x in range(2):
            q0 = pl.multiple_of((2 * rp + x) * GRID_W, GRID_W)
            qx = q_ref[pl.ds(q0, GRID_W), cols]
            parts += [jnp.where(first, qx, jnp.zeros_like(qx)), jnp.where(first, jnp.zeros_like(qx), qx)]
        qblk = jnp.concatenate(parts, axis=0)
        k0 = pl.multiple_of(ws * GRID_W, 2 * GRID_W)
        s_ref[slot, 0:span, :] = _dot_nt(k_ref[pl.ds(k0, span), cols], qblk)
        s_ref[slot, span:span + N_META, :] = _dot_nt(km_ref[:, cols], qblk)

    scores(0, 0, 0)

    def row_pair(rp, carry):
        ra, rs, ws = geometry(rp)
        idx = []
        for jj in range(NA_SPAN_R):
            kr = ws + jj
            idx.append([jnp.where((kr >= rs[x]) & (kr < rs[x] + NA_WIN_R),
                                  kr - (ra + x) + NA_WIN_R - 1, NA_MASKED) for x in range(2)])
        t0 = ws // 2
        for hp in range(n_hp):
            slot = hp % 2
            if hp + 1 < n_hp:
                scores(rp, hp + 1, 1 - slot)
            else:
                scores(jnp.minimum(rp + 1, n_pairs - 1), 0, 1 - slot)
            cols = slice(hp * LANE, (hp + 1) * LANE)
            b = jnp.concatenate(
                [jnp.concatenate([bias_ref[hp, idx[jj][0]], bias_ref[hp, idx[jj][1]]], axis=1)
                 for jj in range(NA_SPAN_R)], axis=0)
            s = s_ref[slot, 0:span, :]
            s = jnp.where(b > 0.5 * NEG_INF, s + b, NEG_INF)
            mb = mb_ref[hp]
            sm = s_ref[slot, span:span + N_META, :] + jnp.concatenate([mb, mb], axis=1)
            s3 = s.reshape(span // SUBLANE, SUBLANE, 2 * LANE)
            sm3 = sm.reshape(N_META // SUBLANE, SUBLANE, 2 * LANE)
            m = _sublane_bcast_max(jnp.maximum(jnp.max(s3, axis=0), jnp.max(sm3, axis=0)))
            p3 = jnp.exp2(s3 - m[None])
            pm3 = jnp.exp2(sm3 - m[None])
            l = jnp.sum(jnp.sum(p3, axis=0) + jnp.sum(pm3, axis=0), axis=0, keepdims=True)
            p = p3.reshape(span, 2 * LANE).astype(BF16)
            pm = jnp.concatenate([pm3.reshape(N_META, 2 * LANE), zeros_m], axis=0).astype(BF16)
            v_all = jnp.concatenate([vt_ref[t0 + t, cols, :] for t in range(span // LANE)]
                                    + [vmt_ref[cols, :]], axis=1)
            o_t = _dot(v_all, jnp.concatenate([p, pm], axis=0)) / l
            for x in range(2):
                blk = o_t[:, x * LANE:(x + 1) * LANE].T
                q0 = pl.multiple_of((2 * rp + x) * GRID_W, GRID_W)
                o_ref[pl.ds(q0, GRID_W), cols] = jnp.where(
                    first, blk[0:GRID_W], blk[GRID_W:2 * GRID_W]).astype(BF16)
        return carry

    lax.fori_loop(0, n_pairs, row_pair, 0)


def _na_attn(q, k, vt, km, vmt, bias, mb, *, n_seq, seq, meta_base):
    rows = seq // GRID_W
    assert rows >= NA_SPAN_R and rows % 2 == 0
    rps = 8
    nsteps = rows // rps
    n = NA_HEADS * NA_HEAD_DIM
    span = NA_SPAN_R * GRID_W
    return pl.pallas_call(
        functools.partial(_na_kernel, rows=rows, rows_per_step=rps),
        grid=(n_seq, nsteps),
        in_specs=[
            pl.BlockSpec((rps * GRID_W, n), lambda b, i: (b * nsteps + i, 0)),
            pl.BlockSpec((seq, n), lambda b, i: (b, 0), pipeline_mode=pl.Buffered(1)),
            pl.BlockSpec((seq // LANE, n, LANE), lambda b, i: (b, 0, 0), pipeline_mode=pl.Buffered(1)),
            pl.BlockSpec((None, N_META, n), lambda b, i: (meta_base + b, 0, 0)),
            pl.BlockSpec((None, n, LANE), lambda b, i: (meta_base + b, 0, 0)),
            _const_spec(bias.shape),
            _const_spec(mb.shape),
        ],
        out_specs=pl.BlockSpec((rps * GRID_W, n), lambda b, i: (b * nsteps + i, 0)),
        out_shape=jax.ShapeDtypeStruct((n_seq * seq, n), BF16),
        scratch_shapes=[pltpu.VMEM((2, span + N_META, 2 * LANE), F32)],
        compiler_params=pltpu.CompilerParams(
            dimension_semantics=("arbitrary", "arbitrary"), vmem_limit_bytes=VMEM_LIMIT),
        name="na_attn",
    )(q, k, vt, km, vmt, bias, mb)


def _na_meta_kernel(q_ref, km_ref, vm_ref, mb_ref, o_ref):
    lane = lax.broadcasted_iota(jnp.int32, (N_META, LANE), 1)
    first = lane < (LANE // 2)
    for j in range(NA_HEADS // 2):
        cols = slice(j * LANE, (j + 1) * LANE)
        qp = q_ref[:, cols]
        km = km_ref[:, cols]
        vm = vm_ref[:, cols]
        outs = []
        for half in range(2):
            h = 2 * j + half
            qh = jnp.where(first if half == 0 else jnp.logical_not(first), qp, jnp.zeros_like(qp))
            sm = _dot_nt(qh, km)
            sm = jnp.where(lane < N_META, sm + mb_ref[h:h + 1, :], NEG_INF)
            m = jnp.max(sm, axis=-1, keepdims=True)
            pm = jnp.exp2(sm - m)
            l = jnp.sum(pm, axis=-1, keepdims=True)
            outs.append(_dot(pm.astype(BF16), vm) / l)
        o_ref[:, cols] = jnp.where(first, outs[0], outs[1]).astype(BF16)


def _na_meta(qm, km, vm, mb):
    n_seq = km.shape[0]
    n = NA_HEADS * NA_HEAD_DIM
    return pl.pallas_call(
        _na_meta_kernel,
        grid=(n_seq,),
        in_specs=[
            pl.BlockSpec((N_META, n), lambda b: (b, 0)),
            pl.BlockSpec((None, LANE, n), lambda b: (b, 0, 0)),
            pl.BlockSpec((None, LANE, n), lambda b: (b, 0, 0)),
            _const_spec(mb.shape),
        ],
        out_specs=pl.BlockSpec((N_META, n), lambda b: (b, 0)),
        out_shape=jax.ShapeDtypeStruct((n_seq * N_META, n), BF16),
        compiler_params=pltpu.CompilerParams(dimension_semantics=("arbitrary",)),
        name="na_meta",
    )(qm, km, vm, mb)


def _take_cols(w, idx):
    idx = np.asarray(idx)
    cols = jnp.take(w, jnp.asarray(np.maximum(idx, 0)), axis=1)
    return jnp.where(jnp.asarray(idx >= 0)[None, :], cols, 0.0)


def _swap_halves(n):
    half = n // 2
    return np.concatenate([np.arange(half, n), np.arange(0, half)])


def _dense_weights(w_in, q_norm, w_uq, kv_norm, w_ukv, gq_norm, gk_norm, w_out):
    pad = lambda k: -np.ones(k, np.int64)
    o_kr = MLA_Q_LORA + MLA_KV_LORA
    o_gq = o_kr + MLA_ROPE
    o_gk = o_gq + GQA_HEADS * GQA_HEAD_DIM
    o_gv = o_gk + GQA_KV_HEADS * GQA_HEAD_DIM
    axial = np.concatenate([_swap_halves(GQA_HEAD_DIM // 2),
                            GQA_HEAD_DIM // 2 + _swap_halves(GQA_HEAD_DIM // 2)])
    idx = [np.arange(0, o_kr)]
    idx += [pad(MLA_NOPE), o_kr + np.arange(MLA_ROPE), pad(LANE - MLA_NOPE - MLA_ROPE)]
    idx += [pad(MLA_NOPE), o_kr + _swap_halves(MLA_ROPE), pad(LANE - MLA_NOPE - MLA_ROPE)]
    for h in range(GQA_HEADS):
        idx += [o_gq + h * GQA_HEAD_DIM + np.arange(GQA_HEAD_DIM), pad(LANE - GQA_HEAD_DIM)]
    for h in range(GQA_HEADS):
        idx += [o_gq + h * GQA_HEAD_DIM + axial, pad(LANE - GQA_HEAD_DIM)]
    for h in range(GQA_KV_HEADS):
        idx += [o_gk + h * GQA_HEAD_DIM + np.arange(GQA_HEAD_DIM), pad(LANE - GQA_HEAD_DIM)]
    for h in range(GQA_KV_HEADS):
        idx += [o_gk + h * GQA_HEAD_DIM + axial, pad(LANE - GQA_HEAD_DIM)]
    idx += [o_gv + np.arange(GQA_KV_HEADS * GQA_HEAD_DIM)]
    idx = np.concatenate(idx)
    assert idx.shape[0] == _C_END
    w_in2 = _take_cols(w_in, idx).astype(BF16)

    hd = MLA_NOPE + MLA_ROPE
    ia, ib = [], []
    for h in range(MLA_HEADS):
        ia += [h * hd + np.arange(hd), pad(LANE - hd)]
        ib += [pad(MLA_NOPE), h * hd + MLA_NOPE + _swap_halves(MLA_ROPE), pad(LANE - hd)]
    w_uq2 = _take_cols(w_uq, np.concatenate(ia + ib)).astype(BF16)

    kvd = MLA_NOPE + MLA_V
    ik, iv = [], []
    for h in range(MLA_HEADS):
        ik += [h * kvd + np.arange(MLA_NOPE), pad(LANE - MLA_NOPE)]
        iv += [h * kvd + MLA_NOPE + np.arange(MLA_V)]
    w_ukv2 = _take_cols(w_ukv, np.concatenate(ik + iv)).astype(BF16)

    def gain_pair(g):
        ga = jnp.concatenate([g, jnp.zeros((LANE - GQA_HEAD_DIM,), F32)])[None, :]
        gb = jnp.concatenate([g[jnp.asarray(axial)], jnp.zeros((LANE - GQA_HEAD_DIM,), F32)])[None, :]
        return ga, gb

    gq_a, gq_b = gain_pair(gq_norm)
    gk_a, gk_b = gain_pair(gk_norm)

    w_out2 = w_out.astype(BF16)

    return dict(w_in=w_in2, q_norm=q_norm[None, :], w_uq=w_uq2, kv_norm=kv_norm[None, :],
                w_ukv=w_ukv2, gq_a=gq_a, gq_b=gq_b, gk_a=gk_a, gk_b=gk_b), w_out2


def _rope_tables(pos, row, col):
    half = MLA_ROPE // 2
    inv = 1.0 / (ROPE_THETA ** (jnp.arange(half, dtype=F32) / half))
    n = pos.shape[0]

    def cs(p):
        ang = p.astype(F32)[:, None] * inv[None, :]
        return jnp.cos(ang), jnp.sin(ang)

    c1, s1 = cs(pos)
    cr, sr = cs(row)
    cc, sc = cs(col)
    z = lambda k: jnp.zeros((n, k), F32)
    tail = LANE - MLA_NOPE - MLA_ROPE
    cos_k = [z(MLA_NOPE), c1, c1, z(tail)]
    sin_k = [z(MLA_NOPE), -s1, s1, z(tail)]
    cos_g = [cr, cr, cc, cc, z(LANE - GQA_HEAD_DIM)]
    sin_g = [-sr, sr, -sc, sc, z(LANE - GQA_HEAD_DIM)]
    return jnp.concatenate(cos_k + sin_k + cos_g + sin_g, axis=1)


def _na_bias_tables(rpb, meta_bias):
    c_idx = np.arange(GRID_W)
    c_start = np.clip(c_idx - NA_WIN_C // 2, 0, GRID_W - NA_WIN_C)
    col_mask = (c_idx[None, :] >= c_start[:, None]) & (c_idx[None, :] < c_start[:, None] + NA_WIN_C)
    col_off = np.clip(c_idx[None, :] - c_idx[:, None] + NA_WIN_C - 1, 0, 2 * NA_WIN_C - 2)
    t = rpb[:, :, jnp.asarray(col_off)] * LOG2E
    t = jnp.where(jnp.asarray(col_mask)[None, None], t, NEG_INF)
    t = jnp.concatenate([t, jnp.full_like(t[:, :1], NEG_INF)], axis=1)
    t = t.transpose(0, 1, 3, 2)
    hp = NA_HEADS // 2
    bias = t.reshape(hp, 2, NA_MASKED + 1, GRID_W, GRID_W).transpose(0, 2, 3, 1, 4)
    bias = bias.reshape(hp, NA_MASKED + 1, GRID_W, LANE)
    mbl = meta_bias * LOG2E
    mb_t = jnp.repeat(mbl.reshape(hp, 2, N_META).transpose(0, 2, 1), GRID_W, axis=2)
    mb = jnp.pad(mbl, ((0, 0), (0, LANE - N_META)))
    return bias, mb_t, mb


def _pad_meta(x, n_seq):
    c = x.shape[1]
    return jnp.pad(x.reshape(n_seq, N_META, c), ((0, 0), (0, LANE - N_META), (0, 0)))


def kernel(x_prompt, x_sample, meta, norm_gains, ffn1_w_gate, ffn1_w_up, ffn1_w_down, ffn2_w_gate, ffn2_w_up, ffn2_w_down, attn_w_in, mla_q_norm, mla_w_uq, mla_kv_norm, mla_w_ukv, gqa_q_norm, gqa_k_norm, attn_w_out, na_w_qkv, na_rpb, na_meta_bias, na_w_out):
    bp, sp, _ = x_prompt.shape
    bs, ss, _ = x_sample.shape
    n_seq = bp + bs
    depth = norm_gains.shape[0]
    groups = [(bp, sp, 0), (bs, ss, bp)]

    n_meta = n_seq * N_META
    meta_rows = -(-n_meta // LANE) * LANE
    pad_rows = lambda x: jnp.pad(x, ((0, meta_rows - x.shape[0]), (0, 0)))
    h_tok = [x_prompt.reshape(bp * sp, D_MODEL), x_sample.reshape(bs * ss, D_MODEL)]
    h_meta = pad_rows(jnp.tile(meta.astype(F32), (n_seq, 1)))

    smax = max(sp, ss)
    t = jnp.arange(smax)
    tab_tok = _rope_tables(t + N_META, t // GRID_W, t % GRID_W)
    mi = jnp.arange(meta_rows) % N_META
    tab_meta = _rope_tables(mi, jnp.full_like(mi, -1), mi)

    for i in range(depth):
        gains = jnp.pad(norm_gains[i], ((0, 2), (0, 0)))
        w1 = (ffn1_w_gate[i].astype(BF16), ffn1_w_up[i].astype(BF16), ffn1_w_down[i].astype(BF16))
        w2 = (ffn2_w_gate[i].astype(BF16), ffn2_w_up[i].astype(BF16), ffn2_w_down[i].astype(BF16))
        j = i // 2
        h_tok = [_ffn1(h, gains, *w1) for h in h_tok]
        h_meta = _ffn1(h_meta, gains, *w1)
        if i % 2 == 0:
            w, w_out = _dense_weights(attn_w_in[j], mla_q_norm[j], mla_w_uq[j], mla_kv_norm[j],
                                      mla_w_ukv[j], gqa_q_norm[j], gqa_k_norm[j], attn_w_out[j])
            qkv_tok = [_proj_dense(h, gains, w, tab_tok, s) for h, (_, s, _) in zip(h_tok, groups)]
            qm, km, vmt = _proj_dense(h_meta, gains, w, tab_meta, 0)
            kmp = _pad_meta(km[:n_meta], n_seq)
            vmt = vmt.transpose(1, 0, 2).reshape(V_ROWS, meta_rows)
            vmtp = vmt[:, :n_meta].reshape(V_ROWS, n_seq, N_META).transpose(1, 0, 2)
            vmtp = jnp.pad(vmtp, ((0, 0), (0, 0), (0, LANE - N_META)))
            qmp = _pad_meta(qm[:n_meta], n_seq).reshape(n_seq * LANE, HEAD_SLOTS * LANE)
            o_tok, o_meta = [], []
            for (q, k, vt), (nb, s, b0) in zip(qkv_tok, groups):
                o_tok.append(_dense_attn(q, k, vt, kmp, vmtp, n_seq=nb, seq=s,
                                         tq=_row_tile(s, 512), q_base=0, meta_base=b0))
                om = _dense_attn(qmp, k, vt, kmp, vmtp, n_seq=nb, seq=s,
                                 tq=LANE, q_base=b0 * LANE, meta_base=b0)
                o_meta.append(om.reshape(nb, LANE, D_MODEL)[:, :N_META].reshape(nb * N_META, D_MODEL))
            o_meta = pad_rows(jnp.concatenate(o_meta, axis=0))
        else:
            w_qkv = na_w_qkv[j].astype(BF16)
            w_out = na_w_out[j].astype(BF16)
            bias, mb_t, mb = _na_bias_tables(na_rpb[j], na_meta_bias[j])
            qkv_tok = [_proj_na(h, gains, w_qkv, True) for h in h_tok]
            qm, km, vm = _proj_na(h_meta, gains, w_qkv, False)
            kmp, vmp = _pad_meta(km[:n_meta], n_seq), _pad_meta(vm[:n_meta], n_seq)
            km16 = km[:n_meta].reshape(n_seq, N_META, NA_HEADS * NA_HEAD_DIM)
            vmtp = vmp.transpose(0, 2, 1)
            o_tok = [_na_attn(q, k, vt, km16, vmtp, bias, mb_t, n_seq=nb, seq=s, meta_base=b0)
                     for (q, k, vt), (nb, s, b0) in zip(qkv_tok, groups)]
            o_meta = pad_rows(_na_meta(qm[:n_meta], kmp, vmp, mb))
        h_tok = [_mix_ffn2(h, o, w_out, gains, *w2) for h, o in zip(h_tok, o_tok)]
        h_meta = _mix_ffn2(h_meta, o_meta, w_out, gains, *w2)

    return (h_tok[0].reshape(bp, sp, D_MODEL), h_tok[1].reshape(bs, ss, D_MODEL))
```

```python
import functools
import math

import jax
import jax.numpy as jnp
import numpy as np
from jax import lax
from jax.experimental import pallas as pl
from jax.experimental.pallas import tpu as pltpu

F32 = jnp.float32
BF16 = jnp.bfloat16

D_MODEL = 1024
N_META = 16
GRID_W = 64
D_FF = 2816
EPS = 1e-6
NEG_INF = -1e30
LOG2E = math.log2(math.e)

MLA_HEADS = 8
MLA_Q_LORA = 256
MLA_KV_LORA = 128
MLA_NOPE = 64
MLA_ROPE = 32
MLA_V = 64
GQA_HEADS = 8
GQA_KV_HEADS = 2
GQA_HEAD_DIM = 64
ROPE_THETA = 10000.0
NA_HEADS = 16
NA_HEAD_DIM = 64
NA_WIN_R = 8
NA_WIN_C = 16

LANE = 128
HEAD_SLOTS = MLA_HEADS + GQA_HEADS
K_SLOTS = MLA_HEADS + GQA_KV_HEADS
V_ROWS = (MLA_HEADS + GQA_KV_HEADS) * MLA_V
SUBLANE = 8
DENSE_SLOTS = 4
NA_LOOKAHEAD, NA_SLOTS = 2, 4
VMEM_LIMIT = 56 * 1024 * 1024

_C_CQ = 0
_C_CKV = _C_CQ + MLA_Q_LORA
_C_KRA = _C_CKV + MLA_KV_LORA
_C_KRB = _C_KRA + LANE
_C_GQA = _C_KRB + LANE
_C_GQB = _C_GQA + GQA_HEADS * LANE
_C_GKA = _C_GQB + GQA_HEADS * LANE
_C_GKB = _C_GKA + GQA_KV_HEADS * LANE
_C_GV = _C_GKB + GQA_KV_HEADS * LANE
_C_END = _C_GV + LANE


def _const_spec(shape):
    nd = len(shape)
    return pl.BlockSpec(shape, lambda *_: (0,) * nd, pipeline_mode=pl.Buffered(1))


def _rms(x, g):
    ms = jnp.mean(x * x, axis=-1, keepdims=True)
    return x * lax.rsqrt(ms + EPS) * g


def _dot(a, b):
    return jnp.dot(a, b, preferred_element_type=F32)


def _dot_nt(a, b):
    return lax.dot_general(a, b, (((1,), (1,)), ((), ())), preferred_element_type=F32)


def _row_tile(rows, want):
    t = min(rows, want)
    while rows % t:
        t //= 2
    return t


def _ffn_body(h, g_ref, pre, post, wg_ref, wu_ref, wd_ref):
    xn = _rms(h, g_ref[pre:pre + 1, :]).astype(BF16)
    gate = _dot(xn, wg_ref[...])
    up = _dot(xn, wu_ref[...])
    act = (gate * jax.nn.sigmoid(gate) * up).astype(BF16)
    y = _dot(act, wd_ref[...])
    return h + 0.5 * _rms(y, g_ref[post:post + 1, :])


def _ffn1_kernel(h_ref, g_ref, wg_ref, wu_ref, wd_ref, out_ref):
    out_ref[...] = _ffn_body(h_ref[...], g_ref, 0, 1, wg_ref, wu_ref, wd_ref)


def _mix_ffn2_kernel(h_ref, o_ref, wo_ref, g_ref, wg_ref, wu_ref, wd_ref, out_ref):
    mixed = _dot(o_ref[...], wo_ref[...])
    h = h_ref[...] + _rms(mixed, g_ref[3:4, :])
    out_ref[...] = _ffn_body(h, g_ref, 4, 5, wg_ref, wu_ref, wd_ref)


def _ffn1(h, gains, wg, wu, wd):
    rows = h.shape[0]
    tm = _row_tile(rows, 512)
    return pl.pallas_call(
        _ffn1_kernel,
        grid=(rows // tm,),
        in_specs=[
            pl.BlockSpec((tm, D_MODEL), lambda i: (i, 0)),
            _const_spec(gains.shape),
            _const_spec(wg.shape), _const_spec(wu.shape), _const_spec(wd.shape),
        ],
        out_specs=pl.BlockSpec((tm, D_MODEL), lambda i: (i, 0)),
        out_shape=jax.ShapeDtypeStruct((rows, D_MODEL), F32),
        compiler_params=pltpu.CompilerParams(
            dimension_semantics=("arbitrary",), vmem_limit_bytes=VMEM_LIMIT),
        name="ffn1",
    )(h, gains, wg, wu, wd)


def _mix_ffn2(h, o, wo, gains, wg, wu, wd):
    rows = h.shape[0]
    tm = _row_tile(rows, 512)
    return pl.pallas_call(
        _mix_ffn2_kernel,
        grid=(rows // tm,),
        in_specs=[
            pl.BlockSpec((tm, D_MODEL), lambda i: (i, 0)),
            pl.BlockSpec((tm, o.shape[1]), lambda i: (i, 0)),
            _const_spec(wo.shape),
            _const_spec(gains.shape),
            _const_spec(wg.shape), _const_spec(wu.shape), _const_spec(wd.shape),
        ],
        out_specs=pl.BlockSpec((tm, D_MODEL), lambda i: (i, 0)),
        out_shape=jax.ShapeDtypeStruct((rows, D_MODEL), F32),
        compiler_params=pltpu.CompilerParams(
            dimension_semantics=("arbitrary",), vmem_limit_bytes=VMEM_LIMIT),
        name="mix_ffn2",
    )(h, o, wo, gains, wg, wu, wd)


def _proj_dense_kernel(h_ref, g_ref, win_ref, qn_ref, wuq_ref, kvn_ref, wukv_ref,
                       gqa_ref, gqb_ref, gka_ref, gkb_ref, tab_ref, q_ref, k_ref, vt_ref):
    a = _rms(h_ref[...], g_ref[2:3, :]).astype(BF16)
    proj = _dot(a, win_ref[...])
    tab = tab_ref[...]
    cos_k, sin_k = tab[:, 0:LANE], tab[:, LANE:2 * LANE]
    cos_g, sin_g = tab[:, 2 * LANE:3 * LANE], tab[:, 3 * LANE:4 * LANE]
    qs = (MLA_NOPE + MLA_ROPE) ** -0.5 * LOG2E
    lane = lax.broadcasted_iota(jnp.int32, cos_k.shape, 1)
    cos_q = jnp.where(lane < MLA_NOPE, qs, cos_k * qs)
    sin_q = sin_k * qs

    cqn = _rms(proj[:, _C_CQ:_C_CQ + MLA_Q_LORA], qn_ref[...]).astype(BF16)
    qab = _dot(cqn, wuq_ref[...])
    nq = MLA_HEADS * LANE
    for h in range(MLA_HEADS):
        qa = qab[:, h * LANE:(h + 1) * LANE]
        qb = qab[:, nq + h * LANE:nq + (h + 1) * LANE]
        q_ref[:, h * LANE:(h + 1) * LANE] = (qa * cos_q + qb * sin_q).astype(BF16)

    ckvn = _rms(proj[:, _C_CKV:_C_CKV + MLA_KV_LORA], kvn_ref[...]).astype(BF16)
    kv = _dot(ckvn, wukv_ref[...])
    k_rope = (proj[:, _C_KRA:_C_KRA + LANE] * cos_k + proj[:, _C_KRB:_C_KRB + LANE] * sin_k)
    for h in range(MLA_HEADS):
        k_ref[:, h * LANE:(h + 1) * LANE] = (kv[:, h * LANE:(h + 1) * LANE] + k_rope).astype(BF16)
    nv = MLA_HEADS * MLA_V

    gq_scale = GQA_HEAD_DIM ** -0.5 * LOG2E
    cq_g = cos_g * (gqa_ref[...] * gq_scale)
    sq_g = sin_g * (gqb_ref[...] * gq_scale)
    for h in range(GQA_HEADS):
        xa = proj[:, _C_GQA + h * LANE:_C_GQA + (h + 1) * LANE]
        xb = proj[:, _C_GQB + h * LANE:_C_GQB + (h + 1) * LANE]
        r = lax.rsqrt(jnp.sum(xa * xa, axis=-1, keepdims=True) * (1.0 / GQA_HEAD_DIM) + EPS)
        q_ref[:, nq + h * LANE:nq + (h + 1) * LANE] = ((xa * cq_g + xb * sq_g) * r).astype(BF16)
    ck_g = cos_g * gka_ref[...]
    sk_g = sin_g * gkb_ref[...]
    for h in range(GQA_KV_HEADS):
        xa = proj[:, _C_GKA + h * LANE:_C_GKA + (h + 1) * LANE]
        xb = proj[:, _C_GKB + h * LANE:_C_GKB + (h + 1) * LANE]
        r = lax.rsqrt(jnp.sum(xa * xa, axis=-1, keepdims=True) * (1.0 / GQA_HEAD_DIM) + EPS)
        k_ref[:, nq + h * LANE:nq + (h + 1) * LANE] = ((xa * ck_g + xb * sk_g) * r).astype(BF16)
    v = jnp.concatenate([kv[:, nq:nq + nv], proj[:, _C_GV:_C_GV + LANE]], axis=1)
    vt_ref[0] = v.T.astype(BF16)


def _proj_dense(h, gains, w, tab, seq):
    rows = h.shape[0]
    tm = _row_tile(seq if seq else rows, 512)
    nblk = (seq // tm) if seq else 1
    ntab = tab.shape[1]
    consts = [w["w_in"], w["q_norm"], w["w_uq"], w["kv_norm"], w["w_ukv"],
              w["gq_a"], w["gq_b"], w["gk_a"], w["gk_b"]]
    return pl.pallas_call(
        _proj_dense_kernel,
        grid=(rows // tm,),
        in_specs=[pl.BlockSpec((tm, D_MODEL), lambda i: (i, 0)), _const_spec(gains.shape)]
        + [_const_spec(c.shape) for c in consts]
        + [pl.BlockSpec((tm, ntab), lambda i: (i % nblk, 0))],
        out_specs=[
            pl.BlockSpec((tm, HEAD_SLOTS * LANE), lambda i: (i, 0)),
            pl.BlockSpec((tm, K_SLOTS * LANE), lambda i: (i, 0)),
            pl.BlockSpec((1, V_ROWS, tm), lambda i: (i, 0, 0)),
        ],
        out_shape=[
            jax.ShapeDtypeStruct((rows, HEAD_SLOTS * LANE), BF16),
            jax.ShapeDtypeStruct((rows, K_SLOTS * LANE), BF16),
            jax.ShapeDtypeStruct((rows // tm, V_ROWS, tm), BF16),
        ],
        compiler_params=pltpu.CompilerParams(
            dimension_semantics=("arbitrary",), vmem_limit_bytes=VMEM_LIMIT),
        name="proj_dense",
    )(h, gains, *consts, tab)


def _head_slots(h):
    if h < MLA_HEADS:
        return h, h
    kvh = (h - MLA_HEADS) // (GQA_HEADS // GQA_KV_HEADS)
    return MLA_HEADS + kvh, MLA_HEADS + kvh


def _sublane_bcast_max(x):
    return jnp.broadcast_to(jnp.max(x, axis=0, keepdims=True), x.shape)


def _dense_attn_kernel(q_ref, k_ref, vt_ref, km_ref, vmt_ref, o_ref,
                       m_ref, alpha_ref, acc_ref, s_ref, sm_ref):
    kv = pl.program_id(2)
    tq = q_ref.shape[0]
    n_sub, _, tk = vt_ref.shape
    hd = MLA_V
    acc_rows = acc_ref.shape[1]

    def with_ones(vt):
        return jnp.concatenate([vt, jnp.ones((acc_rows - hd, vt.shape[1]), BF16)], axis=0)

    @pl.when(kv == 0)
    def _():
        for h in range(HEAD_SLOTS):
            ks, _ = _head_slots(h)
            q = q_ref[:, h * LANE:(h + 1) * LANE]
            sm_ref[h] = _dot_nt(km_ref[0:N_META, ks * LANE:(ks + 1) * LANE], q)
        zeros = jnp.zeros((LANE - N_META, tq), F32)
        for h in range(HEAD_SLOTS):
            _, vh = _head_slots(h)
            s3 = sm_ref[h].reshape(N_META // SUBLANE, SUBLANE, tq)
            m = _sublane_bcast_max(jnp.max(s3, axis=0))
            p3 = jnp.exp2(s3 - m[None])
            m_ref[h] = m
            p = jnp.concatenate([p3.reshape(N_META, tq), zeros], axis=0).astype(BF16)
            acc_ref[h] = _dot(with_ones(vmt_ref[vh * hd:(vh + 1) * hd, :]), p)

    n_slots = s_ref.shape[0]

    def load_scores(h):
        return s_ref[h % n_slots].reshape(tk // SUBLANE, SUBLANE, tq)

    def scores(t, h):
        ks, _ = _head_slots(h)
        k0 = pl.multiple_of(t * tk, tk)
        k = k_ref[pl.ds(k0, tk), ks * LANE:(ks + 1) * LANE]
        s_ref[h % n_slots] = _dot_nt(k, q_ref[:, h * LANE:(h + 1) * LANE])

    def max_update(h, valid=None):
        m_prev = m_ref[h]
        m_new = jnp.maximum(m_prev, _sublane_bcast_max(jnp.max(load_scores(h), axis=0)))
        if valid is not None:
            m_new = jnp.where(valid, m_new, m_prev)
        alpha_ref[h] = jnp.exp2(m_prev - m_new)
        m_ref[h] = m_new

    def exp_pv(t, h):
        _, vh = _head_slots(h)
        p = jnp.exp2(load_scores(h) - m_ref[h][None]).reshape(tk, tq).astype(BF16)
        pv = _dot(with_ones(vt_ref[t, vh * hd:(vh + 1) * hd, :]), p)
        acc = acc_ref[h].reshape(acc_rows // SUBLANE, SUBLANE, tq) * alpha_ref[h][None]
        acc_ref[h] = acc.reshape(acc_rows, tq) + pv

    scores(0, 0)
    scores(0, 1)
    max_update(0)

    def sub_tile(t, carry):
        t_next = jnp.minimum(t + 1, n_sub - 1)
        for h in range(HEAD_SLOTS):
            if h + 2 < HEAD_SLOTS:
                scores(t, h + 2)
            else:
                scores(t_next, h + 2 - HEAD_SLOTS)
            if h + 1 < HEAD_SLOTS:
                max_update(h + 1)
            else:
                max_update(0, valid=t + 1 < n_sub)
            exp_pv(t, h)
        return carry

    lax.fori_loop(0, n_sub, sub_tile, 0, unroll=2)

    @pl.when(kv == pl.num_programs(2) - 1)
    def _():
        for j in range(HEAD_SLOTS // 2):
            outs = []
            for h in (2 * j, 2 * j + 1):
                outs.append(acc_ref[h, 0:hd, :] / acc_ref[h, hd:hd + 1, :])
            o_t = jnp.concatenate(outs, axis=0)
            o_ref[:, j * LANE:(j + 1) * LANE] = o_t.T.astype(BF16)


def _dense_attn(q, k, vt, km, vmt, *, n_seq, seq, tq, q_base, meta_base):
    nq = (q.shape[0] - q_base) // (n_seq * tq)
    tk = vt.shape[2]
    n_sub = _row_tile(seq // tk, 8)
    nk = seq // (tk * n_sub)
    qb0 = q_base // tq
    out_rows = n_seq * nq * tq
    return pl.pallas_call(
        _dense_attn_kernel,
        grid=(n_seq, nq, nk),
        in_specs=[
            pl.BlockSpec((tq, HEAD_SLOTS * LANE), lambda b, i, j: (qb0 + b * nq + i, 0)),
            pl.BlockSpec((n_sub * tk, K_SLOTS * LANE), lambda b, i, j: (b * nk + j, 0)),
            pl.BlockSpec((n_sub, V_ROWS, tk), lambda b, i, j: (b * nk + j, 0, 0)),
            pl.BlockSpec((None, LANE, K_SLOTS * LANE), lambda b, i, j: (meta_base + b, 0, 0)),
            pl.BlockSpec((None, V_ROWS, LANE), lambda b, i, j: (meta_base + b, 0, 0)),
        ],
        out_specs=pl.BlockSpec((tq, D_MODEL), lambda b, i, j: (b * nq + i, 0)),
        out_shape=jax.ShapeDtypeStruct((out_rows, D_MODEL), BF16),
        scratch_shapes=[
            pltpu.VMEM((HEAD_SLOTS, SUBLANE, tq), F32),
            pltpu.VMEM((HEAD_SLOTS, SUBLANE, tq), F32),
            pltpu.VMEM((HEAD_SLOTS, MLA_V + 2 * SUBLANE, tq), F32),
            pltpu.VMEM((DENSE_SLOTS, tk, tq), F32),
            pltpu.VMEM((HEAD_SLOTS, N_META, tq), F32),
        ],
        compiler_params=pltpu.CompilerParams(
            dimension_semantics=("arbitrary", "arbitrary", "arbitrary"),
            vmem_limit_bytes=VMEM_LIMIT),
        name="dense_attn",
    )(q, k, vt, km, vmt)


def _proj_na_kernel(h_ref, g_ref, w_ref, q_ref, k_ref, v_ref, *, transpose_v):
    a = _rms(h_ref[...], g_ref[2:3, :]).astype(BF16)
    qkv = _dot(a, w_ref[...])
    n = NA_HEADS * NA_HEAD_DIM
    q_ref[...] = (qkv[:, 0:n] * (NA_HEAD_DIM ** -0.5 * LOG2E)).astype(BF16)
    k_ref[...] = qkv[:, n:2 * n].astype(BF16)
    v = qkv[:, 2 * n:3 * n]
    if transpose_v:
        vt = v.T.astype(BF16)
        for t in range(v_ref.shape[0]):
            v_ref[t] = vt[:, t * LANE:(t + 1) * LANE]
    else:
        v_ref[...] = v.astype(BF16)


def _proj_na(h, gains, w, transpose_v):
    rows = h.shape[0]
    tm = _row_tile(rows, 512)
    n = NA_HEADS * NA_HEAD_DIM
    if transpose_v:
        v_spec = pl.BlockSpec((tm // LANE, n, LANE), lambda i: (i, 0, 0))
        v_shape = jax.ShapeDtypeStruct((rows // LANE, n, LANE), BF16)
    else:
        v_spec = pl.BlockSpec((tm, n), lambda i: (i, 0))
        v_shape = jax.ShapeDtypeStruct((rows, n), BF16)
    return pl.pallas_call(
        functools.partial(_proj_na_kernel, transpose_v=transpose_v),
        grid=(rows // tm,),
        in_specs=[pl.BlockSpec((tm, D_MODEL), lambda i: (i, 0)), _const_spec(gains.shape),
                  _const_spec(w.shape)],
        out_specs=[pl.BlockSpec((tm, n), lambda i: (i, 0))] * 2 + [v_spec],
        out_shape=[jax.ShapeDtypeStruct((rows, n), BF16)] * 2 + [v_shape],
        compiler_params=pltpu.CompilerParams(
            dimension_semantics=("arbitrary",), vmem_limit_bytes=VMEM_LIMIT),
        name="proj_na",
    )(h, gains, w)


NA_SPAN_R = NA_WIN_R + 2
NA_MASKED = 2 * NA_WIN_R - 1


def _na_kernel(q_ref, k_ref, vt_ref, km_ref, vmt_ref, bias_ref, mb_ref, o_ref, s_ref,
               *, rows, rows_per_step):
    step = pl.program_id(1)
    n_pairs = rows_per_step // 2
    n_hp = NA_HEADS // 2
    span = NA_SPAN_R * GRID_W
    lane = lax.broadcasted_iota(jnp.int32, (GRID_W, LANE), 1)
    first = lane < (LANE // 2)
    zeros_m = jnp.zeros((LANE - N_META, 2 * LANE), F32)

    def geometry(rp):
        ra = step * rows_per_step + 2 * rp
        rs = [jnp.clip(ra + x - NA_WIN_R // 2, 0, rows - NA_WIN_R) for x in range(2)]
        ws = jnp.minimum((rs[0] // 2) * 2, rows - NA_SPAN_R)
        return ra, rs, ws

    n_slots = s_ref.shape[0]

    def scores(rp, hp):
        slot = hp % n_slots
        _, _, ws = geometry(rp)
        cols = slice(hp * LANE, (hp + 1) * LANE)
        parts = []
        for x in range(2):
            q0 = pl.multiple_of((2 * rp + x) * GRID_W, GRID_W)
            qx = q_ref[pl.ds(q0, GRID_W), cols]
            parts += [jnp.where(first, qx, jnp.zeros_like(qx)), jnp.where(first, jnp.zeros_like(qx), qx)]
        qblk = jnp.concatenate(parts, axis=0)
        k0 = pl.multiple_of(ws * GRID_W, 2 * GRID_W)
        s_ref[slot, 0:span, :] = _dot_nt(k_ref[pl.ds(k0, span), cols], qblk)
        s_ref[slot, span:span + N_META, :] = _dot_nt(km_ref[:, cols], qblk)

    for hp in range(NA_LOOKAHEAD):
        scores(0, hp)

    def row_pair(rp, carry):
        ra, rs, ws = geometry(rp)
        idx = []
        for jj in range(NA_SPAN_R):
            kr = ws + jj
            idx.append([jnp.where((kr >= rs[x]) & (kr < rs[x] + NA_WIN_R),
                                  kr - (ra + x) + NA_WIN_R - 1, NA_MASKED) for x in range(2)])
        t0 = ws // 2
        for hp in range(n_hp):
            slot = hp % n_slots
            ahead = hp + NA_LOOKAHEAD
            if ahead < n_hp:
                scores(rp, ahead)
            else:
                scores(jnp.minimum(rp + 1, n_pairs - 1), ahead - n_hp)
            cols = slice(hp * LANE, (hp + 1) * LANE)
            b = jnp.concatenate(
                [jnp.concatenate([bias_ref[hp, idx[jj][0]], bias_ref[hp, idx[jj][1]]], axis=1)
                 for jj in range(NA_SPAN_R)], axis=0)
            s = s_ref[slot, 0:span, :] + b
            mb = mb_ref[hp]
            sm = s_ref[slot, span:span + N_META, :] + jnp.concatenate([mb, mb], axis=1)
            s3 = s.reshape(span // SUBLANE, SUBLANE, 2 * LANE)
            sm3 = sm.reshape(N_META // SUBLANE, SUBLANE, 2 * LANE)
            m = _sublane_bcast_max(jnp.maximum(jnp.max(s3, axis=0), jnp.max(sm3, axis=0)))
            p3 = jnp.exp2(s3 - m[None])
            pm3 = jnp.exp2(sm3 - m[None])
            l = jnp.sum(jnp.sum(p3, axis=0) + jnp.sum(pm3, axis=0), axis=0, keepdims=True)
            p = p3.reshape(span, 2 * LANE).astype(BF16)
            pm = jnp.concatenate([pm3.reshape(N_META, 2 * LANE), zeros_m], axis=0).astype(BF16)
            v_all = jnp.concatenate([vt_ref[t0 + t, cols, :] for t in range(span // LANE)]
                                    + [vmt_ref[cols, :]], axis=1)
            o_t = _dot(v_all, jnp.concatenate([p, pm], axis=0)) / l
            for x in range(2):
                blk = o_t[:, x * LANE:(x + 1) * LANE].T
                q0 = pl.multiple_of((2 * rp + x) * GRID_W, GRID_W)
                o_ref[pl.ds(q0, GRID_W), cols] = jnp.where(
                    first, blk[0:GRID_W], blk[GRID_W:2 * GRID_W]).astype(BF16)
        return carry

    lax.fori_loop(0, n_pairs, row_pair, 0, unroll=2)


def _na_attn(q, k, vt, km, vmt, bias, mb, *, n_seq, seq, meta_base):
    rows = seq // GRID_W
    assert rows >= NA_SPAN_R and rows % 2 == 0
    rps = 8
    nsteps = rows // rps
    n = NA_HEADS * NA_HEAD_DIM
    span = NA_SPAN_R * GRID_W
    return pl.pallas_call(
        functools.partial(_na_kernel, rows=rows, rows_per_step=rps),
        grid=(n_seq, nsteps),
        in_specs=[
            pl.BlockSpec((rps * GRID_W, n), lambda b, i: (b * nsteps + i, 0)),
            pl.BlockSpec((seq, n), lambda b, i: (b, 0), pipeline_mode=pl.Buffered(1)),
            pl.BlockSpec((seq // LANE, n, LANE), lambda b, i: (b, 0, 0), pipeline_mode=pl.Buffered(1)),
            pl.BlockSpec((None, N_META, n), lambda b, i: (meta_base + b, 0, 0)),
            pl.BlockSpec((None, n, LANE), lambda b, i: (meta_base + b, 0, 0)),
            _const_spec(bias.shape),
            _const_spec(mb.shape),
        ],
        out_specs=pl.BlockSpec((rps * GRID_W, n), lambda b, i: (b * nsteps + i, 0)),
        out_shape=jax.ShapeDtypeStruct((n_seq * seq, n), BF16),
        scratch_shapes=[pltpu.VMEM((NA_SLOTS, span + N_META, 2 * LANE), F32)],
        compiler_params=pltpu.CompilerParams(
            dimension_semantics=("arbitrary", "arbitrary"), vmem_limit_bytes=VMEM_LIMIT),
        name="na_attn",
    )(q, k, vt, km, vmt, bias, mb)


def _na_meta_kernel(q_ref, km_ref, vm_ref, mb_ref, o_ref):
    lane = lax.broadcasted_iota(jnp.int32, (N_META, LANE), 1)
    first = lane < (LANE // 2)
    for j in range(NA_HEADS // 2):
        cols = slice(j * LANE, (j + 1) * LANE)
        qp = q_ref[:, cols]
        km = km_ref[:, cols]
        vm = vm_ref[:, cols]
        outs = []
        for half in range(2):
            h = 2 * j + half
            qh = jnp.where(first if half == 0 else jnp.logical_not(first), qp, jnp.zeros_like(qp))
            sm = _dot_nt(qh, km)
            sm = jnp.where(lane < N_META, sm + mb_ref[h:h + 1, :], NEG_INF)
            m = jnp.max(sm, axis=-1, keepdims=True)
            pm = jnp.exp2(sm - m)
            l = jnp.sum(pm, axis=-1, keepdims=True)
            outs.append(_dot(pm.astype(BF16), vm) / l)
        o_ref[:, cols] = jnp.where(first, outs[0], outs[1]).astype(BF16)


def _na_meta(qm, km, vm, mb):
    n_seq = km.shape[0]
    n = NA_HEADS * NA_HEAD_DIM
    return pl.pallas_call(
        _na_meta_kernel,
        grid=(n_seq,),
        in_specs=[
            pl.BlockSpec((N_META, n), lambda b: (b, 0)),
            pl.BlockSpec((None, LANE, n), lambda b: (b, 0, 0)),
            pl.BlockSpec((None, LANE, n), lambda b: (b, 0, 0)),
            _const_spec(mb.shape),
        ],
        out_specs=pl.BlockSpec((N_META, n), lambda b: (b, 0)),
        out_shape=jax.ShapeDtypeStruct((n_seq * N_META, n), BF16),
        compiler_params=pltpu.CompilerParams(dimension_semantics=("arbitrary",)),
        name="na_meta",
    )(qm, km, vm, mb)


def _take_cols(w, idx):
    idx = np.asarray(idx)
    cols = jnp.take(w, jnp.asarray(np.maximum(idx, 0)), axis=1)
    return jnp.where(jnp.asarray(idx >= 0)[None, :], cols, 0.0)


def _swap_halves(n):
    half = n // 2
    return np.concatenate([np.arange(half, n), np.arange(0, half)])


def _dense_weights(w_in, q_norm, w_uq, kv_norm, w_ukv, gq_norm, gk_norm, w_out):
    pad = lambda k: -np.ones(k, np.int64)
    o_kr = MLA_Q_LORA + MLA_KV_LORA
    o_gq = o_kr + MLA_ROPE
    o_gk = o_gq + GQA_HEADS * GQA_HEAD_DIM
    o_gv = o_gk + GQA_KV_HEADS * GQA_HEAD_DIM
    axial = np.concatenate([_swap_halves(GQA_HEAD_DIM // 2),
                            GQA_HEAD_DIM // 2 + _swap_halves(GQA_HEAD_DIM // 2)])
    idx = [np.arange(0, o_kr)]
    idx += [pad(MLA_NOPE), o_kr + np.arange(MLA_ROPE), pad(LANE - MLA_NOPE - MLA_ROPE)]
    idx += [pad(MLA_NOPE), o_kr + _swap_halves(MLA_ROPE), pad(LANE - MLA_NOPE - MLA_ROPE)]
    for h in range(GQA_HEADS):
        idx += [o_gq + h * GQA_HEAD_DIM + np.arange(GQA_HEAD_DIM), pad(LANE - GQA_HEAD_DIM)]
    for h in range(GQA_HEADS):
        idx += [o_gq + h * GQA_HEAD_DIM + axial, pad(LANE - GQA_HEAD_DIM)]
    for h in range(GQA_KV_HEADS):
        idx += [o_gk + h * GQA_HEAD_DIM + np.arange(GQA_HEAD_DIM), pad(LANE - GQA_HEAD_DIM)]
    for h in range(GQA_KV_HEADS):
        idx += [o_gk + h * GQA_HEAD_DIM + axial, pad(LANE - GQA_HEAD_DIM)]
    idx += [o_gv + np.arange(GQA_KV_HEADS * GQA_HEAD_DIM)]
    idx = np.concatenate(idx)
    assert idx.shape[0] == _C_END
    w_in2 = _take_cols(w_in, idx).astype(BF16)

    hd = MLA_NOPE + MLA_ROPE
    ia, ib = [], []
    for h in range(MLA_HEADS):
        ia += [h * hd + np.arange(hd), pad(LANE - hd)]
        ib += [pad(MLA_NOPE), h * hd + MLA_NOPE + _swap_halves(MLA_ROPE), pad(LANE - hd)]
    w_uq2 = _take_cols(w_uq, np.concatenate(ia + ib)).astype(BF16)

    kvd = MLA_NOPE + MLA_V
    ik, iv = [], []
    for h in range(MLA_HEADS):
        ik += [h * kvd + np.arange(MLA_NOPE), pad(LANE - MLA_NOPE)]
        iv += [h * kvd + MLA_NOPE + np.arange(MLA_V)]
    w_ukv2 = _take_cols(w_ukv, np.concatenate(ik + iv)).astype(BF16)

    def gain_pair(g):
        ga = jnp.concatenate([g, jnp.zeros((LANE - GQA_HEAD_DIM,), F32)])[None, :]
        gb = jnp.concatenate([g[jnp.asarray(axial)], jnp.zeros((LANE - GQA_HEAD_DIM,), F32)])[None, :]
        return ga, gb

    gq_a, gq_b = gain_pair(gq_norm)
    gk_a, gk_b = gain_pair(gk_norm)

    w_out2 = w_out.astype(BF16)

    return dict(w_in=w_in2, q_norm=q_norm[None, :], w_uq=w_uq2, kv_norm=kv_norm[None, :],
                w_ukv=w_ukv2, gq_a=gq_a, gq_b=gq_b, gk_a=gk_a, gk_b=gk_b), w_out2


def _rope_tables(pos, row, col):
    half = MLA_ROPE // 2
    inv = 1.0 / (ROPE_THETA ** (jnp.arange(half, dtype=F32) / half))
    n = pos.shape[0]

    def cs(p):
        ang = p.astype(F32)[:, None] * inv[None, :]
        return jnp.cos(ang), jnp.sin(ang)

    c1, s1 = cs(pos)
    cr, sr = cs(row)
    cc, sc = cs(col)
    z = lambda k: jnp.zeros((n, k), F32)
    tail = LANE - MLA_NOPE - MLA_ROPE
    cos_k = [z(MLA_NOPE), c1, c1, z(tail)]
    sin_k = [z(MLA_NOPE), -s1, s1, z(tail)]
    cos_g = [cr, cr, cc, cc, z(LANE - GQA_HEAD_DIM)]
    sin_g = [-sr, sr, -sc, sc, z(LANE - GQA_HEAD_DIM)]
    return jnp.concatenate(cos_k + sin_k + cos_g + sin_g, axis=1)


def _na_bias_tables(rpb, meta_bias):
    c_idx = np.arange(GRID_W)
    c_start = np.clip(c_idx - NA_WIN_C // 2, 0, GRID_W - NA_WIN_C)
    col_mask = (c_idx[None, :] >= c_start[:, None]) & (c_idx[None, :] < c_start[:, None] + NA_WIN_C)
    col_off = np.clip(c_idx[None, :] - c_idx[:, None] + NA_WIN_C - 1, 0, 2 * NA_WIN_C - 2)
    t = rpb[:, :, jnp.asarray(col_off)] * LOG2E
    t = jnp.where(jnp.asarray(col_mask)[None, None], t, NEG_INF)
    t = jnp.concatenate([t, jnp.full_like(t[:, :1], NEG_INF)], axis=1)
    t = t.transpose(0, 1, 3, 2)
    hp = NA_HEADS // 2
    bias = t.reshape(hp, 2, NA_MASKED + 1, GRID_W, GRID_W).transpose(0, 2, 3, 1, 4)
    bias = bias.reshape(hp, NA_MASKED + 1, GRID_W, LANE)
    mbl = meta_bias * LOG2E
    mb_t = jnp.repeat(mbl.reshape(hp, 2, N_META).transpose(0, 2, 1), GRID_W, axis=2)
    mb = jnp.pad(mbl, ((0, 0), (0, LANE - N_META)))
    return bias, mb_t, mb


def _pad_meta(x, n_seq):
    c = x.shape[1]
    return jnp.pad(x.reshape(n_seq, N_META, c), ((0, 0), (0, LANE - N_META), (0, 0)))


def kernel(x_prompt, x_sample, meta, norm_gains, ffn1_w_gate, ffn1_w_up, ffn1_w_down, ffn2_w_gate, ffn2_w_up, ffn2_w_down, attn_w_in, mla_q_norm, mla_w_uq, mla_kv_norm, mla_w_ukv, gqa_q_norm, gqa_k_norm, attn_w_out, na_w_qkv, na_rpb, na_meta_bias, na_w_out):
    bp, sp, _ = x_prompt.shape
    bs, ss, _ = x_sample.shape
    n_seq = bp + bs
    depth = norm_gains.shape[0]
    groups = [(bp, sp, 0), (bs, ss, bp)]

    n_meta = n_seq * N_META
    meta_rows = -(-n_meta // LANE) * LANE
    pad_rows = lambda x: jnp.pad(x, ((0, meta_rows - x.shape[0]), (0, 0)))
    h_tok = [x_prompt.reshape(bp * sp, D_MODEL), x_sample.reshape(bs * ss, D_MODEL)]
    h_meta = pad_rows(jnp.tile(meta.astype(F32), (n_seq, 1)))

    smax = max(sp, ss)
    t = jnp.arange(smax)
    tab_tok = _rope_tables(t + N_META, t // GRID_W, t % GRID_W)
    mi = jnp.arange(meta_rows) % N_META
    tab_meta = _rope_tables(mi, jnp.full_like(mi, -1), mi)

    for i in range(depth):
        gains = jnp.pad(norm_gains[i], ((0, 2), (0, 0)))
        w1 = (ffn1_w_gate[i].astype(BF16), ffn1_w_up[i].astype(BF16), ffn1_w_down[i].astype(BF16))
        w2 = (ffn2_w_gate[i].astype(BF16), ffn2_w_up[i].astype(BF16), ffn2_w_down[i].astype(BF16))
        j = i // 2
        h_tok = [_ffn1(h, gains, *w1) for h in h_tok]
        h_meta = _ffn1(h_meta, gains, *w1)
        if i % 2 == 0:
            w, w_out = _dense_weights(attn_w_in[j], mla_q_norm[j], mla_w_uq[j], mla_kv_norm[j],
                                      mla_w_ukv[j], gqa_q_norm[j], gqa_k_norm[j], attn_w_out[j])
            qkv_tok = [_proj_dense(h, gains, w, tab_tok, s) for h, (_, s, _) in zip(h_tok, groups)]
            qm, km, vmt = _proj_dense(h_meta, gains, w, tab_meta, 0)
            kmp = _pad_meta(km[:n_meta], n_seq)
            vmt = vmt.transpose(1, 0, 2).reshape(V_ROWS, meta_rows)
            vmtp = vmt[:, :n_meta].reshape(V_ROWS, n_seq, N_META).transpose(1, 0, 2)
            vmtp = jnp.pad(vmtp, ((0, 0), (0, 0), (0, LANE - N_META)))
            qmp = _pad_meta(qm[:n_meta], n_seq).reshape(n_seq * LANE, HEAD_SLOTS * LANE)
            o_tok, o_meta = [], []
            for (q, k, vt), (nb, s, b0) in zip(qkv_tok, groups):
                o_tok.append(_dense_attn(q, k, vt, kmp, vmtp, n_seq=nb, seq=s,
                                         tq=_row_tile(s, 512), q_base=0, meta_base=b0))
                om = _dense_attn(qmp, k, vt, kmp, vmtp, n_seq=nb, seq=s,
                                 tq=LANE, q_base=b0 * LANE, meta_base=b0)
                o_meta.append(om.reshape(nb, LANE, D_MODEL)[:, :N_META].reshape(nb * N_META, D_MODEL))
            o_meta = pad_rows(jnp.concatenate(o_meta, axis=0))
        else:
            w_qkv = na_w_qkv[j].astype(BF16)
            w_out = na_w_out[j].astype(BF16)
            bias, mb_t, mb = _na_bias_tables(na_rpb[j], na_meta_bias[j])
            qkv_tok = [_proj_na(h, gains, w_qkv, True) for h in h_tok]
            qm, km, vm = _proj_na(h_meta, gains, w_qkv, False)
            kmp, vmp = _pad_meta(km[:n_meta], n_seq), _pad_meta(vm[:n_meta], n_seq)
            km16 = km[:n_meta].reshape(n_seq, N_META, NA_HEADS * NA_HEAD_DIM)
            vmtp = vmp.transpose(0, 2, 1)
            o_tok = [_na_attn(q, k, vt, km16, vmtp, bias, mb_t, n_seq=nb, seq=s, meta_base=b0)
                     for (q, k, vt), (nb, s, b0) in zip(qkv_tok, groups)]
            o_meta = pad_rows(_na_meta(qm[:n_meta], kmp, vmp, mb))
        h_tok = [_mix_ffn2(h, o, w_out, gains, *w2) for h, o in zip(h_tok, o_tok)]
        h_meta = _mix_ffn2(h_meta, o_meta, w_out, gains, *w2)

    return (h_tok[0].reshape(bp, sp, D_MODEL), h_tok[1].reshape(bs, ss, D_MODEL))
```

```python
import functools
import math

import jax
import jax.numpy as jnp
import numpy as np
from jax import lax
from jax.experimental import pallas as pl
from jax.experimental.pallas import tpu as pltpu

F32 = jnp.float32
BF16 = jnp.bfloat16

D_MODEL = 1024
N_META = 16
GRID_W = 64
D_FF = 2816
EPS = 1e-6
NEG_INF = -1e30
LOG2E = math.log2(math.e)

MLA_HEADS = 8
MLA_Q_LORA = 256
MLA_KV_LORA = 128
MLA_NOPE = 64
MLA_ROPE = 32
MLA_V = 64
GQA_HEADS = 8
GQA_KV_HEADS = 2
GQA_HEAD_DIM = 64
ROPE_THETA = 10000.0
NA_HEADS = 16
NA_HEAD_DIM = 64
NA_WIN_R = 8
NA_WIN_C = 16

LANE = 128
HEAD_SLOTS = MLA_HEADS + GQA_HEADS
K_SLOTS = MLA_HEADS + GQA_KV_HEADS
V_ROWS = (MLA_HEADS + GQA_KV_HEADS) * MLA_V
SUBLANE = 8
DENSE_LOOKAHEAD, DENSE_SLOTS = 2, 4
NA_LOOKAHEAD, NA_SLOTS = 2, 4
VMEM_LIMIT = 56 * 1024 * 1024

_C_CQ = 0
_C_CKV = _C_CQ + MLA_Q_LORA
_C_KRA = _C_CKV + MLA_KV_LORA
_C_KRB = _C_KRA + LANE
_C_GQA = _C_KRB + LANE
_C_GQB = _C_GQA + GQA_HEADS * LANE
_C_GKA = _C_GQB + GQA_HEADS * LANE
_C_GKB = _C_GKA + GQA_KV_HEADS * LANE
_C_GV = _C_GKB + GQA_KV_HEADS * LANE
_C_END = _C_GV + LANE


def _const_spec(shape):
    nd = len(shape)
    return pl.BlockSpec(shape, lambda *_: (0,) * nd, pipeline_mode=pl.Buffered(1))


def _rms(x, g):
    ms = jnp.mean(x * x, axis=-1, keepdims=True)
    return x * lax.rsqrt(ms + EPS) * g


def _dot(a, b):
    return jnp.dot(a, b, preferred_element_type=F32)


def _dot_nt(a, b):
    return lax.dot_general(a, b, (((1,), (1,)), ((), ())), preferred_element_type=F32)


def _row_tile(rows, want):
    t = min(rows, want)
    while rows % t:
        t //= 2
    return t


def _ffn_body(h, g_ref, pre, post, wg_ref, wu_ref, wd_ref):
    xn = _rms(h, g_ref[pre:pre + 1, :]).astype(BF16)
    gate = _dot(xn, wg_ref[...])
    up = _dot(xn, wu_ref[...])
    act = (gate * jax.nn.sigmoid(gate) * up).astype(BF16)
    y = _dot(act, wd_ref[...])
    return h + 0.5 * _rms(y, g_ref[post:post + 1, :])


def _ffn1_kernel(h_ref, g_ref, wg_ref, wu_ref, wd_ref, out_ref):
    out_ref[...] = _ffn_body(h_ref[...], g_ref, 0, 1, wg_ref, wu_ref, wd_ref)


def _mix_ffn2_kernel(h_ref, o_ref, wo_ref, g_ref, wg_ref, wu_ref, wd_ref, out_ref):
    mixed = _dot(o_ref[...], wo_ref[...])
    h = h_ref[...] + _rms(mixed, g_ref[3:4, :])
    out_ref[...] = _ffn_body(h, g_ref, 4, 5, wg_ref, wu_ref, wd_ref)


def _layer_spec(w, layer):
    return pl.BlockSpec((None,) + w.shape[1:], lambda *_: (layer, 0, 0), pipeline_mode=pl.Buffered(1))


def _ffn1(h, gains, wg, wu, wd, layer):
    rows = h.shape[0]
    tm = _row_tile(rows, 512)
    return pl.pallas_call(
        _ffn1_kernel,
        grid=(rows // tm,),
        in_specs=[
            pl.BlockSpec((tm, D_MODEL), lambda i: (i, 0)),
            _const_spec(gains.shape),
            _layer_spec(wg, layer), _layer_spec(wu, layer), _layer_spec(wd, layer),
        ],
        out_specs=pl.BlockSpec((tm, D_MODEL), lambda i: (i, 0)),
        out_shape=jax.ShapeDtypeStruct((rows, D_MODEL), F32),
        compiler_params=pltpu.CompilerParams(
            dimension_semantics=("arbitrary",), vmem_limit_bytes=VMEM_LIMIT),
        name="ffn1",
    )(h, gains, wg, wu, wd)


def _mix_ffn2(h, o, wo, gains, wg, wu, wd, layer):
    rows = h.shape[0]
    tm = _row_tile(rows, 512)
    return pl.pallas_call(
        _mix_ffn2_kernel,
        grid=(rows // tm,),
        in_specs=[
            pl.BlockSpec((tm, D_MODEL), lambda i: (i, 0)),
            pl.BlockSpec((tm, o.shape[1]), lambda i: (i, 0)),
            _const_spec(wo.shape),
            _const_spec(gains.shape),
            _layer_spec(wg, layer), _layer_spec(wu, layer), _layer_spec(wd, layer),
        ],
        out_specs=pl.BlockSpec((tm, D_MODEL), lambda i: (i, 0)),
        out_shape=jax.ShapeDtypeStruct((rows, D_MODEL), F32),
        compiler_params=pltpu.CompilerParams(
            dimension_semantics=("arbitrary",), vmem_limit_bytes=VMEM_LIMIT),
        name="mix_ffn2",
    )(h, o, wo, gains, wg, wu, wd)


def _proj_dense_kernel(h_ref, g_ref, win_ref, qn_ref, wuq_ref, kvn_ref, wukv_ref,
                       gqa_ref, gqb_ref, gka_ref, gkb_ref, tab_ref, q_ref, k_ref, vt_ref):
    a = _rms(h_ref[...], g_ref[2:3, :]).astype(BF16)
    proj = _dot(a, win_ref[...])
    tab = tab_ref[...]
    cos_k, sin_k = tab[:, 0:LANE], tab[:, LANE:2 * LANE]
    cos_g, sin_g = tab[:, 2 * LANE:3 * LANE], tab[:, 3 * LANE:4 * LANE]
    qs = (MLA_NOPE + MLA_ROPE) ** -0.5 * LOG2E
    lane = lax.broadcasted_iota(jnp.int32, cos_k.shape, 1)
    cos_q = jnp.where(lane < MLA_NOPE, qs, cos_k * qs)
    sin_q = sin_k * qs

    cqn = _rms(proj[:, _C_CQ:_C_CQ + MLA_Q_LORA], qn_ref[...]).astype(BF16)
    qab = _dot(cqn, wuq_ref[...])
    nq = MLA_HEADS * LANE
    for h in range(MLA_HEADS):
        qa = qab[:, h * LANE:(h + 1) * LANE]
        qb = qab[:, nq + h * LANE:nq + (h + 1) * LANE]
        q_ref[:, h * LANE:(h + 1) * LANE] = (qa * cos_q + qb * sin_q).astype(BF16)

    ckvn = _rms(proj[:, _C_CKV:_C_CKV + MLA_KV_LORA], kvn_ref[...]).astype(BF16)
    kv = _dot(ckvn, wukv_ref[...])
    k_rope = (proj[:, _C_KRA:_C_KRA + LANE] * cos_k + proj[:, _C_KRB:_C_KRB + LANE] * sin_k)
    for h in range(MLA_HEADS):
        k_ref[:, h * LANE:(h + 1) * LANE] = (kv[:, h * LANE:(h + 1) * LANE] + k_rope).astype(BF16)
    nv = MLA_HEADS * MLA_V

    gq_scale = GQA_HEAD_DIM ** -0.5 * LOG2E
    cq_g = cos_g * (gqa_ref[...] * gq_scale)
    sq_g = sin_g * (gqb_ref[...] * gq_scale)
    for h in range(GQA_HEADS):
        xa = proj[:, _C_GQA + h * LANE:_C_GQA + (h + 1) * LANE]
        xb = proj[:, _C_GQB + h * LANE:_C_GQB + (h + 1) * LANE]
        r = lax.rsqrt(jnp.sum(xa * xa, axis=-1, keepdims=True) * (1.0 / GQA_HEAD_DIM) + EPS)
        q_ref[:, nq + h * LANE:nq + (h + 1) * LANE] = ((xa * cq_g + xb * sq_g) * r).astype(BF16)
    ck_g = cos_g * gka_ref[...]
    sk_g = sin_g * gkb_ref[...]
    for h in range(GQA_KV_HEADS):
        xa = proj[:, _C_GKA + h * LANE:_C_GKA + (h + 1) * LANE]
        xb = proj[:, _C_GKB + h * LANE:_C_GKB + (h + 1) * LANE]
        r = lax.rsqrt(jnp.sum(xa * xa, axis=-1, keepdims=True) * (1.0 / GQA_HEAD_DIM) + EPS)
        k_ref[:, nq + h * LANE:nq + (h + 1) * LANE] = ((xa * ck_g + xb * sk_g) * r).astype(BF16)
    v = jnp.concatenate([kv[:, nq:nq + nv], proj[:, _C_GV:_C_GV + LANE]], axis=1)
    vt_ref[0] = v.T.astype(BF16)


def _proj_dense(h, gains, w, tab, seq):
    rows = h.shape[0]
    tm = _row_tile(seq if seq else rows, 512)
    nblk = (seq // tm) if seq else 1
    ntab = tab.shape[1]
    consts = [w["w_in"], w["q_norm"], w["w_uq"], w["kv_norm"], w["w_ukv"],
              w["gq_a"], w["gq_b"], w["gk_a"], w["gk_b"]]
    return pl.pallas_call(
        _proj_dense_kernel,
        grid=(rows // tm,),
        in_specs=[pl.BlockSpec((tm, D_MODEL), lambda i: (i, 0)), _const_spec(gains.shape)]
        + [_const_spec(c.shape) for c in consts]
        + [pl.BlockSpec((tm, ntab), lambda i: (i % nblk, 0))],
        out_specs=[
            pl.BlockSpec((tm, HEAD_SLOTS * LANE), lambda i: (i, 0)),
            pl.BlockSpec((tm, K_SLOTS * LANE), lambda i: (i, 0)),
            pl.BlockSpec((1, V_ROWS, tm), lambda i: (i, 0, 0)),
        ],
        out_shape=[
            jax.ShapeDtypeStruct((rows, HEAD_SLOTS * LANE), BF16),
            jax.ShapeDtypeStruct((rows, K_SLOTS * LANE), BF16),
            jax.ShapeDtypeStruct((rows // tm, V_ROWS, tm), BF16),
        ],
        compiler_params=pltpu.CompilerParams(
            dimension_semantics=("arbitrary",), vmem_limit_bytes=VMEM_LIMIT),
        name="proj_dense",
    )(h, gains, *consts, tab)


def _head_slots(h):
    if h < MLA_HEADS:
        return h, h
    kvh = (h - MLA_HEADS) // (GQA_HEADS // GQA_KV_HEADS)
    return MLA_HEADS + kvh, MLA_HEADS + kvh


def _sublane_bcast_max(x):
    return jnp.broadcast_to(jnp.max(x, axis=0, keepdims=True), x.shape)


def _dense_attn_kernel(q_ref, k_ref, vt_ref, km_ref, vmt_ref, o_ref,
                       m_ref, smax_ref, acc_ref, s_ref, sm_ref):
    kv = pl.program_id(2)
    tq = q_ref.shape[0]
    n_sub, _, tk = vt_ref.shape
    hd = MLA_V
    acc_rows = acc_ref.shape[1]

    def with_ones(vt):
        return jnp.concatenate([vt, jnp.ones((acc_rows - hd, vt.shape[1]), BF16)], axis=0)

    @pl.when(kv == 0)
    def _():
        for h in range(HEAD_SLOTS):
            ks, _ = _head_slots(h)
            q = q_ref[:, h * LANE:(h + 1) * LANE]
            sm_ref[h] = _dot_nt(km_ref[0:N_META, ks * LANE:(ks + 1) * LANE], q)
        zeros = jnp.zeros((LANE - N_META, tq), F32)
        for h in range(HEAD_SLOTS):
            _, vh = _head_slots(h)
            s3 = sm_ref[h].reshape(N_META // SUBLANE, SUBLANE, tq)
            m = _sublane_bcast_max(jnp.max(s3, axis=0))
            p3 = jnp.exp2(s3 - m[None])
            m_ref[h] = m
            p = jnp.concatenate([p3.reshape(N_META, tq), zeros], axis=0).astype(BF16)
            acc_ref[h] = _dot(with_ones(vmt_ref[vh * hd:(vh + 1) * hd, :]), p)

    n_slots = s_ref.shape[0]

    def scores(t, h):
        ks, _ = _head_slots(h)
        k0 = pl.multiple_of(t * tk, tk)
        k = k_ref[pl.ds(k0, tk), ks * LANE:(ks + 1) * LANE]
        s = _dot_nt(k, q_ref[:, h * LANE:(h + 1) * LANE])
        s_ref[h % n_slots] = s
        smax_ref[h] = jnp.max(s.reshape(tk // SUBLANE, SUBLANE, tq), axis=0)

    def softmax_pv(t, h):
        _, vh = _head_slots(h)
        m_prev = m_ref[h]
        m_new = jnp.maximum(m_prev, _sublane_bcast_max(smax_ref[h]))
        alpha = jnp.exp2(m_prev - m_new)
        s3 = s_ref[h % n_slots].reshape(tk // SUBLANE, SUBLANE, tq)
        p = jnp.exp2(s3 - m_new[None]).reshape(tk, tq).astype(BF16)
        pv = _dot(with_ones(vt_ref[t, vh * hd:(vh + 1) * hd, :]), p)
        acc = acc_ref[h].reshape(acc_rows // SUBLANE, SUBLANE, tq) * alpha[None]
        acc_ref[h] = acc.reshape(acc_rows, tq) + pv
        m_ref[h] = m_new

    for h in range(DENSE_LOOKAHEAD):
        scores(0, h)

    def sub_tile(t, carry):
        t_next = jnp.minimum(t + 1, n_sub - 1)
        for h in range(HEAD_SLOTS):
            ahead = h + DENSE_LOOKAHEAD
            if ahead < HEAD_SLOTS:
                scores(t, ahead)
            else:
                scores(t_next, ahead - HEAD_SLOTS)
            softmax_pv(t, h)
        return carry

    lax.fori_loop(0, n_sub, sub_tile, 0, unroll=2)

    @pl.when(kv == pl.num_programs(2) - 1)
    def _():
        for j in range(HEAD_SLOTS // 2):
            outs = []
            for h in (2 * j, 2 * j + 1):
                outs.append(acc_ref[h, 0:hd, :] / acc_ref[h, hd:hd + 1, :])
            o_t = jnp.concatenate(outs, axis=0)
            o_ref[:, j * LANE:(j + 1) * LANE] = o_t.T.astype(BF16)


def _dense_attn(q, k, vt, km, vmt, *, n_seq, seq, tq, q_base, meta_base):
    nq = (q.shape[0] - q_base) // (n_seq * tq)
    tk = vt.shape[2]
    n_sub = _row_tile(seq // tk, 8)
    nk = seq // (tk * n_sub)
    qb0 = q_base // tq
    out_rows = n_seq * nq * tq
    return pl.pallas_call(
        _dense_attn_kernel,
        grid=(n_seq, nq, nk),
        in_specs=[
            pl.BlockSpec((tq, HEAD_SLOTS * LANE), lambda b, i, j: (qb0 + b * nq + i, 0)),
            pl.BlockSpec((n_sub * tk, K_SLOTS * LANE), lambda b, i, j: (b * nk + j, 0)),
            pl.BlockSpec((n_sub, V_ROWS, tk), lambda b, i, j: (b * nk + j, 0, 0)),
            pl.BlockSpec((None, LANE, K_SLOTS * LANE), lambda b, i, j: (meta_base + b, 0, 0)),
            pl.BlockSpec((None, V_ROWS, LANE), lambda b, i, j: (meta_base + b, 0, 0)),
        ],
        out_specs=pl.BlockSpec((tq, D_MODEL), lambda b, i, j: (b * nq + i, 0)),
        out_shape=jax.ShapeDtypeStruct((out_rows, D_MODEL), BF16),
        scratch_shapes=[
            pltpu.VMEM((HEAD_SLOTS, SUBLANE, tq), F32),
            pltpu.VMEM((HEAD_SLOTS, SUBLANE, tq), F32),
            pltpu.VMEM((HEAD_SLOTS, MLA_V + 2 * SUBLANE, tq), F32),
            pltpu.VMEM((DENSE_SLOTS, tk, tq), F32),
            pltpu.VMEM((HEAD_SLOTS, N_META, tq), F32),
        ],
        compiler_params=pltpu.CompilerParams(
            dimension_semantics=("arbitrary", "arbitrary", "arbitrary"),
            vmem_limit_bytes=VMEM_LIMIT),
        name="dense_attn",
    )(q, k, vt, km, vmt)


def _proj_na_kernel(h_ref, g_ref, w_ref, q_ref, k_ref, v_ref, *, transpose_v):
    a = _rms(h_ref[...], g_ref[2:3, :]).astype(BF16)
    qkv = _dot(a, w_ref[...])
    n = NA_HEADS * NA_HEAD_DIM
    q_ref[...] = (qkv[:, 0:n] * (NA_HEAD_DIM ** -0.5 * LOG2E)).astype(BF16)
    k_ref[...] = qkv[:, n:2 * n].astype(BF16)
    v = qkv[:, 2 * n:3 * n]
    if transpose_v:
        vt = v.T.astype(BF16)
        for t in range(v_ref.shape[0]):
            v_ref[t] = vt[:, t * LANE:(t + 1) * LANE]
    else:
        v_ref[...] = v.astype(BF16)


def _proj_na(h, gains, w, transpose_v):
    rows = h.shape[0]
    tm = _row_tile(rows, 512)
    n = NA_HEADS * NA_HEAD_DIM
    if transpose_v:
        v_spec = pl.BlockSpec((tm // LANE, n, LANE), lambda i: (i, 0, 0))
        v_shape = jax.ShapeDtypeStruct((rows // LANE, n, LANE), BF16)
    else:
        v_spec = pl.BlockSpec((tm, n), lambda i: (i, 0))
        v_shape = jax.ShapeDtypeStruct((rows, n), BF16)
    return pl.pallas_call(
        functools.partial(_proj_na_kernel, transpose_v=transpose_v),
        grid=(rows // tm,),
        in_specs=[pl.BlockSpec((tm, D_MODEL), lambda i: (i, 0)), _const_spec(gains.shape),
                  _const_spec(w.shape)],
        out_specs=[pl.BlockSpec((tm, n), lambda i: (i, 0))] * 2 + [v_spec],
        out_shape=[jax.ShapeDtypeStruct((rows, n), BF16)] * 2 + [v_shape],
        compiler_params=pltpu.CompilerParams(
            dimension_semantics=("arbitrary",), vmem_limit_bytes=VMEM_LIMIT),
        name="proj_na",
    )(h, gains, w)


NA_SPAN_R = NA_WIN_R + 2
NA_MASKED = 2 * NA_WIN_R - 1


def _na_kernel(q_ref, k_ref, vt_ref, km_ref, vmt_ref, bias_ref, mb_ref, o_ref, s_ref,
               *, rows, rows_per_step):
    step = pl.program_id(1)
    n_pairs = rows_per_step // 2
    n_hp = NA_HEADS // 2
    span = NA_SPAN_R * GRID_W
    lane = lax.broadcasted_iota(jnp.int32, (GRID_W, LANE), 1)
    first = lane < (LANE // 2)
    zeros_m = jnp.zeros((LANE - N_META, 2 * LANE), F32)
    ones_v = jnp.ones((2 * SUBLANE, span + LANE), BF16)

    def geometry(rp):
        ra = step * rows_per_step + 2 * rp
        rs = [jnp.clip(ra + x - NA_WIN_R // 2, 0, rows - NA_WIN_R) for x in range(2)]
        ws = jnp.minimum((rs[0] // 2) * 2, rows - NA_SPAN_R)
        return ra, rs, ws

    n_slots = s_ref.shape[0]

    def scores(rp, hp):
        slot = hp % n_slots
        _, _, ws = geometry(rp)
        cols = slice(hp * LANE, (hp + 1) * LANE)
        parts = []
        for x in range(2):
            q0 = pl.multiple_of((2 * rp + x) * GRID_W, GRID_W)
            qx = q_ref[pl.ds(q0, GRID_W), cols]
            parts += [jnp.where(first, qx, jnp.zeros_like(qx)), jnp.where(first, jnp.zeros_like(qx), qx)]
        qblk = jnp.concatenate(parts, axis=0)
        k0 = pl.multiple_of(ws * GRID_W, 2 * GRID_W)
        s_ref[slot, 0:span, :] = _dot_nt(k_ref[pl.ds(k0, span), cols], qblk)
        s_ref[slot, span:span + N_META, :] = _dot_nt(km_ref[:, cols], qblk)

    for hp in range(NA_LOOKAHEAD):
        scores(0, hp)

    def row_pair(rp, carry):
        ra, rs, ws = geometry(rp)
        idx = []
        for jj in range(NA_SPAN_R):
            kr = ws + jj
            idx.append([jnp.where((kr >= rs[x]) & (kr < rs[x] + NA_WIN_R),
                                  kr - (ra + x) + NA_WIN_R - 1, NA_MASKED) for x in range(2)])
        t0 = ws // 2
        for hp in range(n_hp):
            slot = hp % n_slots
            ahead = hp + NA_LOOKAHEAD
            if ahead < n_hp:
                scores(rp, ahead)
            else:
                scores(jnp.minimum(rp + 1, n_pairs - 1), ahead - n_hp)
            cols = slice(hp * LANE, (hp + 1) * LANE)
            b = jnp.concatenate(
                [jnp.concatenate([bias_ref[hp, idx[jj][0]], bias_ref[hp, idx[jj][1]]], axis=1)
                 for jj in range(NA_SPAN_R)], axis=0)
            s = s_ref[slot, 0:span, :] + b
            mb = mb_ref[hp]
            sm = s_ref[slot, span:span + N_META, :] + jnp.concatenate([mb, mb], axis=1)
            s3 = s.reshape(span // SUBLANE, SUBLANE, 2 * LANE)
            sm3 = sm.reshape(N_META // SUBLANE, SUBLANE, 2 * LANE)
            m = _sublane_bcast_max(jnp.maximum(jnp.max(s3, axis=0), jnp.max(sm3, axis=0)))
            p = jnp.exp2(s3 - m[None]).reshape(span, 2 * LANE).astype(BF16)
            pm3 = jnp.exp2(sm3 - m[None])
            pm = jnp.concatenate([pm3.reshape(N_META, 2 * LANE), zeros_m], axis=0).astype(BF16)
            v_all = jnp.concatenate([vt_ref[t0 + t, cols, :] for t in range(span // LANE)]
                                    + [vmt_ref[cols, :]], axis=1)
            o_t = _dot(jnp.concatenate([v_all, ones_v], axis=0),
                       jnp.concatenate([p, pm], axis=0))
            o_t = o_t[0:LANE] / o_t[LANE:LANE + 1]
            for x in range(2):
                blk = o_t[:, x * LANE:(x + 1) * LANE].T
                q0 = pl.multiple_of((2 * rp + x) * GRID_W, GRID_W)
                o_ref[pl.ds(q0, GRID_W), cols] = jnp.where(
                    first, blk[0:GRID_W], blk[GRID_W:2 * GRID_W]).astype(BF16)
        return carry

    lax.fori_loop(0, n_pairs, row_pair, 0, unroll=2)


def _na_attn(q, k, vt, km, vmt, bias, mb, *, n_seq, seq, meta_base):
    rows = seq // GRID_W
    assert rows >= NA_SPAN_R and rows % 2 == 0
    rps = 8
    nsteps = rows // rps
    n = NA_HEADS * NA_HEAD_DIM
    span = NA_SPAN_R * GRID_W
    return pl.pallas_call(
        functools.partial(_na_kernel, rows=rows, rows_per_step=rps),
        grid=(n_seq, nsteps),
        in_specs=[
            pl.BlockSpec((rps * GRID_W, n), lambda b, i: (b * nsteps + i, 0)),
            pl.BlockSpec((seq, n), lambda b, i: (b, 0), pipeline_mode=pl.Buffered(1)),
            pl.BlockSpec((seq // LANE, n, LANE), lambda b, i: (b, 0, 0), pipeline_mode=pl.Buffered(1)),
            pl.BlockSpec((None, N_META, n), lambda b, i: (meta_base + b, 0, 0)),
            pl.BlockSpec((None, n, LANE), lambda b, i: (meta_base + b, 0, 0)),
            _const_spec(bias.shape),
            _const_spec(mb.shape),
        ],
        out_specs=pl.BlockSpec((rps * GRID_W, n), lambda b, i: (b * nsteps + i, 0)),
        out_shape=jax.ShapeDtypeStruct((n_seq * seq, n), BF16),
        scratch_shapes=[pltpu.VMEM((NA_SLOTS, span + N_META, 2 * LANE), F32)],
        compiler_params=pltpu.CompilerParams(
            dimension_semantics=("arbitrary", "arbitrary"), vmem_limit_bytes=VMEM_LIMIT),
        name="na_attn",
    )(q, k, vt, km, vmt, bias, mb)


def _na_meta_kernel(q_ref, km_ref, vm_ref, mb_ref, o_ref):
    lane = lax.broadcasted_iota(jnp.int32, (N_META, LANE), 1)
    first = lane < (LANE // 2)
    for j in range(NA_HEADS // 2):
        cols = slice(j * LANE, (j + 1) * LANE)
        qp = q_ref[:, cols]
        km = km_ref[:, cols]
        vm = vm_ref[:, cols]
        outs = []
        for half in range(2):
            h = 2 * j + half
            qh = jnp.where(first if half == 0 else jnp.logical_not(first), qp, jnp.zeros_like(qp))
            sm = _dot_nt(qh, km)
            sm = jnp.where(lane < N_META, sm + mb_ref[h:h + 1, :], NEG_INF)
            m = jnp.max(sm, axis=-1, keepdims=True)
            pm = jnp.exp2(sm - m)
            l = jnp.sum(pm, axis=-1, keepdims=True)
            outs.append(_dot(pm.astype(BF16), vm) / l)
        o_ref[:, cols] = jnp.where(first, outs[0], outs[1]).astype(BF16)


def _na_meta(qm, km, vm, mb):
    n_seq = km.shape[0]
    n = NA_HEADS * NA_HEAD_DIM
    return pl.pallas_call(
        _na_meta_kernel,
        grid=(n_seq,),
        in_specs=[
            pl.BlockSpec((N_META, n), lambda b: (b, 0)),
            pl.BlockSpec((None, LANE, n), lambda b: (b, 0, 0)),
            pl.BlockSpec((None, LANE, n), lambda b: (b, 0, 0)),
            _const_spec(mb.shape),
        ],
        out_specs=pl.BlockSpec((N_META, n), lambda b: (b, 0)),
        out_shape=jax.ShapeDtypeStruct((n_seq * N_META, n), BF16),
        compiler_params=pltpu.CompilerParams(dimension_semantics=("arbitrary",)),
        name="na_meta",
    )(qm, km, vm, mb)


def _take_cols(w, idx):
    idx = np.asarray(idx)
    neg = idx < 0
    same_run = np.where(neg[1:] | neg[:-1], neg[1:] & neg[:-1], np.diff(idx) == 1)
    breaks = np.flatnonzero(~same_run) + 1
    parts = []
    for run in np.split(idx, breaks):
        if run[0] < 0:
            parts.append(jnp.zeros((w.shape[0], len(run)), w.dtype))
        else:
            parts.append(w[:, int(run[0]):int(run[-1]) + 1])
    return jnp.concatenate(parts, axis=1)


def _swap_halves(n):
    half = n // 2
    return np.concatenate([np.arange(half, n), np.arange(0, half)])


def _dense_weights(w_in, q_norm, w_uq, kv_norm, w_ukv, gq_norm, gk_norm, w_out):
    pad = lambda k: -np.ones(k, np.int64)
    o_kr = MLA_Q_LORA + MLA_KV_LORA
    o_gq = o_kr + MLA_ROPE
    o_gk = o_gq + GQA_HEADS * GQA_HEAD_DIM
    o_gv = o_gk + GQA_KV_HEADS * GQA_HEAD_DIM
    axial = np.concatenate([_swap_halves(GQA_HEAD_DIM // 2),
                            GQA_HEAD_DIM // 2 + _swap_halves(GQA_HEAD_DIM // 2)])
    idx = [np.arange(0, o_kr)]
    idx += [pad(MLA_NOPE), o_kr + np.arange(MLA_ROPE), pad(LANE - MLA_NOPE - MLA_ROPE)]
    idx += [pad(MLA_NOPE), o_kr + _swap_halves(MLA_ROPE), pad(LANE - MLA_NOPE - MLA_ROPE)]
    for h in range(GQA_HEADS):
        idx += [o_gq + h * GQA_HEAD_DIM + np.arange(GQA_HEAD_DIM), pad(LANE - GQA_HEAD_DIM)]
    for h in range(GQA_HEADS):
        idx += [o_gq + h * GQA_HEAD_DIM + axial, pad(LANE - GQA_HEAD_DIM)]
    for h in range(GQA_KV_HEADS):
        idx += [o_gk + h * GQA_HEAD_DIM + np.arange(GQA_HEAD_DIM), pad(LANE - GQA_HEAD_DIM)]
    for h in range(GQA_KV_HEADS):
        idx += [o_gk + h * GQA_HEAD_DIM + axial, pad(LANE - GQA_HEAD_DIM)]
    idx += [o_gv + np.arange(GQA_KV_HEADS * GQA_HEAD_DIM)]
    idx = np.concatenate(idx)
    assert idx.shape[0] == _C_END
    w_in2 = _take_cols(w_in.astype(BF16), idx)

    hd = MLA_NOPE + MLA_ROPE
    ia, ib = [], []
    for h in range(MLA_HEADS):
        ia += [h * hd + np.arange(hd), pad(LANE - hd)]
        ib += [pad(MLA_NOPE), h * hd + MLA_NOPE + _swap_halves(MLA_ROPE), pad(LANE - hd)]
    w_uq2 = _take_cols(w_uq.astype(BF16), np.concatenate(ia + ib))

    kvd = MLA_NOPE + MLA_V
    ik, iv = [], []
    for h in range(MLA_HEADS):
        ik += [h * kvd + np.arange(MLA_NOPE), pad(LANE - MLA_NOPE)]
        iv += [h * kvd + MLA_NOPE + np.arange(MLA_V)]
    w_ukv2 = _take_cols(w_ukv.astype(BF16), np.concatenate(ik + iv))

    def gain_pair(g):
        ga = jnp.concatenate([g, jnp.zeros((LANE - GQA_HEAD_DIM,), F32)])[None, :]
        gb = jnp.concatenate([g[jnp.asarray(axial)], jnp.zeros((LANE - GQA_HEAD_DIM,), F32)])[None, :]
        return ga, gb

    gq_a, gq_b = gain_pair(gq_norm)
    gk_a, gk_b = gain_pair(gk_norm)

    w_out2 = w_out.astype(BF16)

    return dict(w_in=w_in2, q_norm=q_norm[None, :], w_uq=w_uq2, kv_norm=kv_norm[None, :],
                w_ukv=w_ukv2, gq_a=gq_a, gq_b=gq_b, gk_a=gk_a, gk_b=gk_b), w_out2


def _rope_tables(pos, row, col):
    half = MLA_ROPE // 2
    inv = 1.0 / (ROPE_THETA ** (jnp.arange(half, dtype=F32) / half))
    n = pos.shape[0]

    def cs(p):
        ang = p.astype(F32)[:, None] * inv[None, :]
        return jnp.cos(ang), jnp.sin(ang)

    vals = jnp.concatenate(cs(pos) + cs(row) + cs(col), axis=1)
    c1, s1, cr, sr, cc, sc = range(6)
    tail = LANE - MLA_NOPE - MLA_ROPE
    blank = [(None, 0)]
    layout = (blank * (MLA_NOPE // half) + [(c1, 1), (c1, 1)] + blank * (tail // half)
              + blank * (MLA_NOPE // half) + [(s1, -1), (s1, 1)] + blank * (tail // half)
              + [(cr, 1), (cr, 1), (cc, 1), (cc, 1)] + blank * ((LANE - GQA_HEAD_DIM) // half)
              + [(sr, -1), (sr, 1), (sc, -1), (sc, 1)] + blank * ((LANE - GQA_HEAD_DIM) // half))
    expand = np.zeros((6 * half, len(layout) * half), np.float32)
    for blk, (src, sign) in enumerate(layout):
        if src is not None:
            expand[src * half + np.arange(half), blk * half + np.arange(half)] = sign
    return jnp.dot(vals, jnp.asarray(expand), precision=lax.Precision.HIGHEST)


def _na_bias_tables(rpb, meta_bias):
    c_idx = np.arange(GRID_W)
    c_start = np.clip(c_idx - NA_WIN_C // 2, 0, GRID_W - NA_WIN_C)
    col_mask = (c_idx[None, :] >= c_start[:, None]) & (c_idx[None, :] < c_start[:, None] + NA_WIN_C)
    col_off = np.clip(c_idx[None, :] - c_idx[:, None] + NA_WIN_C - 1, 0, 2 * NA_WIN_C - 2)
    hp = NA_HEADS // 2
    n_off = 2 * NA_WIN_C - 1
    select = np.zeros((2, n_off, GRID_W, 2, GRID_W), np.float32)
    kc_g, c_g = np.meshgrid(c_idx, c_idx, indexing="ij")
    for half in range(2):
        select[half, col_off[c_g, kc_g], kc_g, half, c_g] = 1.0
    rows = rpb.reshape(hp, 2, NA_MASKED, n_off).transpose(0, 2, 1, 3).reshape(hp * NA_MASKED, 2 * n_off)
    t = jnp.dot(rows, jnp.asarray(select.reshape(2 * n_off, GRID_W * LANE)),
                precision=lax.Precision.HIGHEST) * LOG2E
    t = t.reshape(hp, NA_MASKED, GRID_W, LANE)
    keep = np.tile(col_mask.T, (1, 2))
    t = jnp.where(jnp.asarray(keep)[None, None], t, NEG_INF)
    bias = jnp.concatenate([t, jnp.full_like(t[:, :1], NEG_INF)], axis=1)
    mbl = meta_bias * LOG2E
    mb_t = jnp.repeat(mbl.reshape(hp, 2, N_META).transpose(0, 2, 1), GRID_W, axis=2)
    mb = jnp.pad(mbl, ((0, 0), (0, LANE - N_META)))
    return bias, mb_t, mb


def _pad_meta(x, n_seq):
    c = x.shape[1]
    return jnp.pad(x.reshape(n_seq, N_META, c), ((0, 0), (0, LANE - N_META), (0, 0)))


def kernel(x_prompt, x_sample, meta, norm_gains, ffn1_w_gate, ffn1_w_up, ffn1_w_down, ffn2_w_gate, ffn2_w_up, ffn2_w_down, attn_w_in, mla_q_norm, mla_w_uq, mla_kv_norm, mla_w_ukv, gqa_q_norm, gqa_k_norm, attn_w_out, na_w_qkv, na_rpb, na_meta_bias, na_w_out):
    bp, sp, _ = x_prompt.shape
    bs, ss, _ = x_sample.shape
    n_seq = bp + bs
    depth = norm_gains.shape[0]
    groups = [(bp, sp, 0), (bs, ss, bp)]

    n_meta = n_seq * N_META
    meta_rows = -(-n_meta // LANE) * LANE
    pad_rows = lambda x: jnp.pad(x, ((0, meta_rows - x.shape[0]), (0, 0)))
    h_tok = [x_prompt.reshape(bp * sp, D_MODEL), x_sample.reshape(bs * ss, D_MODEL)]
    h_meta = pad_rows(jnp.tile(meta.astype(F32), (n_seq, 1)))

    smax = max(sp, ss)
    t = jnp.arange(smax)
    tab_tok = _rope_tables(t + N_META, t // GRID_W, t % GRID_W)
    mi = jnp.arange(meta_rows) % N_META
    tab_meta = _rope_tables(mi, jnp.full_like(mi, -1), mi)

    w1 = (ffn1_w_gate.astype(BF16), ffn1_w_up.astype(BF16), ffn1_w_down.astype(BF16))
    w2 = (ffn2_w_gate.astype(BF16), ffn2_w_up.astype(BF16), ffn2_w_down.astype(BF16))

    for i in range(depth):
        gains = jnp.pad(norm_gains[i], ((0, 2), (0, 0)))
        j = i // 2
        h_tok = [_ffn1(h, gains, *w1, i) for h in h_tok]
        h_meta = _ffn1(h_meta, gains, *w1, i)
        if i % 2 == 0:
            w, w_out = _dense_weights(attn_w_in[j], mla_q_norm[j], mla_w_uq[j], mla_kv_norm[j],
                                      mla_w_ukv[j], gqa_q_norm[j], gqa_k_norm[j], attn_w_out[j])
            qkv_tok = [_proj_dense(h, gains, w, tab_tok, s) for h, (_, s, _) in zip(h_tok, groups)]
            qm, km, vmt = _proj_dense(h_meta, gains, w, tab_meta, 0)
            kmp = _pad_meta(km[:n_meta], n_seq)
            vmt = vmt.transpose(1, 0, 2).reshape(V_ROWS, meta_rows)
            vmtp = vmt[:, :n_meta].reshape(V_ROWS, n_seq, N_META).transpose(1, 0, 2)
            vmtp = jnp.pad(vmtp, ((0, 0), (0, 0), (0, LANE - N_META)))
            qmp = _pad_meta(qm[:n_meta], n_seq).reshape(n_seq * LANE, HEAD_SLOTS * LANE)
            o_tok, o_meta = [], []
            for (q, k, vt), (nb, s, b0) in zip(qkv_tok, groups):
                o_tok.append(_dense_attn(q, k, vt, kmp, vmtp, n_seq=nb, seq=s,
                                         tq=_row_tile(s, 512), q_base=0, meta_base=b0))
                om = _dense_attn(qmp, k, vt, kmp, vmtp, n_seq=nb, seq=s,
                                 tq=LANE, q_base=b0 * LANE, meta_base=b0)
                o_meta.append(om.reshape(nb, LANE, D_MODEL)[:, :N_META].reshape(nb * N_META, D_MODEL))
            o_meta = pad_rows(jnp.concatenate(o_meta, axis=0))
        else:
            w_qkv = na_w_qkv[j].astype(BF16)
            w_out = na_w_out[j].astype(BF16)
            bias, mb_t, mb = _na_bias_tables(na_rpb[j], na_meta_bias[j])
            qkv_tok = [_proj_na(h, gains, w_qkv, True) for h in h_tok]
            qm, km, vm = _proj_na(h_meta, gains, w_qkv, False)
            kmp, vmp = _pad_meta(km[:n_meta], n_seq), _pad_meta(vm[:n_meta], n_seq)
            km16 = km[:n_meta].reshape(n_seq, N_META, NA_HEADS * NA_HEAD_DIM)
            vmtp = vmp.transpose(0, 2, 1)
            o_tok = [_na_attn(q, k, vt, km16, vmtp, bias, mb_t, n_seq=nb, seq=s, meta_base=b0)
                     for (q, k, vt), (nb, s, b0) in zip(qkv_tok, groups)]
            o_meta = pad_rows(_na_meta(qm[:n_meta], kmp, vmp, mb))
        h_tok = [_mix_ffn2(h, o, w_out, gains, *w2, i) for h, o in zip(h_tok, o_tok)]
        h_meta = _mix_ffn2(h_meta, o_meta, w_out, gains, *w2, i)

    return (h_tok[0].reshape(bp, sp, D_MODEL), h_tok[1].reshape(bs, ss, D_MODEL))
```

```python
import functools
import math

import jax
import jax.numpy as jnp
import numpy as np
from jax import lax
from jax.experimental import pallas as pl
from jax.experimental.pallas import tpu as pltpu

F32 = jnp.float32
BF16 = jnp.bfloat16

D_MODEL = 1024
N_META = 16
GRID_W = 64
D_FF = 2816
EPS = 1e-6
NEG_INF = -1e30
LOG2E = math.log2(math.e)

MLA_HEADS = 8
MLA_Q_LORA = 256
MLA_KV_LORA = 128
MLA_NOPE = 64
MLA_ROPE = 32
MLA_V = 64
GQA_HEADS = 8
GQA_KV_HEADS = 2
GQA_HEAD_DIM = 64
ROPE_THETA = 10000.0
NA_HEADS = 16
NA_HEAD_DIM = 64
NA_WIN_R = 8
NA_WIN_C = 16

LANE = 128
HEAD_SLOTS = MLA_HEADS + GQA_HEADS
K_SLOTS = MLA_HEADS + GQA_KV_HEADS
V_ROWS = (MLA_HEADS + GQA_KV_HEADS) * MLA_V
SUBLANE = 8
DENSE_LOOKAHEAD, DENSE_SLOTS = 2, 4
NA_LOOKAHEAD, NA_SLOTS = 2, 4
VMEM_LIMIT = 56 * 1024 * 1024

_C_CQ = 0
_C_CKV = _C_CQ + MLA_Q_LORA
_C_KRA = _C_CKV + MLA_KV_LORA
_C_KRB = _C_KRA + LANE
_C_GQA = _C_KRB + LANE
_C_GQB = _C_GQA + GQA_HEADS * LANE
_C_GKA = _C_GQB + GQA_HEADS * LANE
_C_GKB = _C_GKA + GQA_KV_HEADS * LANE
_C_GV = _C_GKB + GQA_KV_HEADS * LANE
_C_END = _C_GV + LANE


def _const_spec(shape):
    nd = len(shape)
    return pl.BlockSpec(shape, lambda *_: (0,) * nd, pipeline_mode=pl.Buffered(1))


def _rms(x, g):
    ms = jnp.mean(x * x, axis=-1, keepdims=True)
    return x * lax.rsqrt(ms + EPS) * g


def _dot(a, b):
    return jnp.dot(a, b, preferred_element_type=F32)


def _dot_nt(a, b):
    return lax.dot_general(a, b, (((1,), (1,)), ((), ())), preferred_element_type=F32)


def _row_tile(rows, want):
    t = min(rows, want)
    while rows % t:
        t //= 2
    return t


def _ffn_body(h, g_ref, pre, post, wg_ref, wu_ref, wd_ref):
    xn = _rms(h, g_ref[pre:pre + 1, :]).astype(BF16)
    gate = _dot(xn, wg_ref[...])
    up = _dot(xn, wu_ref[...])
    act = (gate * jax.nn.sigmoid(gate) * up).astype(BF16)
    y = _dot(act, wd_ref[...])
    return h + 0.5 * _rms(y, g_ref[post:post + 1, :])


def _ffn1_kernel(h_ref, g_ref, wg_ref, wu_ref, wd_ref, out_ref):
    out_ref[...] = _ffn_body(h_ref[...], g_ref, 0, 1, wg_ref, wu_ref, wd_ref)


def _mix_ffn2_kernel(h_ref, o_ref, wo_ref, g_ref, wg_ref, wu_ref, wd_ref, out_ref):
    mixed = _dot(o_ref[...], wo_ref[...])
    h = h_ref[...] + _rms(mixed, g_ref[3:4, :])
    out_ref[...] = _ffn_body(h, g_ref, 4, 5, wg_ref, wu_ref, wd_ref)


def _layer_spec(w, layer):
    return pl.BlockSpec((None,) + w.shape[1:], lambda *_: (layer, 0, 0), pipeline_mode=pl.Buffered(1))


def _ffn1(h, gains, wg, wu, wd, layer):
    rows = h.shape[0]
    tm = _row_tile(rows, 512)
    return pl.pallas_call(
        _ffn1_kernel,
        grid=(rows // tm,),
        in_specs=[
            pl.BlockSpec((tm, D_MODEL), lambda i: (i, 0)),
            _const_spec(gains.shape),
            _layer_spec(wg, layer), _layer_spec(wu, layer), _layer_spec(wd, layer),
        ],
        out_specs=pl.BlockSpec((tm, D_MODEL), lambda i: (i, 0)),
        out_shape=jax.ShapeDtypeStruct((rows, D_MODEL), F32),
        compiler_params=pltpu.CompilerParams(
            dimension_semantics=("arbitrary",), vmem_limit_bytes=VMEM_LIMIT),
        name="ffn1",
    )(h, gains, wg, wu, wd)


def _mix_ffn2(h, o, wo, gains, wg, wu, wd, layer):
    rows = h.shape[0]
    tm = _row_tile(rows, 512)
    return pl.pallas_call(
        _mix_ffn2_kernel,
        grid=(rows // tm,),
        in_specs=[
            pl.BlockSpec((tm, D_MODEL), lambda i: (i, 0)),
            pl.BlockSpec((tm, o.shape[1]), lambda i: (i, 0)),
            _const_spec(wo.shape),
            _const_spec(gains.shape),
            _layer_spec(wg, layer), _layer_spec(wu, layer), _layer_spec(wd, layer),
        ],
        out_specs=pl.BlockSpec((tm, D_MODEL), lambda i: (i, 0)),
        out_shape=jax.ShapeDtypeStruct((rows, D_MODEL), F32),
        compiler_params=pltpu.CompilerParams(
            dimension_semantics=("arbitrary",), vmem_limit_bytes=VMEM_LIMIT),
        name="mix_ffn2",
    )(h, o, wo, gains, wg, wu, wd)


def _proj_dense_kernel(h_ref, g_ref, win_ref, qn_ref, wuq_ref, kvn_ref, wukv_ref,
                       gqa_ref, gqb_ref, gka_ref, gkb_ref, exp_ref, tab_ref, qt_ref, k_ref, vt_ref):
    a = _rms(h_ref[...], g_ref[2:3, :]).astype(BF16)
    proj = _dot(a, win_ref[...])
    tab = sum(_dot(tab_ref[i], exp_ref[...]) for i in range(tab_ref.shape[0]))
    cos_k, sin_k = tab[:, 0:LANE], tab[:, LANE:2 * LANE]
    cos_g, sin_g = tab[:, 2 * LANE:3 * LANE], tab[:, 3 * LANE:4 * LANE]
    qs = (MLA_NOPE + MLA_ROPE) ** -0.5 * LOG2E
    lane = lax.broadcasted_iota(jnp.int32, cos_k.shape, 1)
    cos_q = jnp.where(lane < MLA_NOPE, qs, cos_k * qs)
    sin_q = sin_k * qs

    cqn = _rms(proj[:, _C_CQ:_C_CQ + MLA_Q_LORA], qn_ref[...]).astype(BF16)
    qab = _dot(cqn, wuq_ref[...])
    nq = MLA_HEADS * LANE
    for h in range(MLA_HEADS):
        qa = qab[:, h * LANE:(h + 1) * LANE]
        qb = qab[:, nq + h * LANE:nq + (h + 1) * LANE]
        qt_ref[0, h * LANE:(h + 1) * LANE, :] = (qa * cos_q + qb * sin_q).T.astype(BF16)

    ckvn = _rms(proj[:, _C_CKV:_C_CKV + MLA_KV_LORA], kvn_ref[...]).astype(BF16)
    kv = _dot(ckvn, wukv_ref[...])
    k_rope = (proj[:, _C_KRA:_C_KRA + LANE] * cos_k + proj[:, _C_KRB:_C_KRB + LANE] * sin_k)
    for h in range(MLA_HEADS):
        k_ref[:, h * LANE:(h + 1) * LANE] = (kv[:, h * LANE:(h + 1) * LANE] + k_rope).astype(BF16)
    nv = MLA_HEADS * MLA_V

    gq_scale = GQA_HEAD_DIM ** -0.5 * LOG2E
    cq_g = cos_g * (gqa_ref[...] * gq_scale)
    sq_g = sin_g * (gqb_ref[...] * gq_scale)
    for h in range(GQA_HEADS):
        xa = proj[:, _C_GQA + h * LANE:_C_GQA + (h + 1) * LANE]
        xb = proj[:, _C_GQB + h * LANE:_C_GQB + (h + 1) * LANE]
        r = lax.rsqrt(jnp.sum(xa * xa, axis=-1, keepdims=True) * (1.0 / GQA_HEAD_DIM) + EPS)
        qt_ref[0, nq + h * LANE:nq + (h + 1) * LANE, :] = ((xa * cq_g + xb * sq_g) * r).T.astype(BF16)
    ck_g = cos_g * gka_ref[...]
    sk_g = sin_g * gkb_ref[...]
    for h in range(GQA_KV_HEADS):
        xa = proj[:, _C_GKA + h * LANE:_C_GKA + (h + 1) * LANE]
        xb = proj[:, _C_GKB + h * LANE:_C_GKB + (h + 1) * LANE]
        r = lax.rsqrt(jnp.sum(xa * xa, axis=-1, keepdims=True) * (1.0 / GQA_HEAD_DIM) + EPS)
        k_ref[:, nq + h * LANE:nq + (h + 1) * LANE] = ((xa * ck_g + xb * sk_g) * r).astype(BF16)
    v = jnp.concatenate([kv[:, nq:nq + nv], proj[:, _C_GV:_C_GV + LANE]], axis=1)
    vt_ref[0] = v.T.astype(BF16)


def _proj_dense(h, gains, w, tab, seq):
    rows = h.shape[0]
    tm = _row_tile(seq if seq else rows, 512)
    nblk = (seq // tm) if seq else 1
    consts = [w["w_in"], w["q_norm"], w["w_uq"], w["kv_norm"], w["w_ukv"],
              w["gq_a"], w["gq_b"], w["gk_a"], w["gk_b"], w["rope_expand"]]
    return pl.pallas_call(
        _proj_dense_kernel,
        grid=(rows // tm,),
        in_specs=[pl.BlockSpec((tm, D_MODEL), lambda i: (i, 0)), _const_spec(gains.shape)]
        + [_const_spec(c.shape) for c in consts]
        + [pl.BlockSpec((tab.shape[0], tm, LANE), lambda i: (0, i % nblk, 0))],
        out_specs=[
            pl.BlockSpec((1, HEAD_SLOTS * LANE, tm), lambda i: (i, 0, 0)),
            pl.BlockSpec((tm, K_SLOTS * LANE), lambda i: (i, 0)),
            pl.BlockSpec((1, V_ROWS, tm), lambda i: (i, 0, 0)),
        ],
        out_shape=[
            jax.ShapeDtypeStruct((rows // tm, HEAD_SLOTS * LANE, tm), BF16),
            jax.ShapeDtypeStruct((rows, K_SLOTS * LANE), BF16),
            jax.ShapeDtypeStruct((rows // tm, V_ROWS, tm), BF16),
        ],
        compiler_params=pltpu.CompilerParams(
            dimension_semantics=("arbitrary",), vmem_limit_bytes=VMEM_LIMIT),
        name="proj_dense",
    )(h, gains, *consts, tab)


def _head_slots(h):
    if h < MLA_HEADS:
        return h, h
    kvh = (h - MLA_HEADS) // (GQA_HEADS // GQA_KV_HEADS)
    return MLA_HEADS + kvh, MLA_HEADS + kvh


def _sublane_bcast_max(x):
    return jnp.broadcast_to(jnp.max(x, axis=0, keepdims=True), x.shape)


def _dense_attn_kernel(qt_ref, k_ref, vt_ref, km_ref, vmt_ref, o_ref,
                       m_ref, smax_ref, acc_ref, s_ref, sm_ref):
    kv = pl.program_id(2)
    tq = qt_ref.shape[1]
    n_sub, _, tk = vt_ref.shape
    hd = MLA_V
    acc_rows = acc_ref.shape[1]

    def with_ones(vt):
        return jnp.concatenate([vt, jnp.ones((acc_rows - hd, vt.shape[1]), BF16)], axis=0)

    @pl.when(kv == 0)
    def _():
        for h in range(HEAD_SLOTS):
            ks, _ = _head_slots(h)
            qt = qt_ref[h * LANE:(h + 1) * LANE, :]
            sm_ref[h] = _dot(km_ref[0:N_META, ks * LANE:(ks + 1) * LANE], qt)
        zeros = jnp.zeros((LANE - N_META, tq), F32)
        for h in range(HEAD_SLOTS):
            _, vh = _head_slots(h)
            s3 = sm_ref[h].reshape(N_META // SUBLANE, SUBLANE, tq)
            m = _sublane_bcast_max(jnp.max(s3, axis=0))
            p3 = jnp.exp2(s3 - m[None])
            m_ref[h] = m
            p = jnp.concatenate([p3.reshape(N_META, tq), zeros], axis=0).astype(BF16)
            acc_ref[h] = _dot(with_ones(vmt_ref[vh * hd:(vh + 1) * hd, :]), p)

    n_slots = s_ref.shape[0]

    def scores(t, h):
        ks, _ = _head_slots(h)
        k0 = pl.multiple_of(t * tk, tk)
        k = k_ref[pl.ds(k0, tk), ks * LANE:(ks + 1) * LANE]
        s = _dot(k, qt_ref[h * LANE:(h + 1) * LANE, :])
        s_ref[h % n_slots] = s
        smax_ref[h] = jnp.max(s.reshape(tk // SUBLANE, SUBLANE, tq), axis=0)

    def softmax_pv(t, h):
        _, vh = _head_slots(h)
        m_prev = m_ref[h]
        m_new = jnp.maximum(m_prev, _sublane_bcast_max(smax_ref[h]))
        alpha = jnp.exp2(m_prev - m_new)
        s3 = s_ref[h % n_slots].reshape(tk // SUBLANE, SUBLANE, tq)
        p = jnp.exp2(s3 - m_new[None]).reshape(tk, tq).astype(BF16)
        pv = _dot(with_ones(vt_ref[t, vh * hd:(vh + 1) * hd, :]), p)
        acc = acc_ref[h].reshape(acc_rows // SUBLANE, SUBLANE, tq) * alpha[None]
        acc_ref[h] = acc.reshape(acc_rows, tq) + pv
        m_ref[h] = m_new

    for h in range(DENSE_LOOKAHEAD):
        scores(0, h)

    def sub_tile(t, carry):
        t_next = jnp.minimum(t + 1, n_sub - 1)
        for h in range(HEAD_SLOTS):
            ahead = h + DENSE_LOOKAHEAD
            if ahead < HEAD_SLOTS:
                scores(t, ahead)
            else:
                scores(t_next, ahead - HEAD_SLOTS)
            softmax_pv(t, h)
        return carry

    lax.fori_loop(0, n_sub, sub_tile, 0, unroll=2)

    @pl.when(kv == pl.num_programs(2) - 1)
    def _():
        for j in range(HEAD_SLOTS // 2):
            outs = []
            for h in (2 * j, 2 * j + 1):
                outs.append(acc_ref[h, 0:hd, :] / acc_ref[h, hd:hd + 1, :])
            o_t = jnp.concatenate(outs, axis=0)
            o_ref[:, j * LANE:(j + 1) * LANE] = o_t.T.astype(BF16)


def _dense_attn(qt, k, vt, km, vmt, *, n_seq, seq, q_base, meta_base):
    tq = qt.shape[2]
    nq = (qt.shape[0] - q_base) // n_seq
    tk = vt.shape[2]
    n_sub = _row_tile(seq // tk, 8)
    nk = seq // (tk * n_sub)
    out_rows = n_seq * nq * tq
    return pl.pallas_call(
        _dense_attn_kernel,
        grid=(n_seq, nq, nk),
        in_specs=[
            pl.BlockSpec((None, HEAD_SLOTS * LANE, tq), lambda b, i, j: (q_base + b * nq + i, 0, 0)),
            pl.BlockSpec((n_sub * tk, K_SLOTS * LANE), lambda b, i, j: (b * nk + j, 0)),
            pl.BlockSpec((n_sub, V_ROWS, tk), lambda b, i, j: (b * nk + j, 0, 0)),
            pl.BlockSpec((None, LANE, K_SLOTS * LANE), lambda b, i, j: (meta_base + b, 0, 0)),
            pl.BlockSpec((None, V_ROWS, LANE), lambda b, i, j: (meta_base + b, 0, 0)),
        ],
        out_specs=pl.BlockSpec((tq, D_MODEL), lambda b, i, j: (b * nq + i, 0)),
        out_shape=jax.ShapeDtypeStruct((out_rows, D_MODEL), BF16),
        scratch_shapes=[
            pltpu.VMEM((HEAD_SLOTS, SUBLANE, tq), F32),
            pltpu.VMEM((HEAD_SLOTS, SUBLANE, tq), F32),
            pltpu.VMEM((HEAD_SLOTS, MLA_V + 2 * SUBLANE, tq), F32),
            pltpu.VMEM((DENSE_SLOTS, tk, tq), F32),
            pltpu.VMEM((HEAD_SLOTS, N_META, tq), F32),
        ],
        compiler_params=pltpu.CompilerParams(
            dimension_semantics=("arbitrary", "arbitrary", "arbitrary"),
            vmem_limit_bytes=VMEM_LIMIT),
        name="dense_attn",
    )(qt, k, vt, km, vmt)


def _proj_na_kernel(h_ref, g_ref, w_ref, q_ref, k_ref, v_ref, *, transpose_v):
    a = _rms(h_ref[...], g_ref[2:3, :]).astype(BF16)
    qkv = _dot(a, w_ref[...])
    n = NA_HEADS * NA_HEAD_DIM
    q_ref[...] = (qkv[:, 0:n] * (NA_HEAD_DIM ** -0.5 * LOG2E)).astype(BF16)
    k_ref[...] = qkv[:, n:2 * n].astype(BF16)
    v = qkv[:, 2 * n:3 * n]
    if transpose_v:
        vt = v.T.astype(BF16)
        for t in range(v_ref.shape[0]):
            v_ref[t] = vt[:, t * LANE:(t + 1) * LANE]
    else:
        v_ref[...] = v.astype(BF16)


def _proj_na(h, gains, w, transpose_v):
    rows = h.shape[0]
    tm = _row_tile(rows, 512)
    n = NA_HEADS * NA_HEAD_DIM
    if transpose_v:
        v_spec = pl.BlockSpec((tm // LANE, n, LANE), lambda i: (i, 0, 0))
        v_shape = jax.ShapeDtypeStruct((rows // LANE, n, LANE), BF16)
    else:
        v_spec = pl.BlockSpec((tm, n), lambda i: (i, 0))
        v_shape = jax.ShapeDtypeStruct((rows, n), BF16)
    return pl.pallas_call(
        functools.partial(_proj_na_kernel, transpose_v=transpose_v),
        grid=(rows // tm,),
        in_specs=[pl.BlockSpec((tm, D_MODEL), lambda i: (i, 0)), _const_spec(gains.shape),
                  _const_spec(w.shape)],
        out_specs=[pl.BlockSpec((tm, n), lambda i: (i, 0))] * 2 + [v_spec],
        out_shape=[jax.ShapeDtypeStruct((rows, n), BF16)] * 2 + [v_shape],
        compiler_params=pltpu.CompilerParams(
            dimension_semantics=("arbitrary",), vmem_limit_bytes=VMEM_LIMIT),
        name="proj_na",
    )(h, gains, w)


NA_SPAN_R = NA_WIN_R + 2
NA_MASKED = 2 * NA_WIN_R - 1


def _na_kernel(q_ref, k_ref, vt_ref, km_ref, vmt_ref, bias_ref, mb_ref, o_ref, s_ref,
               *, rows, rows_per_step):
    step = pl.program_id(1)
    n_pairs = rows_per_step // 2
    n_hp = NA_HEADS // 2
    span = NA_SPAN_R * GRID_W
    lane = lax.broadcasted_iota(jnp.int32, (GRID_W, LANE), 1)
    first = lane < (LANE // 2)
    zeros_m = jnp.zeros((LANE - N_META, 2 * LANE), F32)
    ones_v = jnp.ones((2 * SUBLANE, span + LANE), BF16)

    def geometry(rp):
        ra = step * rows_per_step + 2 * rp
        rs = [jnp.clip(ra + x - NA_WIN_R // 2, 0, rows - NA_WIN_R) for x in range(2)]
        ws = jnp.minimum((rs[0] // 2) * 2, rows - NA_SPAN_R)
        return ra, rs, ws

    n_slots = s_ref.shape[0]

    def scores(rp, hp):
        slot = hp % n_slots
        _, _, ws = geometry(rp)
        cols = slice(hp * LANE, (hp + 1) * LANE)
        parts = []
        for x in range(2):
            q0 = pl.multiple_of((2 * rp + x) * GRID_W, GRID_W)
            qx = q_ref[pl.ds(q0, GRID_W), cols]
            parts += [jnp.where(first, qx, jnp.zeros_like(qx)), jnp.where(first, jnp.zeros_like(qx), qx)]
        qblk = jnp.concatenate(parts, axis=0)
        k0 = pl.multiple_of(ws * GRID_W, 2 * GRID_W)
        s_ref[slot, 0:span, :] = _dot_nt(k_ref[pl.ds(k0, span), cols], qblk)
        s_ref[slot, span:span + N_META, :] = _dot_nt(km_ref[:, cols], qblk)

    for hp in range(NA_LOOKAHEAD):
        scores(0, hp)

    def row_pair(rp, carry):
        ra, rs, ws = geometry(rp)
        idx = []
        for jj in range(NA_SPAN_R):
            kr = ws + jj
            idx.append([jnp.where((kr >= rs[x]) & (kr < rs[x] + NA_WIN_R),
                                  kr - (ra + x) + NA_WIN_R - 1, NA_MASKED) for x in range(2)])
        t0 = ws // 2
        for hp in range(n_hp):
            slot = hp % n_slots
            ahead = hp + NA_LOOKAHEAD
            if ahead < n_hp:
                scores(rp, ahead)
            else:
                scores(jnp.minimum(rp + 1, n_pairs - 1), ahead - n_hp)
            cols = slice(hp * LANE, (hp + 1) * LANE)
            b = jnp.concatenate(
                [jnp.concatenate([bias_ref[hp, idx[jj][0]], bias_ref[hp, idx[jj][1]]], axis=1)
                 for jj in range(NA_SPAN_R)], axis=0)
            s = s_ref[slot, 0:span, :] + b
            mb = mb_ref[hp]
            sm = s_ref[slot, span:span + N_META, :] + jnp.concatenate([mb, mb], axis=1)
            s3 = s.reshape(span // SUBLANE, SUBLANE, 2 * LANE)
            sm3 = sm.reshape(N_META // SUBLANE, SUBLANE, 2 * LANE)
            m = _sublane_bcast_max(jnp.maximum(jnp.max(s3, axis=0), jnp.max(sm3, axis=0)))
            p = jnp.exp2(s3 - m[None]).reshape(span, 2 * LANE).astype(BF16)
            pm3 = jnp.exp2(sm3 - m[None])
            pm = jnp.concatenate([pm3.reshape(N_META, 2 * LANE), zeros_m], axis=0).astype(BF16)
            v_all = jnp.concatenate([vt_ref[t0 + t, cols, :] for t in range(span // LANE)]
                                    + [vmt_ref[cols, :]], axis=1)
            o_t = _dot(jnp.concatenate([v_all, ones_v], axis=0),
                       jnp.concatenate([p, pm], axis=0))
            o_t = o_t[0:LANE] / o_t[LANE:LANE + 1]
            for x in range(2):
                blk = o_t[:, x * LANE:(x + 1) * LANE].T
                q0 = pl.multiple_of((2 * rp + x) * GRID_W, GRID_W)
                o_ref[pl.ds(q0, GRID_W), cols] = jnp.where(
                    first, blk[0:GRID_W], blk[GRID_W:2 * GRID_W]).astype(BF16)
        return carry

    lax.fori_loop(0, n_pairs, row_pair, 0, unroll=2)


def _na_attn(q, k, vt, km, vmt, bias, mb, *, n_seq, seq, meta_base):
    rows = seq // GRID_W
    assert rows >= NA_SPAN_R and rows % 2 == 0
    rps = 8
    nsteps = rows // rps
    n = NA_HEADS * NA_HEAD_DIM
    span = NA_SPAN_R * GRID_W
    return pl.pallas_call(
        functools.partial(_na_kernel, rows=rows, rows_per_step=rps),
        grid=(n_seq, nsteps),
        in_specs=[
            pl.BlockSpec((rps * GRID_W, n), lambda b, i: (b * nsteps + i, 0)),
            pl.BlockSpec((seq, n), lambda b, i: (b, 0), pipeline_mode=pl.Buffered(1)),
            pl.BlockSpec((seq // LANE, n, LANE), lambda b, i: (b, 0, 0), pipeline_mode=pl.Buffered(1)),
            pl.BlockSpec((None, N_META, n), lambda b, i: (meta_base + b, 0, 0)),
            pl.BlockSpec((None, n, LANE), lambda b, i: (meta_base + b, 0, 0)),
            _const_spec(bias.shape),
            _const_spec(mb.shape),
        ],
        out_specs=pl.BlockSpec((rps * GRID_W, n), lambda b, i: (b * nsteps + i, 0)),
        out_shape=jax.ShapeDtypeStruct((n_seq * seq, n), BF16),
        scratch_shapes=[pltpu.VMEM((NA_SLOTS, span + N_META, 2 * LANE), F32)],
        compiler_params=pltpu.CompilerParams(
            dimension_semantics=("arbitrary", "arbitrary"), vmem_limit_bytes=VMEM_LIMIT),
        name="na_attn",
    )(q, k, vt, km, vmt, bias, mb)


def _na_meta_kernel(q_ref, km_ref, vm_ref, mb_ref, o_ref):
    lane = lax.broadcasted_iota(jnp.int32, (N_META, LANE), 1)
    first = lane < (LANE // 2)
    for j in range(NA_HEADS // 2):
        cols = slice(j * LANE, (j + 1) * LANE)
        qp = q_ref[:, cols]
        km = km_ref[:, cols]
        vm = vm_ref[:, cols]
        outs = []
        for half in range(2):
            h = 2 * j + half
            qh = jnp.where(first if half == 0 else jnp.logical_not(first), qp, jnp.zeros_like(qp))
            sm = _dot_nt(qh, km)
            sm = jnp.where(lane < N_META, sm + mb_ref[h:h + 1, :], NEG_INF)
            m = jnp.max(sm, axis=-1, keepdims=True)
            pm = jnp.exp2(sm - m)
            l = jnp.sum(pm, axis=-1, keepdims=True)
            outs.append(_dot(pm.astype(BF16), vm) / l)
        o_ref[:, cols] = jnp.where(first, outs[0], outs[1]).astype(BF16)


def _na_meta(qm, km, vm, mb):
    n_seq = km.shape[0]
    n = NA_HEADS * NA_HEAD_DIM
    return pl.pallas_call(
        _na_meta_kernel,
        grid=(n_seq,),
        in_specs=[
            pl.BlockSpec((N_META, n), lambda b: (b, 0)),
            pl.BlockSpec((None, LANE, n), lambda b: (b, 0, 0)),
            pl.BlockSpec((None, LANE, n), lambda b: (b, 0, 0)),
            _const_spec(mb.shape),
        ],
        out_specs=pl.BlockSpec((N_META, n), lambda b: (b, 0)),
        out_shape=jax.ShapeDtypeStruct((n_seq * N_META, n), BF16),
        compiler_params=pltpu.CompilerParams(dimension_semantics=("arbitrary",)),
        name="na_meta",
    )(qm, km, vm, mb)


def _take_cols(w, idx):
    idx = np.asarray(idx)
    neg = idx < 0
    same_run = np.where(neg[1:] | neg[:-1], neg[1:] & neg[:-1], np.diff(idx) == 1)
    breaks = np.flatnonzero(~same_run) + 1
    parts = []
    for run in np.split(idx, breaks):
        if run[0] < 0:
            parts.append(jnp.zeros((w.shape[0], len(run)), w.dtype))
        else:
            parts.append(w[:, int(run[0]):int(run[-1]) + 1])
    return jnp.concatenate(parts, axis=1)


def _swap_halves(n):
    half = n // 2
    return np.concatenate([np.arange(half, n), np.arange(0, half)])


def _dense_weights(w_in, q_norm, w_uq, kv_norm, w_ukv, gq_norm, gk_norm, w_out):
    pad = lambda k: -np.ones(k, np.int64)
    o_kr = MLA_Q_LORA + MLA_KV_LORA
    o_gq = o_kr + MLA_ROPE
    o_gk = o_gq + GQA_HEADS * GQA_HEAD_DIM
    o_gv = o_gk + GQA_KV_HEADS * GQA_HEAD_DIM
    axial = np.concatenate([_swap_halves(GQA_HEAD_DIM // 2),
                            GQA_HEAD_DIM // 2 + _swap_halves(GQA_HEAD_DIM // 2)])
    idx = [np.arange(0, o_kr)]
    idx += [pad(MLA_NOPE), o_kr + np.arange(MLA_ROPE), pad(LANE - MLA_NOPE - MLA_ROPE)]
    idx += [pad(MLA_NOPE), o_kr + _swap_halves(MLA_ROPE), pad(LANE - MLA_NOPE - MLA_ROPE)]
    for h in range(GQA_HEADS):
        idx += [o_gq + h * GQA_HEAD_DIM + np.arange(GQA_HEAD_DIM), pad(LANE - GQA_HEAD_DIM)]
    for h in range(GQA_HEADS):
        idx += [o_gq + h * GQA_HEAD_DIM + axial, pad(LANE - GQA_HEAD_DIM)]
    for h in range(GQA_KV_HEADS):
        idx += [o_gk + h * GQA_HEAD_DIM + np.arange(GQA_HEAD_DIM), pad(LANE - GQA_HEAD_DIM)]
    for h in range(GQA_KV_HEADS):
        idx += [o_gk + h * GQA_HEAD_DIM + axial, pad(LANE - GQA_HEAD_DIM)]
    idx += [o_gv + np.arange(GQA_KV_HEADS * GQA_HEAD_DIM)]
    idx = np.concatenate(idx)
    assert idx.shape[0] == _C_END
    w_in2 = _take_cols(w_in.astype(BF16), idx)

    hd = MLA_NOPE + MLA_ROPE
    ia, ib = [], []
    for h in range(MLA_HEADS):
        ia += [h * hd + np.arange(hd), pad(LANE - hd)]
        ib += [pad(MLA_NOPE), h * hd + MLA_NOPE + _swap_halves(MLA_ROPE), pad(LANE - hd)]
    w_uq2 = _take_cols(w_uq.astype(BF16), np.concatenate(ia + ib))

    kvd = MLA_NOPE + MLA_V
    ik, iv = [], []
    for h in range(MLA_HEADS):
        ik += [h * kvd + np.arange(MLA_NOPE), pad(LANE - MLA_NOPE)]
        iv += [h * kvd + MLA_NOPE + np.arange(MLA_V)]
    w_ukv2 = _take_cols(w_ukv.astype(BF16), np.concatenate(ik + iv))

    def gain_pair(g):
        ga = jnp.concatenate([g, jnp.zeros((LANE - GQA_HEAD_DIM,), F32)])[None, :]
        gb = jnp.concatenate([g[jnp.asarray(axial)], jnp.zeros((LANE - GQA_HEAD_DIM,), F32)])[None, :]
        return ga, gb

    gq_a, gq_b = gain_pair(gq_norm)
    gk_a, gk_b = gain_pair(gk_norm)

    w_out2 = w_out.astype(BF16)

    return dict(w_in=w_in2, q_norm=q_norm[None, :], w_uq=w_uq2, kv_norm=kv_norm[None, :],
                w_ukv=w_ukv2, gq_a=gq_a, gq_b=gq_b, gk_a=gk_a, gk_b=gk_b,
                rope_expand=_rope_expand_matrix()), w_out2


def _rope_tables(pos, row, col):
    half = MLA_ROPE // 2
    inv = 1.0 / (ROPE_THETA ** (jnp.arange(half, dtype=F32) / half))
    n = pos.shape[0]

    def cs(p):
        ang = p.astype(F32)[:, None] * inv[None, :]
        return jnp.cos(ang), jnp.sin(ang)

    vals = jnp.concatenate(cs(pos) + cs(row) + cs(col) + (jnp.zeros((n, LANE - 6 * half), F32),), axis=1)
    hi = vals.astype(BF16)
    rest = vals - hi.astype(F32)
    mid = rest.astype(BF16)
    lo = (rest - mid.astype(F32)).astype(BF16)
    return jnp.stack([hi, mid, lo])


def _rope_expand_matrix():
    half = MLA_ROPE // 2
    c1, s1, cr, sr, cc, sc = range(6)
    tail = LANE - MLA_NOPE - MLA_ROPE
    blank = [(None, 0)]
    layout = (blank * (MLA_NOPE // half) + [(c1, 1), (c1, 1)] + blank * (tail // half)
              + blank * (MLA_NOPE // half) + [(s1, -1), (s1, 1)] + blank * (tail // half)
              + [(cr, 1), (cr, 1), (cc, 1), (cc, 1)] + blank * ((LANE - GQA_HEAD_DIM) // half)
              + [(sr, -1), (sr, 1), (sc, -1), (sc, 1)] + blank * ((LANE - GQA_HEAD_DIM) // half))
    expand = np.zeros((LANE, len(layout) * half), np.float32)
    for blk, (src, sign) in enumerate(layout):
        if src is not None:
            expand[src * half + np.arange(half), blk * half + np.arange(half)] = sign
    return jnp.asarray(expand, BF16)


def _na_bias_tables(rpb, meta_bias):
    c_idx = np.arange(GRID_W)
    c_start = np.clip(c_idx - NA_WIN_C // 2, 0, GRID_W - NA_WIN_C)
    col_mask = (c_idx[None, :] >= c_start[:, None]) & (c_idx[None, :] < c_start[:, None] + NA_WIN_C)
    col_off = np.clip(c_idx[None, :] - c_idx[:, None] + NA_WIN_C - 1, 0, 2 * NA_WIN_C - 2)
    hp = NA_HEADS // 2
    n_off = 2 * NA_WIN_C - 1
    select = np.zeros((2, n_off, GRID_W, 2, GRID_W), np.float32)
    kc_g, c_g = np.meshgrid(c_idx, c_idx, indexing="ij")
    for half in range(2):
        select[half, col_off[c_g, kc_g], kc_g, half, c_g] = 1.0
    rows = rpb.reshape(hp, 2, NA_MASKED, n_off).transpose(0, 2, 1, 3).reshape(hp * NA_MASKED, 2 * n_off)
    t = jnp.dot(rows, jnp.asarray(select.reshape(2 * n_off, GRID_W * LANE)),
                precision=lax.Precision.HIGHEST) * LOG2E
    t = t.reshape(hp, NA_MASKED, GRID_W, LANE)
    keep = np.tile(col_mask.T, (1, 2))
    t = jnp.where(jnp.asarray(keep)[None, None], t, NEG_INF)
    bias = jnp.concatenate([t, jnp.full_like(t[:, :1], NEG_INF)], axis=1)
    mbl = meta_bias * LOG2E
    mb_t = jnp.repeat(mbl.reshape(hp, 2, N_META).transpose(0, 2, 1), GRID_W, axis=2)
    mb = jnp.pad(mbl, ((0, 0), (0, LANE - N_META)))
    return bias, mb_t, mb


def _pad_meta(x, n_seq):
    c = x.shape[1]
    return jnp.pad(x.reshape(n_seq, N_META, c), ((0, 0), (0, LANE - N_META), (0, 0)))


def kernel(x_prompt, x_sample, meta, norm_gains, ffn1_w_gate, ffn1_w_up, ffn1_w_down, ffn2_w_gate, ffn2_w_up, ffn2_w_down, attn_w_in, mla_q_norm, mla_w_uq, mla_kv_norm, mla_w_ukv, gqa_q_norm, gqa_k_norm, attn_w_out, na_w_qkv, na_rpb, na_meta_bias, na_w_out):
    bp, sp, _ = x_prompt.shape
    bs, ss, _ = x_sample.shape
    n_seq = bp + bs
    depth = norm_gains.shape[0]
    groups = [(bp, sp, 0), (bs, ss, bp)]

    n_meta = n_seq * N_META
    meta_rows = -(-n_meta // LANE) * LANE
    pad_rows = lambda x: jnp.pad(x, ((0, meta_rows - x.shape[0]), (0, 0)))
    h_tok = [x_prompt.reshape(bp * sp, D_MODEL), x_sample.reshape(bs * ss, D_MODEL)]
    h_meta = pad_rows(jnp.tile(meta.astype(F32), (n_seq, 1)))

    smax = max(sp, ss)
    t = jnp.arange(smax)
    tab_tok = _rope_tables(t + N_META, t // GRID_W, t % GRID_W)
    mi = jnp.arange(meta_rows) % N_META
    tab_meta = _rope_tables(mi, jnp.full_like(mi, -1), mi)

    w1 = (ffn1_w_gate.astype(BF16), ffn1_w_up.astype(BF16), ffn1_w_down.astype(BF16))
    w2 = (ffn2_w_gate.astype(BF16), ffn2_w_up.astype(BF16), ffn2_w_down.astype(BF16))

    for i in range(depth):
        gains = jnp.pad(norm_gains[i], ((0, 2), (0, 0)))
        j = i // 2
        h_tok = [_ffn1(h, gains, *w1, i) for h in h_tok]
        h_meta = _ffn1(h_meta, gains, *w1, i)
        if i % 2 == 0:
            w, w_out = _dense_weights(attn_w_in[j], mla_q_norm[j], mla_w_uq[j], mla_kv_norm[j],
                                      mla_w_ukv[j], gqa_q_norm[j], gqa_k_norm[j], attn_w_out[j])
            qkv_tok = [_proj_dense(h, gains, w, tab_tok, s) for h, (_, s, _) in zip(h_tok, groups)]
            qmt, km, vmt = _proj_dense(h_meta, gains, w, tab_meta, 0)
            kmp = _pad_meta(km[:n_meta], n_seq)
            vmt = vmt.transpose(1, 0, 2).reshape(V_ROWS, meta_rows)
            vmtp = vmt[:, :n_meta].reshape(V_ROWS, n_seq, N_META).transpose(1, 0, 2)
            vmtp = jnp.pad(vmtp, ((0, 0), (0, 0), (0, LANE - N_META)))
            qmt = qmt.transpose(1, 0, 2).reshape(HEAD_SLOTS * LANE, meta_rows)
            qmtp = qmt[:, :n_meta].reshape(HEAD_SLOTS * LANE, n_seq, N_META).transpose(1, 0, 2)
            qmtp = jnp.pad(qmtp, ((0, 0), (0, 0), (0, LANE - N_META)))
            o_tok, o_meta = [], []
            for (qt, k, vt), (nb, s, b0) in zip(qkv_tok, groups):
                o_tok.append(_dense_attn(qt, k, vt, kmp, vmtp, n_seq=nb, seq=s, q_base=0, meta_base=b0))
                om = _dense_attn(qmtp, k, vt, kmp, vmtp, n_seq=nb, seq=s, q_base=b0, meta_base=b0)
                o_meta.append(om.reshape(nb, LANE, D_MODEL)[:, :N_META].reshape(nb * N_META, D_MODEL))
            o_meta = pad_rows(jnp.concatenate(o_meta, axis=0))
        else:
            w_qkv = na_w_qkv[j].astype(BF16)
            w_out = na_w_out[j].astype(BF16)
            bias, mb_t, mb = _na_bias_tables(na_rpb[j], na_meta_bias[j])
            qkv_tok = [_proj_na(h, gains, w_qkv, True) for h in h_tok]
            qm, km, vm = _proj_na(h_meta, gains, w_qkv, False)
            kmp, vmp = _pad_meta(km[:n_meta], n_seq), _pad_meta(vm[:n_meta], n_seq)
            km16 = km[:n_meta].reshape(n_seq, N_META, NA_HEADS * NA_HEAD_DIM)
            vmtp = vmp.transpose(0, 2, 1)
            o_tok = [_na_attn(q, k, vt, km16, vmtp, bias, mb_t, n_seq=nb, seq=s, meta_base=b0)
                     for (q, k, vt), (nb, s, b0) in zip(qkv_tok, groups)]
            o_meta = pad_rows(_na_meta(qm[:n_meta], kmp, vmp, mb))
        h_tok = [_mix_ffn2(h, o, w_out, gains, *w2, i) for h, o in zip(h_tok, o_tok)]
        h_meta = _mix_ffn2(h_meta, o_meta, w_out, gains, *w2, i)

    return (h_tok[0].reshape(bp, sp, D_MODEL), h_tok[1].reshape(bs, ss, D_MODEL))
```

```python
import functools
import math

import jax
import jax.numpy as jnp
import numpy as np
from jax import lax
from jax.experimental import pallas as pl
from jax.experimental.pallas import tpu as pltpu

F32 = jnp.float32
BF16 = jnp.bfloat16

D_MODEL = 1024
N_META = 16
GRID_W = 64
D_FF = 2816
EPS = 1e-6
NEG_INF = -1e30
LOG2E = math.log2(math.e)

MLA_HEADS = 8
MLA_Q_LORA = 256
MLA_KV_LORA = 128
MLA_NOPE = 64
MLA_ROPE = 32
MLA_V = 64
GQA_HEADS = 8
GQA_KV_HEADS = 2
GQA_HEAD_DIM = 64
ROPE_THETA = 10000.0
NA_HEADS = 16
NA_HEAD_DIM = 64
NA_WIN_R = 8
NA_WIN_C = 16

LANE = 128
HEAD_SLOTS = MLA_HEADS + GQA_HEADS
K_SLOTS = MLA_HEADS + 1
V_ROWS = (MLA_HEADS + GQA_KV_HEADS) * MLA_V
SUBLANE = 8
DENSE_LOOKAHEAD, DENSE_SLOTS = 2, 4
NA_LOOKAHEAD, NA_SLOTS = 2, 4
VMEM_LIMIT = 56 * 1024 * 1024

_C_CQ = 0
_C_CKV = _C_CQ + MLA_Q_LORA
_C_KRA = _C_CKV + MLA_KV_LORA
_C_KRB = _C_KRA + LANE
_C_GQA = _C_KRB + LANE
_C_GQB = _C_GQA + GQA_HEADS // 2 * LANE
_C_GKA = _C_GQB + GQA_HEADS // 2 * LANE
_C_GKB = _C_GKA + LANE
_C_GV = _C_GKB + LANE
_C_END = _C_GV + LANE


def _const_spec(shape):
    nd = len(shape)
    return pl.BlockSpec(shape, lambda *_: (0,) * nd, pipeline_mode=pl.Buffered(1))


def _rms(x, g):
    ms = jnp.mean(x * x, axis=-1, keepdims=True)
    return x * lax.rsqrt(ms + EPS) * g


def _dot(a, b):
    return jnp.dot(a, b, preferred_element_type=F32)


def _dot_nt(a, b):
    return lax.dot_general(a, b, (((1,), (1,)), ((), ())), preferred_element_type=F32)


def _row_tile(rows, want):
    t = min(rows, want)
    while rows % t:
        t //= 2
    return t


def _ffn_body(h, g_ref, pre, post, wg_ref, wu_ref, wd_ref):
    xn = _rms(h, g_ref[pre:pre + 1, :]).astype(BF16)
    gate = _dot(xn, wg_ref[...])
    up = _dot(xn, wu_ref[...])
    act = (gate * jax.nn.sigmoid(gate) * up).astype(BF16)
    y = _dot(act, wd_ref[...])
    return h + 0.5 * _rms(y, g_ref[post:post + 1, :])


def _ffn1_kernel(h_ref, g_ref, wg_ref, wu_ref, wd_ref, out_ref):
    out_ref[...] = _ffn_body(h_ref[...], g_ref, 0, 1, wg_ref, wu_ref, wd_ref)


def _mix_ffn2_kernel(h_ref, o_ref, wo_ref, g_ref, wg_ref, wu_ref, wd_ref, out_ref):
    mixed = _dot(o_ref[...], wo_ref[...])
    h = h_ref[...] + _rms(mixed, g_ref[3:4, :])
    out_ref[...] = _ffn_body(h, g_ref, 4, 5, wg_ref, wu_ref, wd_ref)


def _layer_spec(w, layer):
    return pl.BlockSpec((None,) + w.shape[1:], lambda *_: (layer, 0, 0), pipeline_mode=pl.Buffered(1))


def _ffn1(h, gains, wg, wu, wd, layer):
    rows = h.shape[0]
    tm = _row_tile(rows, 512)
    return pl.pallas_call(
        _ffn1_kernel,
        grid=(rows // tm,),
        in_specs=[
            pl.BlockSpec((tm, D_MODEL), lambda i: (i, 0)),
            _const_spec(gains.shape),
            _layer_spec(wg, layer), _layer_spec(wu, layer), _layer_spec(wd, layer),
        ],
        out_specs=pl.BlockSpec((tm, D_MODEL), lambda i: (i, 0)),
        out_shape=jax.ShapeDtypeStruct((rows, D_MODEL), F32),
        compiler_params=pltpu.CompilerParams(
            dimension_semantics=("arbitrary",), vmem_limit_bytes=VMEM_LIMIT),
        name="ffn1",
    )(h, gains, wg, wu, wd)


def _mix_ffn2(h, o, wo, gains, wg, wu, wd, layer):
    rows = h.shape[0]
    tm = _row_tile(rows, 512)
    return pl.pallas_call(
        _mix_ffn2_kernel,
        grid=(rows // tm,),
        in_specs=[
            pl.BlockSpec((tm, D_MODEL), lambda i: (i, 0)),
            pl.BlockSpec((tm, o.shape[1]), lambda i: (i, 0)),
            _const_spec(wo.shape),
            _const_spec(gains.shape),
            _layer_spec(wg, layer), _layer_spec(wu, layer), _layer_spec(wd, layer),
        ],
        out_specs=pl.BlockSpec((tm, D_MODEL), lambda i: (i, 0)),
        out_shape=jax.ShapeDtypeStruct((rows, D_MODEL), F32),
        compiler_params=pltpu.CompilerParams(
            dimension_semantics=("arbitrary",), vmem_limit_bytes=VMEM_LIMIT),
        name="mix_ffn2",
    )(h, o, wo, gains, wg, wu, wd)


def _proj_dense_kernel(h_ref, g_ref, win_ref, qn_ref, wuq_ref, kvn_ref, wukv_ref,
                       gqa_ref, gqb_ref, gka_ref, gkb_ref, exp_ref, tab_ref, qt_ref, k_ref, vt_ref):
    a = _rms(h_ref[...], g_ref[2:3, :]).astype(BF16)
    proj = _dot(a, win_ref[...])
    tab = sum(_dot(tab_ref[i], exp_ref[...]) for i in range(tab_ref.shape[0]))
    cos_k, sin_k = tab[:, 0:LANE], tab[:, LANE:2 * LANE]
    cos_g, sin_g = tab[:, 2 * LANE:3 * LANE], tab[:, 3 * LANE:4 * LANE]
    qs = (MLA_NOPE + MLA_ROPE) ** -0.5 * LOG2E
    lane = lax.broadcasted_iota(jnp.int32, cos_k.shape, 1)
    cos_q = jnp.where(lane < MLA_NOPE, qs, cos_k * qs)
    sin_q = sin_k * qs

    cqn = _rms(proj[:, _C_CQ:_C_CQ + MLA_Q_LORA], qn_ref[...]).astype(BF16)
    qab = _dot(cqn, wuq_ref[...])
    nq = MLA_HEADS * LANE
    for h in range(MLA_HEADS):
        qa = qab[:, h * LANE:(h + 1) * LANE]
        qb = qab[:, nq + h * LANE:nq + (h + 1) * LANE]
        qt_ref[0, h * LANE:(h + 1) * LANE, :] = (qa * cos_q + qb * sin_q).T.astype(BF16)

    ckvn = _rms(proj[:, _C_CKV:_C_CKV + MLA_KV_LORA], kvn_ref[...]).astype(BF16)
    kv = _dot(ckvn, wukv_ref[...])
    k_rope = (proj[:, _C_KRA:_C_KRA + LANE] * cos_k + proj[:, _C_KRB:_C_KRB + LANE] * sin_k)
    for h in range(MLA_HEADS):
        k_ref[:, h * LANE:(h + 1) * LANE] = (kv[:, h * LANE:(h + 1) * LANE] + k_rope).astype(BF16)
    nv = MLA_HEADS * MLA_V

    low = lane < GQA_HEAD_DIM

    def normed_rotary(xa, xb, cos, sin):
        sq = xa * xa
        ss_lo = jnp.sum(jnp.where(low, sq, 0.0), axis=-1, keepdims=True)
        ss_hi = jnp.sum(jnp.where(low, 0.0, sq), axis=-1, keepdims=True)
        r = lax.rsqrt(jnp.where(low, ss_lo, ss_hi) * (1.0 / GQA_HEAD_DIM) + EPS)
        return (xa * cos + xb * sin) * r

    gq_scale = GQA_HEAD_DIM ** -0.5 * LOG2E
    cq_g = cos_g * (gqa_ref[...] * gq_scale)
    sq_g = sin_g * (gqb_ref[...] * gq_scale)
    per_kv = GQA_HEADS // GQA_KV_HEADS
    zeros_t = jnp.zeros((LANE - GQA_HEAD_DIM, qt_ref.shape[2]), F32)
    for j in range(GQA_HEADS // 2):
        xa = proj[:, _C_GQA + j * LANE:_C_GQA + (j + 1) * LANE]
        xb = proj[:, _C_GQB + j * LANE:_C_GQB + (j + 1) * LANE]
        y_t = normed_rotary(xa, xb, cq_g, sq_g).T
        for half in range(2):
            h = 2 * j + half
            q_t = y_t[half * GQA_HEAD_DIM:(half + 1) * GQA_HEAD_DIM]
            rows = [q_t, zeros_t] if h // per_kv == 0 else [zeros_t, q_t]
            qt_ref[0, nq + h * LANE:nq + (h + 1) * LANE, :] = jnp.concatenate(rows, axis=0).astype(BF16)
    xa = proj[:, _C_GKA:_C_GKA + LANE]
    xb = proj[:, _C_GKB:_C_GKB + LANE]
    k_ref[:, nq:nq + LANE] = normed_rotary(xa, xb, cos_g * gka_ref[...], sin_g * gkb_ref[...]).astype(BF16)
    v = jnp.concatenate([kv[:, nq:nq + nv], proj[:, _C_GV:_C_GV + LANE]], axis=1)
    vt_ref[0] = v.T.astype(BF16)


def _proj_dense(h, gains, w, tab, seq):
    rows = h.shape[0]
    tm = _row_tile(seq if seq else rows, 512)
    nblk = (seq // tm) if seq else 1
    consts = [w["w_in"], w["q_norm"], w["w_uq"], w["kv_norm"], w["w_ukv"],
              w["gq_a"], w["gq_b"], w["gk_a"], w["gk_b"], w["rope_expand"]]
    return pl.pallas_call(
        _proj_dense_kernel,
        grid=(rows // tm,),
        in_specs=[pl.BlockSpec((tm, D_MODEL), lambda i: (i, 0)), _const_spec(gains.shape)]
        + [_const_spec(c.shape) for c in consts]
        + [pl.BlockSpec((tab.shape[0], tm, LANE), lambda i: (0, i % nblk, 0))],
        out_specs=[
            pl.BlockSpec((1, HEAD_SLOTS * LANE, tm), lambda i: (i, 0, 0)),
            pl.BlockSpec((tm, K_SLOTS * LANE), lambda i: (i, 0)),
            pl.BlockSpec((1, V_ROWS, tm), lambda i: (i, 0, 0)),
        ],
        out_shape=[
            jax.ShapeDtypeStruct((rows // tm, HEAD_SLOTS * LANE, tm), BF16),
            jax.ShapeDtypeStruct((rows, K_SLOTS * LANE), BF16),
            jax.ShapeDtypeStruct((rows // tm, V_ROWS, tm), BF16),
        ],
        compiler_params=pltpu.CompilerParams(
            dimension_semantics=("arbitrary",), vmem_limit_bytes=VMEM_LIMIT),
        name="proj_dense",
    )(h, gains, *consts, tab)


def _head_slots(h):
    if h < MLA_HEADS:
        return h, h
    kvh = (h - MLA_HEADS) // (GQA_HEADS // GQA_KV_HEADS)
    return MLA_HEADS, MLA_HEADS + kvh


def _sublane_bcast_max(x):
    return jnp.broadcast_to(jnp.max(x, axis=0, keepdims=True), x.shape)


def _dense_attn_kernel(qt_ref, k_ref, vt_ref, km_ref, vmt_ref, o_ref,
                       m_ref, smax_ref, acc_ref, s_ref, sm_ref):
    kv = pl.program_id(2)
    tq = qt_ref.shape[1]
    n_sub, _, tk = vt_ref.shape
    hd = MLA_V
    acc_rows = acc_ref.shape[1]

    def with_ones(vt):
        return jnp.concatenate([vt, jnp.ones((acc_rows - hd, vt.shape[1]), BF16)], axis=0)

    @pl.when(kv == 0)
    def _():
        for h in range(HEAD_SLOTS):
            ks, _ = _head_slots(h)
            qt = qt_ref[h * LANE:(h + 1) * LANE, :]
            sm_ref[h] = _dot(km_ref[0:N_META, ks * LANE:(ks + 1) * LANE], qt)
        zeros = jnp.zeros((LANE - N_META, tq), F32)
        for h in range(HEAD_SLOTS):
            _, vh = _head_slots(h)
            s3 = sm_ref[h].reshape(N_META // SUBLANE, SUBLANE, tq)
            m = _sublane_bcast_max(jnp.max(s3, axis=0))
            p3 = jnp.exp2(s3 - m[None])
            m_ref[h] = m
            p = jnp.concatenate([p3.reshape(N_META, tq), zeros], axis=0).astype(BF16)
            acc_ref[h] = _dot(with_ones(vmt_ref[vh * hd:(vh + 1) * hd, :]), p)

    n_slots = s_ref.shape[0]

    def scores(t, h):
        ks, _ = _head_slots(h)
        k0 = pl.multiple_of(t * tk, tk)
        k = k_ref[pl.ds(k0, tk), ks * LANE:(ks + 1) * LANE]
        s = _dot(k, qt_ref[h * LANE:(h + 1) * LANE, :])
        s_ref[h % n_slots] = s
        smax_ref[h] = jnp.max(s.reshape(tk // SUBLANE, SUBLANE, tq), axis=0)

    def softmax_pv(t, h):
        _, vh = _head_slots(h)
        m_prev = m_ref[h]
        m_new = jnp.maximum(m_prev, _sublane_bcast_max(smax_ref[h]))
        alpha = jnp.exp2(m_prev - m_new)
        s3 = s_ref[h % n_slots].reshape(tk // SUBLANE, SUBLANE, tq)
        p = jnp.exp2(s3 - m_new[None]).reshape(tk, tq).astype(BF16)
        pv = _dot(with_ones(vt_ref[t, vh * hd:(vh + 1) * hd, :]), p)
        acc = acc_ref[h].reshape(acc_rows // SUBLANE, SUBLANE, tq) * alpha[None]
        acc_ref[h] = acc.reshape(acc_rows, tq) + pv
        m_ref[h] = m_new

    for h in range(DENSE_LOOKAHEAD):
        scores(0, h)

    def sub_tile(t, carry):
        t_next = jnp.minimum(t + 1, n_sub - 1)
        for h in range(HEAD_SLOTS):
            ahead = h + DENSE_LOOKAHEAD
            if ahead < HEAD_SLOTS:
                scores(t, ahead)
            else:
                scores(t_next, ahead - HEAD_SLOTS)
            softmax_pv(t, h)
        return carry

    lax.fori_loop(0, n_sub, sub_tile, 0, unroll=2)

    @pl.when(kv == pl.num_programs(2) - 1)
    def _():
        for j in range(HEAD_SLOTS // 2):
            outs = []
            for h in (2 * j, 2 * j + 1):
                outs.append(acc_ref[h, 0:hd, :] / acc_ref[h, hd:hd + 1, :])
            o_t = jnp.concatenate(outs, axis=0)
            o_ref[:, j * LANE:(j + 1) * LANE] = o_t.T.astype(BF16)


def _dense_attn(qt, k, vt, km, vmt, *, n_seq, seq, q_base, meta_base):
    tq = qt.shape[2]
    nq = (qt.shape[0] - q_base) // n_seq
    tk = vt.shape[2]
    n_sub = _row_tile(seq // tk, 8)
    nk = seq // (tk * n_sub)
    out_rows = n_seq * nq * tq
    return pl.pallas_call(
        _dense_attn_kernel,
        grid=(n_seq, nq, nk),
        in_specs=[
            pl.BlockSpec((None, HEAD_SLOTS * LANE, tq), lambda b, i, j: (q_base + b * nq + i, 0, 0)),
            pl.BlockSpec((n_sub * tk, K_SLOTS * LANE), lambda b, i, j: (b * nk + j, 0)),
            pl.BlockSpec((n_sub, V_ROWS, tk), lambda b, i, j: (b * nk + j, 0, 0)),
            pl.BlockSpec((None, LANE, K_SLOTS * LANE), lambda b, i, j: (meta_base + b, 0, 0)),
            pl.BlockSpec((None, V_ROWS, LANE), lambda b, i, j: (meta_base + b, 0, 0)),
        ],
        out_specs=pl.BlockSpec((tq, D_MODEL), lambda b, i, j: (b * nq + i, 0)),
        out_shape=jax.ShapeDtypeStruct((out_rows, D_MODEL), BF16),
        scratch_shapes=[
            pltpu.VMEM((HEAD_SLOTS, SUBLANE, tq), F32),
            pltpu.VMEM((HEAD_SLOTS, SUBLANE, tq), F32),
            pltpu.VMEM((HEAD_SLOTS, MLA_V + 2 * SUBLANE, tq), F32),
            pltpu.VMEM((DENSE_SLOTS, tk, tq), F32),
            pltpu.VMEM((HEAD_SLOTS, N_META, tq), F32),
        ],
        compiler_params=pltpu.CompilerParams(
            dimension_semantics=("arbitrary", "arbitrary", "arbitrary"),
            vmem_limit_bytes=VMEM_LIMIT),
        name="dense_attn",
    )(qt, k, vt, km, vmt)


def _proj_na_kernel(h_ref, g_ref, w_ref, q_ref, k_ref, v_ref, *, transpose_v):
    a = _rms(h_ref[...], g_ref[2:3, :]).astype(BF16)
    qkv = _dot(a, w_ref[...])
    n = NA_HEADS * NA_HEAD_DIM
    q_ref[...] = (qkv[:, 0:n] * (NA_HEAD_DIM ** -0.5 * LOG2E)).astype(BF16)
    k_ref[...] = qkv[:, n:2 * n].astype(BF16)
    v = qkv[:, 2 * n:3 * n]
    if transpose_v:
        vt = v.T.astype(BF16)
        for t in range(v_ref.shape[0]):
            v_ref[t] = vt[:, t * LANE:(t + 1) * LANE]
    else:
        v_ref[...] = v.astype(BF16)


def _proj_na(h, gains, w, transpose_v):
    rows = h.shape[0]
    tm = _row_tile(rows, 512)
    n = NA_HEADS * NA_HEAD_DIM
    if transpose_v:
        v_spec = pl.BlockSpec((tm // LANE, n, LANE), lambda i: (i, 0, 0))
        v_shape = jax.ShapeDtypeStruct((rows // LANE, n, LANE), BF16)
    else:
        v_spec = pl.BlockSpec((tm, n), lambda i: (i, 0))
        v_shape = jax.ShapeDtypeStruct((rows, n), BF16)
    return pl.pallas_call(
        functools.partial(_proj_na_kernel, transpose_v=transpose_v),
        grid=(rows // tm,),
        in_specs=[pl.BlockSpec((tm, D_MODEL), lambda i: (i, 0)), _const_spec(gains.shape),
                  _const_spec(w.shape)],
        out_specs=[pl.BlockSpec((tm, n), lambda i: (i, 0))] * 2 + [v_spec],
        out_shape=[jax.ShapeDtypeStruct((rows, n), BF16)] * 2 + [v_shape],
        compiler_params=pltpu.CompilerParams(
            dimension_semantics=("arbitrary",), vmem_limit_bytes=VMEM_LIMIT),
        name="proj_na",
    )(h, gains, w)


NA_SPAN_R = NA_WIN_R + 2
NA_MASKED = 2 * NA_WIN_R - 1


def _na_kernel(q_ref, k_ref, vt_ref, km_ref, vmt_ref, bias_ref, mb_ref, o_ref, s_ref,
               *, rows, rows_per_step):
    step = pl.program_id(1)
    n_pairs = rows_per_step // 2
    n_hp = NA_HEADS // 2
    span = NA_SPAN_R * GRID_W
    lane = lax.broadcasted_iota(jnp.int32, (GRID_W, LANE), 1)
    first = lane < (LANE // 2)
    zeros_m = jnp.zeros((LANE - N_META, 2 * LANE), F32)
    ones_v = jnp.ones((2 * SUBLANE, span + LANE), BF16)

    def geometry(rp):
        ra = step * rows_per_step + 2 * rp
        rs = [jnp.clip(ra + x - NA_WIN_R // 2, 0, rows - NA_WIN_R) for x in range(2)]
        ws = jnp.minimum((rs[0] // 2) * 2, rows - NA_SPAN_R)
        return ra, rs, ws

    n_slots = s_ref.shape[0]

    def scores(rp, hp):
        slot = hp % n_slots
        _, _, ws = geometry(rp)
        cols = slice(hp * LANE, (hp + 1) * LANE)
        parts = []
        for x in range(2):
            q0 = pl.multiple_of((2 * rp + x) * GRID_W, GRID_W)
            qx = q_ref[pl.ds(q0, GRID_W), cols]
            parts += [jnp.where(first, qx, jnp.zeros_like(qx)), jnp.where(first, jnp.zeros_like(qx), qx)]
        qblk = jnp.concatenate(parts, axis=0)
        k0 = pl.multiple_of(ws * GRID_W, 2 * GRID_W)
        s_ref[slot, 0:span, :] = _dot_nt(k_ref[pl.ds(k0, span), cols], qblk)
        s_ref[slot, span:span + N_META, :] = _dot_nt(km_ref[:, cols], qblk)

    for hp in range(NA_LOOKAHEAD):
        scores(0, hp)

    def row_pair(rp, carry):
        ra, rs, ws = geometry(rp)
        idx = []
        for jj in range(NA_SPAN_R):
            kr = ws + jj
            idx.append([jnp.where((kr >= rs[x]) & (kr < rs[x] + NA_WIN_R),
                                  kr - (ra + x) + NA_WIN_R - 1, NA_MASKED) for x in range(2)])
        t0 = ws // 2
        for hp in range(n_hp):
            slot = hp % n_slots
            ahead = hp + NA_LOOKAHEAD
            if ahead < n_hp:
                scores(rp, ahead)
            else:
                scores(jnp.minimum(rp + 1, n_pairs - 1), ahead - n_hp)
            cols = slice(hp * LANE, (hp + 1) * LANE)
            b = jnp.concatenate(
                [jnp.concatenate([bias_ref[hp, idx[jj][0]], bias_ref[hp, idx[jj][1]]], axis=1)
                 for jj in range(NA_SPAN_R)], axis=0)
            s = s_ref[slot, 0:span, :] + b
            mb = mb_ref[hp]
            sm = s_ref[slot, span:span + N_META, :] + jnp.concatenate([mb, mb], axis=1)
            s3 = s.reshape(span // SUBLANE, SUBLANE, 2 * LANE)
            sm3 = sm.reshape(N_META // SUBLANE, SUBLANE, 2 * LANE)
            m = _sublane_bcast_max(jnp.maximum(jnp.max(s3, axis=0), jnp.max(sm3, axis=0)))
            p = jnp.exp2(s3 - m[None]).reshape(span, 2 * LANE).astype(BF16)
            pm3 = jnp.exp2(sm3 - m[None])
            pm = jnp.concatenate([pm3.reshape(N_META, 2 * LANE), zeros_m], axis=0).astype(BF16)
            v_all = jnp.concatenate([vt_ref[t0 + t, cols, :] for t in range(span // LANE)]
                                    + [vmt_ref[cols, :]], axis=1)
            o_t = _dot(jnp.concatenate([v_all, ones_v], axis=0),
                       jnp.concatenate([p, pm], axis=0))
            o_t = o_t[0:LANE] / o_t[LANE:LANE + 1]
            for x in range(2):
                blk = o_t[:, x * LANE:(x + 1) * LANE].T
                q0 = pl.multiple_of((2 * rp + x) * GRID_W, GRID_W)
                o_ref[pl.ds(q0, GRID_W), cols] = jnp.where(
                    first, blk[0:GRID_W], blk[GRID_W:2 * GRID_W]).astype(BF16)
        return carry

    lax.fori_loop(0, n_pairs, row_pair, 0, unroll=2)


def _na_attn(q, k, vt, km, vmt, bias, mb, *, n_seq, seq, meta_base):
    rows = seq // GRID_W
    assert rows >= NA_SPAN_R and rows % 2 == 0
    rps = 8
    nsteps = rows // rps
    n = NA_HEADS * NA_HEAD_DIM
    span = NA_SPAN_R * GRID_W
    return pl.pallas_call(
        functools.partial(_na_kernel, rows=rows, rows_per_step=rps),
        grid=(n_seq, nsteps),
        in_specs=[
            pl.BlockSpec((rps * GRID_W, n), lambda b, i: (b * nsteps + i, 0)),
            pl.BlockSpec((seq, n), lambda b, i: (b, 0), pipeline_mode=pl.Buffered(1)),
            pl.BlockSpec((seq // LANE, n, LANE), lambda b, i: (b, 0, 0), pipeline_mode=pl.Buffered(1)),
            pl.BlockSpec((None, N_META, n), lambda b, i: (meta_base + b, 0, 0)),
            pl.BlockSpec((None, n, LANE), lambda b, i: (meta_base + b, 0, 0)),
            _const_spec(bias.shape),
            _const_spec(mb.shape),
        ],
        out_specs=pl.BlockSpec((rps * GRID_W, n), lambda b, i: (b * nsteps + i, 0)),
        out_shape=jax.ShapeDtypeStruct((n_seq * seq, n), BF16),
        scratch_shapes=[pltpu.VMEM((NA_SLOTS, span + N_META, 2 * LANE), F32)],
        compiler_params=pltpu.CompilerParams(
            dimension_semantics=("arbitrary", "arbitrary"), vmem_limit_bytes=VMEM_LIMIT),
        name="na_attn",
    )(q, k, vt, km, vmt, bias, mb)


def _na_meta_kernel(q_ref, km_ref, vm_ref, mb_ref, o_ref):
    lane = lax.broadcasted_iota(jnp.int32, (N_META, LANE), 1)
    first = lane < (LANE // 2)
    for j in range(NA_HEADS // 2):
        cols = slice(j * LANE, (j + 1) * LANE)
        qp = q_ref[:, cols]
        km = km_ref[:, cols]
        vm = vm_ref[:, cols]
        outs = []
        for half in range(2):
            h = 2 * j + half
            qh = jnp.where(first if half == 0 else jnp.logical_not(first), qp, jnp.zeros_like(qp))
            sm = _dot_nt(qh, km)
            sm = jnp.where(lane < N_META, sm + mb_ref[h:h + 1, :], NEG_INF)
            m = jnp.max(sm, axis=-1, keepdims=True)
            pm = jnp.exp2(sm - m)
            l = jnp.sum(pm, axis=-1, keepdims=True)
            outs.append(_dot(pm.astype(BF16), vm) / l)
        o_ref[:, cols] = jnp.where(first, outs[0], outs[1]).astype(BF16)


def _na_meta(qm, km, vm, mb):
    n_seq = km.shape[0]
    n = NA_HEADS * NA_HEAD_DIM
    return pl.pallas_call(
        _na_meta_kernel,
        grid=(n_seq,),
        in_specs=[
            pl.BlockSpec((N_META, n), lambda b: (b, 0)),
            pl.BlockSpec((None, LANE, n), lambda b: (b, 0, 0)),
            pl.BlockSpec((None, LANE, n), lambda b: (b, 0, 0)),
            _const_spec(mb.shape),
        ],
        out_specs=pl.BlockSpec((N_META, n), lambda b: (b, 0)),
        out_shape=jax.ShapeDtypeStruct((n_seq * N_META, n), BF16),
        compiler_params=pltpu.CompilerParams(dimension_semantics=("arbitrary",)),
        name="na_meta",
    )(qm, km, vm, mb)


def _take_cols(w, idx):
    idx = np.asarray(idx)
    neg = idx < 0
    same_run = np.where(neg[1:] | neg[:-1], neg[1:] & neg[:-1], np.diff(idx) == 1)
    breaks = np.flatnonzero(~same_run) + 1
    parts = []
    for run in np.split(idx, breaks):
        if run[0] < 0:
            parts.append(jnp.zeros((w.shape[0], len(run)), w.dtype))
        else:
            parts.append(w[:, int(run[0]):int(run[-1]) + 1])
    return jnp.concatenate(parts, axis=1)


def _swap_halves(n):
    half = n // 2
    return np.concatenate([np.arange(half, n), np.arange(0, half)])


def _dense_weights(w_in, q_norm, w_uq, kv_norm, w_ukv, gq_norm, gk_norm, w_out):
    pad = lambda k: -np.ones(k, np.int64)
    o_kr = MLA_Q_LORA + MLA_KV_LORA
    o_gq = o_kr + MLA_ROPE
    o_gk = o_gq + GQA_HEADS * GQA_HEAD_DIM
    o_gv = o_gk + GQA_KV_HEADS * GQA_HEAD_DIM
    axial = np.concatenate([_swap_halves(GQA_HEAD_DIM // 2),
                            GQA_HEAD_DIM // 2 + _swap_halves(GQA_HEAD_DIM // 2)])
    idx = [np.arange(0, o_kr)]
    idx += [pad(MLA_NOPE), o_kr + np.arange(MLA_ROPE), pad(LANE - MLA_NOPE - MLA_ROPE)]
    idx += [pad(MLA_NOPE), o_kr + _swap_halves(MLA_ROPE), pad(LANE - MLA_NOPE - MLA_ROPE)]
    assert 2 * GQA_HEAD_DIM == LANE and GQA_KV_HEADS == 2
    for h in range(GQA_HEADS):
        idx += [o_gq + h * GQA_HEAD_DIM + np.arange(GQA_HEAD_DIM)]
    for h in range(GQA_HEADS):
        idx += [o_gq + h * GQA_HEAD_DIM + axial]
    for h in range(GQA_KV_HEADS):
        idx += [o_gk + h * GQA_HEAD_DIM + np.arange(GQA_HEAD_DIM)]
    for h in range(GQA_KV_HEADS):
        idx += [o_gk + h * GQA_HEAD_DIM + axial]
    idx += [o_gv + np.arange(GQA_KV_HEADS * GQA_HEAD_DIM)]
    idx = np.concatenate(idx)
    assert idx.shape[0] == _C_END
    w_in2 = _take_cols(w_in.astype(BF16), idx)

    hd = MLA_NOPE + MLA_ROPE
    ia, ib = [], []
    for h in range(MLA_HEADS):
        ia += [h * hd + np.arange(hd), pad(LANE - hd)]
        ib += [pad(MLA_NOPE), h * hd + MLA_NOPE + _swap_halves(MLA_ROPE), pad(LANE - hd)]
    w_uq2 = _take_cols(w_uq.astype(BF16), np.concatenate(ia + ib))

    kvd = MLA_NOPE + MLA_V
    ik, iv = [], []
    for h in range(MLA_HEADS):
        ik += [h * kvd + np.arange(MLA_NOPE), pad(LANE - MLA_NOPE)]
        iv += [h * kvd + MLA_NOPE + np.arange(MLA_V)]
    w_ukv2 = _take_cols(w_ukv.astype(BF16), np.concatenate(ik + iv))

    def gain_pair(g):
        ga = jnp.concatenate([g, g])[None, :]
        gb = jnp.tile(g[jnp.asarray(axial)], 2)[None, :]
        return ga, gb

    gq_a, gq_b = gain_pair(gq_norm)
    gk_a, gk_b = gain_pair(gk_norm)

    w_out2 = w_out.astype(BF16)

    return dict(w_in=w_in2, q_norm=q_norm[None, :], w_uq=w_uq2, kv_norm=kv_norm[None, :],
                w_ukv=w_ukv2, gq_a=gq_a, gq_b=gq_b, gk_a=gk_a, gk_b=gk_b,
                rope_expand=_rope_expand_matrix()), w_out2


def _rope_tables(pos, row, col):
    half = MLA_ROPE // 2
    inv = 1.0 / (ROPE_THETA ** (jnp.arange(half, dtype=F32) / half))
    n = pos.shape[0]

    def cs(p):
        ang = p.astype(F32)[:, None] * inv[None, :]
        return jnp.cos(ang), jnp.sin(ang)

    vals = jnp.concatenate(cs(pos) + cs(row) + cs(col) + (jnp.zeros((n, LANE - 6 * half), F32),), axis=1)
    hi = vals.astype(BF16)
    rest = vals - hi.astype(F32)
    mid = rest.astype(BF16)
    lo = (rest - mid.astype(F32)).astype(BF16)
    return jnp.stack([hi, mid, lo])


def _rope_expand_matrix():
    half = MLA_ROPE // 2
    c1, s1, cr, sr, cc, sc = range(6)
    tail = LANE - MLA_NOPE - MLA_ROPE
    blank = [(None, 0)]
    layout = (blank * (MLA_NOPE // half) + [(c1, 1), (c1, 1)] + blank * (tail // half)
              + blank * (MLA_NOPE // half) + [(s1, -1), (s1, 1)] + blank * (tail // half)
              + [(cr, 1), (cr, 1), (cc, 1), (cc, 1)] * (LANE // GQA_HEAD_DIM)
              + [(sr, -1), (sr, 1), (sc, -1), (sc, 1)] * (LANE // GQA_HEAD_DIM))
    expand = np.zeros((LANE, len(layout) * half), np.float32)
    for blk, (src, sign) in enumerate(layout):
        if src is not None:
            expand[src * half + np.arange(half), blk * half + np.arange(half)] = sign
    return jnp.asarray(expand, BF16)


def _na_bias_tables(rpb, meta_bias):
    c_idx = np.arange(GRID_W)
    c_start = np.clip(c_idx - NA_WIN_C // 2, 0, GRID_W - NA_WIN_C)
    col_mask = (c_idx[None, :] >= c_start[:, None]) & (c_idx[None, :] < c_start[:, None] + NA_WIN_C)
    col_off = np.clip(c_idx[None, :] - c_idx[:, None] + NA_WIN_C - 1, 0, 2 * NA_WIN_C - 2)
    hp = NA_HEADS // 2
    n_off = 2 * NA_WIN_C - 1
    select = np.zeros((2, n_off, GRID_W, 2, GRID_W), np.float32)
    kc_g, c_g = np.meshgrid(c_idx, c_idx, indexing="ij")
    for half in range(2):
        select[half, col_off[c_g, kc_g], kc_g, half, c_g] = 1.0
    rows = rpb.reshape(hp, 2, NA_MASKED, n_off).transpose(0, 2, 1, 3).reshape(hp * NA_MASKED, 2 * n_off)
    t = jnp.dot(rows, jnp.asarray(select.reshape(2 * n_off, GRID_W * LANE)),
                precision=lax.Precision.HIGHEST) * LOG2E
    t = t.reshape(hp, NA_MASKED, GRID_W, LANE)
    keep = np.tile(col_mask.T, (1, 2))
    t = jnp.where(jnp.asarray(keep)[None, None], t, NEG_INF)
    bias = jnp.concatenate([t, jnp.full_like(t[:, :1], NEG_INF)], axis=1)
    mbl = meta_bias * LOG2E
    mb_t = jnp.repeat(mbl.reshape(hp, 2, N_META).transpose(0, 2, 1), GRID_W, axis=2)
    mb = jnp.pad(mbl, ((0, 0), (0, LANE - N_META)))
    return bias, mb_t, mb


def _pad_meta(x, n_seq):
    c = x.shape[1]
    return jnp.pad(x.reshape(n_seq, N_META, c), ((0, 0), (0, LANE - N_META), (0, 0)))


def kernel(x_prompt, x_sample, meta, norm_gains, ffn1_w_gate, ffn1_w_up, ffn1_w_down, ffn2_w_gate, ffn2_w_up, ffn2_w_down, attn_w_in, mla_q_norm, mla_w_uq, mla_kv_norm, mla_w_ukv, gqa_q_norm, gqa_k_norm, attn_w_out, na_w_qkv, na_rpb, na_meta_bias, na_w_out):
    bp, sp, _ = x_prompt.shape
    bs, ss, _ = x_sample.shape
    n_seq = bp + bs
    depth = norm_gains.shape[0]
    groups = [(bp, sp, 0), (bs, ss, bp)]

    n_meta = n_seq * N_META
    meta_rows = -(-n_meta // LANE) * LANE
    pad_rows = lambda x: jnp.pad(x, ((0, meta_rows - x.shape[0]), (0, 0)))
    h_tok = [x_prompt.reshape(bp * sp, D_MODEL), x_sample.reshape(bs * ss, D_MODEL)]
    h_meta = pad_rows(jnp.tile(meta.astype(F32), (n_seq, 1)))

    smax = max(sp, ss)
    t = jnp.arange(smax)
    tab_tok = _rope_tables(t + N_META, t // GRID_W, t % GRID_W)
    mi = jnp.arange(meta_rows) % N_META
    tab_meta = _rope_tables(mi, jnp.full_like(mi, -1), mi)

    w1 = (ffn1_w_gate.astype(BF16), ffn1_w_up.astype(BF16), ffn1_w_down.astype(BF16))
    w2 = (ffn2_w_gate.astype(BF16), ffn2_w_up.astype(BF16), ffn2_w_down.astype(BF16))

    for i in range(depth):
        gains = jnp.pad(norm_gains[i], ((0, 2), (0, 0)))
        j = i // 2
        h_tok = [_ffn1(h, gains, *w1, i) for h in h_tok]
        h_meta = _ffn1(h_meta, gains, *w1, i)
        if i % 2 == 0:
            w, w_out = _dense_weights(attn_w_in[j], mla_q_norm[j], mla_w_uq[j], mla_kv_norm[j],
                                      mla_w_ukv[j], gqa_q_norm[j], gqa_k_norm[j], attn_w_out[j])
            qkv_tok = [_proj_dense(h, gains, w, tab_tok, s) for h, (_, s, _) in zip(h_tok, groups)]
            qmt, km, vmt = _proj_dense(h_meta, gains, w, tab_meta, 0)
            kmp = _pad_meta(km[:n_meta], n_seq)
            vmt = vmt.transpose(1, 0, 2).reshape(V_ROWS, meta_rows)
            vmtp = vmt[:, :n_meta].reshape(V_ROWS, n_seq, N_META).transpose(1, 0, 2)
            vmtp = jnp.pad(vmtp, ((0, 0), (0, 0), (0, LANE - N_META)))
            qmt = qmt.transpose(1, 0, 2).reshape(HEAD_SLOTS * LANE, meta_rows)
            qmtp = qmt[:, :n_meta].reshape(HEAD_SLOTS * LANE, n_seq, N_META).transpose(1, 0, 2)
            qmtp = jnp.pad(qmtp, ((0, 0), (0, 0), (0, LANE - N_META)))
            o_tok, o_meta = [], []
            for (qt, k, vt), (nb, s, b0) in zip(qkv_tok, groups):
                o_tok.append(_dense_attn(qt, k, vt, kmp, vmtp, n_seq=nb, seq=s, q_base=0, meta_base=b0))
                om = _dense_attn(qmtp, k, vt, kmp, vmtp, n_seq=nb, seq=s, q_base=b0, meta_base=b0)
                o_meta.append(om.reshape(nb, LANE, D_MODEL)[:, :N_META].reshape(nb * N_META, D_MODEL))
            o_meta = pad_rows(jnp.concatenate(o_meta, axis=0))
        else:
            w_qkv = na_w_qkv[j].astype(BF16)
            w_out = na_w_out[j].astype(BF16)
            bias, mb_t, mb = _na_bias_tables(na_rpb[j], na_meta_bias[j])
            qkv_tok = [_proj_na(h, gains, w_qkv, True) for h in h_tok]
            qm, km, vm = _proj_na(h_meta, gains, w_qkv, False)
            kmp, vmp = _pad_meta(km[:n_meta], n_seq), _pad_meta(vm[:n_meta], n_seq)
            km16 = km[:n_meta].reshape(n_seq, N_META, NA_HEADS * NA_HEAD_DIM)
            vmtp = vmp.transpose(0, 2, 1)
            o_tok = [_na_attn(q, k, vt, km16, vmtp, bias, mb_t, n_seq=nb, seq=s, meta_base=b0)
                     for (q, k, vt), (nb, s, b0) in zip(qkv_tok, groups)]
            o_meta = pad_rows(_na_meta(qm[:n_meta], kmp, vmp, mb))
        h_tok = [_mix_ffn2(h, o, w_out, gains, *w2, i) for h, o in zip(h_tok, o_tok)]
        h_meta = _mix_ffn2(h_meta, o_meta, w_out, gains, *w2, i)

    return (h_tok[0].reshape(bp, sp, D_MODEL), h_tok[1].reshape(bs, ss, D_MODEL))
```

```python
import functools
import math

import jax
import jax.numpy as jnp
import numpy as np
from jax import lax
from jax.experimental import pallas as pl
from jax.experimental.pallas import tpu as pltpu

F32 = jnp.float32
BF16 = jnp.bfloat16

D_MODEL = 1024
N_META = 16
GRID_W = 64
D_FF = 2816
EPS = 1e-6
NEG_INF = -1e30
LOG2E = math.log2(math.e)

MLA_HEADS = 8
MLA_Q_LORA = 256
MLA_KV_LORA = 128
MLA_NOPE = 64
MLA_ROPE = 32
MLA_V = 64
GQA_HEADS = 8
GQA_KV_HEADS = 2
GQA_HEAD_DIM = 64
ROPE_THETA = 10000.0
NA_HEADS = 16
NA_HEAD_DIM = 64
NA_WIN_R = 8
NA_WIN_C = 16

LANE = 128
HEAD_SLOTS = MLA_HEADS + GQA_HEADS
K_SLOTS = MLA_HEADS + 1
V_ROWS = (MLA_HEADS + GQA_KV_HEADS) * MLA_V
SUBLANE = 8
DENSE_LOOKAHEAD, DENSE_SLOTS = 2, 4
NA_LOOKAHEAD, NA_SLOTS = 2, 4
VMEM_LIMIT = 56 * 1024 * 1024
ROW_TILE = 512
DENSE_SUBTILES = 8
NA_ROWS_PER_STEP = 16

_C_CQ = 0
_C_CKV = _C_CQ + MLA_Q_LORA
_C_KRA = _C_CKV + MLA_KV_LORA
_C_KRB = _C_KRA + LANE
_C_GQA = _C_KRB + LANE
_C_GQB = _C_GQA + GQA_HEADS // 2 * LANE
_C_GKA = _C_GQB + GQA_HEADS // 2 * LANE
_C_GKB = _C_GKA + LANE
_C_GV = _C_GKB + LANE
_C_END = _C_GV + LANE


def _const_spec(shape):
    nd = len(shape)
    return pl.BlockSpec(shape, lambda *_: (0,) * nd, pipeline_mode=pl.Buffered(1))


def _rms(x, g):
    ms = jnp.mean(x * x, axis=-1, keepdims=True)
    return x * lax.rsqrt(ms + EPS) * g


def _dot(a, b):
    return jnp.dot(a, b, preferred_element_type=F32)


def _dot_nt(a, b):
    return lax.dot_general(a, b, (((1,), (1,)), ((), ())), preferred_element_type=F32)


def _row_tile(rows, want):
    t = min(rows, want)
    while rows % t:
        t //= 2
    return t


def _ffn_body(h, g_ref, pre, post, wg_ref, wu_ref, wd_ref):
    xn = _rms(h, g_ref[pre:pre + 1, :]).astype(BF16)
    gate = _dot(xn, wg_ref[...])
    up = _dot(xn, wu_ref[...])
    act = (gate * jax.nn.sigmoid(gate) * up).astype(BF16)
    y = _dot(act, wd_ref[...])
    return h + 0.5 * _rms(y, g_ref[post:post + 1, :])


def _ffn1_kernel(h_ref, g_ref, wg_ref, wu_ref, wd_ref, out_ref):
    out_ref[...] = _ffn_body(h_ref[...], g_ref, 0, 1, wg_ref, wu_ref, wd_ref)


def _mix_ffn2_kernel(h_ref, o_ref, wo_ref, g_ref, wg_ref, wu_ref, wd_ref, out_ref):
    mixed = _dot(o_ref[...], wo_ref[...])
    h = h_ref[...] + _rms(mixed, g_ref[3:4, :])
    out_ref[...] = _ffn_body(h, g_ref, 4, 5, wg_ref, wu_ref, wd_ref)


def _layer_spec(w, layer):
    return pl.BlockSpec((None,) + w.shape[1:], lambda *_: (layer, 0, 0), pipeline_mode=pl.Buffered(1))


def _ffn1(h, gains, wg, wu, wd, layer):
    rows = h.shape[0]
    tm = _row_tile(rows, ROW_TILE)
    return pl.pallas_call(
        _ffn1_kernel,
        grid=(rows // tm,),
        in_specs=[
            pl.BlockSpec((tm, D_MODEL), lambda i: (i, 0)),
            _const_spec(gains.shape),
            _layer_spec(wg, layer), _layer_spec(wu, layer), _layer_spec(wd, layer),
        ],
        out_specs=pl.BlockSpec((tm, D_MODEL), lambda i: (i, 0)),
        out_shape=jax.ShapeDtypeStruct((rows, D_MODEL), F32),
        compiler_params=pltpu.CompilerParams(
            dimension_semantics=("arbitrary",), vmem_limit_bytes=VMEM_LIMIT),
        name="ffn1",
    )(h, gains, wg, wu, wd)


def _mix_ffn2(h, o, wo, gains, wg, wu, wd, layer):
    rows = h.shape[0]
    tm = _row_tile(rows, ROW_TILE)
    return pl.pallas_call(
        _mix_ffn2_kernel,
        grid=(rows // tm,),
        in_specs=[
            pl.BlockSpec((tm, D_MODEL), lambda i: (i, 0)),
            pl.BlockSpec((tm, o.shape[1]), lambda i: (i, 0)),
            _const_spec(wo.shape),
            _const_spec(gains.shape),
            _layer_spec(wg, layer), _layer_spec(wu, layer), _layer_spec(wd, layer),
        ],
        out_specs=pl.BlockSpec((tm, D_MODEL), lambda i: (i, 0)),
        out_shape=jax.ShapeDtypeStruct((rows, D_MODEL), F32),
        compiler_params=pltpu.CompilerParams(
            dimension_semantics=("arbitrary",), vmem_limit_bytes=VMEM_LIMIT),
        name="mix_ffn2",
    )(h, o, wo, gains, wg, wu, wd)


def _proj_dense_kernel(h_ref, g_ref, win_ref, qn_ref, wuq_ref, kvn_ref, wukv_ref,
                       gqa_ref, gqb_ref, gka_ref, gkb_ref, exp_ref, tab_ref, qt_ref, k_ref, vt_ref):
    a = _rms(h_ref[...], g_ref[2:3, :]).astype(BF16)
    proj = _dot(a, win_ref[...])
    tab = sum(_dot(tab_ref[i], exp_ref[...]) for i in range(tab_ref.shape[0]))
    cos_k, sin_k = tab[:, 0:LANE], tab[:, LANE:2 * LANE]
    cos_g, sin_g = tab[:, 2 * LANE:3 * LANE], tab[:, 3 * LANE:4 * LANE]
    qs = (MLA_NOPE + MLA_ROPE) ** -0.5 * LOG2E
    lane = lax.broadcasted_iota(jnp.int32, cos_k.shape, 1)
    cos_q = jnp.where(lane < MLA_NOPE, qs, cos_k * qs)
    sin_q = sin_k * qs

    cqn = _rms(proj[:, _C_CQ:_C_CQ + MLA_Q_LORA], qn_ref[...]).astype(BF16)
    qab = _dot(cqn, wuq_ref[...])
    nq = MLA_HEADS * LANE
    for h in range(MLA_HEADS):
        qa = qab[:, h * LANE:(h + 1) * LANE]
        qb = qab[:, nq + h * LANE:nq + (h + 1) * LANE]
        qt_ref[0, h * LANE:(h + 1) * LANE, :] = (qa * cos_q + qb * sin_q).T.astype(BF16)

    ckvn = _rms(proj[:, _C_CKV:_C_CKV + MLA_KV_LORA], kvn_ref[...]).astype(BF16)
    kv = _dot(ckvn, wukv_ref[...])
    k_rope = (proj[:, _C_KRA:_C_KRA + LANE] * cos_k + proj[:, _C_KRB:_C_KRB + LANE] * sin_k)
    for h in range(MLA_HEADS):
        k_ref[:, h * LANE:(h + 1) * LANE] = (kv[:, h * LANE:(h + 1) * LANE] + k_rope).astype(BF16)
    nv = MLA_HEADS * MLA_V

    low = lane < GQA_HEAD_DIM

    def normed_rotary(xa, xb, cos, sin):
        sq = xa * xa
        ss_lo = jnp.sum(jnp.where(low, sq, 0.0), axis=-1, keepdims=True)
        ss_hi = jnp.sum(jnp.where(low, 0.0, sq), axis=-1, keepdims=True)
        r = lax.rsqrt(jnp.where(low, ss_lo, ss_hi) * (1.0 / GQA_HEAD_DIM) + EPS)
        return (xa * cos + xb * sin) * r

    gq_scale = GQA_HEAD_DIM ** -0.5 * LOG2E
    cq_g = cos_g * (gqa_ref[...] * gq_scale)
    sq_g = sin_g * (gqb_ref[...] * gq_scale)
    per_kv = GQA_HEADS // GQA_KV_HEADS
    zeros_t = jnp.zeros((LANE - GQA_HEAD_DIM, qt_ref.shape[2]), F32)
    for j in range(GQA_HEADS // 2):
        xa = proj[:, _C_GQA + j * LANE:_C_GQA + (j + 1) * LANE]
        xb = proj[:, _C_GQB + j * LANE:_C_GQB + (j + 1) * LANE]
        y_t = normed_rotary(xa, xb, cq_g, sq_g).T
        for half in range(2):
            h = 2 * j + half
            q_t = y_t[half * GQA_HEAD_DIM:(half + 1) * GQA_HEAD_DIM]
            rows = [q_t, zeros_t] if h // per_kv == 0 else [zeros_t, q_t]
            qt_ref[0, nq + h * LANE:nq + (h + 1) * LANE, :] = jnp.concatenate(rows, axis=0).astype(BF16)
    xa = proj[:, _C_GKA:_C_GKA + LANE]
    xb = proj[:, _C_GKB:_C_GKB + LANE]
    k_ref[:, nq:nq + LANE] = normed_rotary(xa, xb, cos_g * gka_ref[...], sin_g * gkb_ref[...]).astype(BF16)
    v = jnp.concatenate([kv[:, nq:nq + nv], proj[:, _C_GV:_C_GV + LANE]], axis=1)
    vt_ref[0] = v.T.astype(BF16)


def _proj_dense(h, gains, w, tab, seq):
    rows = h.shape[0]
    tm = _row_tile(seq if seq else rows, ROW_TILE)
    nblk = (seq // tm) if seq else 1
    consts = [w["w_in"], w["q_norm"], w["w_uq"], w["kv_norm"], w["w_ukv"],
              w["gq_a"], w["gq_b"], w["gk_a"], w["gk_b"], w["rope_expand"]]
    return pl.pallas_call(
        _proj_dense_kernel,
        grid=(rows // tm,),
        in_specs=[pl.BlockSpec((tm, D_MODEL), lambda i: (i, 0)), _const_spec(gains.shape)]
        + [_const_spec(c.shape) for c in consts]
        + [pl.BlockSpec((tab.shape[0], tm, LANE), lambda i: (0, i % nblk, 0))],
        out_specs=[
            pl.BlockSpec((1, HEAD_SLOTS * LANE, tm), lambda i: (i, 0, 0)),
            pl.BlockSpec((tm, K_SLOTS * LANE), lambda i: (i, 0)),
            pl.BlockSpec((1, V_ROWS, tm), lambda i: (i, 0, 0)),
        ],
        out_shape=[
            jax.ShapeDtypeStruct((rows // tm, HEAD_SLOTS * LANE, tm), BF16),
            jax.ShapeDtypeStruct((rows, K_SLOTS * LANE), BF16),
            jax.ShapeDtypeStruct((rows // tm, V_ROWS, tm), BF16),
        ],
        compiler_params=pltpu.CompilerParams(
            dimension_semantics=("arbitrary",), vmem_limit_bytes=VMEM_LIMIT),
        name="proj_dense",
    )(h, gains, *consts, tab)


def _head_slots(h):
    if h < MLA_HEADS:
        return h, h
    kvh = (h - MLA_HEADS) // (GQA_HEADS // GQA_KV_HEADS)
    return MLA_HEADS, MLA_HEADS + kvh


def _sublane_bcast_max(x):
    return jnp.broadcast_to(jnp.max(x, axis=0, keepdims=True), x.shape)


def _dense_attn_kernel(qt_ref, k_ref, vt_ref, km_ref, vmt_ref, o_ref,
                       m_ref, smax_ref, acc_ref, s_ref, sm_ref):
    kv = pl.program_id(2)
    tq = qt_ref.shape[1]
    n_sub, _, tk = vt_ref.shape
    hd = MLA_V
    acc_rows = acc_ref.shape[1]

    def with_ones(vt):
        return jnp.concatenate([vt, jnp.ones((acc_rows - hd, vt.shape[1]), BF16)], axis=0)

    @pl.when(kv == 0)
    def _():
        for h in range(HEAD_SLOTS):
            ks, _ = _head_slots(h)
            qt = qt_ref[h * LANE:(h + 1) * LANE, :]
            sm_ref[h] = _dot(km_ref[0:N_META, ks * LANE:(ks + 1) * LANE], qt)
        zeros = jnp.zeros((LANE - N_META, tq), F32)
        for h in range(HEAD_SLOTS):
            _, vh = _head_slots(h)
            s3 = sm_ref[h].reshape(N_META // SUBLANE, SUBLANE, tq)
            m = _sublane_bcast_max(jnp.max(s3, axis=0))
            p3 = jnp.exp2(s3 - m[None])
            m_ref[h] = m
            p = jnp.concatenate([p3.reshape(N_META, tq), zeros], axis=0).astype(BF16)
            acc_ref[h] = _dot(with_ones(vmt_ref[vh * hd:(vh + 1) * hd, :]), p)

    n_slots = s_ref.shape[0]

    def scores(t, h):
        ks, _ = _head_slots(h)
        k0 = pl.multiple_of(t * tk, tk)
        k = k_ref[pl.ds(k0, tk), ks * LANE:(ks + 1) * LANE]
        s = _dot(k, qt_ref[h * LANE:(h + 1) * LANE, :])
        s_ref[h % n_slots] = s
        smax_ref[h] = jnp.max(s.reshape(tk // SUBLANE, SUBLANE, tq), axis=0)

    def softmax_pv(t, h):
        _, vh = _head_slots(h)
        m_prev = m_ref[h]
        m_new = jnp.maximum(m_prev, _sublane_bcast_max(smax_ref[h]))
        alpha = jnp.exp2(m_prev - m_new)
        s3 = s_ref[h % n_slots].reshape(tk // SUBLANE, SUBLANE, tq)
        p = jnp.exp2(s3 - m_new[None]).reshape(tk, tq).astype(BF16)
        pv = _dot(with_ones(vt_ref[t, vh * hd:(vh + 1) * hd, :]), p)
        acc = acc_ref[h].reshape(acc_rows // SUBLANE, SUBLANE, tq) * alpha[None]
        acc_ref[h] = acc.reshape(acc_rows, tq) + pv
        m_ref[h] = m_new

    for h in range(DENSE_LOOKAHEAD):
        scores(0, h)

    def sub_tile(t, carry):
        t_next = jnp.minimum(t + 1, n_sub - 1)
        for h in range(HEAD_SLOTS):
            ahead = h + DENSE_LOOKAHEAD
            if ahead < HEAD_SLOTS:
                scores(t, ahead)
            else:
                scores(t_next, ahead - HEAD_SLOTS)
            softmax_pv(t, h)
        return carry

    lax.fori_loop(0, n_sub, sub_tile, 0, unroll=2)

    @pl.when(kv == pl.num_programs(2) - 1)
    def _():
        for j in range(HEAD_SLOTS // 2):
            outs = []
            for h in (2 * j, 2 * j + 1):
                outs.append(acc_ref[h, 0:hd, :] / acc_ref[h, hd:hd + 1, :])
            o_t = jnp.concatenate(outs, axis=0)
            o_ref[:, j * LANE:(j + 1) * LANE] = o_t.T.astype(BF16)


def _dense_attn(qt, k, vt, km, vmt, *, n_seq, seq, q_base, meta_base):
    tq = qt.shape[2]
    nq = (qt.shape[0] - q_base) // n_seq
    tk = vt.shape[2]
    n_sub = _row_tile(seq // tk, DENSE_SUBTILES)
    nk = seq // (tk * n_sub)
    out_rows = n_seq * nq * tq
    return pl.pallas_call(
        _dense_attn_kernel,
        grid=(n_seq, nq, nk),
        in_specs=[
            pl.BlockSpec((None, HEAD_SLOTS * LANE, tq), lambda b, i, j: (q_base + b * nq + i, 0, 0)),
            pl.BlockSpec((n_sub * tk, K_SLOTS * LANE), lambda b, i, j: (b * nk + j, 0)),
            pl.BlockSpec((n_sub, V_ROWS, tk), lambda b, i, j: (b * nk + j, 0, 0)),
            pl.BlockSpec((None, LANE, K_SLOTS * LANE), lambda b, i, j: (meta_base + b, 0, 0)),
            pl.BlockSpec((None, V_ROWS, LANE), lambda b, i, j: (meta_base + b, 0, 0)),
        ],
        out_specs=pl.BlockSpec((tq, D_MODEL), lambda b, i, j: (b * nq + i, 0)),
        out_shape=jax.ShapeDtypeStruct((out_rows, D_MODEL), BF16),
        scratch_shapes=[
            pltpu.VMEM((HEAD_SLOTS, SUBLANE, tq), F32),
            pltpu.VMEM((HEAD_SLOTS, SUBLANE, tq), F32),
            pltpu.VMEM((HEAD_SLOTS, MLA_V + 2 * SUBLANE, tq), F32),
            pltpu.VMEM((DENSE_SLOTS, tk, tq), F32),
            pltpu.VMEM((HEAD_SLOTS, N_META, tq), F32),
        ],
        compiler_params=pltpu.CompilerParams(
            dimension_semantics=("arbitrary", "arbitrary", "arbitrary"),
            vmem_limit_bytes=VMEM_LIMIT),
        name="dense_attn",
    )(qt, k, vt, km, vmt)


def _proj_na_kernel(h_ref, g_ref, w_ref, q_ref, k_ref, v_ref, *, transpose_v):
    a = _rms(h_ref[...], g_ref[2:3, :]).astype(BF16)
    qkv = _dot(a, w_ref[...])
    n = NA_HEADS * NA_HEAD_DIM
    q_ref[...] = (qkv[:, 0:n] * (NA_HEAD_DIM ** -0.5 * LOG2E)).astype(BF16)
    k_ref[...] = qkv[:, n:2 * n].astype(BF16)
    v = qkv[:, 2 * n:3 * n]
    if transpose_v:
        vt = v.T.astype(BF16)
        for t in range(v_ref.shape[0]):
            v_ref[t] = vt[:, t * LANE:(t + 1) * LANE]
    else:
        v_ref[...] = v.astype(BF16)


def _proj_na(h, gains, w, transpose_v):
    rows = h.shape[0]
    tm = _row_tile(rows, ROW_TILE)
    n = NA_HEADS * NA_HEAD_DIM
    if transpose_v:
        v_spec = pl.BlockSpec((tm // LANE, n, LANE), lambda i: (i, 0, 0))
        v_shape = jax.ShapeDtypeStruct((rows // LANE, n, LANE), BF16)
    else:
        v_spec = pl.BlockSpec((tm, n), lambda i: (i, 0))
        v_shape = jax.ShapeDtypeStruct((rows, n), BF16)
    return pl.pallas_call(
        functools.partial(_proj_na_kernel, transpose_v=transpose_v),
        grid=(rows // tm,),
        in_specs=[pl.BlockSpec((tm, D_MODEL), lambda i: (i, 0)), _const_spec(gains.shape),
                  _const_spec(w.shape)],
        out_specs=[pl.BlockSpec((tm, n), lambda i: (i, 0))] * 2 + [v_spec],
        out_shape=[jax.ShapeDtypeStruct((rows, n), BF16)] * 2 + [v_shape],
        compiler_params=pltpu.CompilerParams(
            dimension_semantics=("arbitrary",), vmem_limit_bytes=VMEM_LIMIT),
        name="proj_na",
    )(h, gains, w)


NA_SPAN_R = NA_WIN_R + 2
NA_MASKED = 2 * NA_WIN_R - 1


def _na_kernel(q_ref, k_ref, vt_ref, km_ref, vmt_ref, bias_ref, mb_ref, o_ref, s_ref,
               *, rows, rows_per_step):
    step = pl.program_id(1)
    n_pairs = rows_per_step // 2
    n_hp = NA_HEADS // 2
    span = NA_SPAN_R * GRID_W
    lane = lax.broadcasted_iota(jnp.int32, (GRID_W, LANE), 1)
    first = lane < (LANE // 2)
    zeros_m = jnp.zeros((LANE - N_META, 2 * LANE), F32)
    ones_v = jnp.ones((2 * SUBLANE, span + LANE), BF16)

    def geometry(rp):
        ra = step * rows_per_step + 2 * rp
        rs = [jnp.clip(ra + x - NA_WIN_R // 2, 0, rows - NA_WIN_R) for x in range(2)]
        ws = jnp.minimum((rs[0] // 2) * 2, rows - NA_SPAN_R)
        return ra, rs, ws

    n_slots = s_ref.shape[0]

    def scores(rp, hp):
        slot = hp % n_slots
        _, _, ws = geometry(rp)
        cols = slice(hp * LANE, (hp + 1) * LANE)
        parts = []
        for x in range(2):
            q0 = pl.multiple_of((2 * rp + x) * GRID_W, GRID_W)
            qx = q_ref[pl.ds(q0, GRID_W), cols]
            parts += [jnp.where(first, qx, jnp.zeros_like(qx)), jnp.where(first, jnp.zeros_like(qx), qx)]
        qblk = jnp.concatenate(parts, axis=0)
        k0 = pl.multiple_of(ws * GRID_W, 2 * GRID_W)
        s_ref[slot, 0:span, :] = _dot_nt(k_ref[pl.ds(k0, span), cols], qblk)
        s_ref[slot, span:span + N_META, :] = _dot_nt(km_ref[:, cols], qblk)

    for hp in range(NA_LOOKAHEAD):
        scores(0, hp)

    def row_pair(rp, carry):
        ra, rs, ws = geometry(rp)
        idx = []
        for jj in range(NA_SPAN_R):
            kr = ws + jj
            idx.append([jnp.where((kr >= rs[x]) & (kr < rs[x] + NA_WIN_R),
                                  kr - (ra + x) + NA_WIN_R - 1, NA_MASKED) for x in range(2)])
        t0 = ws // 2
        for hp in range(n_hp):
            slot = hp % n_slots
            ahead = hp + NA_LOOKAHEAD
            if ahead < n_hp:
                scores(rp, ahead)
            else:
                scores(jnp.minimum(rp + 1, n_pairs - 1), ahead - n_hp)
            cols = slice(hp * LANE, (hp + 1) * LANE)
            b = jnp.concatenate(
                [jnp.concatenate([bias_ref[hp, idx[jj][0]], bias_ref[hp, idx[jj][1]]], axis=1)
                 for jj in range(NA_SPAN_R)], axis=0)
            s = s_ref[slot, 0:span, :] + b
            mb = mb_ref[hp]
            sm = s_ref[slot, span:span + N_META, :] + jnp.concatenate([mb, mb], axis=1)
            s3 = s.reshape(span // SUBLANE, SUBLANE, 2 * LANE)
            sm3 = sm.reshape(N_META // SUBLANE, SUBLANE, 2 * LANE)
            m = _sublane_bcast_max(jnp.maximum(jnp.max(s3, axis=0), jnp.max(sm3, axis=0)))
            p = jnp.exp2(s3 - m[None]).reshape(span, 2 * LANE).astype(BF16)
            pm3 = jnp.exp2(sm3 - m[None])
            pm = jnp.concatenate([pm3.reshape(N_META, 2 * LANE), zeros_m], axis=0).astype(BF16)
            v_all = jnp.concatenate([vt_ref[t0 + t, cols, :] for t in range(span // LANE)]
                                    + [vmt_ref[cols, :]], axis=1)
            o_t = _dot(jnp.concatenate([v_all, ones_v], axis=0),
                       jnp.concatenate([p, pm], axis=0))
            o_t = o_t[0:LANE] / o_t[LANE:LANE + 1]
            for x in range(2):
                blk = o_t[:, x * LANE:(x + 1) * LANE].T
                q0 = pl.multiple_of((2 * rp + x) * GRID_W, GRID_W)
                o_ref[pl.ds(q0, GRID_W), cols] = jnp.where(
                    first, blk[0:GRID_W], blk[GRID_W:2 * GRID_W]).astype(BF16)
        return carry

    lax.fori_loop(0, n_pairs, row_pair, 0, unroll=2)


def _na_attn(q, k, vt, km, vmt, bias, mb, *, n_seq, seq, meta_base):
    rows = seq // GRID_W
    assert rows >= NA_SPAN_R and rows % 2 == 0
    rps = _row_tile(rows, NA_ROWS_PER_STEP)
    nsteps = rows // rps
    n = NA_HEADS * NA_HEAD_DIM
    span = NA_SPAN_R * GRID_W
    return pl.pallas_call(
        functools.partial(_na_kernel, rows=rows, rows_per_step=rps),
        grid=(n_seq, nsteps),
        in_specs=[
            pl.BlockSpec((rps * GRID_W, n), lambda b, i: (b * nsteps + i, 0)),
            pl.BlockSpec((seq, n), lambda b, i: (b, 0), pipeline_mode=pl.Buffered(1)),
            pl.BlockSpec((seq // LANE, n, LANE), lambda b, i: (b, 0, 0), pipeline_mode=pl.Buffered(1)),
            pl.BlockSpec((None, N_META, n), lambda b, i: (meta_base + b, 0, 0)),
            pl.BlockSpec((None, n, LANE), lambda b, i: (meta_base + b, 0, 0)),
            _const_spec(bias.shape),
            _const_spec(mb.shape),
        ],
        out_specs=pl.BlockSpec((rps * GRID_W, n), lambda b, i: (b * nsteps + i, 0)),
        out_shape=jax.ShapeDtypeStruct((n_seq * seq, n), BF16),
        scratch_shapes=[pltpu.VMEM((NA_SLOTS, span + N_META, 2 * LANE), F32)],
        compiler_params=pltpu.CompilerParams(
            dimension_semantics=("arbitrary", "arbitrary"), vmem_limit_bytes=VMEM_LIMIT),
        name="na_attn",
    )(q, k, vt, km, vmt, bias, mb)


def _na_meta_kernel(q_ref, km_ref, vm_ref, mb_ref, o_ref):
    lane = lax.broadcasted_iota(jnp.int32, (N_META, LANE), 1)
    first = lane < (LANE // 2)
    for j in range(NA_HEADS // 2):
        cols = slice(j * LANE, (j + 1) * LANE)
        qp = q_ref[:, cols]
        km = km_ref[:, cols]
        vm = vm_ref[:, cols]
        outs = []
        for half in range(2):
            h = 2 * j + half
            qh = jnp.where(first if half == 0 else jnp.logical_not(first), qp, jnp.zeros_like(qp))
            sm = _dot_nt(qh, km)
            sm = jnp.where(lane < N_META, sm + mb_ref[h:h + 1, :], NEG_INF)
            m = jnp.max(sm, axis=-1, keepdims=True)
            pm = jnp.exp2(sm - m)
            l = jnp.sum(pm, axis=-1, keepdims=True)
            outs.append(_dot(pm.astype(BF16), vm) / l)
        o_ref[:, cols] = jnp.where(first, outs[0], outs[1]).astype(BF16)


def _na_meta(qm, km, vm, mb):
    n_seq = km.shape[0]
    n = NA_HEADS * NA_HEAD_DIM
    return pl.pallas_call(
        _na_meta_kernel,
        grid=(n_seq,),
        in_specs=[
            pl.BlockSpec((N_META, n), lambda b: (b, 0)),
            pl.BlockSpec((None, LANE, n), lambda b: (b, 0, 0)),
            pl.BlockSpec((None, LANE, n), lambda b: (b, 0, 0)),
            _const_spec(mb.shape),
        ],
        out_specs=pl.BlockSpec((N_META, n), lambda b: (b, 0)),
        out_shape=jax.ShapeDtypeStruct((n_seq * N_META, n), BF16),
        compiler_params=pltpu.CompilerParams(dimension_semantics=("arbitrary",)),
        name="na_meta",
    )(qm, km, vm, mb)


def _take_cols(w, idx):
    idx = np.asarray(idx)
    neg = idx < 0
    same_run = np.where(neg[1:] | neg[:-1], neg[1:] & neg[:-1], np.diff(idx) == 1)
    breaks = np.flatnonzero(~same_run) + 1
    parts = []
    for run in np.split(idx, breaks):
        if run[0] < 0:
            parts.append(jnp.zeros((w.shape[0], len(run)), w.dtype))
        else:
            parts.append(w[:, int(run[0]):int(run[-1]) + 1])
    return jnp.concatenate(parts, axis=1)


def _swap_halves(n):
    half = n // 2
    return np.concatenate([np.arange(half, n), np.arange(0, half)])


def _dense_weights(w_in, q_norm, w_uq, kv_norm, w_ukv, gq_norm, gk_norm, w_out):
    pad = lambda k: -np.ones(k, np.int64)
    o_kr = MLA_Q_LORA + MLA_KV_LORA
    o_gq = o_kr + MLA_ROPE
    o_gk = o_gq + GQA_HEADS * GQA_HEAD_DIM
    o_gv = o_gk + GQA_KV_HEADS * GQA_HEAD_DIM
    axial = np.concatenate([_swap_halves(GQA_HEAD_DIM // 2),
                            GQA_HEAD_DIM // 2 + _swap_halves(GQA_HEAD_DIM // 2)])
    idx = [np.arange(0, o_kr)]
    idx += [pad(MLA_NOPE), o_kr + np.arange(MLA_ROPE), pad(LANE - MLA_NOPE - MLA_ROPE)]
    idx += [pad(MLA_NOPE), o_kr + _swap_halves(MLA_ROPE), pad(LANE - MLA_NOPE - MLA_ROPE)]
    assert 2 * GQA_HEAD_DIM == LANE and GQA_KV_HEADS == 2
    for h in range(GQA_HEADS):
        idx += [o_gq + h * GQA_HEAD_DIM + np.arange(GQA_HEAD_DIM)]
    for h in range(GQA_HEADS):
        idx += [o_gq + h * GQA_HEAD_DIM + axial]
    for h in range(GQA_KV_HEADS):
        idx += [o_gk + h * GQA_HEAD_DIM + np.arange(GQA_HEAD_DIM)]
    for h in range(GQA_KV_HEADS):
        idx += [o_gk + h * GQA_HEAD_DIM + axial]
    idx += [o_gv + np.arange(GQA_KV_HEADS * GQA_HEAD_DIM)]
    idx = np.concatenate(idx)
    assert idx.shape[0] == _C_END
    w_in2 = _take_cols(w_in.astype(BF16), idx)

    hd = MLA_NOPE + MLA_ROPE
    ia, ib = [], []
    for h in range(MLA_HEADS):
        ia += [h * hd + np.arange(hd), pad(LANE - hd)]
        ib += [pad(MLA_NOPE), h * hd + MLA_NOPE + _swap_halves(MLA_ROPE), pad(LANE - hd)]
    w_uq2 = _take_cols(w_uq.astype(BF16), np.concatenate(ia + ib))

    kvd = MLA_NOPE + MLA_V
    ik, iv = [], []
    for h in range(MLA_HEADS):
        ik += [h * kvd + np.arange(MLA_NOPE), pad(LANE - MLA_NOPE)]
        iv += [h * kvd + MLA_NOPE + np.arange(MLA_V)]
    w_ukv2 = _take_cols(w_ukv.astype(BF16), np.concatenate(ik + iv))

    def gain_pair(g):
        ga = jnp.concatenate([g, g])[None, :]
        gb = jnp.tile(g[jnp.asarray(axial)], 2)[None, :]
        return ga, gb

    gq_a, gq_b = gain_pair(gq_norm)
    gk_a, gk_b = gain_pair(gk_norm)

    w_out2 = w_out.astype(BF16)

    return dict(w_in=w_in2, q_norm=q_norm[None, :], w_uq=w_uq2, kv_norm=kv_norm[None, :],
                w_ukv=w_ukv2, gq_a=gq_a, gq_b=gq_b, gk_a=gk_a, gk_b=gk_b,
                rope_expand=_rope_expand_matrix()), w_out2


def _rope_tables(pos, row, col):
    half = MLA_ROPE // 2
    inv = 1.0 / (ROPE_THETA ** (jnp.arange(half, dtype=F32) / half))
    n = pos.shape[0]

    def cs(p):
        ang = p.astype(F32)[None, :] * inv[:, None]
        return jnp.cos(ang), jnp.sin(ang)

    vals = jnp.concatenate(cs(pos) + cs(row) + cs(col) + (jnp.zeros((LANE - 6 * half, n), F32),), axis=0).T
    hi = vals.astype(BF16)
    rest = vals - hi.astype(F32)
    mid = rest.astype(BF16)
    lo = (rest - mid.astype(F32)).astype(BF16)
    return jnp.stack([hi, mid, lo])


def _rope_expand_matrix():
    half = MLA_ROPE // 2
    c1, s1, cr, sr, cc, sc = range(6)
    tail = LANE - MLA_NOPE - MLA_ROPE
    blank = [(None, 0)]
    layout = (blank * (MLA_NOPE // half) + [(c1, 1), (c1, 1)] + blank * (tail // half)
              + blank * (MLA_NOPE // half) + [(s1, -1), (s1, 1)] + blank * (tail // half)
              + [(cr, 1), (cr, 1), (cc, 1), (cc, 1)] * (LANE // GQA_HEAD_DIM)
              + [(sr, -1), (sr, 1), (sc, -1), (sc, 1)] * (LANE // GQA_HEAD_DIM))
    expand = np.zeros((LANE, len(layout) * half), np.float32)
    for blk, (src, sign) in enumerate(layout):
        if src is not None:
            expand[src * half + np.arange(half), blk * half + np.arange(half)] = sign
    return jnp.asarray(expand, BF16)


def _na_bias_tables(rpb, meta_bias):
    c_idx = np.arange(GRID_W)
    c_start = np.clip(c_idx - NA_WIN_C // 2, 0, GRID_W - NA_WIN_C)
    col_mask = (c_idx[None, :] >= c_start[:, None]) & (c_idx[None, :] < c_start[:, None] + NA_WIN_C)
    col_off = np.clip(c_idx[None, :] - c_idx[:, None] + NA_WIN_C - 1, 0, 2 * NA_WIN_C - 2)
    hp = NA_HEADS // 2
    n_off = 2 * NA_WIN_C - 1
    select = np.zeros((2, n_off, GRID_W, 2, GRID_W), np.float32)
    kc_g, c_g = np.meshgrid(c_idx, c_idx, indexing="ij")
    for half in range(2):
        select[half, col_off[c_g, kc_g], kc_g, half, c_g] = 1.0
    rows = rpb.reshape(hp, 2, NA_MASKED, n_off).transpose(0, 2, 1, 3).reshape(hp * NA_MASKED, 2 * n_off)
    t = jnp.dot(rows, jnp.asarray(select.reshape(2 * n_off, GRID_W * LANE)),
                precision=lax.Precision.HIGHEST) * LOG2E
    t = t.reshape(hp, NA_MASKED, GRID_W, LANE)
    keep = np.tile(col_mask.T, (1, 2))
    t = jnp.where(jnp.asarray(keep)[None, None], t, NEG_INF)
    bias = jnp.concatenate([t, jnp.full_like(t[:, :1], NEG_INF)], axis=1)
    mbl = meta_bias * LOG2E
    mb_t = jnp.repeat(mbl.reshape(hp, 2, N_META).transpose(0, 2, 1), GRID_W, axis=2)
    mb = jnp.pad(mbl, ((0, 0), (0, LANE - N_META)))
    return bias, mb_t, mb


def _pad_meta(x, n_seq):
    c = x.shape[1]
    return jnp.pad(x.reshape(n_seq, N_META, c), ((0, 0), (0, LANE - N_META), (0, 0)))


def kernel(x_prompt, x_sample, meta, norm_gains, ffn1_w_gate, ffn1_w_up, ffn1_w_down, ffn2_w_gate, ffn2_w_up, ffn2_w_down, attn_w_in, mla_q_norm, mla_w_uq, mla_kv_norm, mla_w_ukv, gqa_q_norm, gqa_k_norm, attn_w_out, na_w_qkv, na_rpb, na_meta_bias, na_w_out):
    bp, sp, _ = x_prompt.shape
    bs, ss, _ = x_sample.shape
    n_seq = bp + bs
    depth = norm_gains.shape[0]
    groups = [(bp, sp, 0), (bs, ss, bp)]

    n_meta = n_seq * N_META
    meta_rows = -(-n_meta // LANE) * LANE
    pad_rows = lambda x: jnp.pad(x, ((0, meta_rows - x.shape[0]), (0, 0)))
    h_tok = [x_prompt.reshape(bp * sp, D_MODEL), x_sample.reshape(bs * ss, D_MODEL)]
    h_meta = pad_rows(jnp.tile(meta.astype(F32), (n_seq, 1)))

    smax = max(sp, ss)
    t = jnp.arange(smax)
    tab_tok = _rope_tables(t + N_META, t // GRID_W, t % GRID_W)
    mi = jnp.arange(meta_rows) % N_META
    tab_meta = _rope_tables(mi, jnp.full_like(mi, -1), mi)

    w1 = (ffn1_w_gate.astype(BF16), ffn1_w_up.astype(BF16), ffn1_w_down.astype(BF16))
    w2 = (ffn2_w_gate.astype(BF16), ffn2_w_up.astype(BF16), ffn2_w_down.astype(BF16))

    for i in range(depth):
        gains = jnp.pad(norm_gains[i], ((0, 2), (0, 0)))
        j = i // 2
        h_tok = [_ffn1(h, gains, *w1, i) for h in h_tok]
        h_meta = _ffn1(h_meta, gains, *w1, i)
        if i % 2 == 0:
            w, w_out = _dense_weights(attn_w_in[j], mla_q_norm[j], mla_w_uq[j], mla_kv_norm[j],
                                      mla_w_ukv[j], gqa_q_norm[j], gqa_k_norm[j], attn_w_out[j])
            qkv_tok = [_proj_dense(h, gains, w, tab_tok, s) for h, (_, s, _) in zip(h_tok, groups)]
            qmt, km, vmt = _proj_dense(h_meta, gains, w, tab_meta, 0)
            kmp = _pad_meta(km[:n_meta], n_seq)
            vmt = vmt.transpose(1, 0, 2).reshape(V_ROWS, meta_rows)
            vmtp = vmt[:, :n_meta].reshape(V_ROWS, n_seq, N_META).transpose(1, 0, 2)
            vmtp = jnp.pad(vmtp, ((0, 0), (0, 0), (0, LANE - N_META)))
            qmt = qmt.transpose(1, 0, 2).reshape(HEAD_SLOTS * LANE, meta_rows)
            qmtp = qmt[:, :n_meta].reshape(HEAD_SLOTS * LANE, n_seq, N_META).transpose(1, 0, 2)
            qmtp = jnp.pad(qmtp, ((0, 0), (0, 0), (0, LANE - N_META)))
            o_tok, o_meta = [], []
            for (qt, k, vt), (nb, s, b0) in zip(qkv_tok, groups):
                o_tok.append(_dense_attn(qt, k, vt, kmp, vmtp, n_seq=nb, seq=s, q_base=0, meta_base=b0))
                om = _dense_attn(qmtp, k, vt, kmp, vmtp, n_seq=nb, seq=s, q_base=b0, meta_base=b0)
                o_meta.append(om.reshape(nb, LANE, D_MODEL)[:, :N_META].reshape(nb * N_META, D_MODEL))
            o_meta = pad_rows(jnp.concatenate(o_meta, axis=0))
        else:
            w_qkv = na_w_qkv[j].astype(BF16)
            w_out = na_w_out[j].astype(BF16)
            bias, mb_t, mb = _na_bias_tables(na_rpb[j], na_meta_bias[j])
            qkv_tok = [_proj_na(h, gains, w_qkv, True) for h in h_tok]
            qm, km, vm = _proj_na(h_meta, gains, w_qkv, False)
            kmp, vmp = _pad_meta(km[:n_meta], n_seq), _pad_meta(vm[:n_meta], n_seq)
            km16 = km[:n_meta].reshape(n_seq, N_META, NA_HEADS * NA_HEAD_DIM)
            vmtp = vmp.transpose(0, 2, 1)
            o_tok = [_na_attn(q, k, vt, km16, vmtp, bias, mb_t, n_seq=nb, seq=s, meta_base=b0)
                     for (q, k, vt), (nb, s, b0) in zip(qkv_tok, groups)]
            o_meta = pad_rows(_na_meta(qm[:n_meta], kmp, vmp, mb))
        h_tok = [_mix_ffn2(h, o, w_out, gains, *w2, i) for h, o in zip(h_tok, o_tok)]
        h_meta = _mix_ffn2(h_meta, o_meta, w_out, gains, *w2, i)

    return (h_tok[0].reshape(bp, sp, D_MODEL), h_tok[1].reshape(bs, ss, D_MODEL))
```

```python
import functools
import math

import jax
import jax.numpy as jnp
import numpy as np
from jax import lax
from jax.experimental import pallas as pl
from jax.experimental.pallas import tpu as pltpu

F32 = jnp.float32
BF16 = jnp.bfloat16

D_MODEL = 1024
N_META = 16
GRID_W = 64
D_FF = 2816
EPS = 1e-6
NEG_INF = -1e30
LOG2E = math.log2(math.e)

MLA_HEADS = 8
MLA_Q_LORA = 256
MLA_KV_LORA = 128
MLA_NOPE = 64
MLA_ROPE = 32
MLA_V = 64
GQA_HEADS = 8
GQA_KV_HEADS = 2
GQA_HEAD_DIM = 64
ROPE_THETA = 10000.0
NA_HEADS = 16
NA_HEAD_DIM = 64
NA_WIN_R = 8
NA_WIN_C = 16

LANE = 128
HEAD_SLOTS = MLA_HEADS + GQA_HEADS
K_SLOTS = MLA_HEADS + 1
V_ROWS = (MLA_HEADS + GQA_KV_HEADS) * MLA_V
SUBLANE = 8
DENSE_LOOKAHEAD, DENSE_SLOTS = 2, 4
NA_LOOKAHEAD, NA_SLOTS = 2, 4
VMEM_LIMIT = 56 * 1024 * 1024
ROW_TILE = 512
DENSE_SUBTILES = 8
NA_ROWS_PER_STEP = 16

_C_CQ = 0
_C_CKV = _C_CQ + MLA_Q_LORA
_C_KRA = _C_CKV + MLA_KV_LORA
_C_KRB = _C_KRA + LANE
_C_GQA = _C_KRB + LANE
_C_GQB = _C_GQA + GQA_HEADS // 2 * LANE
_C_GKA = _C_GQB + GQA_HEADS // 2 * LANE
_C_GKB = _C_GKA + LANE
_C_GV = _C_GKB + LANE
_C_END = _C_GV + LANE


def _const_spec(shape):
    nd = len(shape)
    return pl.BlockSpec(shape, lambda *_: (0,) * nd, pipeline_mode=pl.Buffered(1))


def _rms(x, g):
    ms = jnp.mean(x * x, axis=-1, keepdims=True)
    return x * lax.rsqrt(ms + EPS) * g


def _dot(a, b):
    return jnp.dot(a, b, preferred_element_type=F32)


def _dot_nt(a, b):
    return lax.dot_general(a, b, (((1,), (1,)), ((), ())), preferred_element_type=F32)


def _row_tile(rows, want):
    t = min(rows, want)
    while rows % t:
        t //= 2
    return t


def _ffn_body(h, g_ref, pre, post, wg_ref, wu_ref, wd_ref):
    xn = _rms(h, g_ref[pre:pre + 1, :]).astype(BF16)
    gate = _dot(xn, wg_ref[...])
    up = _dot(xn, wu_ref[...])
    act = (gate * jax.nn.sigmoid(gate) * up).astype(BF16)
    y = _dot(act, wd_ref[...])
    return h + 0.5 * _rms(y, g_ref[post:post + 1, :])


def _ffn1_kernel(h_ref, g_ref, wg_ref, wu_ref, wd_ref, out_ref):
    out_ref[...] = _ffn_body(h_ref[...], g_ref, 0, 1, wg_ref, wu_ref, wd_ref)


def _mix_ffn2_kernel(h_ref, o_ref, wo_ref, g_ref, wg_ref, wu_ref, wd_ref, out_ref):
    mixed = _dot(o_ref[...], wo_ref[...])
    h = h_ref[...] + _rms(mixed, g_ref[3:4, :])
    out_ref[...] = _ffn_body(h, g_ref, 4, 5, wg_ref, wu_ref, wd_ref)


def _layer_spec(w, layer):
    return pl.BlockSpec((None,) + w.shape[1:], lambda *_: (layer, 0, 0), pipeline_mode=pl.Buffered(1))


def _ffn1(h, gains, wg, wu, wd, layer):
    rows = h.shape[0]
    tm = _row_tile(rows, ROW_TILE)
    return pl.pallas_call(
        _ffn1_kernel,
        grid=(rows // tm,),
        in_specs=[
            pl.BlockSpec((tm, D_MODEL), lambda i: (i, 0)),
            _const_spec(gains.shape),
            _layer_spec(wg, layer), _layer_spec(wu, layer), _layer_spec(wd, layer),
        ],
        out_specs=pl.BlockSpec((tm, D_MODEL), lambda i: (i, 0)),
        out_shape=jax.ShapeDtypeStruct((rows, D_MODEL), F32),
        compiler_params=pltpu.CompilerParams(
            dimension_semantics=("arbitrary",), vmem_limit_bytes=VMEM_LIMIT),
        name="ffn1",
    )(h, gains, wg, wu, wd)


def _mix_ffn2(h, o, wo, gains, wg, wu, wd, layer):
    rows = h.shape[0]
    tm = _row_tile(rows, ROW_TILE)
    return pl.pallas_call(
        _mix_ffn2_kernel,
        grid=(rows // tm,),
        in_specs=[
            pl.BlockSpec((tm, D_MODEL), lambda i: (i, 0)),
            pl.BlockSpec((tm, o.shape[1]), lambda i: (i, 0)),
            _const_spec(wo.shape),
            _const_spec(gains.shape),
            _layer_spec(wg, layer), _layer_spec(wu, layer), _layer_spec(wd, layer),
        ],
        out_specs=pl.BlockSpec((tm, D_MODEL), lambda i: (i, 0)),
        out_shape=jax.ShapeDtypeStruct((rows, D_MODEL), F32),
        compiler_params=pltpu.CompilerParams(
            dimension_semantics=("arbitrary",), vmem_limit_bytes=VMEM_LIMIT),
        name="mix_ffn2",
    )(h, o, wo, gains, wg, wu, wd)


def _proj_dense_kernel(h_ref, g_ref, win_ref, qn_ref, wuq_ref, kvn_ref, wukv_ref,
                       gqa_ref, gqb_ref, gka_ref, gkb_ref, exp_ref, tab_ref, qt_ref, k_ref, vt_ref):
    a = _rms(h_ref[...], g_ref[2:3, :]).astype(BF16)
    proj = _dot(a, win_ref[...])
    tab = sum(_dot(tab_ref[i], exp_ref[...]) for i in range(tab_ref.shape[0]))
    cos_k, sin_k = tab[:, 0:LANE], tab[:, LANE:2 * LANE]
    cos_g, sin_g = tab[:, 2 * LANE:3 * LANE], tab[:, 3 * LANE:4 * LANE]
    qs = (MLA_NOPE + MLA_ROPE) ** -0.5 * LOG2E
    lane = lax.broadcasted_iota(jnp.int32, cos_k.shape, 1)
    cos_q = jnp.where(lane < MLA_NOPE, qs, cos_k * qs)
    sin_q = sin_k * qs

    cqn = _rms(proj[:, _C_CQ:_C_CQ + MLA_Q_LORA], qn_ref[...]).astype(BF16)
    qab = _dot(cqn, wuq_ref[...])
    nq = MLA_HEADS * LANE
    for h in range(MLA_HEADS):
        qa = qab[:, h * LANE:(h + 1) * LANE]
        qb = qab[:, nq + h * LANE:nq + (h + 1) * LANE]
        qt_ref[0, h * LANE:(h + 1) * LANE, :] = (qa * cos_q + qb * sin_q).T.astype(BF16)

    ckvn = _rms(proj[:, _C_CKV:_C_CKV + MLA_KV_LORA], kvn_ref[...]).astype(BF16)
    kv = _dot(ckvn, wukv_ref[...])
    k_rope = (proj[:, _C_KRA:_C_KRA + LANE] * cos_k + proj[:, _C_KRB:_C_KRB + LANE] * sin_k)
    for h in range(MLA_HEADS):
        k_ref[:, h * LANE:(h + 1) * LANE] = (kv[:, h * LANE:(h + 1) * LANE] + k_rope).astype(BF16)
    nv = MLA_HEADS * MLA_V

    low = lane < GQA_HEAD_DIM

    def normed_rotary(xa, xb, cos, sin):
        sq = xa * xa
        ss_lo = jnp.sum(jnp.where(low, sq, 0.0), axis=-1, keepdims=True)
        ss_hi = jnp.sum(jnp.where(low, 0.0, sq), axis=-1, keepdims=True)
        r = lax.rsqrt(jnp.where(low, ss_lo, ss_hi) * (1.0 / GQA_HEAD_DIM) + EPS)
        return (xa * cos + xb * sin) * r

    gq_scale = GQA_HEAD_DIM ** -0.5 * LOG2E
    cq_g = cos_g * (gqa_ref[...] * gq_scale)
    sq_g = sin_g * (gqb_ref[...] * gq_scale)
    per_kv = GQA_HEADS // GQA_KV_HEADS
    zeros_t = jnp.zeros((LANE - GQA_HEAD_DIM, qt_ref.shape[2]), F32)
    for j in range(GQA_HEADS // 2):
        xa = proj[:, _C_GQA + j * LANE:_C_GQA + (j + 1) * LANE]
        xb = proj[:, _C_GQB + j * LANE:_C_GQB + (j + 1) * LANE]
        y_t = normed_rotary(xa, xb, cq_g, sq_g).T
        for half in range(2):
            h = 2 * j + half
            q_t = y_t[half * GQA_HEAD_DIM:(half + 1) * GQA_HEAD_DIM]
            rows = [q_t, zeros_t] if h // per_kv == 0 else [zeros_t, q_t]
            qt_ref[0, nq + h * LANE:nq + (h + 1) * LANE, :] = jnp.concatenate(rows, axis=0).astype(BF16)
    xa = proj[:, _C_GKA:_C_GKA + LANE]
    xb = proj[:, _C_GKB:_C_GKB + LANE]
    k_ref[:, nq:nq + LANE] = normed_rotary(xa, xb, cos_g * gka_ref[...], sin_g * gkb_ref[...]).astype(BF16)
    v = jnp.concatenate([kv[:, nq:nq + nv], proj[:, _C_GV:_C_GV + LANE]], axis=1)
    vt_ref[0] = v.T.astype(BF16)


def _proj_dense(h, gains, w, tab, seq):
    rows = h.shape[0]
    tm = _row_tile(seq if seq else rows, ROW_TILE)
    nblk = (seq // tm) if seq else 1
    consts = [w["w_in"], w["q_norm"], w["w_uq"], w["kv_norm"], w["w_ukv"],
              w["gq_a"], w["gq_b"], w["gk_a"], w["gk_b"], w["rope_expand"]]
    return pl.pallas_call(
        _proj_dense_kernel,
        grid=(rows // tm,),
        in_specs=[pl.BlockSpec((tm, D_MODEL), lambda i: (i, 0)), _const_spec(gains.shape)]
        + [_const_spec(c.shape) for c in consts]
        + [pl.BlockSpec((tab.shape[0], tm, LANE), lambda i: (0, i % nblk, 0))],
        out_specs=[
            pl.BlockSpec((1, HEAD_SLOTS * LANE, tm), lambda i: (i, 0, 0)),
            pl.BlockSpec((tm, K_SLOTS * LANE), lambda i: (i, 0)),
            pl.BlockSpec((1, V_ROWS, tm), lambda i: (i, 0, 0)),
        ],
        out_shape=[
            jax.ShapeDtypeStruct((rows // tm, HEAD_SLOTS * LANE, tm), BF16),
            jax.ShapeDtypeStruct((rows, K_SLOTS * LANE), BF16),
            jax.ShapeDtypeStruct((rows // tm, V_ROWS, tm), BF16),
        ],
        compiler_params=pltpu.CompilerParams(
            dimension_semantics=("arbitrary",), vmem_limit_bytes=VMEM_LIMIT),
        name="proj_dense",
    )(h, gains, *consts, tab)


def _head_slots(h):
    if h < MLA_HEADS:
        return h, h
    kvh = (h - MLA_HEADS) // (GQA_HEADS // GQA_KV_HEADS)
    return MLA_HEADS, MLA_HEADS + kvh


def _sublane_bcast_max(x):
    return jnp.broadcast_to(jnp.max(x, axis=0, keepdims=True), x.shape)


def _dense_attn_kernel(qt_ref, k_ref, vt_ref, km_ref, vmt_ref, o_ref,
                       m_ref, smax_ref, acc_ref, s_ref, sm_ref):
    kv = pl.program_id(2)
    tq = qt_ref.shape[1]
    n_sub, _, tk = vt_ref.shape
    hd = MLA_V
    acc_rows = acc_ref.shape[1]

    def with_ones(vt):
        return jnp.concatenate([vt, jnp.ones((acc_rows - hd, vt.shape[1]), BF16)], axis=0)

    @pl.when(kv == 0)
    def _():
        for h in range(HEAD_SLOTS):
            ks, _ = _head_slots(h)
            qt = qt_ref[h * LANE:(h + 1) * LANE, :]
            sm_ref[h] = _dot(km_ref[0:N_META, ks * LANE:(ks + 1) * LANE], qt)
        zeros = jnp.zeros((LANE - N_META, tq), F32)
        for h in range(HEAD_SLOTS):
            _, vh = _head_slots(h)
            s3 = sm_ref[h].reshape(N_META // SUBLANE, SUBLANE, tq)
            m = _sublane_bcast_max(jnp.max(s3, axis=0))
            p3 = jnp.exp2(s3 - m[None])
            m_ref[h] = m
            p = jnp.concatenate([p3.reshape(N_META, tq), zeros], axis=0).astype(BF16)
            acc_ref[h] = _dot(with_ones(vmt_ref[vh * hd:(vh + 1) * hd, :]), p)

    n_slots = s_ref.shape[0]

    def scores(t, h):
        ks, _ = _head_slots(h)
        k0 = pl.multiple_of(t * tk, tk)
        k = k_ref[pl.ds(k0, tk), ks * LANE:(ks + 1) * LANE]
        s = _dot(k, qt_ref[h * LANE:(h + 1) * LANE, :])
        s_ref[h % n_slots, :, 0:tq] = s
        smax_ref[h] = jnp.max(s.reshape(tk // SUBLANE, SUBLANE, tq), axis=0)

    def softmax_pv(t, h):
        _, vh = _head_slots(h)
        m_prev = m_ref[h]
        m_new = jnp.maximum(m_prev, _sublane_bcast_max(smax_ref[h]))
        alpha = jnp.exp2(m_prev - m_new)
        s3 = s_ref[h % n_slots, :, 0:tq].reshape(tk // SUBLANE, SUBLANE, tq)
        p = jnp.exp2(s3 - m_new[None]).reshape(tk, tq).astype(BF16)
        pv = _dot(with_ones(vt_ref[t, vh * hd:(vh + 1) * hd, :]), p)
        acc = acc_ref[h].reshape(acc_rows // SUBLANE, SUBLANE, tq) * alpha[None]
        acc_ref[h] = acc.reshape(acc_rows, tq) + pv
        m_ref[h] = m_new

    for h in range(DENSE_LOOKAHEAD):
        scores(0, h)

    def sub_tile(t, carry):
        t_next = jnp.minimum(t + 1, n_sub - 1)
        for h in range(HEAD_SLOTS):
            ahead = h + DENSE_LOOKAHEAD
            if ahead < HEAD_SLOTS:
                scores(t, ahead)
            else:
                scores(t_next, ahead - HEAD_SLOTS)
            softmax_pv(t, h)
        return carry

    lax.fori_loop(0, n_sub, sub_tile, 0, unroll=2)

    @pl.when(kv == pl.num_programs(2) - 1)
    def _():
        for j in range(HEAD_SLOTS // 2):
            outs = []
            for h in (2 * j, 2 * j + 1):
                outs.append(acc_ref[h, 0:hd, :] / acc_ref[h, hd:hd + 1, :])
            o_t = jnp.concatenate(outs, axis=0)
            o_ref[:, j * LANE:(j + 1) * LANE] = o_t.T.astype(BF16)


def _dense_attn(qt, k, vt, km, vmt, *, n_seq, seq, q_base, meta_base):
    tq = qt.shape[2]
    nq = (qt.shape[0] - q_base) // n_seq
    tk = vt.shape[2]
    n_sub = _row_tile(seq // tk, DENSE_SUBTILES)
    nk = seq // (tk * n_sub)
    out_rows = n_seq * nq * tq
    return pl.pallas_call(
        _dense_attn_kernel,
        grid=(n_seq, nq, nk),
        in_specs=[
            pl.BlockSpec((None, HEAD_SLOTS * LANE, tq), lambda b, i, j: (q_base + b * nq + i, 0, 0)),
            pl.BlockSpec((n_sub * tk, K_SLOTS * LANE), lambda b, i, j: (b * nk + j, 0)),
            pl.BlockSpec((n_sub, V_ROWS, tk), lambda b, i, j: (b * nk + j, 0, 0)),
            pl.BlockSpec((None, LANE, K_SLOTS * LANE), lambda b, i, j: (meta_base + b, 0, 0)),
            pl.BlockSpec((None, V_ROWS, LANE), lambda b, i, j: (meta_base + b, 0, 0)),
        ],
        out_specs=pl.BlockSpec((tq, D_MODEL), lambda b, i, j: (b * nq + i, 0)),
        out_shape=jax.ShapeDtypeStruct((out_rows, D_MODEL), BF16),
        scratch_shapes=[
            pltpu.VMEM((HEAD_SLOTS, SUBLANE, tq), F32),
            pltpu.VMEM((HEAD_SLOTS, SUBLANE, tq), F32),
            pltpu.VMEM((HEAD_SLOTS, MLA_V + 2 * SUBLANE, tq), F32),
            pltpu.VMEM((DENSE_SLOTS, tk, tq + LANE), F32),
            pltpu.VMEM((HEAD_SLOTS, N_META, tq), F32),
        ],
        compiler_params=pltpu.CompilerParams(
            dimension_semantics=("arbitrary", "arbitrary", "arbitrary"),
            vmem_limit_bytes=VMEM_LIMIT),
        name="dense_attn",
    )(qt, k, vt, km, vmt)


def _proj_na_kernel(h_ref, g_ref, w_ref, q_ref, k_ref, v_ref, *, transpose_v):
    a = _rms(h_ref[...], g_ref[2:3, :]).astype(BF16)
    qkv = _dot(a, w_ref[...])
    n = NA_HEADS * NA_HEAD_DIM
    q_ref[...] = (qkv[:, 0:n] * (NA_HEAD_DIM ** -0.5 * LOG2E)).astype(BF16)
    k_ref[...] = qkv[:, n:2 * n].astype(BF16)
    v = qkv[:, 2 * n:3 * n]
    if transpose_v:
        vt = v.T.astype(BF16)
        for t in range(v_ref.shape[0]):
            v_ref[t] = vt[:, t * LANE:(t + 1) * LANE]
    else:
        v_ref[...] = v.astype(BF16)


def _proj_na(h, gains, w, transpose_v):
    rows = h.shape[0]
    tm = _row_tile(rows, ROW_TILE)
    n = NA_HEADS * NA_HEAD_DIM
    if transpose_v:
        v_spec = pl.BlockSpec((tm // LANE, n, LANE), lambda i: (i, 0, 0))
        v_shape = jax.ShapeDtypeStruct((rows // LANE, n, LANE), BF16)
    else:
        v_spec = pl.BlockSpec((tm, n), lambda i: (i, 0))
        v_shape = jax.ShapeDtypeStruct((rows, n), BF16)
    return pl.pallas_call(
        functools.partial(_proj_na_kernel, transpose_v=transpose_v),
        grid=(rows // tm,),
        in_specs=[pl.BlockSpec((tm, D_MODEL), lambda i: (i, 0)), _const_spec(gains.shape),
                  _const_spec(w.shape)],
        out_specs=[pl.BlockSpec((tm, n), lambda i: (i, 0))] * 2 + [v_spec],
        out_shape=[jax.ShapeDtypeStruct((rows, n), BF16)] * 2 + [v_shape],
        compiler_params=pltpu.CompilerParams(
            dimension_semantics=("arbitrary",), vmem_limit_bytes=VMEM_LIMIT),
        name="proj_na",
    )(h, gains, w)


NA_SPAN_R = NA_WIN_R + 2
NA_MASKED = 2 * NA_WIN_R - 1


def _na_kernel(q_ref, k_ref, vt_ref, km_ref, vmt_ref, bias_ref, mb_ref, o_ref, s_ref,
               *, rows, rows_per_step):
    step = pl.program_id(1)
    n_pairs = rows_per_step // 2
    n_hp = NA_HEADS // 2
    span = NA_SPAN_R * GRID_W
    lane = lax.broadcasted_iota(jnp.int32, (GRID_W, LANE), 1)
    first = lane < (LANE // 2)
    zeros_m = jnp.zeros((LANE - N_META, 2 * LANE), F32)
    ones_v = jnp.ones((2 * SUBLANE, span + LANE), BF16)

    def geometry(rp):
        ra = step * rows_per_step + 2 * rp
        rs = [jnp.clip(ra + x - NA_WIN_R // 2, 0, rows - NA_WIN_R) for x in range(2)]
        ws = jnp.minimum((rs[0] // 2) * 2, rows - NA_SPAN_R)
        return ra, rs, ws

    n_slots = s_ref.shape[0]

    def scores(rp, hp):
        slot = hp % n_slots
        _, _, ws = geometry(rp)
        cols = slice(hp * LANE, (hp + 1) * LANE)
        parts = []
        for x in range(2):
            q0 = pl.multiple_of((2 * rp + x) * GRID_W, GRID_W)
            qx = q_ref[pl.ds(q0, GRID_W), cols]
            parts += [jnp.where(first, qx, jnp.zeros_like(qx)), jnp.where(first, jnp.zeros_like(qx), qx)]
        qblk = jnp.concatenate(parts, axis=0)
        k0 = pl.multiple_of(ws * GRID_W, 2 * GRID_W)
        s_ref[slot, 0:span, 0:2 * LANE] = _dot_nt(k_ref[pl.ds(k0, span), cols], qblk)
        s_ref[slot, span:span + N_META, 0:2 * LANE] = _dot_nt(km_ref[:, cols], qblk)

    for hp in range(NA_LOOKAHEAD):
        scores(0, hp)

    def row_pair(rp, carry):
        ra, rs, ws = geometry(rp)
        idx = []
        for jj in range(NA_SPAN_R):
            kr = ws + jj
            idx.append([jnp.where((kr >= rs[x]) & (kr < rs[x] + NA_WIN_R),
                                  kr - (ra + x) + NA_WIN_R - 1, NA_MASKED) for x in range(2)])
        t0 = ws // 2
        for hp in range(n_hp):
            slot = hp % n_slots
            ahead = hp + NA_LOOKAHEAD
            if ahead < n_hp:
                scores(rp, ahead)
            else:
                scores(jnp.minimum(rp + 1, n_pairs - 1), ahead - n_hp)
            cols = slice(hp * LANE, (hp + 1) * LANE)
            b = jnp.concatenate(
                [jnp.concatenate([bias_ref[hp, idx[jj][0]], bias_ref[hp, idx[jj][1]]], axis=1)
                 for jj in range(NA_SPAN_R)], axis=0)
            s = s_ref[slot, 0:span, 0:2 * LANE] + b
            mb = mb_ref[hp]
            sm = s_ref[slot, span:span + N_META, 0:2 * LANE] + jnp.concatenate([mb, mb], axis=1)
            s3 = s.reshape(span // SUBLANE, SUBLANE, 2 * LANE)
            sm3 = sm.reshape(N_META // SUBLANE, SUBLANE, 2 * LANE)
            m = _sublane_bcast_max(jnp.maximum(jnp.max(s3, axis=0), jnp.max(sm3, axis=0)))
            p = jnp.exp2(s3 - m[None]).reshape(span, 2 * LANE).astype(BF16)
            pm3 = jnp.exp2(sm3 - m[None])
            pm = jnp.concatenate([pm3.reshape(N_META, 2 * LANE), zeros_m], axis=0).astype(BF16)
            v_all = jnp.concatenate([vt_ref[t0 + t, cols, :] for t in range(span // LANE)]
                                    + [vmt_ref[cols, :]], axis=1)
            o_t = _dot(jnp.concatenate([v_all, ones_v], axis=0),
                       jnp.concatenate([p, pm], axis=0))
            o_t = o_t[0:LANE] / o_t[LANE:LANE + 1]
            for x in range(2):
                blk = o_t[:, x * LANE:(x + 1) * LANE].T
                q0 = pl.multiple_of((2 * rp + x) * GRID_W, GRID_W)
                o_ref[pl.ds(q0, GRID_W), cols] = jnp.where(
                    first, blk[0:GRID_W], blk[GRID_W:2 * GRID_W]).astype(BF16)
        return carry

    lax.fori_loop(0, n_pairs, row_pair, 0, unroll=2)


def _na_attn(q, k, vt, km, vmt, bias, mb, *, n_seq, seq, meta_base):
    rows = seq // GRID_W
    assert rows >= NA_SPAN_R and rows % 2 == 0
    rps = _row_tile(rows, NA_ROWS_PER_STEP)
    nsteps = rows // rps
    n = NA_HEADS * NA_HEAD_DIM
    span = NA_SPAN_R * GRID_W
    return pl.pallas_call(
        functools.partial(_na_kernel, rows=rows, rows_per_step=rps),
        grid=(n_seq, nsteps),
        in_specs=[
            pl.BlockSpec((rps * GRID_W, n), lambda b, i: (b * nsteps + i, 0)),
            pl.BlockSpec((seq, n), lambda b, i: (b, 0), pipeline_mode=pl.Buffered(1)),
            pl.BlockSpec((seq // LANE, n, LANE), lambda b, i: (b, 0, 0), pipeline_mode=pl.Buffered(1)),
            pl.BlockSpec((None, N_META, n), lambda b, i: (meta_base + b, 0, 0)),
            pl.BlockSpec((None, n, LANE), lambda b, i: (meta_base + b, 0, 0)),
            _const_spec(bias.shape),
            _const_spec(mb.shape),
        ],
        out_specs=pl.BlockSpec((rps * GRID_W, n), lambda b, i: (b * nsteps + i, 0)),
        out_shape=jax.ShapeDtypeStruct((n_seq * seq, n), BF16),
        scratch_shapes=[pltpu.VMEM((NA_SLOTS, span + N_META, 3 * LANE), F32)],
        compiler_params=pltpu.CompilerParams(
            dimension_semantics=("arbitrary", "arbitrary"), vmem_limit_bytes=VMEM_LIMIT),
        name="na_attn",
    )(q, k, vt, km, vmt, bias, mb)


def _na_meta_kernel(q_ref, km_ref, vm_ref, mb_ref, o_ref):
    lane = lax.broadcasted_iota(jnp.int32, (N_META, LANE), 1)
    first = lane < (LANE // 2)
    for j in range(NA_HEADS // 2):
        cols = slice(j * LANE, (j + 1) * LANE)
        qp = q_ref[:, cols]
        km = km_ref[:, cols]
        vm = vm_ref[:, cols]
        outs = []
        for half in range(2):
            h = 2 * j + half
            qh = jnp.where(first if half == 0 else jnp.logical_not(first), qp, jnp.zeros_like(qp))
            sm = _dot_nt(qh, km)
            sm = jnp.where(lane < N_META, sm + mb_ref[h:h + 1, :], NEG_INF)
            m = jnp.max(sm, axis=-1, keepdims=True)
            pm = jnp.exp2(sm - m)
            l = jnp.sum(pm, axis=-1, keepdims=True)
            outs.append(_dot(pm.astype(BF16), vm) / l)
        o_ref[:, cols] = jnp.where(first, outs[0], outs[1]).astype(BF16)


def _na_meta(qm, km, vm, mb):
    n_seq = km.shape[0]
    n = NA_HEADS * NA_HEAD_DIM
    return pl.pallas_call(
        _na_meta_kernel,
        grid=(n_seq,),
        in_specs=[
            pl.BlockSpec((N_META, n), lambda b: (b, 0)),
            pl.BlockSpec((None, LANE, n), lambda b: (b, 0, 0)),
            pl.BlockSpec((None, LANE, n), lambda b: (b, 0, 0)),
            _const_spec(mb.shape),
        ],
        out_specs=pl.BlockSpec((N_META, n), lambda b: (b, 0)),
        out_shape=jax.ShapeDtypeStruct((n_seq * N_META, n), BF16),
        compiler_params=pltpu.CompilerParams(dimension_semantics=("arbitrary",)),
        name="na_meta",
    )(qm, km, vm, mb)


def _take_cols(w, idx):
    idx = np.asarray(idx)
    neg = idx < 0
    same_run = np.where(neg[1:] | neg[:-1], neg[1:] & neg[:-1], np.diff(idx) == 1)
    breaks = np.flatnonzero(~same_run) + 1
    parts = []
    for run in np.split(idx, breaks):
        if run[0] < 0:
            parts.append(jnp.zeros((w.shape[0], len(run)), w.dtype))
        else:
            parts.append(w[:, int(run[0]):int(run[-1]) + 1])
    return jnp.concatenate(parts, axis=1)


def _swap_halves(n):
    half = n // 2
    return np.concatenate([np.arange(half, n), np.arange(0, half)])


def _dense_weights(w_in, q_norm, w_uq, kv_norm, w_ukv, gq_norm, gk_norm, w_out):
    pad = lambda k: -np.ones(k, np.int64)
    o_kr = MLA_Q_LORA + MLA_KV_LORA
    o_gq = o_kr + MLA_ROPE
    o_gk = o_gq + GQA_HEADS * GQA_HEAD_DIM
    o_gv = o_gk + GQA_KV_HEADS * GQA_HEAD_DIM
    axial = np.concatenate([_swap_halves(GQA_HEAD_DIM // 2),
                            GQA_HEAD_DIM // 2 + _swap_halves(GQA_HEAD_DIM // 2)])
    idx = [np.arange(0, o_kr)]
    idx += [pad(MLA_NOPE), o_kr + np.arange(MLA_ROPE), pad(LANE - MLA_NOPE - MLA_ROPE)]
    idx += [pad(MLA_NOPE), o_kr + _swap_halves(MLA_ROPE), pad(LANE - MLA_NOPE - MLA_ROPE)]
    assert 2 * GQA_HEAD_DIM == LANE and GQA_KV_HEADS == 2
    for h in range(GQA_HEADS):
        idx += [o_gq + h * GQA_HEAD_DIM + np.arange(GQA_HEAD_DIM)]
    for h in range(GQA_HEADS):
        idx += [o_gq + h * GQA_HEAD_DIM + axial]
    for h in range(GQA_KV_HEADS):
        idx += [o_gk + h * GQA_HEAD_DIM + np.arange(GQA_HEAD_DIM)]
    for h in range(GQA_KV_HEADS):
        idx += [o_gk + h * GQA_HEAD_DIM + axial]
    idx += [o_gv + np.arange(GQA_KV_HEADS * GQA_HEAD_DIM)]
    idx = np.concatenate(idx)
    assert idx.shape[0] == _C_END
    w_in2 = _take_cols(w_in.astype(BF16), idx)

    hd = MLA_NOPE + MLA_ROPE
    ia, ib = [], []
    for h in range(MLA_HEADS):
        ia += [h * hd + np.arange(hd), pad(LANE - hd)]
        ib += [pad(MLA_NOPE), h * hd + MLA_NOPE + _swap_halves(MLA_ROPE), pad(LANE - hd)]
    w_uq2 = _take_cols(w_uq.astype(BF16), np.concatenate(ia + ib))

    kvd = MLA_NOPE + MLA_V
    ik, iv = [], []
    for h in range(MLA_HEADS):
        ik += [h * kvd + np.arange(MLA_NOPE), pad(LANE - MLA_NOPE)]
        iv += [h * kvd + MLA_NOPE + np.arange(MLA_V)]
    w_ukv2 = _take_cols(w_ukv.astype(BF16), np.concatenate(ik + iv))

    def gain_pair(g):
        ga = jnp.concatenate([g, g])[None, :]
        gb = jnp.tile(g[jnp.asarray(axial)], 2)[None, :]
        return ga, gb

    gq_a, gq_b = gain_pair(gq_norm)
    gk_a, gk_b = gain_pair(gk_norm)

    w_out2 = w_out.astype(BF16)

    return dict(w_in=w_in2, q_norm=q_norm[None, :], w_uq=w_uq2, kv_norm=kv_norm[None, :],
                w_ukv=w_ukv2, gq_a=gq_a, gq_b=gq_b, gk_a=gk_a, gk_b=gk_b,
                rope_expand=_rope_expand_matrix()), w_out2


def _rope_tables(pos, row, col):
    half = MLA_ROPE // 2
    inv = 1.0 / (ROPE_THETA ** (jnp.arange(half, dtype=F32) / half))
    n = pos.shape[0]

    def cs(p):
        ang = p.astype(F32)[None, :] * inv[:, None]
        return jnp.cos(ang), jnp.sin(ang)

    vals = jnp.concatenate(cs(pos) + cs(row) + cs(col) + (jnp.zeros((LANE - 6 * half, n), F32),), axis=0).T
    hi = vals.astype(BF16)
    rest = vals - hi.astype(F32)
    mid = rest.astype(BF16)
    lo = (rest - mid.astype(F32)).astype(BF16)
    return jnp.stack([hi, mid, lo])


def _rope_expand_matrix():
    half = MLA_ROPE // 2
    c1, s1, cr, sr, cc, sc = range(6)
    tail = LANE - MLA_NOPE - MLA_ROPE
    blank = [(None, 0)]
    layout = (blank * (MLA_NOPE // half) + [(c1, 1), (c1, 1)] + blank * (tail // half)
              + blank * (MLA_NOPE // half) + [(s1, -1), (s1, 1)] + blank * (tail // half)
              + [(cr, 1), (cr, 1), (cc, 1), (cc, 1)] * (LANE // GQA_HEAD_DIM)
              + [(sr, -1), (sr, 1), (sc, -1), (sc, 1)] * (LANE // GQA_HEAD_DIM))
    expand = np.zeros((LANE, len(layout) * half), np.float32)
    for blk, (src, sign) in enumerate(layout):
        if src is not None:
            expand[src * half + np.arange(half), blk * half + np.arange(half)] = sign
    return jnp.asarray(expand, BF16)


def _na_bias_tables(rpb, meta_bias):
    c_idx = np.arange(GRID_W)
    c_start = np.clip(c_idx - NA_WIN_C // 2, 0, GRID_W - NA_WIN_C)
    col_mask = (c_idx[None, :] >= c_start[:, None]) & (c_idx[None, :] < c_start[:, None] + NA_WIN_C)
    col_off = np.clip(c_idx[None, :] - c_idx[:, None] + NA_WIN_C - 1, 0, 2 * NA_WIN_C - 2)
    hp = NA_HEADS // 2
    n_off = 2 * NA_WIN_C - 1
    select = np.zeros((2, n_off, GRID_W, 2, GRID_W), np.float32)
    kc_g, c_g = np.meshgrid(c_idx, c_idx, indexing="ij")
    for half in range(2):
        select[half, col_off[c_g, kc_g], kc_g, half, c_g] = 1.0
    rows = rpb.reshape(hp, 2, NA_MASKED, n_off).transpose(0, 2, 1, 3).reshape(hp * NA_MASKED, 2 * n_off)
    t = jnp.dot(rows, jnp.asarray(select.reshape(2 * n_off, GRID_W * LANE)),
                precision=lax.Precision.HIGHEST) * LOG2E
    t = t.reshape(hp, NA_MASKED, GRID_W, LANE)
    keep = np.tile(col_mask.T, (1, 2))
    t = jnp.where(jnp.asarray(keep)[None, None], t, NEG_INF)
    bias = jnp.concatenate([t, jnp.full_like(t[:, :1], NEG_INF)], axis=1)
    mbl = meta_bias * LOG2E
    mb_t = jnp.repeat(mbl.reshape(hp, 2, N_META).transpose(0, 2, 1), GRID_W, axis=2)
    mb = jnp.pad(mbl, ((0, 0), (0, LANE - N_META)))
    return bias, mb_t, mb


def _pad_meta(x, n_seq):
    c = x.shape[1]
    return jnp.pad(x.reshape(n_seq, N_META, c), ((0, 0), (0, LANE - N_META), (0, 0)))


def kernel(x_prompt, x_sample, meta, norm_gains, ffn1_w_gate, ffn1_w_up, ffn1_w_down, ffn2_w_gate, ffn2_w_up, ffn2_w_down, attn_w_in, mla_q_norm, mla_w_uq, mla_kv_norm, mla_w_ukv, gqa_q_norm, gqa_k_norm, attn_w_out, na_w_qkv, na_rpb, na_meta_bias, na_w_out):
    bp, sp, _ = x_prompt.shape
    bs, ss, _ = x_sample.shape
    n_seq = bp + bs
    depth = norm_gains.shape[0]
    groups = [(bp, sp, 0), (bs, ss, bp)]

    n_meta = n_seq * N_META
    meta_rows = -(-n_meta // LANE) * LANE
    pad_rows = lambda x: jnp.pad(x, ((0, meta_rows - x.shape[0]), (0, 0)))
    h_tok = [x_prompt.reshape(bp * sp, D_MODEL), x_sample.reshape(bs * ss, D_MODEL)]
    h_meta = pad_rows(jnp.tile(meta.astype(F32), (n_seq, 1)))

    smax = max(sp, ss)
    t = jnp.arange(smax)
    tab_tok = _rope_tables(t + N_META, t // GRID_W, t % GRID_W)
    mi = jnp.arange(meta_rows) % N_META
    tab_meta = _rope_tables(mi, jnp.full_like(mi, -1), mi)

    w1 = (ffn1_w_gate.astype(BF16), ffn1_w_up.astype(BF16), ffn1_w_down.astype(BF16))
    w2 = (ffn2_w_gate.astype(BF16), ffn2_w_up.astype(BF16), ffn2_w_down.astype(BF16))

    for i in range(depth):
        gains = jnp.pad(norm_gains[i], ((0, 2), (0, 0)))
        j = i // 2
        h_tok = [_ffn1(h, gains, *w1, i) for h in h_tok]
        h_meta = _ffn1(h_meta, gains, *w1, i)
        if i % 2 == 0:
            w, w_out = _dense_weights(attn_w_in[j], mla_q_norm[j], mla_w_uq[j], mla_kv_norm[j],
                                      mla_w_ukv[j], gqa_q_norm[j], gqa_k_norm[j], attn_w_out[j])
            qkv_tok = [_proj_dense(h, gains, w, tab_tok, s) for h, (_, s, _) in zip(h_tok, groups)]
            qmt, km, vmt = _proj_dense(h_meta, gains, w, tab_meta, 0)
            kmp = _pad_meta(km[:n_meta], n_seq)
            vmt = vmt.transpose(1, 0, 2).reshape(V_ROWS, meta_rows)
            vmtp = vmt[:, :n_meta].reshape(V_ROWS, n_seq, N_META).transpose(1, 0, 2)
            vmtp = jnp.pad(vmtp, ((0, 0), (0, 0), (0, LANE - N_META)))
            qmt = qmt.transpose(1, 0, 2).reshape(HEAD_SLOTS * LANE, meta_rows)
            qmtp = qmt[:, :n_meta].reshape(HEAD_SLOTS * LANE, n_seq, N_META).transpose(1, 0, 2)
            qmtp = jnp.pad(qmtp, ((0, 0), (0, 0), (0, LANE - N_META)))
            o_tok, o_meta = [], []
            for (qt, k, vt), (nb, s, b0) in zip(qkv_tok, groups):
                o_tok.append(_dense_attn(qt, k, vt, kmp, vmtp, n_seq=nb, seq=s, q_base=0, meta_base=b0))
                om = _dense_attn(qmtp, k, vt, kmp, vmtp, n_seq=nb, seq=s, q_base=b0, meta_base=b0)
                o_meta.append(om.reshape(nb, LANE, D_MODEL)[:, :N_META].reshape(nb * N_META, D_MODEL))
            o_meta = pad_rows(jnp.concatenate(o_meta, axis=0))
        else:
            w_qkv = na_w_qkv[j].astype(BF16)
            w_out = na_w_out[j].astype(BF16)
            bias, mb_t, mb = _na_bias_tables(na_rpb[j], na_meta_bias[j])
            qkv_tok = [_proj_na(h, gains, w_qkv, True) for h in h_tok]
            qm, km, vm = _proj_na(h_meta, gains, w_qkv, False)
            kmp, vmp = _pad_meta(km[:n_meta], n_seq), _pad_meta(vm[:n_meta], n_seq)
            km16 = km[:n_meta].reshape(n_seq, N_META, NA_HEADS * NA_HEAD_DIM)
            vmtp = vmp.transpose(0, 2, 1)
            o_tok = [_na_attn(q, k, vt, km16, vmtp, bias, mb_t, n_seq=nb, seq=s, meta_base=b0)
                     for (q, k, vt), (nb, s, b0) in zip(qkv_tok, groups)]
            o_meta = pad_rows(_na_meta(qm[:n_meta], kmp, vmp, mb))
        h_tok = [_mix_ffn2(h, o, w_out, gains, *w2, i) for h, o in zip(h_tok, o_tok)]
        h_meta = _mix_ffn2(h_meta, o_meta, w_out, gains, *w2, i)

    return (h_tok[0].reshape(bp, sp, D_MODEL), h_tok[1].reshape(bs, ss, D_MODEL))
```

```python
import functools
import math

import jax
import jax.numpy as jnp
import numpy as np
from jax import lax
from jax.experimental import pallas as pl
from jax.experimental.pallas import tpu as pltpu

F32 = jnp.float32
BF16 = jnp.bfloat16

D_MODEL = 1024
N_META = 16
GRID_W = 64
D_FF = 2816
EPS = 1e-6
NEG_INF = -1e30
LOG2E = math.log2(math.e)

MLA_HEADS = 8
MLA_Q_LORA = 256
MLA_KV_LORA = 128
MLA_NOPE = 64
MLA_ROPE = 32
MLA_V = 64
GQA_HEADS = 8
GQA_KV_HEADS = 2
GQA_HEAD_DIM = 64
ROPE_THETA = 10000.0
NA_HEADS = 16
NA_HEAD_DIM = 64
NA_WIN_R = 8
NA_WIN_C = 16

LANE = 128
HEAD_SLOTS = MLA_HEADS + GQA_HEADS
K_SLOTS = MLA_HEADS + 1
V_ROWS = (MLA_HEADS + GQA_KV_HEADS) * MLA_V
SUBLANE = 8
DENSE_LOOKAHEAD, DENSE_SLOTS = 2, 4
NA_LOOKAHEAD, NA_SLOTS = 2, 4
VMEM_LIMIT = 56 * 1024 * 1024
ROW_TILE = 512
DENSE_SUBTILES = 8
NA_ROWS_PER_STEP = 16

_C_CQ = 0
_C_CKV = _C_CQ + MLA_Q_LORA
_C_KRA = _C_CKV + MLA_KV_LORA
_C_KRB = _C_KRA + LANE
_C_GQA = _C_KRB + LANE
_C_GQB = _C_GQA + GQA_HEADS // 2 * LANE
_C_GKA = _C_GQB + GQA_HEADS // 2 * LANE
_C_GKB = _C_GKA + LANE
_C_GV = _C_GKB + LANE
_C_END = _C_GV + LANE


def _const_spec(shape):
    nd = len(shape)
    return pl.BlockSpec(shape, lambda *_: (0,) * nd, pipeline_mode=pl.Buffered(1))


def _rms(x, g):
    ms = jnp.mean(x * x, axis=-1, keepdims=True)
    return x * lax.rsqrt(ms + EPS) * g


def _dot(a, b):
    return jnp.dot(a, b, preferred_element_type=F32)


def _dot_nt(a, b):
    return lax.dot_general(a, b, (((1,), (1,)), ((), ())), preferred_element_type=F32)


def _row_tile(rows, want):
    t = min(rows, want)
    while rows % t:
        t //= 2
    return t


def _ffn_body(h, g_ref, pre, post, wg_ref, wu_ref, wd_ref):
    xn = _rms(h, g_ref[pre:pre + 1, :]).astype(BF16)
    gate = _dot(xn, wg_ref[...])
    up = _dot(xn, wu_ref[...])
    act = (gate * jax.nn.sigmoid(gate) * up).astype(BF16)
    y = _dot(act, wd_ref[...])
    return h + 0.5 * _rms(y, g_ref[post:post + 1, :])


def _ffn1_kernel(h_ref, g_ref, wg_ref, wu_ref, wd_ref, out_ref):
    out_ref[...] = _ffn_body(h_ref[...], g_ref, 0, 1, wg_ref, wu_ref, wd_ref)


def _mix_ffn2_kernel(h_ref, o_ref, wo_ref, g_ref, wg_ref, wu_ref, wd_ref, out_ref):
    mixed = _dot(o_ref[...], wo_ref[...])
    h = h_ref[...] + _rms(mixed, g_ref[3:4, :])
    out_ref[...] = _ffn_body(h, g_ref, 4, 5, wg_ref, wu_ref, wd_ref)


def _layer_spec(w, layer):
    return pl.BlockSpec((None,) + w.shape[1:], lambda *_: (layer, 0, 0), pipeline_mode=pl.Buffered(1))


def _ffn1(h, gains, wg, wu, wd, layer):
    rows = h.shape[0]
    tm = _row_tile(rows, ROW_TILE)
    return pl.pallas_call(
        _ffn1_kernel,
        grid=(rows // tm,),
        in_specs=[
            pl.BlockSpec((tm, D_MODEL), lambda i: (i, 0)),
            _const_spec(gains.shape),
            _layer_spec(wg, layer), _layer_spec(wu, layer), _layer_spec(wd, layer),
        ],
        out_specs=pl.BlockSpec((tm, D_MODEL), lambda i: (i, 0)),
        out_shape=jax.ShapeDtypeStruct((rows, D_MODEL), F32),
        compiler_params=pltpu.CompilerParams(
            dimension_semantics=("arbitrary",), vmem_limit_bytes=VMEM_LIMIT),
        name="ffn1",
    )(h, gains, wg, wu, wd)


def _mix_ffn2(h, o, wo, gains, wg, wu, wd, layer):
    rows = h.shape[0]
    tm = _row_tile(rows, ROW_TILE)
    return pl.pallas_call(
        _mix_ffn2_kernel,
        grid=(rows // tm,),
        in_specs=[
            pl.BlockSpec((tm, D_MODEL), lambda i: (i, 0)),
            pl.BlockSpec((tm, o.shape[1]), lambda i: (i, 0)),
            _const_spec(wo.shape),
            _const_spec(gains.shape),
            _layer_spec(wg, layer), _layer_spec(wu, layer), _layer_spec(wd, layer),
        ],
        out_specs=pl.BlockSpec((tm, D_MODEL), lambda i: (i, 0)),
        out_shape=jax.ShapeDtypeStruct((rows, D_MODEL), F32),
        compiler_params=pltpu.CompilerParams(
            dimension_semantics=("arbitrary",), vmem_limit_bytes=VMEM_LIMIT),
        name="mix_ffn2",
    )(h, o, wo, gains, wg, wu, wd)


def _proj_dense_kernel(h_ref, g_ref, win_ref, qn_ref, wuq_ref, kvn_ref, wukv_ref,
                       gqa_ref, gqb_ref, gka_ref, gkb_ref, exp_ref, tab_ref, qt_ref, k_ref, vt_ref):
    a = _rms(h_ref[...], g_ref[2:3, :]).astype(BF16)
    proj = _dot(a, win_ref[...])
    tab = sum(_dot(tab_ref[i], exp_ref[...]) for i in range(tab_ref.shape[0]))
    cos_k, sin_k = tab[:, 0:LANE], tab[:, LANE:2 * LANE]
    cos_g, sin_g = tab[:, 2 * LANE:3 * LANE], tab[:, 3 * LANE:4 * LANE]
    qs = (MLA_NOPE + MLA_ROPE) ** -0.5 * LOG2E
    lane = lax.broadcasted_iota(jnp.int32, cos_k.shape, 1)
    cos_q = jnp.where(lane < MLA_NOPE, qs, cos_k * qs)
    sin_q = sin_k * qs

    cqn = _rms(proj[:, _C_CQ:_C_CQ + MLA_Q_LORA], qn_ref[...]).astype(BF16)
    qab = _dot(cqn, wuq_ref[...])
    nq = MLA_HEADS * LANE
    for h in range(MLA_HEADS):
        qa = qab[:, h * LANE:(h + 1) * LANE]
        qb = qab[:, nq + h * LANE:nq + (h + 1) * LANE]
        qt_ref[0, h * LANE:(h + 1) * LANE, :] = (qa * cos_q + qb * sin_q).T.astype(BF16)

    ckvn = _rms(proj[:, _C_CKV:_C_CKV + MLA_KV_LORA], kvn_ref[...]).astype(BF16)
    kv = _dot(ckvn, wukv_ref[...])
    k_rope = (proj[:, _C_KRA:_C_KRA + LANE] * cos_k + proj[:, _C_KRB:_C_KRB + LANE] * sin_k)
    for h in range(MLA_HEADS):
        k_ref[:, h * LANE:(h + 1) * LANE] = (kv[:, h * LANE:(h + 1) * LANE] + k_rope).astype(BF16)
    nv = MLA_HEADS * MLA_V

    low = lane < GQA_HEAD_DIM

    def normed_rotary(xa, xb, cos, sin):
        sq = xa * xa
        ss_lo = jnp.sum(jnp.where(low, sq, 0.0), axis=-1, keepdims=True)
        ss_hi = jnp.sum(jnp.where(low, 0.0, sq), axis=-1, keepdims=True)
        r = lax.rsqrt(jnp.where(low, ss_lo, ss_hi) * (1.0 / GQA_HEAD_DIM) + EPS)
        return (xa * cos + xb * sin) * r

    gq_scale = GQA_HEAD_DIM ** -0.5 * LOG2E
    cq_g = cos_g * (gqa_ref[...] * gq_scale)
    sq_g = sin_g * (gqb_ref[...] * gq_scale)
    per_kv = GQA_HEADS // GQA_KV_HEADS
    zeros_t = jnp.zeros((LANE - GQA_HEAD_DIM, qt_ref.shape[2]), F32)
    for j in range(GQA_HEADS // 2):
        xa = proj[:, _C_GQA + j * LANE:_C_GQA + (j + 1) * LANE]
        xb = proj[:, _C_GQB + j * LANE:_C_GQB + (j + 1) * LANE]
        y_t = normed_rotary(xa, xb, cq_g, sq_g).T
        for half in range(2):
            h = 2 * j + half
            q_t = y_t[half * GQA_HEAD_DIM:(half + 1) * GQA_HEAD_DIM]
            rows = [q_t, zeros_t] if h // per_kv == 0 else [zeros_t, q_t]
            qt_ref[0, nq + h * LANE:nq + (h + 1) * LANE, :] = jnp.concatenate(rows, axis=0).astype(BF16)
    xa = proj[:, _C_GKA:_C_GKA + LANE]
    xb = proj[:, _C_GKB:_C_GKB + LANE]
    k_ref[:, nq:nq + LANE] = normed_rotary(xa, xb, cos_g * gka_ref[...], sin_g * gkb_ref[...]).astype(BF16)
    v = jnp.concatenate([kv[:, nq:nq + nv], proj[:, _C_GV:_C_GV + LANE]], axis=1)
    vt_ref[0] = v.T.astype(BF16)


def _proj_dense(h, gains, w, tab, seq):
    rows = h.shape[0]
    tm = _row_tile(seq if seq else rows, ROW_TILE)
    nblk = (seq // tm) if seq else 1
    consts = [w["w_in"], w["q_norm"], w["w_uq"], w["kv_norm"], w["w_ukv"],
              w["gq_a"], w["gq_b"], w["gk_a"], w["gk_b"], w["rope_expand"]]
    return pl.pallas_call(
        _proj_dense_kernel,
        grid=(rows // tm,),
        in_specs=[pl.BlockSpec((tm, D_MODEL), lambda i: (i, 0)), _const_spec(gains.shape)]
        + [_const_spec(c.shape) for c in consts]
        + [pl.BlockSpec((tab.shape[0], tm, LANE), lambda i: (0, i % nblk, 0))],
        out_specs=[
            pl.BlockSpec((1, HEAD_SLOTS * LANE, tm), lambda i: (i, 0, 0)),
            pl.BlockSpec((tm, K_SLOTS * LANE), lambda i: (i, 0)),
            pl.BlockSpec((1, V_ROWS, tm), lambda i: (i, 0, 0)),
        ],
        out_shape=[
            jax.ShapeDtypeStruct((rows // tm, HEAD_SLOTS * LANE, tm), BF16),
            jax.ShapeDtypeStruct((rows, K_SLOTS * LANE), BF16),
            jax.ShapeDtypeStruct((rows // tm, V_ROWS, tm), BF16),
        ],
        compiler_params=pltpu.CompilerParams(
            dimension_semantics=("arbitrary",), vmem_limit_bytes=VMEM_LIMIT),
        name="proj_dense",
    )(h, gains, *consts, tab)


def _head_slots(h):
    if h < MLA_HEADS:
        return h, h
    kvh = (h - MLA_HEADS) // (GQA_HEADS // GQA_KV_HEADS)
    return MLA_HEADS, MLA_HEADS + kvh


def _sublane_bcast_max(x):
    return jnp.broadcast_to(jnp.max(x, axis=0, keepdims=True), x.shape)


def _dense_attn_kernel(qt_ref, k_ref, vt_ref, km_ref, vmt_ref, o_ref,
                       m_ref, smax_ref, acc_ref, s_ref, sm_ref):
    kv = pl.program_id(2)
    tq = qt_ref.shape[1]
    n_sub, _, tk = vt_ref.shape
    hd = MLA_V
    acc_rows = acc_ref.shape[1]

    def with_ones(vt):
        return jnp.concatenate([vt, jnp.ones((acc_rows - hd, vt.shape[1]), BF16)], axis=0)

    @pl.when(kv == 0)
    def _():
        for h in range(HEAD_SLOTS):
            ks, _ = _head_slots(h)
            qt = qt_ref[h * LANE:(h + 1) * LANE, :]
            sm_ref[h] = _dot(km_ref[0:N_META, ks * LANE:(ks + 1) * LANE], qt)
        zeros = jnp.zeros((LANE - N_META, tq), F32)
        for h in range(HEAD_SLOTS):
            _, vh = _head_slots(h)
            s3 = sm_ref[h].reshape(N_META // SUBLANE, SUBLANE, tq)
            m = _sublane_bcast_max(jnp.max(s3, axis=0))
            p3 = jnp.exp2(s3 - m[None])
            m_ref[h] = m
            p = jnp.concatenate([p3.reshape(N_META, tq), zeros], axis=0).astype(BF16)
            acc_ref[h] = _dot(with_ones(vmt_ref[vh * hd:(vh + 1) * hd, :]), p)

    n_slots = s_ref.shape[0]

    def scores(t, h):
        ks, _ = _head_slots(h)
        k0 = pl.multiple_of(t * tk, tk)
        k = k_ref[pl.ds(k0, tk), ks * LANE:(ks + 1) * LANE]
        s = _dot(k, qt_ref[h * LANE:(h + 1) * LANE, :])
        s_ref[h % n_slots] = s
        smax_ref[h] = jnp.max(s.reshape(tk // SUBLANE, SUBLANE, tq), axis=0)

    def softmax_pv(t, h):
        _, vh = _head_slots(h)
        m_prev = m_ref[h]
        m_new = jnp.maximum(m_prev, _sublane_bcast_max(smax_ref[h]))
        alpha = jnp.exp2(m_prev - m_new)
        s3 = s_ref[h % n_slots].reshape(tk // SUBLANE, SUBLANE, tq)
        p = jnp.exp2(s3 - m_new[None]).reshape(tk, tq).astype(BF16)
        pv = _dot(with_ones(vt_ref[t, vh * hd:(vh + 1) * hd, :]), p)
        acc = acc_ref[h].reshape(acc_rows // SUBLANE, SUBLANE, tq) * alpha[None]
        acc_ref[h] = acc.reshape(acc_rows, tq) + pv
        m_ref[h] = m_new

    for h in range(DENSE_LOOKAHEAD):
        scores(0, h)

    def sub_tile(t, carry):
        t_next = jnp.minimum(t + 1, n_sub - 1)
        for h in range(HEAD_SLOTS):
            ahead = h + DENSE_LOOKAHEAD
            if ahead < HEAD_SLOTS:
                scores(t, ahead)
            else:
                scores(t_next, ahead - HEAD_SLOTS)
            softmax_pv(t, h)
        return carry

    lax.fori_loop(0, n_sub, sub_tile, 0, unroll=2)

    @pl.when(kv == pl.num_programs(2) - 1)
    def _():
        for j in range(HEAD_SLOTS // 2):
            outs = []
            for h in (2 * j, 2 * j + 1):
                outs.append(acc_ref[h, 0:hd, :] / acc_ref[h, hd:hd + 1, :])
            o_t = jnp.concatenate(outs, axis=0)
            o_ref[:, j * LANE:(j + 1) * LANE] = o_t.T.astype(BF16)


def _dense_attn(qt, k, vt, km, vmt, *, n_seq, seq, q_base, meta_base):
    tq = qt.shape[2]
    nq = (qt.shape[0] - q_base) // n_seq
    tk = vt.shape[2]
    n_sub = _row_tile(seq // tk, DENSE_SUBTILES)
    nk = seq // (tk * n_sub)
    out_rows = n_seq * nq * tq
    return pl.pallas_call(
        _dense_attn_kernel,
        grid=(n_seq, nq, nk),
        in_specs=[
            pl.BlockSpec((None, HEAD_SLOTS * LANE, tq), lambda b, i, j: (q_base + b * nq + i, 0, 0)),
            pl.BlockSpec((n_sub * tk, K_SLOTS * LANE), lambda b, i, j: (b * nk + j, 0)),
            pl.BlockSpec((n_sub, V_ROWS, tk), lambda b, i, j: (b * nk + j, 0, 0)),
            pl.BlockSpec((None, LANE, K_SLOTS * LANE), lambda b, i, j: (meta_base + b, 0, 0)),
            pl.BlockSpec((None, V_ROWS, LANE), lambda b, i, j: (meta_base + b, 0, 0)),
        ],
        out_specs=pl.BlockSpec((tq, D_MODEL), lambda b, i, j: (b * nq + i, 0)),
        out_shape=jax.ShapeDtypeStruct((out_rows, D_MODEL), BF16),
        scratch_shapes=[
            pltpu.VMEM((HEAD_SLOTS, SUBLANE, tq), F32),
            pltpu.VMEM((HEAD_SLOTS, SUBLANE, tq), F32),
            pltpu.VMEM((HEAD_SLOTS, MLA_V + 2 * SUBLANE, tq), F32),
            pltpu.VMEM((DENSE_SLOTS, tk, tq), F32),
            pltpu.VMEM((HEAD_SLOTS, N_META, tq), F32),
        ],
        compiler_params=pltpu.CompilerParams(
            dimension_semantics=("arbitrary", "arbitrary", "arbitrary"),
            vmem_limit_bytes=VMEM_LIMIT),
        name="dense_attn",
    )(qt, k, vt, km, vmt)


def _proj_na_kernel(h_ref, g_ref, w_ref, q_ref, k_ref, v_ref, *, transpose_v):
    a = _rms(h_ref[...], g_ref[2:3, :]).astype(BF16)
    qkv = _dot(a, w_ref[...])
    n = NA_HEADS * NA_HEAD_DIM
    q_ref[...] = (qkv[:, 0:n] * (NA_HEAD_DIM ** -0.5 * LOG2E)).astype(BF16)
    k_ref[...] = qkv[:, n:2 * n].astype(BF16)
    v = qkv[:, 2 * n:3 * n]
    if transpose_v:
        vt = v.T.astype(BF16)
        for t in range(v_ref.shape[0]):
            v_ref[t] = vt[:, t * LANE:(t + 1) * LANE]
    else:
        v_ref[...] = v.astype(BF16)


def _proj_na(h, gains, w, transpose_v):
    rows = h.shape[0]
    tm = _row_tile(rows, ROW_TILE)
    n = NA_HEADS * NA_HEAD_DIM
    if transpose_v:
        v_spec = pl.BlockSpec((tm // LANE, n, LANE), lambda i: (i, 0, 0))
        v_shape = jax.ShapeDtypeStruct((rows // LANE, n, LANE), BF16)
    else:
        v_spec = pl.BlockSpec((tm, n), lambda i: (i, 0))
        v_shape = jax.ShapeDtypeStruct((rows, n), BF16)
    return pl.pallas_call(
        functools.partial(_proj_na_kernel, transpose_v=transpose_v),
        grid=(rows // tm,),
        in_specs=[pl.BlockSpec((tm, D_MODEL), lambda i: (i, 0)), _const_spec(gains.shape),
                  _const_spec(w.shape)],
        out_specs=[pl.BlockSpec((tm, n), lambda i: (i, 0))] * 2 + [v_spec],
        out_shape=[jax.ShapeDtypeStruct((rows, n), BF16)] * 2 + [v_shape],
        compiler_params=pltpu.CompilerParams(
            dimension_semantics=("arbitrary",), vmem_limit_bytes=VMEM_LIMIT),
        name="proj_na",
    )(h, gains, w)


NA_SPAN_R = NA_WIN_R + 2
NA_MASKED = 2 * NA_WIN_R - 1


def _na_kernel(q_ref, k_ref, vt_ref, km_ref, vmt_ref, bias_ref, mb_ref, o_ref, s_ref,
               *, rows, rows_per_step):
    step = pl.program_id(1)
    n_pairs = rows_per_step // 2
    n_hp = NA_HEADS // 2
    span = NA_SPAN_R * GRID_W
    lane = lax.broadcasted_iota(jnp.int32, (GRID_W, LANE), 1)
    first = lane < (LANE // 2)
    zeros_m = jnp.zeros((LANE - N_META, 2 * LANE), F32)
    ones_v = jnp.ones((2 * SUBLANE, span + LANE), BF16)

    def geometry(rp):
        ra = step * rows_per_step + 2 * rp
        rs = [jnp.clip(ra + x - NA_WIN_R // 2, 0, rows - NA_WIN_R) for x in range(2)]
        ws = jnp.minimum((rs[0] // 2) * 2, rows - NA_SPAN_R)
        return ra, rs, ws

    n_slots = s_ref.shape[0]

    def scores(rp, hp):
        slot = hp % n_slots
        _, _, ws = geometry(rp)
        cols = slice(hp * LANE, (hp + 1) * LANE)
        parts = []
        for x in range(2):
            q0 = pl.multiple_of((2 * rp + x) * GRID_W, GRID_W)
            qx = q_ref[pl.ds(q0, GRID_W), cols]
            parts += [jnp.where(first, qx, jnp.zeros_like(qx)), jnp.where(first, jnp.zeros_like(qx), qx)]
        qblk = jnp.concatenate(parts, axis=0)
        k0 = pl.multiple_of(ws * GRID_W, 2 * GRID_W)
        s_ref[slot, 0:span, :] = _dot_nt(k_ref[pl.ds(k0, span), cols], qblk)
        s_ref[slot, span:span + N_META, :] = _dot_nt(km_ref[:, cols], qblk)

    for hp in range(NA_LOOKAHEAD):
        scores(0, hp)

    def row_pair(rp, carry):
        ra, rs, ws = geometry(rp)
        idx = []
        for jj in range(NA_SPAN_R):
            kr = ws + jj
            idx.append([jnp.where((kr >= rs[x]) & (kr < rs[x] + NA_WIN_R),
                                  kr - (ra + x) + NA_WIN_R - 1, NA_MASKED) for x in range(2)])
        t0 = ws // 2
        for hp in range(n_hp):
            slot = hp % n_slots
            ahead = hp + NA_LOOKAHEAD
            if ahead < n_hp:
                scores(rp, ahead)
            else:
                scores(jnp.minimum(rp + 1, n_pairs - 1), ahead - n_hp)
            cols = slice(hp * LANE, (hp + 1) * LANE)
            b = jnp.concatenate(
                [jnp.concatenate([bias_ref[hp, idx[jj][0]], bias_ref[hp, idx[jj][1]]], axis=1)
                 for jj in range(NA_SPAN_R)], axis=0)
            s = s_ref[slot, 0:span, :] + b
            mb = mb_ref[hp]
            sm = s_ref[slot, span:span + N_META, :] + jnp.concatenate([mb, mb], axis=1)
            s3 = s.reshape(span // SUBLANE, SUBLANE, 2 * LANE)
            sm3 = sm.reshape(N_META // SUBLANE, SUBLANE, 2 * LANE)
            m = _sublane_bcast_max(jnp.maximum(jnp.max(s3, axis=0), jnp.max(sm3, axis=0)))
            p = jnp.exp2(s3 - m[None]).reshape(span, 2 * LANE).astype(BF16)
            pm3 = jnp.exp2(sm3 - m[None])
            pm = jnp.concatenate([pm3.reshape(N_META, 2 * LANE), zeros_m], axis=0).astype(BF16)
            v_all = jnp.concatenate([vt_ref[t0 + t, cols, :] for t in range(span // LANE)]
                                    + [vmt_ref[cols, :]], axis=1)
            o_t = _dot(jnp.concatenate([v_all, ones_v], axis=0),
                       jnp.concatenate([p, pm], axis=0))
            o_t = o_t[0:LANE] / o_t[LANE:LANE + 1]
            for x in range(2):
                blk = o_t[:, x * LANE:(x + 1) * LANE].T
                q0 = pl.multiple_of((2 * rp + x) * GRID_W, GRID_W)
                o_ref[pl.ds(q0, GRID_W), cols] = jnp.where(
                    first, blk[0:GRID_W], blk[GRID_W:2 * GRID_W]).astype(BF16)
        return carry

    lax.fori_loop(0, n_pairs, row_pair, 0, unroll=4)


def _na_attn(q, k, vt, km, vmt, bias, mb, *, n_seq, seq, meta_base):
    rows = seq // GRID_W
    assert rows >= NA_SPAN_R and rows % 2 == 0
    rps = _row_tile(rows, NA_ROWS_PER_STEP)
    nsteps = rows // rps
    n = NA_HEADS * NA_HEAD_DIM
    span = NA_SPAN_R * GRID_W
    return pl.pallas_call(
        functools.partial(_na_kernel, rows=rows, rows_per_step=rps),
        grid=(n_seq, nsteps),
        in_specs=[
            pl.BlockSpec((rps * GRID_W, n), lambda b, i: (b * nsteps + i, 0)),
            pl.BlockSpec((seq, n), lambda b, i: (b, 0), pipeline_mode=pl.Buffered(1)),
            pl.BlockSpec((seq // LANE, n, LANE), lambda b, i: (b, 0, 0), pipeline_mode=pl.Buffered(1)),
            pl.BlockSpec((None, N_META, n), lambda b, i: (meta_base + b, 0, 0)),
            pl.BlockSpec((None, n, LANE), lambda b, i: (meta_base + b, 0, 0)),
            _const_spec(bias.shape),
            _const_spec(mb.shape),
        ],
        out_specs=pl.BlockSpec((rps * GRID_W, n), lambda b, i: (b * nsteps + i, 0)),
        out_shape=jax.ShapeDtypeStruct((n_seq * seq, n), BF16),
        scratch_shapes=[pltpu.VMEM((NA_SLOTS, span + N_META, 2 * LANE), F32)],
        compiler_params=pltpu.CompilerParams(
            dimension_semantics=("arbitrary", "arbitrary"), vmem_limit_bytes=VMEM_LIMIT),
        name="na_attn",
    )(q, k, vt, km, vmt, bias, mb)


def _na_meta_kernel(q_ref, km_ref, vm_ref, mb_ref, o_ref):
    lane = lax.broadcasted_iota(jnp.int32, (N_META, LANE), 1)
    first = lane < (LANE // 2)
    for j in range(NA_HEADS // 2):
        cols = slice(j * LANE, (j + 1) * LANE)
        qp = q_ref[:, cols]
        km = km_ref[:, cols]
        vm = vm_ref[:, cols]
        outs = []
        for half in range(2):
            h = 2 * j + half
            qh = jnp.where(first if half == 0 else jnp.logical_not(first), qp, jnp.zeros_like(qp))
            sm = _dot_nt(qh, km)
            sm = jnp.where(lane < N_META, sm + mb_ref[h:h + 1, :], NEG_INF)
            m = jnp.max(sm, axis=-1, keepdims=True)
            pm = jnp.exp2(sm - m)
            l = jnp.sum(pm, axis=-1, keepdims=True)
            outs.append(_dot(pm.astype(BF16), vm) / l)
        o_ref[:, cols] = jnp.where(first, outs[0], outs[1]).astype(BF16)


def _na_meta(qm, km, vm, mb):
    n_seq = km.shape[0]
    n = NA_HEADS * NA_HEAD_DIM
    return pl.pallas_call(
        _na_meta_kernel,
        grid=(n_seq,),
        in_specs=[
            pl.BlockSpec((N_META, n), lambda b: (b, 0)),
            pl.BlockSpec((None, LANE, n), lambda b: (b, 0, 0)),
            pl.BlockSpec((None, LANE, n), lambda b: (b, 0, 0)),
            _const_spec(mb.shape),
        ],
        out_specs=pl.BlockSpec((N_META, n), lambda b: (b, 0)),
        out_shape=jax.ShapeDtypeStruct((n_seq * N_META, n), BF16),
        compiler_params=pltpu.CompilerParams(dimension_semantics=("arbitrary",)),
        name="na_meta",
    )(qm, km, vm, mb)


def _take_cols(w, idx):
    idx = np.asarray(idx)
    neg = idx < 0
    same_run = np.where(neg[1:] | neg[:-1], neg[1:] & neg[:-1], np.diff(idx) == 1)
    breaks = np.flatnonzero(~same_run) + 1
    parts = []
    for run in np.split(idx, breaks):
        if run[0] < 0:
            parts.append(jnp.zeros((w.shape[0], len(run)), w.dtype))
        else:
            parts.append(w[:, int(run[0]):int(run[-1]) + 1])
    return jnp.concatenate(parts, axis=1)


def _swap_halves(n):
    half = n // 2
    return np.concatenate([np.arange(half, n), np.arange(0, half)])


def _dense_weights(w_in, q_norm, w_uq, kv_norm, w_ukv, gq_norm, gk_norm, w_out):
    pad = lambda k: -np.ones(k, np.int64)
    o_kr = MLA_Q_LORA + MLA_KV_LORA
    o_gq = o_kr + MLA_ROPE
    o_gk = o_gq + GQA_HEADS * GQA_HEAD_DIM
    o_gv = o_gk + GQA_KV_HEADS * GQA_HEAD_DIM
    axial = np.concatenate([_swap_halves(GQA_HEAD_DIM // 2),
                            GQA_HEAD_DIM // 2 + _swap_halves(GQA_HEAD_DIM // 2)])
    idx = [np.arange(0, o_kr)]
    idx += [pad(MLA_NOPE), o_kr + np.arange(MLA_ROPE), pad(LANE - MLA_NOPE - MLA_ROPE)]
    idx += [pad(MLA_NOPE), o_kr + _swap_halves(MLA_ROPE), pad(LANE - MLA_NOPE - MLA_ROPE)]
    assert 2 * GQA_HEAD_DIM == LANE and GQA_KV_HEADS == 2
    for h in range(GQA_HEADS):
        idx += [o_gq + h * GQA_HEAD_DIM + np.arange(GQA_HEAD_DIM)]
    for h in range(GQA_HEADS):
        idx += [o_gq + h * GQA_HEAD_DIM + axial]
    for h in range(GQA_KV_HEADS):
        idx += [o_gk + h * GQA_HEAD_DIM + np.arange(GQA_HEAD_DIM)]
    for h in range(GQA_KV_HEADS):
        idx += [o_gk + h * GQA_HEAD_DIM + axial]
    idx += [o_gv + np.arange(GQA_KV_HEADS * GQA_HEAD_DIM)]
    idx = np.concatenate(idx)
    assert idx.shape[0] == _C_END
    w_in2 = _take_cols(w_in.astype(BF16), idx)

    hd = MLA_NOPE + MLA_ROPE
    ia, ib = [], []
    for h in range(MLA_HEADS):
        ia += [h * hd + np.arange(hd), pad(LANE - hd)]
        ib += [pad(MLA_NOPE), h * hd + MLA_NOPE + _swap_halves(MLA_ROPE), pad(LANE - hd)]
    w_uq2 = _take_cols(w_uq.astype(BF16), np.concatenate(ia + ib))

    kvd = MLA_NOPE + MLA_V
    ik, iv = [], []
    for h in range(MLA_HEADS):
        ik += [h * kvd + np.arange(MLA_NOPE), pad(LANE - MLA_NOPE)]
        iv += [h * kvd + MLA_NOPE + np.arange(MLA_V)]
    w_ukv2 = _take_cols(w_ukv.astype(BF16), np.concatenate(ik + iv))

    def gain_pair(g):
        ga = jnp.concatenate([g, g])[None, :]
        gb = jnp.tile(g[jnp.asarray(axial)], 2)[None, :]
        return ga, gb

    gq_a, gq_b = gain_pair(gq_norm)
    gk_a, gk_b = gain_pair(gk_norm)

    w_out2 = w_out.astype(BF16)

    return dict(w_in=w_in2, q_norm=q_norm[None, :], w_uq=w_uq2, kv_norm=kv_norm[None, :],
                w_ukv=w_ukv2, gq_a=gq_a, gq_b=gq_b, gk_a=gk_a, gk_b=gk_b,
                rope_expand=_rope_expand_matrix()), w_out2


def _rope_tables(pos, row, row_repeat, col, col_tile):
    half = MLA_ROPE // 2
    inv = 1.0 / (ROPE_THETA ** (jnp.arange(half, dtype=F32) / half))
    n = pos.shape[0]

    def cs(p):
        ang = p.astype(F32)[None, :] * inv[:, None]
        return jnp.cos(ang), jnp.sin(ang)

    by_row = tuple(jnp.repeat(x, row_repeat, axis=1) for x in cs(row))
    by_col = tuple(jnp.tile(x, (1, col_tile)) for x in cs(col))
    vals = jnp.concatenate(cs(pos) + by_row + by_col + (jnp.zeros((LANE - 6 * half, n), F32),), axis=0).T
    hi = vals.astype(BF16)
    rest = vals - hi.astype(F32)
    mid = rest.astype(BF16)
    lo = (rest - mid.astype(F32)).astype(BF16)
    return jnp.stack([hi, mid, lo])


def _rope_expand_matrix():
    half = MLA_ROPE // 2
    c1, s1, cr, sr, cc, sc = range(6)
    tail = LANE - MLA_NOPE - MLA_ROPE
    blank = [(None, 0)]
    layout = (blank * (MLA_NOPE // half) + [(c1, 1), (c1, 1)] + blank * (tail // half)
              + blank * (MLA_NOPE // half) + [(s1, -1), (s1, 1)] + blank * (tail // half)
              + [(cr, 1), (cr, 1), (cc, 1), (cc, 1)] * (LANE // GQA_HEAD_DIM)
              + [(sr, -1), (sr, 1), (sc, -1), (sc, 1)] * (LANE // GQA_HEAD_DIM))
    expand = np.zeros((LANE, len(layout) * half), np.float32)
    for blk, (src, sign) in enumerate(layout):
        if src is not None:
            expand[src * half + np.arange(half), blk * half + np.arange(half)] = sign
    return jnp.asarray(expand, BF16)


def _na_bias_tables(rpb, meta_bias):
    c_idx = np.arange(GRID_W)
    c_start = np.clip(c_idx - NA_WIN_C // 2, 0, GRID_W - NA_WIN_C)
    col_mask = (c_idx[None, :] >= c_start[:, None]) & (c_idx[None, :] < c_start[:, None] + NA_WIN_C)
    col_off = np.clip(c_idx[None, :] - c_idx[:, None] + NA_WIN_C - 1, 0, 2 * NA_WIN_C - 2)
    hp = NA_HEADS // 2
    n_off = 2 * NA_WIN_C - 1
    select = np.zeros((2, n_off, GRID_W, 2, GRID_W), np.float32)
    kc_g, c_g = np.meshgrid(c_idx, c_idx, indexing="ij")
    for half in range(2):
        select[half, col_off[c_g, kc_g], kc_g, half, c_g] = 1.0
    rows = rpb.reshape(hp, 2, NA_MASKED, n_off).transpose(0, 2, 1, 3).reshape(hp * NA_MASKED, 2 * n_off)
    t = jnp.dot(rows, jnp.asarray(select.reshape(2 * n_off, GRID_W * LANE)),
                precision=lax.Precision.HIGHEST) * LOG2E
    t = t.reshape(hp, NA_MASKED, GRID_W, LANE)
    keep = np.tile(col_mask.T, (1, 2))
    t = jnp.where(jnp.asarray(keep)[None, None], t, NEG_INF)
    bias = jnp.concatenate([t, jnp.full_like(t[:, :1], NEG_INF)], axis=1)
    mbl = meta_bias * LOG2E
    mb_t = jnp.repeat(mbl.reshape(hp, 2, N_META).transpose(0, 2, 1), GRID_W, axis=2)
    mb = jnp.pad(mbl, ((0, 0), (0, LANE - N_META)))
    return bias, mb_t, mb


def _pad_meta(x, n_seq):
    c = x.shape[1]
    return jnp.pad(x.reshape(n_seq, N_META, c), ((0, 0), (0, LANE - N_META), (0, 0)))


def kernel(x_prompt, x_sample, meta, norm_gains, ffn1_w_gate, ffn1_w_up, ffn1_w_down, ffn2_w_gate, ffn2_w_up, ffn2_w_down, attn_w_in, mla_q_norm, mla_w_uq, mla_kv_norm, mla_w_ukv, gqa_q_norm, gqa_k_norm, attn_w_out, na_w_qkv, na_rpb, na_meta_bias, na_w_out):
    bp, sp, _ = x_prompt.shape
    bs, ss, _ = x_sample.shape
    n_seq = bp + bs
    depth = norm_gains.shape[0]
    groups = [(bp, sp, 0), (bs, ss, bp)]

    n_meta = n_seq * N_META
    meta_rows = -(-n_meta // LANE) * LANE
    pad_rows = lambda x: jnp.pad(x, ((0, meta_rows - x.shape[0]), (0, 0)))
    h_tok = [x_prompt.reshape(bp * sp, D_MODEL), x_sample.reshape(bs * ss, D_MODEL)]
    h_meta = pad_rows(jnp.tile(meta.astype(F32), (n_seq, 1)))

    smax = max(sp, ss)
    tab_tok = _rope_tables(jnp.arange(smax) + N_META, jnp.arange(smax // GRID_W), GRID_W,
                           jnp.arange(GRID_W), smax // GRID_W)
    tab_meta = _rope_tables(jnp.arange(meta_rows) % N_META, jnp.full((1,), -1), meta_rows,
                            jnp.arange(N_META), meta_rows // N_META)

    w1 = (ffn1_w_gate.astype(BF16), ffn1_w_up.astype(BF16), ffn1_w_down.astype(BF16))
    w2 = (ffn2_w_gate.astype(BF16), ffn2_w_up.astype(BF16), ffn2_w_down.astype(BF16))

    for i in range(depth):
        gains = jnp.pad(norm_gains[i], ((0, 2), (0, 0)))
        j = i // 2
        h_tok = [_ffn1(h, gains, *w1, i) for h in h_tok]
        h_meta = _ffn1(h_meta, gains, *w1, i)
        if i % 2 == 0:
            w, w_out = _dense_weights(attn_w_in[j], mla_q_norm[j], mla_w_uq[j], mla_kv_norm[j],
                                      mla_w_ukv[j], gqa_q_norm[j], gqa_k_norm[j], attn_w_out[j])
            qkv_tok = [_proj_dense(h, gains, w, tab_tok, s) for h, (_, s, _) in zip(h_tok, groups)]
            qmt, km, vmt = _proj_dense(h_meta, gains, w, tab_meta, 0)
            kmp = _pad_meta(km[:n_meta], n_seq)
            vmt = vmt.transpose(1, 0, 2).reshape(V_ROWS, meta_rows)
            vmtp = vmt[:, :n_meta].reshape(V_ROWS, n_seq, N_META).transpose(1, 0, 2)
            vmtp = jnp.pad(vmtp, ((0, 0), (0, 0), (0, LANE - N_META)))
            qmt = qmt.transpose(1, 0, 2).reshape(HEAD_SLOTS * LANE, meta_rows)
            qmtp = qmt[:, :n_meta].reshape(HEAD_SLOTS * LANE, n_seq, N_META).transpose(1, 0, 2)
            qmtp = jnp.pad(qmtp, ((0, 0), (0, 0), (0, LANE - N_META)))
            o_tok, o_meta = [], []
            for (qt, k, vt), (nb, s, b0) in zip(qkv_tok, groups):
                o_tok.append(_dense_attn(qt, k, vt, kmp, vmtp, n_seq=nb, seq=s, q_base=0, meta_base=b0))
                om = _dense_attn(qmtp, k, vt, kmp, vmtp, n_seq=nb, seq=s, q_base=b0, meta_base=b0)
                o_meta.append(om.reshape(nb, LANE, D_MODEL)[:, :N_META].reshape(nb * N_META, D_MODEL))
            o_meta = pad_rows(jnp.concatenate(o_meta, axis=0))
        else:
            w_qkv = na_w_qkv[j].astype(BF16)
            w_out = na_w_out[j].astype(BF16)
            bias, mb_t, mb = _na_bias_tables(na_rpb[j], na_meta_bias[j])
            qkv_tok = [_proj_na(h, gains, w_qkv, True) for h in h_tok]
            qm, km, vm = _proj_na(h_meta, gains, w_qkv, False)
            kmp, vmp = _pad_meta(km[:n_meta], n_seq), _pad_meta(vm[:n_meta], n_seq)
            km16 = km[:n_meta].reshape(n_seq, N_META, NA_HEADS * NA_HEAD_DIM)
            vmtp = vmp.transpose(0, 2, 1)
            o_tok = [_na_attn(q, k, vt, km16, vmtp, bias, mb_t, n_seq=nb, seq=s, meta_base=b0)
                     for (q, k, vt), (nb, s, b0) in zip(qkv_tok, groups)]
            o_meta = pad_rows(_na_meta(qm[:n_meta], kmp, vmp, mb))
        h_tok = [_mix_ffn2(h, o, w_out, gains, *w2, i) for h, o in zip(h_tok, o_tok)]
        h_meta = _mix_ffn2(h_meta, o_meta, w_out, gains, *w2, i)

    return (h_tok[0].reshape(bp, sp, D_MODEL), h_tok[1].reshape(bs, ss, D_MODEL))
```

```python
import functools
import math

import jax
import jax.numpy as jnp
import numpy as np
from jax import lax
from jax.experimental import pallas as pl
from jax.experimental.pallas import tpu as pltpu

F32 = jnp.float32
BF16 = jnp.bfloat16

D_MODEL = 1024
N_META = 16
GRID_W = 64
D_FF = 2816
EPS = 1e-6
NEG_INF = -1e30
LOG2E = math.log2(math.e)

MLA_HEADS = 8
MLA_Q_LORA = 256
MLA_KV_LORA = 128
MLA_NOPE = 64
MLA_ROPE = 32
MLA_V = 64
GQA_HEADS = 8
GQA_KV_HEADS = 2
GQA_HEAD_DIM = 64
ROPE_THETA = 10000.0
NA_HEADS = 16
NA_HEAD_DIM = 64
NA_WIN_R = 8
NA_WIN_C = 16

LANE = 128
HEAD_SLOTS = MLA_HEADS + GQA_HEADS
K_SLOTS = MLA_HEADS + 1
V_ROWS = (MLA_HEADS + GQA_KV_HEADS) * MLA_V
SUBLANE = 8
DENSE_LOOKAHEAD, DENSE_SLOTS = 1, 2
NA_LOOKAHEAD, NA_SLOTS = 2, 4
VMEM_LIMIT = 56 * 1024 * 1024
ROW_TILE = 512
DENSE_SUBTILES = 8
NA_ROWS_PER_STEP = 16

_C_CQ = 0
_C_CKV = _C_CQ + MLA_Q_LORA
_C_KRA = _C_CKV + MLA_KV_LORA
_C_KRB = _C_KRA + LANE
_C_GQA = _C_KRB + LANE
_C_GQB = _C_GQA + GQA_HEADS // 2 * LANE
_C_GKA = _C_GQB + GQA_HEADS // 2 * LANE
_C_GKB = _C_GKA + LANE
_C_GV = _C_GKB + LANE
_C_END = _C_GV + LANE


def _const_spec(shape):
    nd = len(shape)
    return pl.BlockSpec(shape, lambda *_: (0,) * nd, pipeline_mode=pl.Buffered(1))


def _rms(x, g):
    ms = jnp.mean(x * x, axis=-1, keepdims=True)
    return x * lax.rsqrt(ms + EPS) * g


def _dot(a, b):
    return jnp.dot(a, b, preferred_element_type=F32)


def _dot_nt(a, b):
    return lax.dot_general(a, b, (((1,), (1,)), ((), ())), preferred_element_type=F32)


def _row_tile(rows, want):
    t = min(rows, want)
    while rows % t:
        t //= 2
    return t


def _ffn_body(h, g_ref, pre, post, wg_ref, wu_ref, wd_ref):
    xn = _rms(h, g_ref[pre:pre + 1, :]).astype(BF16)
    gate = _dot(xn, wg_ref[...])
    up = _dot(xn, wu_ref[...])
    act = (gate * jax.nn.sigmoid(gate) * up).astype(BF16)
    y = _dot(act, wd_ref[...])
    return h + 0.5 * _rms(y, g_ref[post:post + 1, :])


def _ffn1_kernel(h_ref, g_ref, wg_ref, wu_ref, wd_ref, out_ref):
    out_ref[...] = _ffn_body(h_ref[...], g_ref, 0, 1, wg_ref, wu_ref, wd_ref)


def _mix_ffn2_kernel(h_ref, o_ref, wo_ref, g_ref, wg_ref, wu_ref, wd_ref, out_ref):
    mixed = _dot(o_ref[...], wo_ref[...])
    h = h_ref[...] + _rms(mixed, g_ref[3:4, :])
    out_ref[...] = _ffn_body(h, g_ref, 4, 5, wg_ref, wu_ref, wd_ref)


def _layer_spec(w, layer):
    return pl.BlockSpec((None,) + w.shape[1:], lambda *_: (layer, 0, 0), pipeline_mode=pl.Buffered(1))


def _ffn1(h, gains, wg, wu, wd, layer):
    rows = h.shape[0]
    tm = _row_tile(rows, ROW_TILE)
    return pl.pallas_call(
        _ffn1_kernel,
        grid=(rows // tm,),
        in_specs=[
            pl.BlockSpec((tm, D_MODEL), lambda i: (i, 0)),
            _const_spec(gains.shape),
            _layer_spec(wg, layer), _layer_spec(wu, layer), _layer_spec(wd, layer),
        ],
        out_specs=pl.BlockSpec((tm, D_MODEL), lambda i: (i, 0)),
        out_shape=jax.ShapeDtypeStruct((rows, D_MODEL), F32),
        compiler_params=pltpu.CompilerParams(
            dimension_semantics=("arbitrary",), vmem_limit_bytes=VMEM_LIMIT),
        name="ffn1",
    )(h, gains, wg, wu, wd)


def _mix_ffn2(h, o, wo, gains, wg, wu, wd, layer):
    rows = h.shape[0]
    tm = _row_tile(rows, ROW_TILE)
    return pl.pallas_call(
        _mix_ffn2_kernel,
        grid=(rows // tm,),
        in_specs=[
            pl.BlockSpec((tm, D_MODEL), lambda i: (i, 0)),
            pl.BlockSpec((tm, o.shape[1]), lambda i: (i, 0)),
            _const_spec(wo.shape),
            _const_spec(gains.shape),
            _layer_spec(wg, layer), _layer_spec(wu, layer), _layer_spec(wd, layer),
        ],
        out_specs=pl.BlockSpec((tm, D_MODEL), lambda i: (i, 0)),
        out_shape=jax.ShapeDtypeStruct((rows, D_MODEL), F32),
        compiler_params=pltpu.CompilerParams(
            dimension_semantics=("arbitrary",), vmem_limit_bytes=VMEM_LIMIT),
        name="mix_ffn2",
    )(h, o, wo, gains, wg, wu, wd)


def _proj_dense_kernel(h_ref, g_ref, win_ref, qn_ref, wuq_ref, kvn_ref, wukv_ref,
                       gqa_ref, gqb_ref, gka_ref, gkb_ref, exp_ref, tab_ref, qt_ref, k_ref, vt_ref):
    a = _rms(h_ref[...], g_ref[2:3, :]).astype(BF16)
    proj = _dot(a, win_ref[...])
    tab = sum(_dot(tab_ref[i], exp_ref[...]) for i in range(tab_ref.shape[0]))
    cos_k, sin_k = tab[:, 0:LANE], tab[:, LANE:2 * LANE]
    cos_g, sin_g = tab[:, 2 * LANE:3 * LANE], tab[:, 3 * LANE:4 * LANE]
    qs = (MLA_NOPE + MLA_ROPE) ** -0.5 * LOG2E
    lane = lax.broadcasted_iota(jnp.int32, cos_k.shape, 1)
    cos_q = jnp.where(lane < MLA_NOPE, qs, cos_k * qs)
    sin_q = sin_k * qs

    cqn = _rms(proj[:, _C_CQ:_C_CQ + MLA_Q_LORA], qn_ref[...]).astype(BF16)
    qab = _dot(cqn, wuq_ref[...])
    nq = MLA_HEADS * LANE
    for h in range(MLA_HEADS):
        qa = qab[:, h * LANE:(h + 1) * LANE]
        qb = qab[:, nq + h * LANE:nq + (h + 1) * LANE]
        qt_ref[0, h * LANE:(h + 1) * LANE, :] = (qa * cos_q + qb * sin_q).T.astype(BF16)

    ckvn = _rms(proj[:, _C_CKV:_C_CKV + MLA_KV_LORA], kvn_ref[...]).astype(BF16)
    kv = _dot(ckvn, wukv_ref[...])
    k_rope = (proj[:, _C_KRA:_C_KRA + LANE] * cos_k + proj[:, _C_KRB:_C_KRB + LANE] * sin_k)
    for h in range(MLA_HEADS):
        k_ref[:, h * LANE:(h + 1) * LANE] = (kv[:, h * LANE:(h + 1) * LANE] + k_rope).astype(BF16)
    nv = MLA_HEADS * MLA_V

    low = lane < GQA_HEAD_DIM

    def normed_rotary(xa, xb, cos, sin):
        sq = xa * xa
        ss_lo = jnp.sum(jnp.where(low, sq, 0.0), axis=-1, keepdims=True)
        ss_hi = jnp.sum(jnp.where(low, 0.0, sq), axis=-1, keepdims=True)
        r = lax.rsqrt(jnp.where(low, ss_lo, ss_hi) * (1.0 / GQA_HEAD_DIM) + EPS)
        return (xa * cos + xb * sin) * r

    gq_scale = GQA_HEAD_DIM ** -0.5 * LOG2E
    cq_g = cos_g * (gqa_ref[...] * gq_scale)
    sq_g = sin_g * (gqb_ref[...] * gq_scale)
    per_kv = GQA_HEADS // GQA_KV_HEADS
    zeros_t = jnp.zeros((LANE - GQA_HEAD_DIM, qt_ref.shape[2]), F32)
    for j in range(GQA_HEADS // 2):
        xa = proj[:, _C_GQA + j * LANE:_C_GQA + (j + 1) * LANE]
        xb = proj[:, _C_GQB + j * LANE:_C_GQB + (j + 1) * LANE]
        y_t = normed_rotary(xa, xb, cq_g, sq_g).T
        for half in range(2):
            h = 2 * j + half
            q_t = y_t[half * GQA_HEAD_DIM:(half + 1) * GQA_HEAD_DIM]
            rows = [q_t, zeros_t] if h // per_kv == 0 else [zeros_t, q_t]
            qt_ref[0, nq + h * LANE:nq + (h + 1) * LANE, :] = jnp.concatenate(rows, axis=0).astype(BF16)
    xa = proj[:, _C_GKA:_C_GKA + LANE]
    xb = proj[:, _C_GKB:_C_GKB + LANE]
    k_ref[:, nq:nq + LANE] = normed_rotary(xa, xb, cos_g * gka_ref[...], sin_g * gkb_ref[...]).astype(BF16)
    v = jnp.concatenate([kv[:, nq:nq + nv], proj[:, _C_GV:_C_GV + LANE]], axis=1)
    vt_ref[0] = v.T.astype(BF16)


def _proj_dense(h, gains, w, tab, seq):
    rows = h.shape[0]
    tm = _row_tile(seq if seq else rows, ROW_TILE)
    nblk = (seq // tm) if seq else 1
    consts = [w["w_in"], w["q_norm"], w["w_uq"], w["kv_norm"], w["w_ukv"],
              w["gq_a"], w["gq_b"], w["gk_a"], w["gk_b"], w["rope_expand"]]
    return pl.pallas_call(
        _proj_dense_kernel,
        grid=(rows // tm,),
        in_specs=[pl.BlockSpec((tm, D_MODEL), lambda i: (i, 0)), _const_spec(gains.shape)]
        + [_const_spec(c.shape) for c in consts]
        + [pl.BlockSpec((tab.shape[0], tm, LANE), lambda i: (0, i % nblk, 0))],
        out_specs=[
            pl.BlockSpec((1, HEAD_SLOTS * LANE, tm), lambda i: (i, 0, 0)),
            pl.BlockSpec((tm, K_SLOTS * LANE), lambda i: (i, 0)),
            pl.BlockSpec((1, V_ROWS, tm), lambda i: (i, 0, 0)),
        ],
        out_shape=[
            jax.ShapeDtypeStruct((rows // tm, HEAD_SLOTS * LANE, tm), BF16),
            jax.ShapeDtypeStruct((rows, K_SLOTS * LANE), BF16),
            jax.ShapeDtypeStruct((rows // tm, V_ROWS, tm), BF16),
        ],
        compiler_params=pltpu.CompilerParams(
            dimension_semantics=("arbitrary",), vmem_limit_bytes=VMEM_LIMIT),
        name="proj_dense",
    )(h, gains, *consts, tab)


def _head_slots(h):
    if h < MLA_HEADS:
        return h, h
    kvh = (h - MLA_HEADS) // (GQA_HEADS // GQA_KV_HEADS)
    return MLA_HEADS, MLA_HEADS + kvh


def _sublane_bcast_max(x):
    return jnp.broadcast_to(jnp.max(x, axis=0, keepdims=True), x.shape)


def _dense_attn_kernel(qt_ref, k_ref, vt_ref, km_ref, vmt_ref, o_ref,
                       m_ref, smax_ref, acc_ref, s_ref, sm_ref):
    kv = pl.program_id(2)
    tq = qt_ref.shape[1]
    n_sub, _, tk = vt_ref.shape
    hd = MLA_V
    acc_rows = acc_ref.shape[1]

    def with_ones(vt):
        return jnp.concatenate([vt, jnp.ones((acc_rows - hd, vt.shape[1]), BF16)], axis=0)

    @pl.when(kv == 0)
    def _():
        for h in range(HEAD_SLOTS):
            ks, _ = _head_slots(h)
            qt = qt_ref[h * LANE:(h + 1) * LANE, :]
            sm_ref[h] = _dot(km_ref[0:N_META, ks * LANE:(ks + 1) * LANE], qt)
        zeros = jnp.zeros((LANE - N_META, tq), F32)
        for h in range(HEAD_SLOTS):
            _, vh = _head_slots(h)
            s3 = sm_ref[h].reshape(N_META // SUBLANE, SUBLANE, tq)
            m = _sublane_bcast_max(jnp.max(s3, axis=0))
            p3 = jnp.exp2(s3 - m[None])
            m_ref[h] = m
            p = jnp.concatenate([p3.reshape(N_META, tq), zeros], axis=0).astype(BF16)
            acc_ref[h] = _dot(with_ones(vmt_ref[vh * hd:(vh + 1) * hd, :]), p)

    n_slots = s_ref.shape[0]

    def scores(t, h):
        ks, _ = _head_slots(h)
        k0 = pl.multiple_of(t * tk, tk)
        k = k_ref[pl.ds(k0, tk), ks * LANE:(ks + 1) * LANE]
        s = _dot(k, qt_ref[h * LANE:(h + 1) * LANE, :])
        s_ref[h % n_slots] = s
        smax_ref[h] = jnp.max(s.reshape(tk // SUBLANE, SUBLANE, tq), axis=0)

    def softmax_pv(t, h):
        _, vh = _head_slots(h)
        m_prev = m_ref[h]
        m_new = jnp.maximum(m_prev, _sublane_bcast_max(smax_ref[h]))
        alpha = jnp.exp2(m_prev - m_new)
        s3 = s_ref[h % n_slots].reshape(tk // SUBLANE, SUBLANE, tq)
        p = jnp.exp2(s3 - m_new[None]).reshape(tk, tq).astype(BF16)
        pv = _dot(with_ones(vt_ref[t, vh * hd:(vh + 1) * hd, :]), p)
        acc = acc_ref[h].reshape(acc_rows // SUBLANE, SUBLANE, tq) * alpha[None]
        acc_ref[h] = acc.reshape(acc_rows, tq) + pv
        m_ref[h] = m_new

    for h in range(DENSE_LOOKAHEAD):
        scores(0, h)

    def sub_tile(t, carry):
        t_next = jnp.minimum(t + 1, n_sub - 1)
        for h in range(HEAD_SLOTS):
            ahead = h + DENSE_LOOKAHEAD
            if ahead < HEAD_SLOTS:
                scores(t, ahead)
            else:
                scores(t_next, ahead - HEAD_SLOTS)
            softmax_pv(t, h)
        return carry

    lax.fori_loop(0, n_sub, sub_tile, 0, unroll=2)

    @pl.when(kv == pl.num_programs(2) - 1)
    def _():
        for j in range(HEAD_SLOTS // 2):
            outs = []
            for h in (2 * j, 2 * j + 1):
                outs.append(acc_ref[h, 0:hd, :] / acc_ref[h, hd:hd + 1, :])
            o_t = jnp.concatenate(outs, axis=0)
            o_ref[:, j * LANE:(j + 1) * LANE] = o_t.T.astype(BF16)


def _dense_attn(qt, k, vt, km, vmt, *, n_seq, seq, q_base, meta_base):
    tq = qt.shape[2]
    nq = (qt.shape[0] - q_base) // n_seq
    tk = vt.shape[2]
    n_sub = _row_tile(seq // tk, DENSE_SUBTILES)
    nk = seq // (tk * n_sub)
    out_rows = n_seq * nq * tq
    return pl.pallas_call(
        _dense_attn_kernel,
        grid=(n_seq, nq, nk),
        in_specs=[
            pl.BlockSpec((None, HEAD_SLOTS * LANE, tq), lambda b, i, j: (q_base + b * nq + i, 0, 0)),
            pl.BlockSpec((n_sub * tk, K_SLOTS * LANE), lambda b, i, j: (b * nk + j, 0)),
            pl.BlockSpec((n_sub, V_ROWS, tk), lambda b, i, j: (b * nk + j, 0, 0)),
            pl.BlockSpec((None, LANE, K_SLOTS * LANE), lambda b, i, j: (meta_base + b, 0, 0)),
            pl.BlockSpec((None, V_ROWS, LANE), lambda b, i, j: (meta_base + b, 0, 0)),
        ],
        out_specs=pl.BlockSpec((tq, D_MODEL), lambda b, i, j: (b * nq + i, 0)),
        out_shape=jax.ShapeDtypeStruct((out_rows, D_MODEL), BF16),
        scratch_shapes=[
            pltpu.VMEM((HEAD_SLOTS, SUBLANE, tq), F32),
            pltpu.VMEM((HEAD_SLOTS, SUBLANE, tq), F32),
            pltpu.VMEM((HEAD_SLOTS, MLA_V + 2 * SUBLANE, tq), F32),
            pltpu.VMEM((DENSE_SLOTS, tk, tq), F32),
            pltpu.VMEM((HEAD_SLOTS, N_META, tq), F32),
        ],
        compiler_params=pltpu.CompilerParams(
            dimension_semantics=("arbitrary", "arbitrary", "arbitrary"),
            vmem_limit_bytes=VMEM_LIMIT),
        name="dense_attn",
    )(qt, k, vt, km, vmt)


def _proj_na_kernel(h_ref, g_ref, w_ref, q_ref, k_ref, v_ref, *, transpose_v):
    a = _rms(h_ref[...], g_ref[2:3, :]).astype(BF16)
    qkv = _dot(a, w_ref[...])
    n = NA_HEADS * NA_HEAD_DIM
    q_ref[...] = (qkv[:, 0:n] * (NA_HEAD_DIM ** -0.5 * LOG2E)).astype(BF16)
    k_ref[...] = qkv[:, n:2 * n].astype(BF16)
    v = qkv[:, 2 * n:3 * n]
    if transpose_v:
        vt = v.T.astype(BF16)
        for t in range(v_ref.shape[0]):
            v_ref[t] = vt[:, t * LANE:(t + 1) * LANE]
    else:
        v_ref[...] = v.astype(BF16)


def _proj_na(h, gains, w, transpose_v):
    rows = h.shape[0]
    tm = _row_tile(rows, ROW_TILE)
    n = NA_HEADS * NA_HEAD_DIM
    if transpose_v:
        v_spec = pl.BlockSpec((tm // LANE, n, LANE), lambda i: (i, 0, 0))
        v_shape = jax.ShapeDtypeStruct((rows // LANE, n, LANE), BF16)
    else:
        v_spec = pl.BlockSpec((tm, n), lambda i: (i, 0))
        v_shape = jax.ShapeDtypeStruct((rows, n), BF16)
    return pl.pallas_call(
        functools.partial(_proj_na_kernel, transpose_v=transpose_v),
        grid=(rows // tm,),
        in_specs=[pl.BlockSpec((tm, D_MODEL), lambda i: (i, 0)), _const_spec(gains.shape),
                  _const_spec(w.shape)],
        out_specs=[pl.BlockSpec((tm, n), lambda i: (i, 0))] * 2 + [v_spec],
        out_shape=[jax.ShapeDtypeStruct((rows, n), BF16)] * 2 + [v_shape],
        compiler_params=pltpu.CompilerParams(
            dimension_semantics=("arbitrary",), vmem_limit_bytes=VMEM_LIMIT),
        name="proj_na",
    )(h, gains, w)


NA_SPAN_R = NA_WIN_R + 2
NA_MASKED = 2 * NA_WIN_R - 1


def _na_kernel(q_ref, k_ref, vt_ref, km_ref, vmt_ref, bias_ref, mb_ref, o_ref, s_ref,
               *, rows, rows_per_step):
    step = pl.program_id(1)
    n_pairs = rows_per_step // 2
    n_hp = NA_HEADS // 2
    span = NA_SPAN_R * GRID_W
    lane = lax.broadcasted_iota(jnp.int32, (GRID_W, LANE), 1)
    first = lane < (LANE // 2)
    zeros_m = jnp.zeros((LANE - N_META, 2 * LANE), F32)
    ones_v = jnp.ones((2 * SUBLANE, span + LANE), BF16)

    def geometry(rp):
        ra = step * rows_per_step + 2 * rp
        rs = [jnp.clip(ra + x - NA_WIN_R // 2, 0, rows - NA_WIN_R) for x in range(2)]
        ws = jnp.minimum((rs[0] // 2) * 2, rows - NA_SPAN_R)
        return ra, rs, ws

    n_slots = s_ref.shape[0]

    def scores(rp, hp):
        slot = hp % n_slots
        _, _, ws = geometry(rp)
        cols = slice(hp * LANE, (hp + 1) * LANE)
        parts = []
        for x in range(2):
            q0 = pl.multiple_of((2 * rp + x) * GRID_W, GRID_W)
            qx = q_ref[pl.ds(q0, GRID_W), cols]
            parts += [jnp.where(first, qx, jnp.zeros_like(qx)), jnp.where(first, jnp.zeros_like(qx), qx)]
        qblk = jnp.concatenate(parts, axis=0)
        k0 = pl.multiple_of(ws * GRID_W, 2 * GRID_W)
        s_ref[slot, 0:span, :] = _dot_nt(k_ref[pl.ds(k0, span), cols], qblk)
        s_ref[slot, span:span + N_META, :] = _dot_nt(km_ref[:, cols], qblk)

    for hp in range(NA_LOOKAHEAD):
        scores(0, hp)

    def row_pair(rp, carry):
        ra, rs, ws = geometry(rp)
        idx = []
        for jj in range(NA_SPAN_R):
            kr = ws + jj
            idx.append([jnp.where((kr >= rs[x]) & (kr < rs[x] + NA_WIN_R),
                                  kr - (ra + x) + NA_WIN_R - 1, NA_MASKED) for x in range(2)])
        t0 = ws // 2
        for hp in range(n_hp):
            slot = hp % n_slots
            ahead = hp + NA_LOOKAHEAD
            if ahead < n_hp:
                scores(rp, ahead)
            else:
                scores(jnp.minimum(rp + 1, n_pairs - 1), ahead - n_hp)
            cols = slice(hp * LANE, (hp + 1) * LANE)
            b = jnp.concatenate(
                [jnp.concatenate([bias_ref[hp, idx[jj][0]], bias_ref[hp, idx[jj][1]]], axis=1)
                 for jj in range(NA_SPAN_R)], axis=0)
            s = s_ref[slot, 0:span, :] + b
            mb = mb_ref[hp]
            sm = s_ref[slot, span:span + N_META, :] + jnp.concatenate([mb, mb], axis=1)
            s3 = s.reshape(span // SUBLANE, SUBLANE, 2 * LANE)
            sm3 = sm.reshape(N_META // SUBLANE, SUBLANE, 2 * LANE)
            m = _sublane_bcast_max(jnp.maximum(jnp.max(s3, axis=0), jnp.max(sm3, axis=0)))
            p = jnp.exp2(s3 - m[None]).reshape(span, 2 * LANE).astype(BF16)
            pm3 = jnp.exp2(sm3 - m[None])
            pm = jnp.concatenate([pm3.reshape(N_META, 2 * LANE), zeros_m], axis=0).astype(BF16)
            v_all = jnp.concatenate([vt_ref[t0 + t, cols, :] for t in range(span // LANE)]
                                    + [vmt_ref[cols, :]], axis=1)
            o_t = _dot(jnp.concatenate([v_all, ones_v], axis=0),
                       jnp.concatenate([p, pm], axis=0))
            o_t = o_t[0:LANE] / o_t[LANE:LANE + 1]
            for x in range(2):
                blk = o_t[:, x * LANE:(x + 1) * LANE].T
                q0 = pl.multiple_of((2 * rp + x) * GRID_W, GRID_W)
                o_ref[pl.ds(q0, GRID_W), cols] = jnp.where(
                    first, blk[0:GRID_W], blk[GRID_W:2 * GRID_W]).astype(BF16)
        return carry

    lax.fori_loop(0, n_pairs, row_pair, 0, unroll=4)


def _na_attn(q, k, vt, km, vmt, bias, mb, *, n_seq, seq, meta_base):
    rows = seq // GRID_W
    assert rows >= NA_SPAN_R and rows % 2 == 0
    rps = _row_tile(rows, NA_ROWS_PER_STEP)
    nsteps = rows // rps
    n = NA_HEADS * NA_HEAD_DIM
    span = NA_SPAN_R * GRID_W
    return pl.pallas_call(
        functools.partial(_na_kernel, rows=rows, rows_per_step=rps),
        grid=(n_seq, nsteps),
        in_specs=[
            pl.BlockSpec((rps * GRID_W, n), lambda b, i: (b * nsteps + i, 0)),
            pl.BlockSpec((seq, n), lambda b, i: (b, 0), pipeline_mode=pl.Buffered(1)),
            pl.BlockSpec((seq // LANE, n, LANE), lambda b, i: (b, 0, 0), pipeline_mode=pl.Buffered(1)),
            pl.BlockSpec((None, N_META, n), lambda b, i: (meta_base + b, 0, 0)),
            pl.BlockSpec((None, n, LANE), lambda b, i: (meta_base + b, 0, 0)),
            _const_spec(bias.shape),
            _const_spec(mb.shape),
        ],
        out_specs=pl.BlockSpec((rps * GRID_W, n), lambda b, i: (b * nsteps + i, 0)),
        out_shape=jax.ShapeDtypeStruct((n_seq * seq, n), BF16),
        scratch_shapes=[pltpu.VMEM((NA_SLOTS, span + N_META, 2 * LANE), F32)],
        compiler_params=pltpu.CompilerParams(
            dimension_semantics=("arbitrary", "arbitrary"), vmem_limit_bytes=VMEM_LIMIT),
        name="na_attn",
    )(q, k, vt, km, vmt, bias, mb)


def _na_meta_kernel(q_ref, km_ref, vm_ref, mb_ref, o_ref):
    lane = lax.broadcasted_iota(jnp.int32, (N_META, LANE), 1)
    first = lane < (LANE // 2)
    for j in range(NA_HEADS // 2):
        cols = slice(j * LANE, (j + 1) * LANE)
        qp = q_ref[:, cols]
        km = km_ref[:, cols]
        vm = vm_ref[:, cols]
        outs = []
        for half in range(2):
            h = 2 * j + half
            qh = jnp.where(first if half == 0 else jnp.logical_not(first), qp, jnp.zeros_like(qp))
            sm = _dot_nt(qh, km)
            sm = jnp.where(lane < N_META, sm + mb_ref[h:h + 1, :], NEG_INF)
            m = jnp.max(sm, axis=-1, keepdims=True)
            pm = jnp.exp2(sm - m)
            l = jnp.sum(pm, axis=-1, keepdims=True)
            outs.append(_dot(pm.astype(BF16), vm) / l)
        o_ref[:, cols] = jnp.where(first, outs[0], outs[1]).astype(BF16)


def _na_meta(qm, km, vm, mb):
    n_seq = km.shape[0]
    n = NA_HEADS * NA_HEAD_DIM
    return pl.pallas_call(
        _na_meta_kernel,
        grid=(n_seq,),
        in_specs=[
            pl.BlockSpec((N_META, n), lambda b: (b, 0)),
            pl.BlockSpec((None, LANE, n), lambda b: (b, 0, 0)),
            pl.BlockSpec((None, LANE, n), lambda b: (b, 0, 0)),
            _const_spec(mb.shape),
        ],
        out_specs=pl.BlockSpec((N_META, n), lambda b: (b, 0)),
        out_shape=jax.ShapeDtypeStruct((n_seq * N_META, n), BF16),
        compiler_params=pltpu.CompilerParams(dimension_semantics=("arbitrary",)),
        name="na_meta",
    )(qm, km, vm, mb)


def _take_cols(w, idx):
    idx = np.asarray(idx)
    neg = idx < 0
    same_run = np.where(neg[1:] | neg[:-1], neg[1:] & neg[:-1], np.diff(idx) == 1)
    breaks = np.flatnonzero(~same_run) + 1
    parts = []
    for run in np.split(idx, breaks):
        if run[0] < 0:
            parts.append(jnp.zeros((w.shape[0], len(run)), w.dtype))
        else:
            parts.append(w[:, int(run[0]):int(run[-1]) + 1])
    return jnp.concatenate(parts, axis=1)


def _swap_halves(n):
    half = n // 2
    return np.concatenate([np.arange(half, n), np.arange(0, half)])


def _dense_weights(w_in, q_norm, w_uq, kv_norm, w_ukv, gq_norm, gk_norm, w_out):
    pad = lambda k: -np.ones(k, np.int64)
    o_kr = MLA_Q_LORA + MLA_KV_LORA
    o_gq = o_kr + MLA_ROPE
    o_gk = o_gq + GQA_HEADS * GQA_HEAD_DIM
    o_gv = o_gk + GQA_KV_HEADS * GQA_HEAD_DIM
    axial = np.concatenate([_swap_halves(GQA_HEAD_DIM // 2),
                            GQA_HEAD_DIM // 2 + _swap_halves(GQA_HEAD_DIM // 2)])
    idx = [np.arange(0, o_kr)]
    idx += [pad(MLA_NOPE), o_kr + np.arange(MLA_ROPE), pad(LANE - MLA_NOPE - MLA_ROPE)]
    idx += [pad(MLA_NOPE), o_kr + _swap_halves(MLA_ROPE), pad(LANE - MLA_NOPE - MLA_ROPE)]
    assert 2 * GQA_HEAD_DIM == LANE and GQA_KV_HEADS == 2
    for h in range(GQA_HEADS):
        idx += [o_gq + h * GQA_HEAD_DIM + np.arange(GQA_HEAD_DIM)]
    for h in range(GQA_HEADS):
        idx += [o_gq + h * GQA_HEAD_DIM + axial]
    for h in range(GQA_KV_HEADS):
        idx += [o_gk + h * GQA_HEAD_DIM + np.arange(GQA_HEAD_DIM)]
    for h in range(GQA_KV_HEADS):
        idx += [o_gk + h * GQA_HEAD_DIM + axial]
    idx += [o_gv + np.arange(GQA_KV_HEADS * GQA_HEAD_DIM)]
    idx = np.concatenate(idx)
    assert idx.shape[0] == _C_END
    w_in2 = _take_cols(w_in.astype(BF16), idx)

    hd = MLA_NOPE + MLA_ROPE
    ia, ib = [], []
    for h in range(MLA_HEADS):
        ia += [h * hd + np.arange(hd), pad(LANE - hd)]
        ib += [pad(MLA_NOPE), h * hd + MLA_NOPE + _swap_halves(MLA_ROPE), pad(LANE - hd)]
    w_uq2 = _take_cols(w_uq.astype(BF16), np.concatenate(ia + ib))

    kvd = MLA_NOPE + MLA_V
    ik, iv = [], []
    for h in range(MLA_HEADS):
        ik += [h * kvd + np.arange(MLA_NOPE), pad(LANE - MLA_NOPE)]
        iv += [h * kvd + MLA_NOPE + np.arange(MLA_V)]
    w_ukv2 = _take_cols(w_ukv.astype(BF16), np.concatenate(ik + iv))

    def gain_pair(g):
        ga = jnp.concatenate([g, g])[None, :]
        gb = jnp.tile(g[jnp.asarray(axial)], 2)[None, :]
        return ga, gb

    gq_a, gq_b = gain_pair(gq_norm)
    gk_a, gk_b = gain_pair(gk_norm)

    w_out2 = w_out.astype(BF16)

    return dict(w_in=w_in2, q_norm=q_norm[None, :], w_uq=w_uq2, kv_norm=kv_norm[None, :],
                w_ukv=w_ukv2, gq_a=gq_a, gq_b=gq_b, gk_a=gk_a, gk_b=gk_b,
                rope_expand=_rope_expand_matrix()), w_out2


def _rope_tables(pos, row, row_repeat, col, col_tile):
    half = MLA_ROPE // 2
    inv = 1.0 / (ROPE_THETA ** (jnp.arange(half, dtype=F32) / half))
    n = pos.shape[0]

    def cs(p):
        ang = p.astype(F32)[None, :] * inv[:, None]
        return jnp.cos(ang), jnp.sin(ang)

    by_row = tuple(jnp.repeat(x, row_repeat, axis=1) for x in cs(row))
    by_col = tuple(jnp.tile(x, (1, col_tile)) for x in cs(col))
    vals = jnp.concatenate(cs(pos) + by_row + by_col + (jnp.zeros((LANE - 6 * half, n), F32),), axis=0).T
    hi = vals.astype(BF16)
    rest = vals - hi.astype(F32)
    mid = rest.astype(BF16)
    lo = (rest - mid.astype(F32)).astype(BF16)
    return jnp.stack([hi, mid, lo])


def _rope_expand_matrix():
    half = MLA_ROPE // 2
    c1, s1, cr, sr, cc, sc = range(6)
    tail = LANE - MLA_NOPE - MLA_ROPE
    blank = [(None, 0)]
    layout = (blank * (MLA_NOPE // half) + [(c1, 1), (c1, 1)] + blank * (tail // half)
              + blank * (MLA_NOPE // half) + [(s1, -1), (s1, 1)] + blank * (tail // half)
              + [(cr, 1), (cr, 1), (cc, 1), (cc, 1)] * (LANE // GQA_HEAD_DIM)
              + [(sr, -1), (sr, 1), (sc, -1), (sc, 1)] * (LANE // GQA_HEAD_DIM))
    expand = np.zeros((LANE, len(layout) * half), np.float32)
    for blk, (src, sign) in enumerate(layout):
        if src is not None:
            expand[src * half + np.arange(half), blk * half + np.arange(half)] = sign
    return jnp.asarray(expand, BF16)


def _na_bias_tables(rpb, meta_bias):
    c_idx = np.arange(GRID_W)
    c_start = np.clip(c_idx - NA_WIN_C // 2, 0, GRID_W - NA_WIN_C)
    col_mask = (c_idx[None, :] >= c_start[:, None]) & (c_idx[None, :] < c_start[:, None] + NA_WIN_C)
    col_off = np.clip(c_idx[None, :] - c_idx[:, None] + NA_WIN_C - 1, 0, 2 * NA_WIN_C - 2)
    hp = NA_HEADS // 2
    n_off = 2 * NA_WIN_C - 1
    select = np.zeros((2, n_off, GRID_W, 2, GRID_W), np.float32)
    kc_g, c_g = np.meshgrid(c_idx, c_idx, indexing="ij")
    for half in range(2):
        select[half, col_off[c_g, kc_g], kc_g, half, c_g] = 1.0
    rows = rpb.reshape(hp, 2, NA_MASKED, n_off).transpose(0, 2, 1, 3).reshape(hp * NA_MASKED, 2 * n_off)
    t = jnp.dot(rows, jnp.asarray(select.reshape(2 * n_off, GRID_W * LANE)),
                precision=lax.Precision.HIGHEST) * LOG2E
    t = t.reshape(hp, NA_MASKED, GRID_W, LANE)
    keep = np.tile(col_mask.T, (1, 2))
    t = jnp.where(jnp.asarray(keep)[None, None], t, NEG_INF)
    bias = jnp.concatenate([t, jnp.full_like(t[:, :1], NEG_INF)], axis=1)
    mbl = meta_bias * LOG2E
    mb_t = jnp.repeat(mbl.reshape(hp, 2, N_META).transpose(0, 2, 1), GRID_W, axis=2)
    mb = jnp.pad(mbl, ((0, 0), (0, LANE - N_META)))
    return bias, mb_t, mb


def _pad_meta(x, n_seq):
    c = x.shape[1]
    return jnp.pad(x.reshape(n_seq, N_META, c), ((0, 0), (0, LANE - N_META), (0, 0)))


def kernel(x_prompt, x_sample, meta, norm_gains, ffn1_w_gate, ffn1_w_up, ffn1_w_down, ffn2_w_gate, ffn2_w_up, ffn2_w_down, attn_w_in, mla_q_norm, mla_w_uq, mla_kv_norm, mla_w_ukv, gqa_q_norm, gqa_k_norm, attn_w_out, na_w_qkv, na_rpb, na_meta_bias, na_w_out):
    bp, sp, _ = x_prompt.shape
    bs, ss, _ = x_sample.shape
    n_seq = bp + bs
    depth = norm_gains.shape[0]
    groups = [(bp, sp, 0), (bs, ss, bp)]

    n_meta = n_seq * N_META
    meta_rows = -(-n_meta // LANE) * LANE
    pad_rows = lambda x: jnp.pad(x, ((0, meta_rows - x.shape[0]), (0, 0)))
    h_tok = [x_prompt.reshape(bp * sp, D_MODEL), x_sample.reshape(bs * ss, D_MODEL)]
    h_meta = pad_rows(jnp.tile(meta.astype(F32), (n_seq, 1)))

    smax = max(sp, ss)
    tab_tok = _rope_tables(jnp.arange(smax) + N_META, jnp.arange(smax // GRID_W), GRID_W,
                           jnp.arange(GRID_W), smax // GRID_W)
    tab_meta = _rope_tables(jnp.arange(meta_rows) % N_META, jnp.full((1,), -1), meta_rows,
                            jnp.arange(N_META), meta_rows // N_META)

    w1 = (ffn1_w_gate.astype(BF16), ffn1_w_up.astype(BF16), ffn1_w_down.astype(BF16))
    w2 = (ffn2_w_gate.astype(BF16), ffn2_w_up.astype(BF16), ffn2_w_down.astype(BF16))

    for i in range(depth):
        gains = jnp.pad(norm_gains[i], ((0, 2), (0, 0)))
        j = i // 2
        h_tok = [_ffn1(h, gains, *w1, i) for h in h_tok]
        h_meta = _ffn1(h_meta, gains, *w1, i)
        if i % 2 == 0:
            w, w_out = _dense_weights(attn_w_in[j], mla_q_norm[j], mla_w_uq[j], mla_kv_norm[j],
                                      mla_w_ukv[j], gqa_q_norm[j], gqa_k_norm[j], attn_w_out[j])
            qkv_tok = [_proj_dense(h, gains, w, tab_tok, s) for h, (_, s, _) in zip(h_tok, groups)]
            qmt, km, vmt = _proj_dense(h_meta, gains, w, tab_meta, 0)
            kmp = _pad_meta(km[:n_meta], n_seq)
            vmt = vmt.transpose(1, 0, 2).reshape(V_ROWS, meta_rows)
            vmtp = vmt[:, :n_meta].reshape(V_ROWS, n_seq, N_META).transpose(1, 0, 2)
            vmtp = jnp.pad(vmtp, ((0, 0), (0, 0), (0, LANE - N_META)))
            qmt = qmt.transpose(1, 0, 2).reshape(HEAD_SLOTS * LANE, meta_rows)
            qmtp = qmt[:, :n_meta].reshape(HEAD_SLOTS * LANE, n_seq, N_META).transpose(1, 0, 2)
            qmtp = jnp.pad(qmtp, ((0, 0), (0, 0), (0, LANE - N_META)))
            o_tok, o_meta = [], []
            for (qt, k, vt), (nb, s, b0) in zip(qkv_tok, groups):
                o_tok.append(_dense_attn(qt, k, vt, kmp, vmtp, n_seq=nb, seq=s, q_base=0, meta_base=b0))
                om = _dense_attn(qmtp, k, vt, kmp, vmtp, n_seq=nb, seq=s, q_base=b0, meta_base=b0)
                o_meta.append(om.reshape(nb, LANE, D_MODEL)[:, :N_META].reshape(nb * N_META, D_MODEL))
            o_meta = pad_rows(jnp.concatenate(o_meta, axis=0))
        else:
            w_qkv = na_w_qkv[j].astype(BF16)
            w_out = na_w_out[j].astype(BF16)
            bias, mb_t, mb = _na_bias_tables(na_rpb[j], na_meta_bias[j])
            qkv_tok = [_proj_na(h, gains, w_qkv, True) for h in h_tok]
            qm, km, vm = _proj_na(h_meta, gains, w_qkv, False)
            kmp, vmp = _pad_meta(km[:n_meta], n_seq), _pad_meta(vm[:n_meta], n_seq)
            km16 = km[:n_meta].reshape(n_seq, N_META, NA_HEADS * NA_HEAD_DIM)
            vmtp = vmp.transpose(0, 2, 1)
            o_tok = [_na_attn(q, k, vt, km16, vmtp, bias, mb_t, n_seq=nb, seq=s, meta_base=b0)
                     for (q, k, vt), (nb, s, b0) in zip(qkv_tok, groups)]
            o_meta = pad_rows(_na_meta(qm[:n_meta], kmp, vmp, mb))
        h_tok = [_mix_ffn2(h, o, w_out, gains, *w2, i) for h, o in zip(h_tok, o_tok)]
        h_meta = _mix_ffn2(h_meta, o_meta, w_out, gains, *w2, i)

    return (h_tok[0].reshape(bp, sp, D_MODEL), h_tok[1].reshape(bs, ss, D_MODEL))
```

```python
import functools
import math

import jax
import jax.numpy as jnp
import numpy as np
from jax import lax
from jax.experimental import pallas as pl
from jax.experimental.pallas import tpu as pltpu

F32 = jnp.float32
BF16 = jnp.bfloat16

D_MODEL = 1024
N_META = 16
GRID_W = 64
D_FF = 2816
EPS = 1e-6
NEG_INF = -1e30
LOG2E = math.log2(math.e)

MLA_HEADS = 8
MLA_Q_LORA = 256
MLA_KV_LORA = 128
MLA_NOPE = 64
MLA_ROPE = 32
MLA_V = 64
GQA_HEADS = 8
GQA_KV_HEADS = 2
GQA_HEAD_DIM = 64
ROPE_THETA = 10000.0
NA_HEADS = 16
NA_HEAD_DIM = 64
NA_WIN_R = 8
NA_WIN_C = 16

LANE = 128
HEAD_SLOTS = MLA_HEADS + GQA_HEADS
K_SLOTS = MLA_HEADS + 1
V_ROWS = (MLA_HEADS + GQA_KV_HEADS) * MLA_V
SUBLANE = 8
DENSE_LOOKAHEAD, DENSE_SLOTS = 2, 4
NA_LOOKAHEAD, NA_SLOTS = 2, 4
VMEM_LIMIT = 56 * 1024 * 1024
ROW_TILE = 512
DENSE_SUBTILES = 8
NA_ROWS_PER_STEP = 16

_C_CQ = 0
_C_CKV = _C_CQ + MLA_Q_LORA
_C_KRA = _C_CKV + MLA_KV_LORA
_C_KRB = _C_KRA + LANE
_C_GQA = _C_KRB + LANE
_C_GQB = _C_GQA + GQA_HEADS // 2 * LANE
_C_GKA = _C_GQB + GQA_HEADS // 2 * LANE
_C_GKB = _C_GKA + LANE
_C_GV = _C_GKB + LANE
_C_END = _C_GV + LANE


def _const_spec(shape):
    nd = len(shape)
    return pl.BlockSpec(shape, lambda *_: (0,) * nd, pipeline_mode=pl.Buffered(1))


def _rms(x, g):
    ms = jnp.mean(x * x, axis=-1, keepdims=True)
    return x * lax.rsqrt(ms + EPS) * g


def _dot(a, b):
    return jnp.dot(a, b, preferred_element_type=F32)


def _dot_nt(a, b):
    return lax.dot_general(a, b, (((1,), (1,)), ((), ())), preferred_element_type=F32)


def _row_tile(rows, want):
    t = min(rows, want)
    while rows % t:
        t //= 2
    return t


def _ffn_body(h, g_ref, pre, post, wg_ref, wu_ref, wd_ref):
    xn = _rms(h, g_ref[pre:pre + 1, :]).astype(BF16)
    gate = _dot(xn, wg_ref[...])
    up = _dot(xn, wu_ref[...])
    act = (gate * jax.nn.sigmoid(gate) * up).astype(BF16)
    y = _dot(act, wd_ref[...])
    return h + 0.5 * _rms(y, g_ref[post:post + 1, :])


def _ffn1_kernel(h_ref, g_ref, wg_ref, wu_ref, wd_ref, out_ref):
    out_ref[...] = _ffn_body(h_ref[...], g_ref, 0, 1, wg_ref, wu_ref, wd_ref)


def _mix_ffn2_kernel(h_ref, o_ref, wo_ref, g_ref, wg_ref, wu_ref, wd_ref, out_ref):
    mixed = _dot(o_ref[...], wo_ref[...])
    h = h_ref[...] + _rms(mixed, g_ref[3:4, :])
    out_ref[...] = _ffn_body(h, g_ref, 4, 5, wg_ref, wu_ref, wd_ref)


def _layer_spec(w, layer):
    return pl.BlockSpec((None,) + w.shape[1:], lambda *_: (layer, 0, 0), pipeline_mode=pl.Buffered(1))


def _ffn1(h, gains, wg, wu, wd, layer):
    rows = h.shape[0]
    tm = _row_tile(rows, ROW_TILE)
    return pl.pallas_call(
        _ffn1_kernel,
        grid=(rows // tm,),
        in_specs=[
            pl.BlockSpec((tm, D_MODEL), lambda i: (i, 0)),
            _const_spec(gains.shape),
            _layer_spec(wg, layer), _layer_spec(wu, layer), _layer_spec(wd, layer),
        ],
        out_specs=pl.BlockSpec((tm, D_MODEL), lambda i: (i, 0)),
        out_shape=jax.ShapeDtypeStruct((rows, D_MODEL), F32),
        compiler_params=pltpu.CompilerParams(
            dimension_semantics=("arbitrary",), vmem_limit_bytes=VMEM_LIMIT),
        name="ffn1",
    )(h, gains, wg, wu, wd)


def _mix_ffn2(h, o, wo, gains, wg, wu, wd, layer):
    rows = h.shape[0]
    tm = _row_tile(rows, ROW_TILE)
    return pl.pallas_call(
        _mix_ffn2_kernel,
        grid=(rows // tm,),
        in_specs=[
            pl.BlockSpec((tm, D_MODEL), lambda i: (i, 0)),
            pl.BlockSpec((tm, o.shape[1]), lambda i: (i, 0)),
            _const_spec(wo.shape),
            _const_spec(gains.shape),
            _layer_spec(wg, layer), _layer_spec(wu, layer), _layer_spec(wd, layer),
        ],
        out_specs=pl.BlockSpec((tm, D_MODEL), lambda i: (i, 0)),
        out_shape=jax.ShapeDtypeStruct((rows, D_MODEL), F32),
        compiler_params=pltpu.CompilerParams(
            dimension_semantics=("arbitrary",), vmem_limit_bytes=VMEM_LIMIT),
        name="mix_ffn2",
    )(h, o, wo, gains, wg, wu, wd)


def _proj_dense_kernel(h_ref, g_ref, win_ref, qn_ref, wuq_ref, kvn_ref, wukv_ref,
                       gqa_ref, gqb_ref, gka_ref, gkb_ref, exp_ref, tab_ref, qt_ref, k_ref, vt_ref):
    a = _rms(h_ref[...], g_ref[2:3, :]).astype(BF16)
    proj = _dot(a, win_ref[...])
    tab = sum(_dot(tab_ref[i], exp_ref[...]) for i in range(tab_ref.shape[0]))
    cos_k, sin_k = tab[:, 0:LANE], tab[:, LANE:2 * LANE]
    cos_g, sin_g = tab[:, 2 * LANE:3 * LANE], tab[:, 3 * LANE:4 * LANE]
    qs = (MLA_NOPE + MLA_ROPE) ** -0.5 * LOG2E
    lane = lax.broadcasted_iota(jnp.int32, cos_k.shape, 1)
    cos_q = jnp.where(lane < MLA_NOPE, qs, cos_k * qs)
    sin_q = sin_k * qs

    cqn = _rms(proj[:, _C_CQ:_C_CQ + MLA_Q_LORA], qn_ref[...]).astype(BF16)
    qab = _dot(cqn, wuq_ref[...])
    nq = MLA_HEADS * LANE
    for h in range(MLA_HEADS):
        qa = qab[:, h * LANE:(h + 1) * LANE]
        qb = qab[:, nq + h * LANE:nq + (h + 1) * LANE]
        qt_ref[0, h * LANE:(h + 1) * LANE, :] = (qa * cos_q + qb * sin_q).T.astype(BF16)

    ckvn = _rms(proj[:, _C_CKV:_C_CKV + MLA_KV_LORA], kvn_ref[...]).astype(BF16)
    kv = _dot(ckvn, wukv_ref[...])
    k_rope = (proj[:, _C_KRA:_C_KRA + LANE] * cos_k + proj[:, _C_KRB:_C_KRB + LANE] * sin_k)
    for h in range(MLA_HEADS):
        k_ref[:, h * LANE:(h + 1) * LANE] = (kv[:, h * LANE:(h + 1) * LANE] + k_rope).astype(BF16)
    nv = MLA_HEADS * MLA_V

    low = lane < GQA_HEAD_DIM

    def normed_rotary(xa, xb, cos, sin):
        sq = xa * xa
        ss_lo = jnp.sum(jnp.where(low, sq, 0.0), axis=-1, keepdims=True)
        ss_hi = jnp.sum(jnp.where(low, 0.0, sq), axis=-1, keepdims=True)
        r = lax.rsqrt(jnp.where(low, ss_lo, ss_hi) * (1.0 / GQA_HEAD_DIM) + EPS)
        return (xa * cos + xb * sin) * r

    gq_scale = GQA_HEAD_DIM ** -0.5 * LOG2E
    cq_g = cos_g * (gqa_ref[...] * gq_scale)
    sq_g = sin_g * (gqb_ref[...] * gq_scale)
    per_kv = GQA_HEADS // GQA_KV_HEADS
    zeros_t = jnp.zeros((LANE - GQA_HEAD_DIM, qt_ref.shape[2]), F32)
    for j in range(GQA_HEADS // 2):
        xa = proj[:, _C_GQA + j * LANE:_C_GQA + (j + 1) * LANE]
        xb = proj[:, _C_GQB + j * LANE:_C_GQB + (j + 1) * LANE]
        y_t = normed_rotary(xa, xb, cq_g, sq_g).T
        for half in range(2):
            h = 2 * j + half
            q_t = y_t[half * GQA_HEAD_DIM:(half + 1) * GQA_HEAD_DIM]
            rows = [q_t, zeros_t] if h // per_kv == 0 else [zeros_t, q_t]
            qt_ref[0, nq + h * LANE:nq + (h + 1) * LANE, :] = jnp.concatenate(rows, axis=0).astype(BF16)
    xa = proj[:, _C_GKA:_C_GKA + LANE]
    xb = proj[:, _C_GKB:_C_GKB + LANE]
    k_ref[:, nq:nq + LANE] = normed_rotary(xa, xb, cos_g * gka_ref[...], sin_g * gkb_ref[...]).astype(BF16)
    v = jnp.concatenate([kv[:, nq:nq + nv], proj[:, _C_GV:_C_GV + LANE]], axis=1)
    vt_ref[0] = v.T.astype(BF16)


def _proj_dense(h, gains, w, tab, seq):
    rows = h.shape[0]
    tm = _row_tile(seq if seq else rows, ROW_TILE)
    nblk = (seq // tm) if seq else 1
    consts = [w["w_in"], w["q_norm"], w["w_uq"], w["kv_norm"], w["w_ukv"],
              w["gq_a"], w["gq_b"], w["gk_a"], w["gk_b"], w["rope_expand"]]
    return pl.pallas_call(
        _proj_dense_kernel,
        grid=(rows // tm,),
        in_specs=[pl.BlockSpec((tm, D_MODEL), lambda i: (i, 0)), _const_spec(gains.shape)]
        + [_const_spec(c.shape) for c in consts]
        + [pl.BlockSpec((tab.shape[0], tm, LANE), lambda i: (0, i % nblk, 0))],
        out_specs=[
            pl.BlockSpec((1, HEAD_SLOTS * LANE, tm), lambda i: (i, 0, 0)),
            pl.BlockSpec((tm, K_SLOTS * LANE), lambda i: (i, 0)),
            pl.BlockSpec((1, V_ROWS, tm), lambda i: (i, 0, 0)),
        ],
        out_shape=[
            jax.ShapeDtypeStruct((rows // tm, HEAD_SLOTS * LANE, tm), BF16),
            jax.ShapeDtypeStruct((rows, K_SLOTS * LANE), BF16),
            jax.ShapeDtypeStruct((rows // tm, V_ROWS, tm), BF16),
        ],
        compiler_params=pltpu.CompilerParams(
            dimension_semantics=("arbitrary",), vmem_limit_bytes=VMEM_LIMIT),
        name="proj_dense",
    )(h, gains, *consts, tab)


def _head_slots(h):
    if h < MLA_HEADS:
        return h, h
    kvh = (h - MLA_HEADS) // (GQA_HEADS // GQA_KV_HEADS)
    return MLA_HEADS, MLA_HEADS + kvh


def _sublane_bcast_max(x):
    return jnp.broadcast_to(jnp.max(x, axis=0, keepdims=True), x.shape)


def _dense_attn_kernel(qt_ref, k_ref, vt_ref, km_ref, vmt_ref, o_ref,
                       m_ref, smax_ref, acc_ref, s_ref, sm_ref):
    kv = pl.program_id(2)
    tq = qt_ref.shape[1]
    n_sub, _, tk = vt_ref.shape
    hd = MLA_V
    acc_rows = acc_ref.shape[1]

    def with_ones(vt):
        return jnp.concatenate([vt, jnp.ones((acc_rows - hd, vt.shape[1]), BF16)], axis=0)

    @pl.when(kv == 0)
    def _():
        for h in range(HEAD_SLOTS):
            ks, _ = _head_slots(h)
            qt = qt_ref[h * LANE:(h + 1) * LANE, :]
            sm_ref[h] = _dot(km_ref[0:N_META, ks * LANE:(ks + 1) * LANE], qt)
        zeros = jnp.zeros((LANE - N_META, tq), F32)
        for h in range(HEAD_SLOTS):
            _, vh = _head_slots(h)
            s3 = sm_ref[h].reshape(N_META // SUBLANE, SUBLANE, tq)
            m = _sublane_bcast_max(jnp.max(s3, axis=0))
            p3 = jnp.exp2(s3 - m[None])
            m_ref[h] = m
            p = jnp.concatenate([p3.reshape(N_META, tq), zeros], axis=0).astype(BF16)
            acc_ref[h] = _dot(with_ones(vmt_ref[vh * hd:(vh + 1) * hd, :]), p)

    n_slots = s_ref.shape[0]

    def scores(t, h):
        ks, _ = _head_slots(h)
        k0 = pl.multiple_of(t * tk, tk)
        k = k_ref[pl.ds(k0, tk), ks * LANE:(ks + 1) * LANE]
        s = _dot(k, qt_ref[h * LANE:(h + 1) * LANE, :])
        s_ref[h % n_slots] = s
        smax_ref[h] = jnp.max(s.reshape(tk // SUBLANE, SUBLANE, tq), axis=0)

    def softmax_pv(t, h):
        _, vh = _head_slots(h)
        m_prev = m_ref[h]
        m_new = jnp.maximum(m_prev, _sublane_bcast_max(smax_ref[h]))
        alpha = jnp.exp2(m_prev - m_new)
        s3 = s_ref[h % n_slots].reshape(tk // SUBLANE, SUBLANE, tq)
        p = jnp.exp2(s3 - m_new[None]).reshape(tk, tq).astype(BF16)
        pv = _dot(with_ones(vt_ref[t, vh * hd:(vh + 1) * hd, :]), p)
        acc = acc_ref[h].reshape(acc_rows // SUBLANE, SUBLANE, tq) * alpha[None]
        acc_ref[h] = acc.reshape(acc_rows, tq) + pv
        m_ref[h] = m_new

    for h in range(DENSE_LOOKAHEAD):
        scores(0, h)

    def sub_tile(t, carry):
        t_next = jnp.minimum(t + 1, n_sub - 1)
        for h in range(HEAD_SLOTS):
            ahead = h + DENSE_LOOKAHEAD
            if ahead < HEAD_SLOTS:
                scores(t, ahead)
            else:
                scores(t_next, ahead - HEAD_SLOTS)
            softmax_pv(t, h)
        return carry

    lax.fori_loop(0, n_sub, sub_tile, 0, unroll=2)

    @pl.when(kv == pl.num_programs(2) - 1)
    def _():
        for j in range(HEAD_SLOTS // 2):
            outs = []
            for h in (2 * j, 2 * j + 1):
                outs.append(acc_ref[h, 0:hd, :] / acc_ref[h, hd:hd + 1, :])
            o_t = jnp.concatenate(outs, axis=0)
            o_ref[:, j * LANE:(j + 1) * LANE] = o_t.T.astype(BF16)


def _dense_attn(qt, k, vt, km, vmt, *, n_seq, seq, nq, q_base, meta_base):
    tq = qt.shape[2]
    tk = vt.shape[2]
    n_sub = _row_tile(seq // tk, DENSE_SUBTILES)
    nk = seq // (tk * n_sub)
    out_rows = n_seq * nq * tq
    return pl.pallas_call(
        _dense_attn_kernel,
        grid=(n_seq, nq, nk),
        in_specs=[
            pl.BlockSpec((None, HEAD_SLOTS * LANE, tq), lambda b, i, j: (q_base + b * nq + i, 0, 0)),
            pl.BlockSpec((n_sub * tk, K_SLOTS * LANE), lambda b, i, j: (b * nk + j, 0)),
            pl.BlockSpec((n_sub, V_ROWS, tk), lambda b, i, j: (b * nk + j, 0, 0)),
            pl.BlockSpec((None, LANE, K_SLOTS * LANE), lambda b, i, j: (meta_base + b, 0, 0)),
            pl.BlockSpec((None, V_ROWS, LANE), lambda b, i, j: (meta_base + b, 0, 0)),
        ],
        out_specs=pl.BlockSpec((tq, D_MODEL), lambda b, i, j: (b * nq + i, 0)),
        out_shape=jax.ShapeDtypeStruct((out_rows, D_MODEL), BF16),
        scratch_shapes=[
            pltpu.VMEM((HEAD_SLOTS, SUBLANE, tq), F32),
            pltpu.VMEM((HEAD_SLOTS, SUBLANE, tq), F32),
            pltpu.VMEM((HEAD_SLOTS, MLA_V + 2 * SUBLANE, tq), F32),
            pltpu.VMEM((DENSE_SLOTS, tk, tq), F32),
            pltpu.VMEM((HEAD_SLOTS, N_META, tq), F32),
        ],
        compiler_params=pltpu.CompilerParams(
            dimension_semantics=("arbitrary", "arbitrary", "arbitrary"),
            vmem_limit_bytes=VMEM_LIMIT),
        name="dense_attn",
    )(qt, k, vt, km, vmt)


def _proj_na_kernel(h_ref, g_ref, w_ref, q_ref, k_ref, v_ref, *, transpose_v):
    a = _rms(h_ref[...], g_ref[2:3, :]).astype(BF16)
    qkv = _dot(a, w_ref[...])
    n = NA_HEADS * NA_HEAD_DIM
    q_ref[...] = (qkv[:, 0:n] * (NA_HEAD_DIM ** -0.5 * LOG2E)).astype(BF16)
    k_ref[...] = qkv[:, n:2 * n].astype(BF16)
    v = qkv[:, 2 * n:3 * n]
    if transpose_v:
        vt = v.T.astype(BF16)
        for t in range(v_ref.shape[0]):
            v_ref[t] = vt[:, t * LANE:(t + 1) * LANE]
    else:
        v_ref[...] = v.astype(BF16)


def _proj_na(h, gains, w, transpose_v):
    rows = h.shape[0]
    tm = _row_tile(rows, ROW_TILE)
    n = NA_HEADS * NA_HEAD_DIM
    if transpose_v:
        v_spec = pl.BlockSpec((tm // LANE, n, LANE), lambda i: (i, 0, 0))
        v_shape = jax.ShapeDtypeStruct((rows // LANE, n, LANE), BF16)
    else:
        v_spec = pl.BlockSpec((tm, n), lambda i: (i, 0))
        v_shape = jax.ShapeDtypeStruct((rows, n), BF16)
    return pl.pallas_call(
        functools.partial(_proj_na_kernel, transpose_v=transpose_v),
        grid=(rows // tm,),
        in_specs=[pl.BlockSpec((tm, D_MODEL), lambda i: (i, 0)), _const_spec(gains.shape),
                  _const_spec(w.shape)],
        out_specs=[pl.BlockSpec((tm, n), lambda i: (i, 0))] * 2 + [v_spec],
        out_shape=[jax.ShapeDtypeStruct((rows, n), BF16)] * 2 + [v_shape],
        compiler_params=pltpu.CompilerParams(
            dimension_semantics=("arbitrary",), vmem_limit_bytes=VMEM_LIMIT),
        name="proj_na",
    )(h, gains, w)


NA_SPAN_R = NA_WIN_R + 2
NA_MASKED = 2 * NA_WIN_R - 1


def _na_kernel(q_ref, k_ref, vt_ref, km_ref, vmt_ref, bias_ref, mb_ref, o_ref, s_ref,
               *, rows, rows_per_step):
    step = pl.program_id(1)
    n_pairs = rows_per_step // 2
    n_hp = NA_HEADS // 2
    span = NA_SPAN_R * GRID_W
    lane = lax.broadcasted_iota(jnp.int32, (GRID_W, LANE), 1)
    first = lane < (LANE // 2)
    zeros_m = jnp.zeros((LANE - N_META, 2 * LANE), F32)
    ones_v = jnp.ones((2 * SUBLANE, span + LANE), BF16)

    def geometry(rp):
        ra = step * rows_per_step + 2 * rp
        rs = [jnp.clip(ra + x - NA_WIN_R // 2, 0, rows - NA_WIN_R) for x in range(2)]
        ws = jnp.minimum((rs[0] // 2) * 2, rows - NA_SPAN_R)
        return ra, rs, ws

    n_slots = s_ref.shape[0]

    def scores(rp, hp):
        slot = hp % n_slots
        _, _, ws = geometry(rp)
        cols = slice(hp * LANE, (hp + 1) * LANE)
        parts = []
        for x in range(2):
            q0 = pl.multiple_of((2 * rp + x) * GRID_W, GRID_W)
            qx = q_ref[pl.ds(q0, GRID_W), cols]
            parts += [jnp.where(first, qx, jnp.zeros_like(qx)), jnp.where(first, jnp.zeros_like(qx), qx)]
        qblk = jnp.concatenate(parts, axis=0)
        k0 = pl.multiple_of(ws * GRID_W, 2 * GRID_W)
        s_ref[slot, 0:span, :] = _dot_nt(k_ref[pl.ds(k0, span), cols], qblk)
        s_ref[slot, span:span + N_META, :] = _dot_nt(km_ref[:, cols], qblk)

    for hp in range(NA_LOOKAHEAD):
        scores(0, hp)

    def row_pair(rp, carry):
        ra, rs, ws = geometry(rp)
        idx = []
        for jj in range(NA_SPAN_R):
            kr = ws + jj
            idx.append([jnp.where((kr >= rs[x]) & (kr < rs[x] + NA_WIN_R),
                                  kr - (ra + x) + NA_WIN_R - 1, NA_MASKED) for x in range(2)])
        t0 = ws // 2
        for hp in range(n_hp):
            slot = hp % n_slots
            ahead = hp + NA_LOOKAHEAD
            if ahead < n_hp:
                scores(rp, ahead)
            else:
                scores(jnp.minimum(rp + 1, n_pairs - 1), ahead - n_hp)
            cols = slice(hp * LANE, (hp + 1) * LANE)
            b = jnp.concatenate(
                [jnp.concatenate([bias_ref[hp, idx[jj][0]], bias_ref[hp, idx[jj][1]]], axis=1)
                 for jj in range(NA_SPAN_R)], axis=0)
            s = s_ref[slot, 0:span, :] + b
            mb = mb_ref[hp]
            sm = s_ref[slot, span:span + N_META, :] + jnp.concatenate([mb, mb], axis=1)
            s3 = s.reshape(span // SUBLANE, SUBLANE, 2 * LANE)
            sm3 = sm.reshape(N_META // SUBLANE, SUBLANE, 2 * LANE)
            m = _sublane_bcast_max(jnp.maximum(jnp.max(s3, axis=0), jnp.max(sm3, axis=0)))
            p = jnp.exp2(s3 - m[None]).reshape(span, 2 * LANE).astype(BF16)
            pm3 = jnp.exp2(sm3 - m[None])
            pm = jnp.concatenate([pm3.reshape(N_META, 2 * LANE), zeros_m], axis=0).astype(BF16)
            v_all = jnp.concatenate([vt_ref[t0 + t, cols, :] for t in range(span // LANE)]
                                    + [vmt_ref[cols, :]], axis=1)
            o_t = _dot(jnp.concatenate([v_all, ones_v], axis=0),
                       jnp.concatenate([p, pm], axis=0))
            o_t = o_t[0:LANE] / o_t[LANE:LANE + 1]
            for x in range(2):
                blk = o_t[:, x * LANE:(x + 1) * LANE].T
                q0 = pl.multiple_of((2 * rp + x) * GRID_W, GRID_W)
                o_ref[pl.ds(q0, GRID_W), cols] = jnp.where(
                    first, blk[0:GRID_W], blk[GRID_W:2 * GRID_W]).astype(BF16)
        return carry

    lax.fori_loop(0, n_pairs, row_pair, 0, unroll=4)


def _na_attn(q, k, vt, km, vmt, bias, mb, *, n_seq, seq, meta_base):
    rows = seq // GRID_W
    assert rows >= NA_SPAN_R and rows % 2 == 0
    rps = _row_tile(rows, NA_ROWS_PER_STEP)
    nsteps = rows // rps
    n = NA_HEADS * NA_HEAD_DIM
    span = NA_SPAN_R * GRID_W
    return pl.pallas_call(
        functools.partial(_na_kernel, rows=rows, rows_per_step=rps),
        grid=(n_seq, nsteps),
        in_specs=[
            pl.BlockSpec((rps * GRID_W, n), lambda b, i: (b * nsteps + i, 0)),
            pl.BlockSpec((seq, n), lambda b, i: (b, 0), pipeline_mode=pl.Buffered(1)),
            pl.BlockSpec((seq // LANE, n, LANE), lambda b, i: (b, 0, 0), pipeline_mode=pl.Buffered(1)),
            pl.BlockSpec((None, N_META, n), lambda b, i: (meta_base + b, 0, 0)),
            pl.BlockSpec((None, n, LANE), lambda b, i: (meta_base + b, 0, 0)),
            _const_spec(bias.shape),
            _const_spec(mb.shape),
        ],
        out_specs=pl.BlockSpec((rps * GRID_W, n), lambda b, i: (b * nsteps + i, 0)),
        out_shape=jax.ShapeDtypeStruct((n_seq * seq, n), BF16),
        scratch_shapes=[pltpu.VMEM((NA_SLOTS, span + N_META, 2 * LANE), F32)],
        compiler_params=pltpu.CompilerParams(
            dimension_semantics=("arbitrary", "arbitrary"), vmem_limit_bytes=VMEM_LIMIT),
        name="na_attn",
    )(q, k, vt, km, vmt, bias, mb)


def _na_meta_kernel(q_ref, km_ref, vm_ref, mb_ref, o_ref):
    lane = lax.broadcasted_iota(jnp.int32, (N_META, LANE), 1)
    first = lane < (LANE // 2)
    for j in range(NA_HEADS // 2):
        cols = slice(j * LANE, (j + 1) * LANE)
        qp = q_ref[:, cols]
        km = km_ref[:, cols]
        vm = vm_ref[:, cols]
        outs = []
        for half in range(2):
            h = 2 * j + half
            qh = jnp.where(first if half == 0 else jnp.logical_not(first), qp, jnp.zeros_like(qp))
            sm = _dot_nt(qh, km)
            sm = jnp.where(lane < N_META, sm + mb_ref[h:h + 1, :], NEG_INF)
            m = jnp.max(sm, axis=-1, keepdims=True)
            pm = jnp.exp2(sm - m)
            l = jnp.sum(pm, axis=-1, keepdims=True)
            outs.append(_dot(pm.astype(BF16), vm) / l)
        o_ref[:, cols] = jnp.where(first, outs[0], outs[1]).astype(BF16)


def _na_meta(qm, km, vm, mb):
    n_seq = km.shape[0]
    n = NA_HEADS * NA_HEAD_DIM
    return pl.pallas_call(
        _na_meta_kernel,
        grid=(n_seq,),
        in_specs=[
            pl.BlockSpec((N_META, n), lambda b: (b, 0)),
            pl.BlockSpec((None, LANE, n), lambda b: (b, 0, 0)),
            pl.BlockSpec((None, LANE, n), lambda b: (b, 0, 0)),
            _const_spec(mb.shape),
        ],
        out_specs=pl.BlockSpec((N_META, n), lambda b: (b, 0)),
        out_shape=jax.ShapeDtypeStruct((n_seq * N_META, n), BF16),
        compiler_params=pltpu.CompilerParams(dimension_semantics=("arbitrary",)),
        name="na_meta",
    )(qm, km, vm, mb)


def _take_cols(w, idx):
    idx = np.asarray(idx)
    neg = idx < 0
    same_run = np.where(neg[1:] | neg[:-1], neg[1:] & neg[:-1], np.diff(idx) == 1)
    breaks = np.flatnonzero(~same_run) + 1
    parts = []
    for run in np.split(idx, breaks):
        if run[0] < 0:
            parts.append(jnp.zeros((w.shape[0], len(run)), w.dtype))
        else:
            parts.append(w[:, int(run[0]):int(run[-1]) + 1])
    return jnp.concatenate(parts, axis=1)


def _swap_halves(n):
    half = n // 2
    return np.concatenate([np.arange(half, n), np.arange(0, half)])


def _dense_weights(w_in, q_norm, w_uq, kv_norm, w_ukv, gq_norm, gk_norm, w_out):
    pad = lambda k: -np.ones(k, np.int64)
    o_kr = MLA_Q_LORA + MLA_KV_LORA
    o_gq = o_kr + MLA_ROPE
    o_gk = o_gq + GQA_HEADS * GQA_HEAD_DIM
    o_gv = o_gk + GQA_KV_HEADS * GQA_HEAD_DIM
    axial = np.concatenate([_swap_halves(GQA_HEAD_DIM // 2),
                            GQA_HEAD_DIM // 2 + _swap_halves(GQA_HEAD_DIM // 2)])
    idx = [np.arange(0, o_kr)]
    idx += [pad(MLA_NOPE), o_kr + np.arange(MLA_ROPE), pad(LANE - MLA_NOPE - MLA_ROPE)]
    idx += [pad(MLA_NOPE), o_kr + _swap_halves(MLA_ROPE), pad(LANE - MLA_NOPE - MLA_ROPE)]
    assert 2 * GQA_HEAD_DIM == LANE and GQA_KV_HEADS == 2
    for h in range(GQA_HEADS):
        idx += [o_gq + h * GQA_HEAD_DIM + np.arange(GQA_HEAD_DIM)]
    for h in range(GQA_HEADS):
        idx += [o_gq + h * GQA_HEAD_DIM + axial]
    for h in range(GQA_KV_HEADS):
        idx += [o_gk + h * GQA_HEAD_DIM + np.arange(GQA_HEAD_DIM)]
    for h in range(GQA_KV_HEADS):
        idx += [o_gk + h * GQA_HEAD_DIM + axial]
    idx += [o_gv + np.arange(GQA_KV_HEADS * GQA_HEAD_DIM)]
    idx = np.concatenate(idx)
    assert idx.shape[0] == _C_END
    w_in2 = _take_cols(w_in.astype(BF16), idx)

    hd = MLA_NOPE + MLA_ROPE
    ia, ib = [], []
    for h in range(MLA_HEADS):
        ia += [h * hd + np.arange(hd), pad(LANE - hd)]
        ib += [pad(MLA_NOPE), h * hd + MLA_NOPE + _swap_halves(MLA_ROPE), pad(LANE - hd)]
    w_uq2 = _take_cols(w_uq.astype(BF16), np.concatenate(ia + ib))

    kvd = MLA_NOPE + MLA_V
    ik, iv = [], []
    for h in range(MLA_HEADS):
        ik += [h * kvd + np.arange(MLA_NOPE), pad(LANE - MLA_NOPE)]
        iv += [h * kvd + MLA_NOPE + np.arange(MLA_V)]
    w_ukv2 = _take_cols(w_ukv.astype(BF16), np.concatenate(ik + iv))

    def gain_pair(g):
        ga = jnp.concatenate([g, g])[None, :]
        gb = jnp.tile(g[jnp.asarray(axial)], 2)[None, :]
        return ga, gb

    gq_a, gq_b = gain_pair(gq_norm)
    gk_a, gk_b = gain_pair(gk_norm)

    w_out2 = w_out.astype(BF16)

    return dict(w_in=w_in2, q_norm=q_norm[None, :], w_uq=w_uq2, kv_norm=kv_norm[None, :],
                w_ukv=w_ukv2, gq_a=gq_a, gq_b=gq_b, gk_a=gk_a, gk_b=gk_b,
                rope_expand=_rope_expand_matrix()), w_out2


def _rope_tables(pos, row, row_repeat, col, col_tile):
    half = MLA_ROPE // 2
    inv = 1.0 / (ROPE_THETA ** (jnp.arange(half, dtype=F32) / half))
    n = pos.shape[0]

    def cs(p):
        ang = p.astype(F32)[None, :] * inv[:, None]
        return jnp.cos(ang), jnp.sin(ang)

    by_row = tuple(jnp.repeat(x, row_repeat, axis=1) for x in cs(row))
    by_col = tuple(jnp.tile(x, (1, col_tile)) for x in cs(col))
    vals = jnp.concatenate(cs(pos) + by_row + by_col + (jnp.zeros((LANE - 6 * half, n), F32),), axis=0).T
    hi = vals.astype(BF16)
    rest = vals - hi.astype(F32)
    mid = rest.astype(BF16)
    lo = (rest - mid.astype(F32)).astype(BF16)
    return jnp.stack([hi, mid, lo])


def _rope_expand_matrix():
    half = MLA_ROPE // 2
    c1, s1, cr, sr, cc, sc = range(6)
    tail = LANE - MLA_NOPE - MLA_ROPE
    blank = [(None, 0)]
    layout = (blank * (MLA_NOPE // half) + [(c1, 1), (c1, 1)] + blank * (tail // half)
              + blank * (MLA_NOPE // half) + [(s1, -1), (s1, 1)] + blank * (tail // half)
              + [(cr, 1), (cr, 1), (cc, 1), (cc, 1)] * (LANE // GQA_HEAD_DIM)
              + [(sr, -1), (sr, 1), (sc, -1), (sc, 1)] * (LANE // GQA_HEAD_DIM))
    expand = np.zeros((LANE, len(layout) * half), np.float32)
    for blk, (src, sign) in enumerate(layout):
        if src is not None:
            expand[src * half + np.arange(half), blk * half + np.arange(half)] = sign
    return jnp.asarray(expand, BF16)


def _na_bias_tables(rpb, meta_bias):
    c_idx = np.arange(GRID_W)
    c_start = np.clip(c_idx - NA_WIN_C // 2, 0, GRID_W - NA_WIN_C)
    col_mask = (c_idx[None, :] >= c_start[:, None]) & (c_idx[None, :] < c_start[:, None] + NA_WIN_C)
    col_off = np.clip(c_idx[None, :] - c_idx[:, None] + NA_WIN_C - 1, 0, 2 * NA_WIN_C - 2)
    hp = NA_HEADS // 2
    n_off = 2 * NA_WIN_C - 1
    select = np.zeros((2, n_off, GRID_W, 2, GRID_W), np.float32)
    kc_g, c_g = np.meshgrid(c_idx, c_idx, indexing="ij")
    for half in range(2):
        select[half, col_off[c_g, kc_g], kc_g, half, c_g] = 1.0
    rows = rpb.reshape(hp, 2, NA_MASKED, n_off).transpose(0, 2, 1, 3).reshape(hp * NA_MASKED, 2 * n_off)
    t = jnp.dot(rows, jnp.asarray(select.reshape(2 * n_off, GRID_W * LANE)),
                precision=lax.Precision.HIGHEST) * LOG2E
    t = t.reshape(hp, NA_MASKED, GRID_W, LANE)
    keep = np.tile(col_mask.T, (1, 2))
    t = jnp.where(jnp.asarray(keep)[None, None], t, NEG_INF)
    bias = jnp.concatenate([t, jnp.full_like(t[:, :1], NEG_INF)], axis=1)
    mbl = meta_bias * LOG2E
    mb_t = jnp.repeat(mbl.reshape(hp, 2, N_META).transpose(0, 2, 1), GRID_W, axis=2)
    mb = jnp.pad(mbl, ((0, 0), (0, LANE - N_META)))
    return bias, mb_t, mb


def _pad_meta(x, n_seq):
    c = x.shape[1]
    return jnp.pad(x.reshape(n_seq, N_META, c), ((0, 0), (0, LANE - N_META), (0, 0)))


def kernel(x_prompt, x_sample, meta, norm_gains, ffn1_w_gate, ffn1_w_up, ffn1_w_down, ffn2_w_gate, ffn2_w_up, ffn2_w_down, attn_w_in, mla_q_norm, mla_w_uq, mla_kv_norm, mla_w_ukv, gqa_q_norm, gqa_k_norm, attn_w_out, na_w_qkv, na_rpb, na_meta_bias, na_w_out):
    bp, sp, _ = x_prompt.shape
    bs, ss, _ = x_sample.shape
    n_seq = bp + bs
    depth = norm_gains.shape[0]
    groups = [(bp, sp, 0), (bs, ss, bp)]

    n_meta = n_seq * N_META
    meta_rows = -(-n_meta // LANE) * LANE
    pad_rows = lambda x: jnp.pad(x, ((0, meta_rows - x.shape[0]), (0, 0)))
    h_tok = [x_prompt.reshape(bp * sp, D_MODEL), x_sample.reshape(bs * ss, D_MODEL)]
    h_meta = pad_rows(jnp.tile(meta.astype(F32), (n_seq, 1)))

    smax = max(sp, ss)
    tab_tok = _rope_tables(jnp.arange(smax) + N_META, jnp.arange(smax // GRID_W), GRID_W,
                           jnp.arange(GRID_W), smax // GRID_W)
    tab_meta = _rope_tables(jnp.arange(meta_rows) % N_META, jnp.full((1,), -1), meta_rows,
                            jnp.arange(N_META), meta_rows // N_META)

    w1 = (ffn1_w_gate.astype(BF16), ffn1_w_up.astype(BF16), ffn1_w_down.astype(BF16))
    w2 = (ffn2_w_gate.astype(BF16), ffn2_w_up.astype(BF16), ffn2_w_down.astype(BF16))

    for i in range(depth):
        gains = jnp.pad(norm_gains[i], ((0, 2), (0, 0)))
        j = i // 2
        h_tok = [_ffn1(h, gains, *w1, i) for h in h_tok]
        h_meta = _ffn1(h_meta, gains, *w1, i)
        if i % 2 == 0:
            w, w_out = _dense_weights(attn_w_in[j], mla_q_norm[j], mla_w_uq[j], mla_kv_norm[j],
                                      mla_w_ukv[j], gqa_q_norm[j], gqa_k_norm[j], attn_w_out[j])
            qkv_tok = [_proj_dense(h, gains, w, tab_tok, s) for h, (_, s, _) in zip(h_tok, groups)]
            qmt, km, vmt = _proj_dense(h_meta, gains, w, tab_meta, 0)
            kmp = _pad_meta(km[:n_meta], n_seq)
            vmt = vmt.transpose(1, 0, 2).reshape(V_ROWS, meta_rows)
            vmtp = vmt[:, :n_meta].reshape(V_ROWS, n_seq, N_META).transpose(1, 0, 2)
            vmtp = jnp.pad(vmtp, ((0, 0), (0, 0), (0, LANE - N_META)))
            qmt = qmt.transpose(1, 0, 2).reshape(HEAD_SLOTS * LANE, meta_rows)
            qmtp = qmt[:, :n_meta].reshape(HEAD_SLOTS * LANE, n_seq, N_META).transpose(1, 0, 2)
            qmtp = jnp.pad(qmtp, ((0, 0), (0, 0), (0, LANE - N_META)))
            o_tok, o_meta = [], []
            for (qt, k, vt), (nb, s, b0) in zip(qkv_tok, groups):
                o_tok.append(_dense_attn(qt, k, vt, kmp, vmtp, n_seq=nb, seq=s, nq=qt.shape[0] // nb,
                                         q_base=0, meta_base=b0))
                om = _dense_attn(qmtp, k, vt, kmp, vmtp, n_seq=nb, seq=s, nq=1, q_base=b0, meta_base=b0)
                o_meta.append(om.reshape(nb, LANE, D_MODEL)[:, :N_META].reshape(nb * N_META, D_MODEL))
            o_meta = pad_rows(jnp.concatenate(o_meta, axis=0))
        else:
            w_qkv = na_w_qkv[j].astype(BF16)
            w_out = na_w_out[j].astype(BF16)
            bias, mb_t, mb = _na_bias_tables(na_rpb[j], na_meta_bias[j])
            qkv_tok = [_proj_na(h, gains, w_qkv, True) for h in h_tok]
            qm, km, vm = _proj_na(h_meta, gains, w_qkv, False)
            kmp, vmp = _pad_meta(km[:n_meta], n_seq), _pad_meta(vm[:n_meta], n_seq)
            km16 = km[:n_meta].reshape(n_seq, N_META, NA_HEADS * NA_HEAD_DIM)
            vmtp = vmp.transpose(0, 2, 1)
            o_tok = [_na_attn(q, k, vt, km16, vmtp, bias, mb_t, n_seq=nb, seq=s, meta_base=b0)
                     for (q, k, vt), (nb, s, b0) in zip(qkv_tok, groups)]
            o_meta = pad_rows(_na_meta(qm[:n_meta], kmp, vmp, mb))
        h_tok = [_mix_ffn2(h, o, w_out, gains, *w2, i) for h, o in zip(h_tok, o_tok)]
        h_meta = _mix_ffn2(h_meta, o_meta, w_out, gains, *w2, i)

    return (h_tok[0].reshape(bp, sp, D_MODEL), h_tok[1].reshape(bs, ss, D_MODEL))
```

```python
import functools
import math

import jax
import jax.numpy as jnp
import numpy as np
from jax import lax
from jax.experimental import pallas as pl
from jax.experimental.pallas import tpu as pltpu

F32 = jnp.float32
BF16 = jnp.bfloat16

D_MODEL = 1024
N_META = 16
GRID_W = 64
D_FF = 2816
EPS = 1e-6
NEG_INF = -1e30
LOG2E = math.log2(math.e)

MLA_HEADS = 8
MLA_Q_LORA = 256
MLA_KV_LORA = 128
MLA_NOPE = 64
MLA_ROPE = 32
MLA_V = 64
GQA_HEADS = 8
GQA_KV_HEADS = 2
GQA_HEAD_DIM = 64
ROPE_THETA = 10000.0
NA_HEADS = 16
NA_HEAD_DIM = 64
NA_WIN_R = 8
NA_WIN_C = 16

LANE = 128
HEAD_SLOTS = MLA_HEADS + GQA_HEADS
K_SLOTS = MLA_HEADS + 1
V_ROWS = (MLA_HEADS + GQA_KV_HEADS) * MLA_V
SUBLANE = 8
DENSE_LOOKAHEAD, DENSE_SLOTS = 2, 4
NA_LOOKAHEAD, NA_SLOTS = 2, 4
VMEM_LIMIT = 56 * 1024 * 1024
ROW_TILE = 512
DENSE_SUBTILES = 8
NA_ROWS_PER_STEP = 16

_C_CQ = 0
_C_CKV = _C_CQ + MLA_Q_LORA
_C_KRA = _C_CKV + MLA_KV_LORA
_C_KRB = _C_KRA + LANE
_C_GQA = _C_KRB + LANE
_C_GQB = _C_GQA + GQA_HEADS // 2 * LANE
_C_GKA = _C_GQB + GQA_HEADS // 2 * LANE
_C_GKB = _C_GKA + LANE
_C_GV = _C_GKB + LANE
_C_END = _C_GV + LANE


def _const_spec(shape):
    nd = len(shape)
    return pl.BlockSpec(shape, lambda *_: (0,) * nd, pipeline_mode=pl.Buffered(1))


def _rms(x, g):
    ms = jnp.mean(x * x, axis=-1, keepdims=True)
    return x * lax.rsqrt(ms + EPS) * g


def _dot(a, b):
    return jnp.dot(a, b, preferred_element_type=F32)


def _dot_nt(a, b):
    return lax.dot_general(a, b, (((1,), (1,)), ((), ())), preferred_element_type=F32)


def _row_tile(rows, want):
    t = min(rows, want)
    while rows % t:
        t //= 2
    return t


def _ffn_body(h, g_ref, pre, post, wg_ref, wu_ref, wd_ref):
    xn = _rms(h, g_ref[pre:pre + 1, :]).astype(BF16)
    gate = _dot(xn, wg_ref[...])
    up = _dot(xn, wu_ref[...])
    act = (gate * jax.nn.sigmoid(gate) * up).astype(BF16)
    y = _dot(act, wd_ref[...])
    return h + 0.5 * _rms(y, g_ref[post:post + 1, :])


def _ffn1_kernel(h_ref, g_ref, wg_ref, wu_ref, wd_ref, out_ref):
    out_ref[...] = _ffn_body(h_ref[...], g_ref, 0, 1, wg_ref, wu_ref, wd_ref)


def _mix_ffn2_kernel(h_ref, o_ref, wo_ref, g_ref, wg_ref, wu_ref, wd_ref, out_ref):
    mixed = _dot(o_ref[...], wo_ref[...])
    h = h_ref[...] + _rms(mixed, g_ref[3:4, :])
    out_ref[...] = _ffn_body(h, g_ref, 4, 5, wg_ref, wu_ref, wd_ref)


def _layer_spec(w, layer):
    return pl.BlockSpec((None,) + w.shape[1:], lambda *_: (layer, 0, 0), pipeline_mode=pl.Buffered(1))


def _ffn1(h, gains, wg, wu, wd, layer):
    rows = h.shape[0]
    tm = _row_tile(rows, ROW_TILE)
    return pl.pallas_call(
        _ffn1_kernel,
        grid=(rows // tm,),
        in_specs=[
            pl.BlockSpec((tm, D_MODEL), lambda i: (i, 0)),
            _const_spec(gains.shape),
            _layer_spec(wg, layer), _layer_spec(wu, layer), _layer_spec(wd, layer),
        ],
        out_specs=pl.BlockSpec((tm, D_MODEL), lambda i: (i, 0)),
        out_shape=jax.ShapeDtypeStruct((rows, D_MODEL), F32),
        compiler_params=pltpu.CompilerParams(
            dimension_semantics=("arbitrary",), vmem_limit_bytes=VMEM_LIMIT),
        name="ffn1",
    )(h, gains, wg, wu, wd)


def _mix_ffn2(h, o, wo, gains, wg, wu, wd, layer):
    rows = h.shape[0]
    tm = _row_tile(rows, ROW_TILE)
    return pl.pallas_call(
        _mix_ffn2_kernel,
        grid=(rows // tm,),
        in_specs=[
            pl.BlockSpec((tm, D_MODEL), lambda i: (i, 0)),
            pl.BlockSpec((tm, o.shape[1]), lambda i: (i, 0)),
            _const_spec(wo.shape),
            _const_spec(gains.shape),
            _layer_spec(wg, layer), _layer_spec(wu, layer), _layer_spec(wd, layer),
        ],
        out_specs=pl.BlockSpec((tm, D_MODEL), lambda i: (i, 0)),
        out_shape=jax.ShapeDtypeStruct((rows, D_MODEL), F32),
        compiler_params=pltpu.CompilerParams(
            dimension_semantics=("arbitrary",), vmem_limit_bytes=VMEM_LIMIT),
        name="mix_ffn2",
    )(h, o, wo, gains, wg, wu, wd)


def _proj_dense_kernel(h_ref, g_ref, win_ref, qn_ref, wuq_ref, kvn_ref, wukv_ref,
                       gqa_ref, gqb_ref, gka_ref, gkb_ref, exp_ref, tab_ref, qt_ref, k_ref, vt_ref):
    a = _rms(h_ref[...], g_ref[2:3, :]).astype(BF16)
    proj = _dot(a, win_ref[...])
    pieces = jnp.concatenate([tab_ref[i] for i in range(tab_ref.shape[0])], axis=1)
    tab = _dot(pieces, exp_ref[...])
    cos_k, sin_k = tab[:, 0:LANE], tab[:, LANE:2 * LANE]
    cos_g, sin_g = tab[:, 2 * LANE:3 * LANE], tab[:, 3 * LANE:4 * LANE]
    qs = (MLA_NOPE + MLA_ROPE) ** -0.5 * LOG2E
    lane = lax.broadcasted_iota(jnp.int32, cos_k.shape, 1)
    cos_q = jnp.where(lane < MLA_NOPE, qs, cos_k * qs)
    sin_q = sin_k * qs

    cqn = _rms(proj[:, _C_CQ:_C_CQ + MLA_Q_LORA], qn_ref[...]).astype(BF16)
    qab = _dot(cqn, wuq_ref[...])
    nq = MLA_HEADS * LANE
    for h in range(MLA_HEADS):
        qa = qab[:, h * LANE:(h + 1) * LANE]
        qb = qab[:, nq + h * LANE:nq + (h + 1) * LANE]
        qt_ref[0, h * LANE:(h + 1) * LANE, :] = (qa * cos_q + qb * sin_q).T.astype(BF16)

    ckvn = _rms(proj[:, _C_CKV:_C_CKV + MLA_KV_LORA], kvn_ref[...]).astype(BF16)
    kv = _dot(ckvn, wukv_ref[...])
    k_rope = (proj[:, _C_KRA:_C_KRA + LANE] * cos_k + proj[:, _C_KRB:_C_KRB + LANE] * sin_k)
    for h in range(MLA_HEADS):
        k_ref[:, h * LANE:(h + 1) * LANE] = (kv[:, h * LANE:(h + 1) * LANE] + k_rope).astype(BF16)
    nv = MLA_HEADS * MLA_V

    low = lane < GQA_HEAD_DIM

    def normed_rotary(xa, xb, cos, sin):
        sq = xa * xa
        ss_lo = jnp.sum(jnp.where(low, sq, 0.0), axis=-1, keepdims=True)
        ss_hi = jnp.sum(jnp.where(low, 0.0, sq), axis=-1, keepdims=True)
        r = lax.rsqrt(jnp.where(low, ss_lo, ss_hi) * (1.0 / GQA_HEAD_DIM) + EPS)
        return (xa * cos + xb * sin) * r

    gq_scale = GQA_HEAD_DIM ** -0.5 * LOG2E
    cq_g = cos_g * (gqa_ref[...] * gq_scale)
    sq_g = sin_g * (gqb_ref[...] * gq_scale)
    per_kv = GQA_HEADS // GQA_KV_HEADS
    zeros_t = jnp.zeros((LANE - GQA_HEAD_DIM, qt_ref.shape[2]), F32)
    for j in range(GQA_HEADS // 2):
        xa = proj[:, _C_GQA + j * LANE:_C_GQA + (j + 1) * LANE]
        xb = proj[:, _C_GQB + j * LANE:_C_GQB + (j + 1) * LANE]
        y_t = normed_rotary(xa, xb, cq_g, sq_g).T
        for half in range(2):
            h = 2 * j + half
            q_t = y_t[half * GQA_HEAD_DIM:(half + 1) * GQA_HEAD_DIM]
            rows = [q_t, zeros_t] if h // per_kv == 0 else [zeros_t, q_t]
            qt_ref[0, nq + h * LANE:nq + (h + 1) * LANE, :] = jnp.concatenate(rows, axis=0).astype(BF16)
    xa = proj[:, _C_GKA:_C_GKA + LANE]
    xb = proj[:, _C_GKB:_C_GKB + LANE]
    k_ref[:, nq:nq + LANE] = normed_rotary(xa, xb, cos_g * gka_ref[...], sin_g * gkb_ref[...]).astype(BF16)
    v = jnp.concatenate([kv[:, nq:nq + nv], proj[:, _C_GV:_C_GV + LANE]], axis=1)
    vt_ref[0] = v.T.astype(BF16)


def _proj_dense(h, gains, w, tab, seq):
    rows = h.shape[0]
    tm = _row_tile(seq if seq else rows, ROW_TILE)
    nblk = (seq // tm) if seq else 1
    consts = [w["w_in"], w["q_norm"], w["w_uq"], w["kv_norm"], w["w_ukv"],
              w["gq_a"], w["gq_b"], w["gk_a"], w["gk_b"], w["rope_expand"]]
    return pl.pallas_call(
        _proj_dense_kernel,
        grid=(rows // tm,),
        in_specs=[pl.BlockSpec((tm, D_MODEL), lambda i: (i, 0)), _const_spec(gains.shape)]
        + [_const_spec(c.shape) for c in consts]
        + [pl.BlockSpec((tab.shape[0], tm, LANE), lambda i: (0, i % nblk, 0))],
        out_specs=[
            pl.BlockSpec((1, HEAD_SLOTS * LANE, tm), lambda i: (i, 0, 0)),
            pl.BlockSpec((tm, K_SLOTS * LANE), lambda i: (i, 0)),
            pl.BlockSpec((1, V_ROWS, tm), lambda i: (i, 0, 0)),
        ],
        out_shape=[
            jax.ShapeDtypeStruct((rows // tm, HEAD_SLOTS * LANE, tm), BF16),
            jax.ShapeDtypeStruct((rows, K_SLOTS * LANE), BF16),
            jax.ShapeDtypeStruct((rows // tm, V_ROWS, tm), BF16),
        ],
        compiler_params=pltpu.CompilerParams(
            dimension_semantics=("arbitrary",), vmem_limit_bytes=VMEM_LIMIT),
        name="proj_dense",
    )(h, gains, *consts, tab)


def _head_slots(h):
    if h < MLA_HEADS:
        return h, h
    kvh = (h - MLA_HEADS) // (GQA_HEADS // GQA_KV_HEADS)
    return MLA_HEADS, MLA_HEADS + kvh


def _sublane_bcast_max(x):
    return jnp.broadcast_to(jnp.max(x, axis=0, keepdims=True), x.shape)


def _dense_attn_kernel(qt_ref, k_ref, vt_ref, km_ref, vmt_ref, o_ref,
                       m_ref, smax_ref, acc_ref, s_ref, sm_ref):
    kv = pl.program_id(2)
    tq = qt_ref.shape[1]
    n_sub, _, tk = vt_ref.shape
    hd = MLA_V
    acc_rows = acc_ref.shape[1]

    def with_ones(vt):
        return jnp.concatenate([vt, jnp.ones((acc_rows - hd, vt.shape[1]), BF16)], axis=0)

    @pl.when(kv == 0)
    def _():
        for h in range(HEAD_SLOTS):
            ks, _ = _head_slots(h)
            qt = qt_ref[h * LANE:(h + 1) * LANE, :]
            sm_ref[h] = _dot(km_ref[0:N_META, ks * LANE:(ks + 1) * LANE], qt)
        for h in range(HEAD_SLOTS):
            _, vh = _head_slots(h)
            s3 = sm_ref[h].reshape(N_META // SUBLANE, SUBLANE, tq)
            m = _sublane_bcast_max(jnp.max(s3, axis=0))
            p3 = jnp.exp2(s3 - m[None])
            m_ref[h] = m
            p = p3.reshape(N_META, tq).astype(BF16)
            acc_ref[h] = _dot(with_ones(vmt_ref[vh * hd:(vh + 1) * hd, 0:N_META]), p)

    n_slots = s_ref.shape[0]

    def scores(t, h):
        ks, _ = _head_slots(h)
        k0 = pl.multiple_of(t * tk, tk)
        k = k_ref[pl.ds(k0, tk), ks * LANE:(ks + 1) * LANE]
        s = _dot(k, qt_ref[h * LANE:(h + 1) * LANE, :])
        s_ref[h % n_slots] = s
        smax_ref[h] = jnp.max(s.reshape(tk // SUBLANE, SUBLANE, tq), axis=0)

    def softmax_pv(t, h):
        _, vh = _head_slots(h)
        m_prev = m_ref[h]
        m_new = jnp.maximum(m_prev, _sublane_bcast_max(smax_ref[h]))
        alpha = jnp.exp2(m_prev - m_new)
        s3 = s_ref[h % n_slots].reshape(tk // SUBLANE, SUBLANE, tq)
        p = jnp.exp2(s3 - m_new[None]).reshape(tk, tq).astype(BF16)
        pv = _dot(with_ones(vt_ref[t, vh * hd:(vh + 1) * hd, :]), p)
        acc = acc_ref[h].reshape(acc_rows // SUBLANE, SUBLANE, tq) * alpha[None]
        acc_ref[h] = acc.reshape(acc_rows, tq) + pv
        m_ref[h] = m_new

    for h in range(DENSE_LOOKAHEAD):
        scores(0, h)

    def sub_tile(t, carry):
        t_next = jnp.minimum(t + 1, n_sub - 1)
        for h in range(HEAD_SLOTS):
            ahead = h + DENSE_LOOKAHEAD
            if ahead < HEAD_SLOTS:
                scores(t, ahead)
            else:
                scores(t_next, ahead - HEAD_SLOTS)
            softmax_pv(t, h)
        return carry

    lax.fori_loop(0, n_sub, sub_tile, 0, unroll=2)

    @pl.when(kv == pl.num_programs(2) - 1)
    def _():
        for j in range(HEAD_SLOTS // 2):
            outs = []
            for h in (2 * j, 2 * j + 1):
                outs.append(acc_ref[h, 0:hd, :] / acc_ref[h, hd:hd + 1, :])
            o_t = jnp.concatenate(outs, axis=0)
            o_ref[:, j * LANE:(j + 1) * LANE] = o_t.T.astype(BF16)


def _dense_attn(qt, k, vt, km, vmt, *, n_seq, seq, nq, q_base, meta_base):
    tq = qt.shape[2]
    tk = vt.shape[2]
    n_sub = _row_tile(seq // tk, DENSE_SUBTILES)
    nk = seq // (tk * n_sub)
    out_rows = n_seq * nq * tq
    return pl.pallas_call(
        _dense_attn_kernel,
        grid=(n_seq, nq, nk),
        in_specs=[
            pl.BlockSpec((None, HEAD_SLOTS * LANE, tq), lambda b, i, j: (q_base + b * nq + i, 0, 0)),
            pl.BlockSpec((n_sub * tk, K_SLOTS * LANE), lambda b, i, j: (b * nk + j, 0)),
            pl.BlockSpec((n_sub, V_ROWS, tk), lambda b, i, j: (b * nk + j, 0, 0)),
            pl.BlockSpec((None, LANE, K_SLOTS * LANE), lambda b, i, j: (meta_base + b, 0, 0)),
            pl.BlockSpec((None, V_ROWS, LANE), lambda b, i, j: (meta_base + b, 0, 0)),
        ],
        out_specs=pl.BlockSpec((tq, D_MODEL), lambda b, i, j: (b * nq + i, 0)),
        out_shape=jax.ShapeDtypeStruct((out_rows, D_MODEL), BF16),
        scratch_shapes=[
            pltpu.VMEM((HEAD_SLOTS, SUBLANE, tq), F32),
            pltpu.VMEM((HEAD_SLOTS, SUBLANE, tq), F32),
            pltpu.VMEM((HEAD_SLOTS, MLA_V + 2 * SUBLANE, tq), F32),
            pltpu.VMEM((DENSE_SLOTS, tk, tq), F32),
            pltpu.VMEM((HEAD_SLOTS, N_META, tq), F32),
        ],
        compiler_params=pltpu.CompilerParams(
            dimension_semantics=("arbitrary", "arbitrary", "arbitrary"),
            vmem_limit_bytes=VMEM_LIMIT),
        name="dense_attn",
    )(qt, k, vt, km, vmt)


def _proj_na_kernel(h_ref, g_ref, w_ref, q_ref, k_ref, v_ref, *, transpose_v):
    a = _rms(h_ref[...], g_ref[2:3, :]).astype(BF16)
    qkv = _dot(a, w_ref[...])
    n = NA_HEADS * NA_HEAD_DIM
    q_ref[...] = (qkv[:, 0:n] * (NA_HEAD_DIM ** -0.5 * LOG2E)).astype(BF16)
    k_ref[...] = qkv[:, n:2 * n].astype(BF16)
    v = qkv[:, 2 * n:3 * n]
    if transpose_v:
        vt = v.T.astype(BF16)
        for t in range(v_ref.shape[0]):
            v_ref[t] = vt[:, t * LANE:(t + 1) * LANE]
    else:
        v_ref[...] = v.astype(BF16)


def _proj_na(h, gains, w, transpose_v):
    rows = h.shape[0]
    tm = _row_tile(rows, ROW_TILE)
    n = NA_HEADS * NA_HEAD_DIM
    if transpose_v:
        v_spec = pl.BlockSpec((tm // LANE, n, LANE), lambda i: (i, 0, 0))
        v_shape = jax.ShapeDtypeStruct((rows // LANE, n, LANE), BF16)
    else:
        v_spec = pl.BlockSpec((tm, n), lambda i: (i, 0))
        v_shape = jax.ShapeDtypeStruct((rows, n), BF16)
    return pl.pallas_call(
        functools.partial(_proj_na_kernel, transpose_v=transpose_v),
        grid=(rows // tm,),
        in_specs=[pl.BlockSpec((tm, D_MODEL), lambda i: (i, 0)), _const_spec(gains.shape),
                  _const_spec(w.shape)],
        out_specs=[pl.BlockSpec((tm, n), lambda i: (i, 0))] * 2 + [v_spec],
        out_shape=[jax.ShapeDtypeStruct((rows, n), BF16)] * 2 + [v_shape],
        compiler_params=pltpu.CompilerParams(
            dimension_semantics=("arbitrary",), vmem_limit_bytes=VMEM_LIMIT),
        name="proj_na",
    )(h, gains, w)


NA_SPAN_R = NA_WIN_R + 2
NA_MASKED = 2 * NA_WIN_R - 1


def _na_kernel(q_ref, k_ref, vt_ref, km_ref, vmt_ref, bias_ref, mb_ref, o_ref, s_ref,
               *, rows, rows_per_step):
    step = pl.program_id(1)
    n_pairs = rows_per_step // 2
    n_hp = NA_HEADS // 2
    span = NA_SPAN_R * GRID_W
    lane = lax.broadcasted_iota(jnp.int32, (GRID_W, LANE), 1)
    first = lane < (LANE // 2)
    zeros_m = jnp.zeros((LANE - N_META, 2 * LANE), F32)
    ones_v = jnp.ones((2 * SUBLANE, span + LANE), BF16)

    def geometry(rp):
        ra = step * rows_per_step + 2 * rp
        rs = [jnp.clip(ra + x - NA_WIN_R // 2, 0, rows - NA_WIN_R) for x in range(2)]
        ws = jnp.minimum((rs[0] // 2) * 2, rows - NA_SPAN_R)
        return ra, rs, ws

    n_slots = s_ref.shape[0]

    def scores(rp, hp):
        slot = hp % n_slots
        _, _, ws = geometry(rp)
        cols = slice(hp * LANE, (hp + 1) * LANE)
        parts = []
        for x in range(2):
            q0 = pl.multiple_of((2 * rp + x) * GRID_W, GRID_W)
            qx = q_ref[pl.ds(q0, GRID_W), cols]
            parts += [jnp.where(first, qx, jnp.zeros_like(qx)), jnp.where(first, jnp.zeros_like(qx), qx)]
        qblk = jnp.concatenate(parts, axis=0)
        k0 = pl.multiple_of(ws * GRID_W, 2 * GRID_W)
        s_ref[slot, 0:span, :] = _dot_nt(k_ref[pl.ds(k0, span), cols], qblk)
        s_ref[slot, span:span + N_META, :] = _dot_nt(km_ref[:, cols], qblk)

    for hp in range(NA_LOOKAHEAD):
        scores(0, hp)

    def row_pair(rp, carry):
        ra, rs, ws = geometry(rp)
        idx = []
        for jj in range(NA_SPAN_R):
            kr = ws + jj
            idx.append([jnp.where((kr >= rs[x]) & (kr < rs[x] + NA_WIN_R),
                                  kr - (ra + x) + NA_WIN_R - 1, NA_MASKED) for x in range(2)])
        t0 = ws // 2
        for hp in range(n_hp):
            slot = hp % n_slots
            ahead = hp + NA_LOOKAHEAD
            if ahead < n_hp:
                scores(rp, ahead)
            else:
                scores(jnp.minimum(rp + 1, n_pairs - 1), ahead - n_hp)
            cols = slice(hp * LANE, (hp + 1) * LANE)
            b = jnp.concatenate(
                [jnp.concatenate([bias_ref[hp, idx[jj][0]], bias_ref[hp, idx[jj][1]]], axis=1)
                 for jj in range(NA_SPAN_R)], axis=0)
            s = s_ref[slot, 0:span, :] + b
            mb = mb_ref[hp]
            sm = s_ref[slot, span:span + N_META, :] + jnp.concatenate([mb, mb], axis=1)
            s3 = s.reshape(span // SUBLANE, SUBLANE, 2 * LANE)
            sm3 = sm.reshape(N_META // SUBLANE, SUBLANE, 2 * LANE)
            m = _sublane_bcast_max(jnp.maximum(jnp.max(s3, axis=0), jnp.max(sm3, axis=0)))
            p = jnp.exp2(s3 - m[None]).reshape(span, 2 * LANE).astype(BF16)
            pm3 = jnp.exp2(sm3 - m[None])
            pm = jnp.concatenate([pm3.reshape(N_META, 2 * LANE), zeros_m], axis=0).astype(BF16)
            v_all = jnp.concatenate([vt_ref[t0 + t, cols, :] for t in range(span // LANE)]
                                    + [vmt_ref[cols, :]], axis=1)
            o_t = _dot(jnp.concatenate([v_all, ones_v], axis=0),
                       jnp.concatenate([p, pm], axis=0))
            o_t = o_t[0:LANE] / o_t[LANE:LANE + 1]
            for x in range(2):
                blk = o_t[:, x * LANE:(x + 1) * LANE].T
                q0 = pl.multiple_of((2 * rp + x) * GRID_W, GRID_W)
                o_ref[pl.ds(q0, GRID_W), cols] = jnp.where(
                    first, blk[0:GRID_W], blk[GRID_W:2 * GRID_W]).astype(BF16)
        return carry

    lax.fori_loop(0, n_pairs, row_pair, 0, unroll=4)


def _na_attn(q, k, vt, km, vmt, bias, mb, *, n_seq, seq, meta_base):
    rows = seq // GRID_W
    assert rows >= NA_SPAN_R and rows % 2 == 0
    rps = _row_tile(rows, NA_ROWS_PER_STEP)
    nsteps = rows // rps
    n = NA_HEADS * NA_HEAD_DIM
    span = NA_SPAN_R * GRID_W
    return pl.pallas_call(
        functools.partial(_na_kernel, rows=rows, rows_per_step=rps),
        grid=(n_seq, nsteps),
        in_specs=[
            pl.BlockSpec((rps * GRID_W, n), lambda b, i: (b * nsteps + i, 0)),
            pl.BlockSpec((seq, n), lambda b, i: (b, 0), pipeline_mode=pl.Buffered(1)),
            pl.BlockSpec((seq // LANE, n, LANE), lambda b, i: (b, 0, 0), pipeline_mode=pl.Buffered(1)),
            pl.BlockSpec((None, N_META, n), lambda b, i: (meta_base + b, 0, 0)),
            pl.BlockSpec((None, n, LANE), lambda b, i: (meta_base + b, 0, 0)),
            _const_spec(bias.shape),
            _const_spec(mb.shape),
        ],
        out_specs=pl.BlockSpec((rps * GRID_W, n), lambda b, i: (b * nsteps + i, 0)),
        out_shape=jax.ShapeDtypeStruct((n_seq * seq, n), BF16),
        scratch_shapes=[pltpu.VMEM((NA_SLOTS, span + N_META, 2 * LANE), F32)],
        compiler_params=pltpu.CompilerParams(
            dimension_semantics=("arbitrary", "arbitrary"), vmem_limit_bytes=VMEM_LIMIT),
        name="na_attn",
    )(q, k, vt, km, vmt, bias, mb)


def _na_meta_kernel(q_ref, km_ref, vm_ref, mb_ref, o_ref):
    lane = lax.broadcasted_iota(jnp.int32, (N_META, LANE), 1)
    first = lane < (LANE // 2)
    for j in range(NA_HEADS // 2):
        cols = slice(j * LANE, (j + 1) * LANE)
        qp = q_ref[:, cols]
        km = km_ref[:, cols]
        vm = vm_ref[:, cols]
        outs = []
        for half in range(2):
            h = 2 * j + half
            qh = jnp.where(first if half == 0 else jnp.logical_not(first), qp, jnp.zeros_like(qp))
            sm = _dot_nt(qh, km)
            sm = jnp.where(lane < N_META, sm + mb_ref[h:h + 1, :], NEG_INF)
            m = jnp.max(sm, axis=-1, keepdims=True)
            pm = jnp.exp2(sm - m)
            l = jnp.sum(pm, axis=-1, keepdims=True)
            outs.append(_dot(pm.astype(BF16), vm) / l)
        o_ref[:, cols] = jnp.where(first, outs[0], outs[1]).astype(BF16)


def _na_meta(qm, km, vm, mb):
    n_seq = km.shape[0]
    n = NA_HEADS * NA_HEAD_DIM
    return pl.pallas_call(
        _na_meta_kernel,
        grid=(n_seq,),
        in_specs=[
            pl.BlockSpec((N_META, n), lambda b: (b, 0)),
            pl.BlockSpec((None, LANE, n), lambda b: (b, 0, 0)),
            pl.BlockSpec((None, LANE, n), lambda b: (b, 0, 0)),
            _const_spec(mb.shape),
        ],
        out_specs=pl.BlockSpec((N_META, n), lambda b: (b, 0)),
        out_shape=jax.ShapeDtypeStruct((n_seq * N_META, n), BF16),
        compiler_params=pltpu.CompilerParams(dimension_semantics=("arbitrary",)),
        name="na_meta",
    )(qm, km, vm, mb)


def _take_cols(w, idx):
    idx = np.asarray(idx)
    neg = idx < 0
    same_run = np.where(neg[1:] | neg[:-1], neg[1:] & neg[:-1], np.diff(idx) == 1)
    breaks = np.flatnonzero(~same_run) + 1
    parts = []
    for run in np.split(idx, breaks):
        if run[0] < 0:
            parts.append(jnp.zeros((w.shape[0], len(run)), w.dtype))
        else:
            parts.append(w[:, int(run[0]):int(run[-1]) + 1])
    return jnp.concatenate(parts, axis=1)


def _swap_halves(n):
    half = n // 2
    return np.concatenate([np.arange(half, n), np.arange(0, half)])


def _dense_weights(w_in, q_norm, w_uq, kv_norm, w_ukv, gq_norm, gk_norm, w_out):
    pad = lambda k: -np.ones(k, np.int64)
    o_kr = MLA_Q_LORA + MLA_KV_LORA
    o_gq = o_kr + MLA_ROPE
    o_gk = o_gq + GQA_HEADS * GQA_HEAD_DIM
    o_gv = o_gk + GQA_KV_HEADS * GQA_HEAD_DIM
    axial = np.concatenate([_swap_halves(GQA_HEAD_DIM // 2),
                            GQA_HEAD_DIM // 2 + _swap_halves(GQA_HEAD_DIM // 2)])
    idx = [np.arange(0, o_kr)]
    idx += [pad(MLA_NOPE), o_kr + np.arange(MLA_ROPE), pad(LANE - MLA_NOPE - MLA_ROPE)]
    idx += [pad(MLA_NOPE), o_kr + _swap_halves(MLA_ROPE), pad(LANE - MLA_NOPE - MLA_ROPE)]
    assert 2 * GQA_HEAD_DIM == LANE and GQA_KV_HEADS == 2
    for h in range(GQA_HEADS):
        idx += [o_gq + h * GQA_HEAD_DIM + np.arange(GQA_HEAD_DIM)]
    for h in range(GQA_HEADS):
        idx += [o_gq + h * GQA_HEAD_DIM + axial]
    for h in range(GQA_KV_HEADS):
        idx += [o_gk + h * GQA_HEAD_DIM + np.arange(GQA_HEAD_DIM)]
    for h in range(GQA_KV_HEADS):
        idx += [o_gk + h * GQA_HEAD_DIM + axial]
    idx += [o_gv + np.arange(GQA_KV_HEADS * GQA_HEAD_DIM)]
    idx = np.concatenate(idx)
    assert idx.shape[0] == _C_END
    w_in2 = _take_cols(w_in.astype(BF16), idx)

    hd = MLA_NOPE + MLA_ROPE
    ia, ib = [], []
    for h in range(MLA_HEADS):
        ia += [h * hd + np.arange(hd), pad(LANE - hd)]
        ib += [pad(MLA_NOPE), h * hd + MLA_NOPE + _swap_halves(MLA_ROPE), pad(LANE - hd)]
    w_uq2 = _take_cols(w_uq.astype(BF16), np.concatenate(ia + ib))

    kvd = MLA_NOPE + MLA_V
    ik, iv = [], []
    for h in range(MLA_HEADS):
        ik += [h * kvd + np.arange(MLA_NOPE), pad(LANE - MLA_NOPE)]
        iv += [h * kvd + MLA_NOPE + np.arange(MLA_V)]
    w_ukv2 = _take_cols(w_ukv.astype(BF16), np.concatenate(ik + iv))

    def gain_pair(g):
        ga = jnp.concatenate([g, g])[None, :]
        gb = jnp.tile(g[jnp.asarray(axial)], 2)[None, :]
        return ga, gb

    gq_a, gq_b = gain_pair(gq_norm)
    gk_a, gk_b = gain_pair(gk_norm)

    w_out2 = w_out.astype(BF16)

    return dict(w_in=w_in2, q_norm=q_norm[None, :], w_uq=w_uq2, kv_norm=kv_norm[None, :],
                w_ukv=w_ukv2, gq_a=gq_a, gq_b=gq_b, gk_a=gk_a, gk_b=gk_b,
                rope_expand=_rope_expand_matrix()), w_out2


def _rope_tables(pos, row, row_repeat, col, col_tile):
    half = MLA_ROPE // 2
    inv = 1.0 / (ROPE_THETA ** (jnp.arange(half, dtype=F32) / half))
    n = pos.shape[0]

    def cs(p):
        ang = p.astype(F32)[None, :] * inv[:, None]
        return jnp.cos(ang), jnp.sin(ang)

    by_row = tuple(jnp.repeat(x, row_repeat, axis=1) for x in cs(row))
    by_col = tuple(jnp.tile(x, (1, col_tile)) for x in cs(col))
    vals = jnp.concatenate(cs(pos) + by_row + by_col + (jnp.zeros((LANE - 6 * half, n), F32),), axis=0).T
    hi = vals.astype(BF16)
    rest = vals - hi.astype(F32)
    mid = rest.astype(BF16)
    lo = (rest - mid.astype(F32)).astype(BF16)
    return jnp.stack([hi, mid, lo])


def _rope_expand_matrix():
    half = MLA_ROPE // 2
    c1, s1, cr, sr, cc, sc = range(6)
    tail = LANE - MLA_NOPE - MLA_ROPE
    blank = [(None, 0)]
    layout = (blank * (MLA_NOPE // half) + [(c1, 1), (c1, 1)] + blank * (tail // half)
              + blank * (MLA_NOPE // half) + [(s1, -1), (s1, 1)] + blank * (tail // half)
              + [(cr, 1), (cr, 1), (cc, 1), (cc, 1)] * (LANE // GQA_HEAD_DIM)
              + [(sr, -1), (sr, 1), (sc, -1), (sc, 1)] * (LANE // GQA_HEAD_DIM))
    expand = np.zeros((LANE, len(layout) * half), np.float32)
    for blk, (src, sign) in enumerate(layout):
        if src is not None:
            expand[src * half + np.arange(half), blk * half + np.arange(half)] = sign
    return jnp.asarray(np.tile(expand, (3, 1)), BF16)


def _na_bias_tables(rpb, meta_bias):
    c_idx = np.arange(GRID_W)
    c_start = np.clip(c_idx - NA_WIN_C // 2, 0, GRID_W - NA_WIN_C)
    col_mask = (c_idx[None, :] >= c_start[:, None]) & (c_idx[None, :] < c_start[:, None] + NA_WIN_C)
    col_off = np.clip(c_idx[None, :] - c_idx[:, None] + NA_WIN_C - 1, 0, 2 * NA_WIN_C - 2)
    hp = NA_HEADS // 2
    n_off = 2 * NA_WIN_C - 1
    select = np.zeros((2, n_off, GRID_W, 2, GRID_W), np.float32)
    kc_g, c_g = np.meshgrid(c_idx, c_idx, indexing="ij")
    for half in range(2):
        select[half, col_off[c_g, kc_g], kc_g, half, c_g] = 1.0
    rows = rpb.reshape(hp, 2, NA_MASKED, n_off).transpose(0, 2, 1, 3).reshape(hp * NA_MASKED, 2 * n_off)
    t = jnp.dot(rows, jnp.asarray(select.reshape(2 * n_off, GRID_W * LANE)),
                precision=lax.Precision.HIGHEST) * LOG2E
    t = t.reshape(hp, NA_MASKED, GRID_W, LANE)
    keep = np.tile(col_mask.T, (1, 2))
    t = jnp.where(jnp.asarray(keep)[None, None], t, NEG_INF)
    bias = jnp.concatenate([t, jnp.full_like(t[:, :1], NEG_INF)], axis=1)
    mbl = meta_bias * LOG2E
    mb_t = jnp.repeat(mbl.reshape(hp, 2, N_META).transpose(0, 2, 1), GRID_W, axis=2)
    mb = jnp.pad(mbl, ((0, 0), (0, LANE - N_META)))
    return bias, mb_t, mb


def _pad_meta(x, n_seq):
    c = x.shape[1]
    return jnp.pad(x.reshape(n_seq, N_META, c), ((0, 0), (0, LANE - N_META), (0, 0)))


def kernel(x_prompt, x_sample, meta, norm_gains, ffn1_w_gate, ffn1_w_up, ffn1_w_down, ffn2_w_gate, ffn2_w_up, ffn2_w_down, attn_w_in, mla_q_norm, mla_w_uq, mla_kv_norm, mla_w_ukv, gqa_q_norm, gqa_k_norm, attn_w_out, na_w_qkv, na_rpb, na_meta_bias, na_w_out):
    bp, sp, _ = x_prompt.shape
    bs, ss, _ = x_sample.shape
    n_seq = bp + bs
    depth = norm_gains.shape[0]
    groups = [(bp, sp, 0), (bs, ss, bp)]

    n_meta = n_seq * N_META
    meta_rows = -(-n_meta // LANE) * LANE
    pad_rows = lambda x: jnp.pad(x, ((0, meta_rows - x.shape[0]), (0, 0)))
    h_tok = [x_prompt.reshape(bp * sp, D_MODEL), x_sample.reshape(bs * ss, D_MODEL)]
    h_meta = pad_rows(jnp.tile(meta.astype(F32), (n_seq, 1)))

    smax = max(sp, ss)
    tab_tok = _rope_tables(jnp.arange(smax) + N_META, jnp.arange(smax // GRID_W), GRID_W,
                           jnp.arange(GRID_W), smax // GRID_W)
    tab_meta = _rope_tables(jnp.arange(meta_rows) % N_META, jnp.full((1,), -1), meta_rows,
                            jnp.arange(N_META), meta_rows // N_META)

    w1 = (ffn1_w_gate.astype(BF16), ffn1_w_up.astype(BF16), ffn1_w_down.astype(BF16))
    w2 = (ffn2_w_gate.astype(BF16), ffn2_w_up.astype(BF16), ffn2_w_down.astype(BF16))

    for i in range(depth):
        gains = jnp.pad(norm_gains[i], ((0, 2), (0, 0)))
        j = i // 2
        h_tok = [_ffn1(h, gains, *w1, i) for h in h_tok]
        h_meta = _ffn1(h_meta, gains, *w1, i)
        if i % 2 == 0:
            w, w_out = _dense_weights(attn_w_in[j], mla_q_norm[j], mla_w_uq[j], mla_kv_norm[j],
                                      mla_w_ukv[j], gqa_q_norm[j], gqa_k_norm[j], attn_w_out[j])
            qkv_tok = [_proj_dense(h, gains, w, tab_tok, s) for h, (_, s, _) in zip(h_tok, groups)]
            qmt, km, vmt = _proj_dense(h_meta, gains, w, tab_meta, 0)
            kmp = _pad_meta(km[:n_meta], n_seq)
            vmt = vmt.transpose(1, 0, 2).reshape(V_ROWS, meta_rows)
            vmtp = vmt[:, :n_meta].reshape(V_ROWS, n_seq, N_META).transpose(1, 0, 2)
            vmtp = jnp.pad(vmtp, ((0, 0), (0, 0), (0, LANE - N_META)))
            qmt = qmt.transpose(1, 0, 2).reshape(HEAD_SLOTS * LANE, meta_rows)
            qmtp = qmt[:, :n_meta].reshape(HEAD_SLOTS * LANE, n_seq, N_META).transpose(1, 0, 2)
            qmtp = jnp.pad(qmtp, ((0, 0), (0, 0), (0, LANE - N_META)))
            o_tok, o_meta = [], []
            for (qt, k, vt), (nb, s, b0) in zip(qkv_tok, groups):
                o_tok.append(_dense_attn(qt, k, vt, kmp, vmtp, n_seq=nb, seq=s, nq=qt.shape[0] // nb,
                                         q_base=0, meta_base=b0))
                om = _dense_attn(qmtp, k, vt, kmp, vmtp, n_seq=nb, seq=s, nq=1, q_base=b0, meta_base=b0)
                o_meta.append(om.reshape(nb, LANE, D_MODEL)[:, :N_META].reshape(nb * N_META, D_MODEL))
            o_meta = pad_rows(jnp.concatenate(o_meta, axis=0))
        else:
            w_qkv = na_w_qkv[j].astype(BF16)
            w_out = na_w_out[j].astype(BF16)
            bias, mb_t, mb = _na_bias_tables(na_rpb[j], na_meta_bias[j])
            qkv_tok = [_proj_na(h, gains, w_qkv, True) for h in h_tok]
            qm, km, vm = _proj_na(h_meta, gains, w_qkv, False)
            kmp, vmp = _pad_meta(km[:n_meta], n_seq), _pad_meta(vm[:n_meta], n_seq)
            km16 = km[:n_meta].reshape(n_seq, N_META, NA_HEADS * NA_HEAD_DIM)
            vmtp = vmp.transpose(0, 2, 1)
            o_tok = [_na_attn(q, k, vt, km16, vmtp, bias, mb_t, n_seq=nb, seq=s, meta_base=b0)
                     for (q, k, vt), (nb, s, b0) in zip(qkv_tok, groups)]
            o_meta = pad_rows(_na_meta(qm[:n_meta], kmp, vmp, mb))
        h_tok = [_mix_ffn2(h, o, w_out, gains, *w2, i) for h, o in zip(h_tok, o_tok)]
        h_meta = _mix_ffn2(h_meta, o_meta, w_out, gains, *w2, i)

    return (h_tok[0].reshape(bp, sp, D_MODEL), h_tok[1].reshape(bs, ss, D_MODEL))
```

```python
import functools
import math

import jax
import jax.numpy as jnp
import numpy as np
from jax import lax
from jax.experimental import pallas as pl
from jax.experimental.pallas import tpu as pltpu

F32 = jnp.float32
BF16 = jnp.bfloat16

D_MODEL = 1024
N_META = 16
GRID_W = 64
D_FF = 2816
EPS = 1e-6
NEG_INF = -1e30
LOG2E = math.log2(math.e)

MLA_HEADS = 8
MLA_Q_LORA = 256
MLA_KV_LORA = 128
MLA_NOPE = 64
MLA_ROPE = 32
MLA_V = 64
GQA_HEADS = 8
GQA_KV_HEADS = 2
GQA_HEAD_DIM = 64
ROPE_THETA = 10000.0
NA_HEADS = 16
NA_HEAD_DIM = 64
NA_WIN_R = 8
NA_WIN_C = 16

LANE = 128
HEAD_SLOTS = MLA_HEADS + GQA_HEADS
K_SLOTS = MLA_HEADS + 1
V_ROWS = (MLA_HEADS + GQA_KV_HEADS) * MLA_V
SUBLANE = 8
DENSE_LOOKAHEAD, DENSE_SLOTS = 2, 4
NA_LOOKAHEAD, NA_SLOTS = 2, 4
VMEM_LIMIT = 56 * 1024 * 1024
ROW_TILE = 512
FFN_CHAINS = 2
DENSE_SUBTILES = 8
NA_ROWS_PER_STEP = 16

_C_CQ = 0
_C_CKV = _C_CQ + MLA_Q_LORA
_C_KRA = _C_CKV + MLA_KV_LORA
_C_KRB = _C_KRA + LANE
_C_GQA = _C_KRB + LANE
_C_GQB = _C_GQA + GQA_HEADS // 2 * LANE
_C_GKA = _C_GQB + GQA_HEADS // 2 * LANE
_C_GKB = _C_GKA + LANE
_C_GV = _C_GKB + LANE
_C_END = _C_GV + LANE


def _const_spec(shape):
    nd = len(shape)
    return pl.BlockSpec(shape, lambda *_: (0,) * nd, pipeline_mode=pl.Buffered(1))


def _rms(x, g):
    ms = jnp.mean(x * x, axis=-1, keepdims=True)
    return x * lax.rsqrt(ms + EPS) * g


def _dot(a, b):
    return jnp.dot(a, b, preferred_element_type=F32)


def _dot_nt(a, b):
    return lax.dot_general(a, b, (((1,), (1,)), ((), ())), preferred_element_type=F32)


def _row_tile(rows, want):
    t = min(rows, want)
    while rows % t:
        t //= 2
    return t


def _ffn_rows(h, g_ref, pre, post, wg_ref, wu_ref, wd_ref):
    xn = _rms(h, g_ref[pre:pre + 1, :]).astype(BF16)
    gate = _dot(xn, wg_ref[...])
    up = _dot(xn, wu_ref[...])
    act = (gate * jax.nn.sigmoid(gate) * up).astype(BF16)
    y = _dot(act, wd_ref[...])
    return h + 0.5 * _rms(y, g_ref[post:post + 1, :])


def _ffn_body(h, g_ref, pre, post, wg_ref, wu_ref, wd_ref):
    part = h.shape[0] // FFN_CHAINS
    return jnp.concatenate([_ffn_rows(h[i * part:(i + 1) * part], g_ref, pre, post, wg_ref, wu_ref, wd_ref)
                            for i in range(FFN_CHAINS)], axis=0)


def _ffn1_kernel(h_ref, g_ref, wg_ref, wu_ref, wd_ref, out_ref):
    out_ref[...] = _ffn_body(h_ref[...], g_ref, 0, 1, wg_ref, wu_ref, wd_ref)


def _mix_ffn2_kernel(h_ref, o_ref, wo_ref, g_ref, wg_ref, wu_ref, wd_ref, out_ref):
    mixed = _dot(o_ref[...], wo_ref[...])
    h = h_ref[...] + _rms(mixed, g_ref[3:4, :])
    out_ref[...] = _ffn_body(h, g_ref, 4, 5, wg_ref, wu_ref, wd_ref)


def _layer_spec(w, layer):
    return pl.BlockSpec((None,) + w.shape[1:], lambda *_: (layer, 0, 0), pipeline_mode=pl.Buffered(1))


def _ffn1(h, gains, wg, wu, wd, layer):
    rows = h.shape[0]
    tm = _row_tile(rows, ROW_TILE)
    return pl.pallas_call(
        _ffn1_kernel,
        grid=(rows // tm,),
        in_specs=[
            pl.BlockSpec((tm, D_MODEL), lambda i: (i, 0)),
            _const_spec(gains.shape),
            _layer_spec(wg, layer), _layer_spec(wu, layer), _layer_spec(wd, layer),
        ],
        out_specs=pl.BlockSpec((tm, D_MODEL), lambda i: (i, 0)),
        out_shape=jax.ShapeDtypeStruct((rows, D_MODEL), F32),
        compiler_params=pltpu.CompilerParams(
            dimension_semantics=("arbitrary",), vmem_limit_bytes=VMEM_LIMIT),
        name="ffn1",
    )(h, gains, wg, wu, wd)


def _mix_ffn2(h, o, wo, gains, wg, wu, wd, layer):
    rows = h.shape[0]
    tm = _row_tile(rows, ROW_TILE)
    return pl.pallas_call(
        _mix_ffn2_kernel,
        grid=(rows // tm,),
        in_specs=[
            pl.BlockSpec((tm, D_MODEL), lambda i: (i, 0)),
            pl.BlockSpec((tm, o.shape[1]), lambda i: (i, 0)),
            _const_spec(wo.shape),
            _const_spec(gains.shape),
            _layer_spec(wg, layer), _layer_spec(wu, layer), _layer_spec(wd, layer),
        ],
        out_specs=pl.BlockSpec((tm, D_MODEL), lambda i: (i, 0)),
        out_shape=jax.ShapeDtypeStruct((rows, D_MODEL), F32),
        compiler_params=pltpu.CompilerParams(
            dimension_semantics=("arbitrary",), vmem_limit_bytes=VMEM_LIMIT),
        name="mix_ffn2",
    )(h, o, wo, gains, wg, wu, wd)


def _proj_dense_kernel(h_ref, g_ref, win_ref, qn_ref, wuq_ref, kvn_ref, wukv_ref,
                       gqa_ref, gqb_ref, gka_ref, gkb_ref, exp_ref, tab_ref, qt_ref, k_ref, vt_ref):
    a = _rms(h_ref[...], g_ref[2:3, :]).astype(BF16)
    proj = _dot(a, win_ref[...])
    pieces = jnp.concatenate([tab_ref[i] for i in range(tab_ref.shape[0])], axis=1)
    tab = _dot(pieces, exp_ref[...])
    cos_k, sin_k = tab[:, 0:LANE], tab[:, LANE:2 * LANE]
    cos_g, sin_g = tab[:, 2 * LANE:3 * LANE], tab[:, 3 * LANE:4 * LANE]
    qs = (MLA_NOPE + MLA_ROPE) ** -0.5 * LOG2E
    lane = lax.broadcasted_iota(jnp.int32, cos_k.shape, 1)
    cos_q = jnp.where(lane < MLA_NOPE, qs, cos_k * qs)
    sin_q = sin_k * qs

    cqn = _rms(proj[:, _C_CQ:_C_CQ + MLA_Q_LORA], qn_ref[...]).astype(BF16)
    qab = _dot(cqn, wuq_ref[...])
    nq = MLA_HEADS * LANE
    for h in range(MLA_HEADS):
        qa = qab[:, h * LANE:(h + 1) * LANE]
        qb = qab[:, nq + h * LANE:nq + (h + 1) * LANE]
        qt_ref[0, h * LANE:(h + 1) * LANE, :] = (qa * cos_q + qb * sin_q).T.astype(BF16)

    ckvn = _rms(proj[:, _C_CKV:_C_CKV + MLA_KV_LORA], kvn_ref[...]).astype(BF16)
    kv = _dot(ckvn, wukv_ref[...])
    k_rope = (proj[:, _C_KRA:_C_KRA + LANE] * cos_k + proj[:, _C_KRB:_C_KRB + LANE] * sin_k)
    for h in range(MLA_HEADS):
        k_ref[:, h * LANE:(h + 1) * LANE] = (kv[:, h * LANE:(h + 1) * LANE] + k_rope).astype(BF16)
    nv = MLA_HEADS * MLA_V

    low = lane < GQA_HEAD_DIM

    def normed_rotary(xa, xb, cos, sin):
        sq = xa * xa
        ss_lo = jnp.sum(jnp.where(low, sq, 0.0), axis=-1, keepdims=True)
        ss_hi = jnp.sum(jnp.where(low, 0.0, sq), axis=-1, keepdims=True)
        r = lax.rsqrt(jnp.where(low, ss_lo, ss_hi) * (1.0 / GQA_HEAD_DIM) + EPS)
        return (xa * cos + xb * sin) * r

    gq_scale = GQA_HEAD_DIM ** -0.5 * LOG2E
    cq_g = cos_g * (gqa_ref[...] * gq_scale)
    sq_g = sin_g * (gqb_ref[...] * gq_scale)
    per_kv = GQA_HEADS // GQA_KV_HEADS
    zeros_t = jnp.zeros((LANE - GQA_HEAD_DIM, qt_ref.shape[2]), F32)
    for j in range(GQA_HEADS // 2):
        xa = proj[:, _C_GQA + j * LANE:_C_GQA + (j + 1) * LANE]
        xb = proj[:, _C_GQB + j * LANE:_C_GQB + (j + 1) * LANE]
        y_t = normed_rotary(xa, xb, cq_g, sq_g).T
        for half in range(2):
            h = 2 * j + half
            q_t = y_t[half * GQA_HEAD_DIM:(half + 1) * GQA_HEAD_DIM]
            rows = [q_t, zeros_t] if h // per_kv == 0 else [zeros_t, q_t]
            qt_ref[0, nq + h * LANE:nq + (h + 1) * LANE, :] = jnp.concatenate(rows, axis=0).astype(BF16)
    xa = proj[:, _C_GKA:_C_GKA + LANE]
    xb = proj[:, _C_GKB:_C_GKB + LANE]
    k_ref[:, nq:nq + LANE] = normed_rotary(xa, xb, cos_g * gka_ref[...], sin_g * gkb_ref[...]).astype(BF16)
    v = jnp.concatenate([kv[:, nq:nq + nv], proj[:, _C_GV:_C_GV + LANE]], axis=1)
    vt_ref[0] = v.T.astype(BF16)


def _proj_dense(h, gains, w, tab, seq):
    rows = h.shape[0]
    tm = _row_tile(seq if seq else rows, ROW_TILE)
    nblk = (seq // tm) if seq else 1
    consts = [w["w_in"], w["q_norm"], w["w_uq"], w["kv_norm"], w["w_ukv"],
              w["gq_a"], w["gq_b"], w["gk_a"], w["gk_b"], w["rope_expand"]]
    return pl.pallas_call(
        _proj_dense_kernel,
        grid=(rows // tm,),
        in_specs=[pl.BlockSpec((tm, D_MODEL), lambda i: (i, 0)), _const_spec(gains.shape)]
        + [_const_spec(c.shape) for c in consts]
        + [pl.BlockSpec((tab.shape[0], tm, LANE), lambda i: (0, i % nblk, 0))],
        out_specs=[
            pl.BlockSpec((1, HEAD_SLOTS * LANE, tm), lambda i: (i, 0, 0)),
            pl.BlockSpec((tm, K_SLOTS * LANE), lambda i: (i, 0)),
            pl.BlockSpec((1, V_ROWS, tm), lambda i: (i, 0, 0)),
        ],
        out_shape=[
            jax.ShapeDtypeStruct((rows // tm, HEAD_SLOTS * LANE, tm), BF16),
            jax.ShapeDtypeStruct((rows, K_SLOTS * LANE), BF16),
            jax.ShapeDtypeStruct((rows // tm, V_ROWS, tm), BF16),
        ],
        compiler_params=pltpu.CompilerParams(
            dimension_semantics=("arbitrary",), vmem_limit_bytes=VMEM_LIMIT),
        name="proj_dense",
    )(h, gains, *consts, tab)


def _head_slots(h):
    if h < MLA_HEADS:
        return h, h
    kvh = (h - MLA_HEADS) // (GQA_HEADS // GQA_KV_HEADS)
    return MLA_HEADS, MLA_HEADS + kvh


def _sublane_bcast_max(x):
    return jnp.broadcast_to(jnp.max(x, axis=0, keepdims=True), x.shape)


def _dense_attn_kernel(qt_ref, k_ref, vt_ref, km_ref, vmt_ref, o_ref,
                       m_ref, smax_ref, acc_ref, s_ref, sm_ref):
    kv = pl.program_id(2)
    tq = qt_ref.shape[1]
    n_sub, _, tk = vt_ref.shape
    hd = MLA_V
    acc_rows = acc_ref.shape[1]

    def with_ones(vt):
        return jnp.concatenate([vt, jnp.ones((acc_rows - hd, vt.shape[1]), BF16)], axis=0)

    @pl.when(kv == 0)
    def _():
        for h in range(HEAD_SLOTS):
            ks, _ = _head_slots(h)
            qt = qt_ref[h * LANE:(h + 1) * LANE, :]
            sm_ref[h] = _dot(km_ref[0:N_META, ks * LANE:(ks + 1) * LANE], qt)
        zeros = jnp.zeros((LANE - N_META, tq), F32)
        for h in range(HEAD_SLOTS):
            _, vh = _head_slots(h)
            s3 = sm_ref[h].reshape(N_META // SUBLANE, SUBLANE, tq)
            m = _sublane_bcast_max(jnp.max(s3, axis=0))
            p3 = jnp.exp2(s3 - m[None])
            m_ref[h] = m
            p = jnp.concatenate([p3.reshape(N_META, tq), zeros], axis=0).astype(BF16)
            acc_ref[h] = _dot(with_ones(vmt_ref[vh * hd:(vh + 1) * hd, :]), p)

    n_slots = s_ref.shape[0]

    def scores(t, h):
        ks, _ = _head_slots(h)
        k0 = pl.multiple_of(t * tk, tk)
        k = k_ref[pl.ds(k0, tk), ks * LANE:(ks + 1) * LANE]
        s = _dot(k, qt_ref[h * LANE:(h + 1) * LANE, :])
        s_ref[h % n_slots] = s
        smax_ref[h] = jnp.max(s.reshape(tk // SUBLANE, SUBLANE, tq), axis=0)

    def softmax_pv(t, h):
        _, vh = _head_slots(h)
        m_prev = m_ref[h]
        m_new = jnp.maximum(m_prev, _sublane_bcast_max(smax_ref[h]))
        alpha = jnp.exp2(m_prev - m_new)
        s3 = s_ref[h % n_slots].reshape(tk // SUBLANE, SUBLANE, tq)
        p = jnp.exp2(s3 - m_new[None]).reshape(tk, tq).astype(BF16)
        pv = _dot(with_ones(vt_ref[t, vh * hd:(vh + 1) * hd, :]), p)
        acc = acc_ref[h].reshape(acc_rows // SUBLANE, SUBLANE, tq) * alpha[None]
        acc_ref[h] = acc.reshape(acc_rows, tq) + pv
        m_ref[h] = m_new

    for h in range(DENSE_LOOKAHEAD):
        scores(0, h)

    def sub_tile(t, carry):
        t_next = jnp.minimum(t + 1, n_sub - 1)
        for h in range(HEAD_SLOTS):
            ahead = h + DENSE_LOOKAHEAD
            if ahead < HEAD_SLOTS:
                scores(t, ahead)
            else:
                scores(t_next, ahead - HEAD_SLOTS)
            softmax_pv(t, h)
        return carry

    lax.fori_loop(0, n_sub, sub_tile, 0, unroll=2)

    @pl.when(kv == pl.num_programs(2) - 1)
    def _():
        for j in range(HEAD_SLOTS // 2):
            outs = []
            for h in (2 * j, 2 * j + 1):
                outs.append(acc_ref[h, 0:hd, :] / acc_ref[h, hd:hd + 1, :])
            o_t = jnp.concatenate(outs, axis=0)
            o_ref[:, j * LANE:(j + 1) * LANE] = o_t.T.astype(BF16)


def _dense_attn(qt, k, vt, km, vmt, *, n_seq, seq, nq, q_base, meta_base):
    tq = qt.shape[2]
    tk = vt.shape[2]
    n_sub = _row_tile(seq // tk, DENSE_SUBTILES)
    nk = seq // (tk * n_sub)
    out_rows = n_seq * nq * tq
    return pl.pallas_call(
        _dense_attn_kernel,
        grid=(n_seq, nq, nk),
        in_specs=[
            pl.BlockSpec((None, HEAD_SLOTS * LANE, tq), lambda b, i, j: (q_base + b * nq + i, 0, 0)),
            pl.BlockSpec((n_sub * tk, K_SLOTS * LANE), lambda b, i, j: (b * nk + j, 0)),
            pl.BlockSpec((n_sub, V_ROWS, tk), lambda b, i, j: (b * nk + j, 0, 0)),
            pl.BlockSpec((None, LANE, K_SLOTS * LANE), lambda b, i, j: (meta_base + b, 0, 0)),
            pl.BlockSpec((None, V_ROWS, LANE), lambda b, i, j: (meta_base + b, 0, 0)),
        ],
        out_specs=pl.BlockSpec((tq, D_MODEL), lambda b, i, j: (b * nq + i, 0)),
        out_shape=jax.ShapeDtypeStruct((out_rows, D_MODEL), BF16),
        scratch_shapes=[
            pltpu.VMEM((HEAD_SLOTS, SUBLANE, tq), F32),
            pltpu.VMEM((HEAD_SLOTS, SUBLANE, tq), F32),
            pltpu.VMEM((HEAD_SLOTS, MLA_V + 2 * SUBLANE, tq), F32),
            pltpu.VMEM((DENSE_SLOTS, tk, tq), F32),
            pltpu.VMEM((HEAD_SLOTS, N_META, tq), F32),
        ],
        compiler_params=pltpu.CompilerParams(
            dimension_semantics=("arbitrary", "arbitrary", "arbitrary"),
            vmem_limit_bytes=VMEM_LIMIT),
        name="dense_attn",
    )(qt, k, vt, km, vmt)


def _proj_na_kernel(h_ref, g_ref, w_ref, q_ref, k_ref, v_ref, *, transpose_v):
    a = _rms(h_ref[...], g_ref[2:3, :]).astype(BF16)
    qkv = _dot(a, w_ref[...])
    n = NA_HEADS * NA_HEAD_DIM
    q_ref[...] = (qkv[:, 0:n] * (NA_HEAD_DIM ** -0.5 * LOG2E)).astype(BF16)
    k_ref[...] = qkv[:, n:2 * n].astype(BF16)
    v = qkv[:, 2 * n:3 * n]
    if transpose_v:
        vt = v.T.astype(BF16)
        for t in range(v_ref.shape[0]):
            v_ref[t] = vt[:, t * LANE:(t + 1) * LANE]
    else:
        v_ref[...] = v.astype(BF16)


def _proj_na(h, gains, w, transpose_v):
    rows = h.shape[0]
    tm = _row_tile(rows, ROW_TILE)
    n = NA_HEADS * NA_HEAD_DIM
    if transpose_v:
        v_spec = pl.BlockSpec((tm // LANE, n, LANE), lambda i: (i, 0, 0))
        v_shape = jax.ShapeDtypeStruct((rows // LANE, n, LANE), BF16)
    else:
        v_spec = pl.BlockSpec((tm, n), lambda i: (i, 0))
        v_shape = jax.ShapeDtypeStruct((rows, n), BF16)
    return pl.pallas_call(
        functools.partial(_proj_na_kernel, transpose_v=transpose_v),
        grid=(rows // tm,),
        in_specs=[pl.BlockSpec((tm, D_MODEL), lambda i: (i, 0)), _const_spec(gains.shape),
                  _const_spec(w.shape)],
        out_specs=[pl.BlockSpec((tm, n), lambda i: (i, 0))] * 2 + [v_spec],
        out_shape=[jax.ShapeDtypeStruct((rows, n), BF16)] * 2 + [v_shape],
        compiler_params=pltpu.CompilerParams(
            dimension_semantics=("arbitrary",), vmem_limit_bytes=VMEM_LIMIT),
        name="proj_na",
    )(h, gains, w)


NA_SPAN_R = NA_WIN_R + 2
NA_MASKED = 2 * NA_WIN_R - 1


def _na_kernel(q_ref, k_ref, vt_ref, km_ref, vmt_ref, bias_ref, mb_ref, o_ref, s_ref,
               *, rows, rows_per_step):
    step = pl.program_id(1)
    n_pairs = rows_per_step // 2
    n_hp = NA_HEADS // 2
    span = NA_SPAN_R * GRID_W
    lane = lax.broadcasted_iota(jnp.int32, (GRID_W, LANE), 1)
    first = lane < (LANE // 2)
    zeros_m = jnp.zeros((LANE - N_META, 2 * LANE), F32)
    ones_v = jnp.ones((2 * SUBLANE, span + LANE), BF16)

    def geometry(rp):
        ra = step * rows_per_step + 2 * rp
        rs = [jnp.clip(ra + x - NA_WIN_R // 2, 0, rows - NA_WIN_R) for x in range(2)]
        ws = jnp.minimum((rs[0] // 2) * 2, rows - NA_SPAN_R)
        return ra, rs, ws

    n_slots = s_ref.shape[0]

    def scores(rp, hp):
        slot = hp % n_slots
        _, _, ws = geometry(rp)
        cols = slice(hp * LANE, (hp + 1) * LANE)
        parts = []
        for x in range(2):
            q0 = pl.multiple_of((2 * rp + x) * GRID_W, GRID_W)
            qx = q_ref[pl.ds(q0, GRID_W), cols]
            parts += [jnp.where(first, qx, jnp.zeros_like(qx)), jnp.where(first, jnp.zeros_like(qx), qx)]
        qblk = jnp.concatenate(parts, axis=0)
        k0 = pl.multiple_of(ws * GRID_W, 2 * GRID_W)
        s_ref[slot, 0:span, :] = _dot_nt(k_ref[pl.ds(k0, span), cols], qblk)
        s_ref[slot, span:span + N_META, :] = _dot_nt(km_ref[:, cols], qblk)

    for hp in range(NA_LOOKAHEAD):
        scores(0, hp)

    def row_pair(rp, carry):
        ra, rs, ws = geometry(rp)
        idx = []
        for jj in range(NA_SPAN_R):
            kr = ws + jj
            idx.append([jnp.where((kr >= rs[x]) & (kr < rs[x] + NA_WIN_R),
                                  kr - (ra + x) + NA_WIN_R - 1, NA_MASKED) for x in range(2)])
        t0 = ws // 2
        for hp in range(n_hp):
            slot = hp % n_slots
            ahead = hp + NA_LOOKAHEAD
            if ahead < n_hp:
                scores(rp, ahead)
            else:
                scores(jnp.minimum(rp + 1, n_pairs - 1), ahead - n_hp)
            cols = slice(hp * LANE, (hp + 1) * LANE)
            b = jnp.concatenate(
                [jnp.concatenate([bias_ref[hp, idx[jj][0]], bias_ref[hp, idx[jj][1]]], axis=1)
                 for jj in range(NA_SPAN_R)], axis=0)
            s = s_ref[slot, 0:span, :] + b
            mb = mb_ref[hp]
            sm = s_ref[slot, span:span + N_META, :] + jnp.concatenate([mb, mb], axis=1)
            s3 = s.reshape(span // SUBLANE, SUBLANE, 2 * LANE)
            sm3 = sm.reshape(N_META // SUBLANE, SUBLANE, 2 * LANE)
            m = _sublane_bcast_max(jnp.maximum(jnp.max(s3, axis=0), jnp.max(sm3, axis=0)))
            p = jnp.exp2(s3 - m[None]).reshape(span, 2 * LANE).astype(BF16)
            pm3 = jnp.exp2(sm3 - m[None])
            pm = jnp.concatenate([pm3.reshape(N_META, 2 * LANE), zeros_m], axis=0).astype(BF16)
            v_all = jnp.concatenate([vt_ref[t0 + t, cols, :] for t in range(span // LANE)]
                                    + [vmt_ref[cols, :]], axis=1)
            o_t = _dot(jnp.concatenate([v_all, ones_v], axis=0),
                       jnp.concatenate([p, pm], axis=0))
            o_t = o_t[0:LANE] / o_t[LANE:LANE + 1]
            for x in range(2):
                blk = o_t[:, x * LANE:(x + 1) * LANE].T
                q0 = pl.multiple_of((2 * rp + x) * GRID_W, GRID_W)
                o_ref[pl.ds(q0, GRID_W), cols] = jnp.where(
                    first, blk[0:GRID_W], blk[GRID_W:2 * GRID_W]).astype(BF16)
        return carry

    lax.fori_loop(0, n_pairs, row_pair, 0, unroll=4)


def _na_attn(q, k, vt, km, vmt, bias, mb, *, n_seq, seq, meta_base):
    rows = seq // GRID_W
    assert rows >= NA_SPAN_R and rows % 2 == 0
    rps = _row_tile(rows, NA_ROWS_PER_STEP)
    nsteps = rows // rps
    n = NA_HEADS * NA_HEAD_DIM
    span = NA_SPAN_R * GRID_W
    return pl.pallas_call(
        functools.partial(_na_kernel, rows=rows, rows_per_step=rps),
        grid=(n_seq, nsteps),
        in_specs=[
            pl.BlockSpec((rps * GRID_W, n), lambda b, i: (b * nsteps + i, 0)),
            pl.BlockSpec((seq, n), lambda b, i: (b, 0), pipeline_mode=pl.Buffered(1)),
            pl.BlockSpec((seq // LANE, n, LANE), lambda b, i: (b, 0, 0), pipeline_mode=pl.Buffered(1)),
            pl.BlockSpec((None, N_META, n), lambda b, i: (meta_base + b, 0, 0)),
            pl.BlockSpec((None, n, LANE), lambda b, i: (meta_base + b, 0, 0)),
            _const_spec(bias.shape),
            _const_spec(mb.shape),
        ],
        out_specs=pl.BlockSpec((rps * GRID_W, n), lambda b, i: (b * nsteps + i, 0)),
        out_shape=jax.ShapeDtypeStruct((n_seq * seq, n), BF16),
        scratch_shapes=[pltpu.VMEM((NA_SLOTS, span + N_META, 2 * LANE), F32)],
        compiler_params=pltpu.CompilerParams(
            dimension_semantics=("arbitrary", "arbitrary"), vmem_limit_bytes=VMEM_LIMIT),
        name="na_attn",
    )(q, k, vt, km, vmt, bias, mb)


def _na_meta_kernel(q_ref, km_ref, vm_ref, mb_ref, o_ref):
    lane = lax.broadcasted_iota(jnp.int32, (N_META, LANE), 1)
    first = lane < (LANE // 2)
    for j in range(NA_HEADS // 2):
        cols = slice(j * LANE, (j + 1) * LANE)
        qp = q_ref[:, cols]
        km = km_ref[:, cols]
        vm = vm_ref[:, cols]
        outs = []
        for half in range(2):
            h = 2 * j + half
            qh = jnp.where(first if half == 0 else jnp.logical_not(first), qp, jnp.zeros_like(qp))
            sm = _dot_nt(qh, km)
            sm = jnp.where(lane < N_META, sm + mb_ref[h:h + 1, :], NEG_INF)
            m = jnp.max(sm, axis=-1, keepdims=True)
            pm = jnp.exp2(sm - m)
            l = jnp.sum(pm, axis=-1, keepdims=True)
            outs.append(_dot(pm.astype(BF16), vm) / l)
        o_ref[:, cols] = jnp.where(first, outs[0], outs[1]).astype(BF16)


def _na_meta(qm, km, vm, mb):
    n_seq = km.shape[0]
    n = NA_HEADS * NA_HEAD_DIM
    return pl.pallas_call(
        _na_meta_kernel,
        grid=(n_seq,),
        in_specs=[
            pl.BlockSpec((N_META, n), lambda b: (b, 0)),
            pl.BlockSpec((None, LANE, n), lambda b: (b, 0, 0)),
            pl.BlockSpec((None, LANE, n), lambda b: (b, 0, 0)),
            _const_spec(mb.shape),
        ],
        out_specs=pl.BlockSpec((N_META, n), lambda b: (b, 0)),
        out_shape=jax.ShapeDtypeStruct((n_seq * N_META, n), BF16),
        compiler_params=pltpu.CompilerParams(dimension_semantics=("arbitrary",)),
        name="na_meta",
    )(qm, km, vm, mb)


def _take_cols(w, idx):
    idx = np.asarray(idx)
    neg = idx < 0
    same_run = np.where(neg[1:] | neg[:-1], neg[1:] & neg[:-1], np.diff(idx) == 1)
    breaks = np.flatnonzero(~same_run) + 1
    parts = []
    for run in np.split(idx, breaks):
        if run[0] < 0:
            parts.append(jnp.zeros((w.shape[0], len(run)), w.dtype))
        else:
            parts.append(w[:, int(run[0]):int(run[-1]) + 1])
    return jnp.concatenate(parts, axis=1)


def _swap_halves(n):
    half = n // 2
    return np.concatenate([np.arange(half, n), np.arange(0, half)])


def _dense_weights(w_in, q_norm, w_uq, kv_norm, w_ukv, gq_norm, gk_norm, w_out):
    pad = lambda k: -np.ones(k, np.int64)
    o_kr = MLA_Q_LORA + MLA_KV_LORA
    o_gq = o_kr + MLA_ROPE
    o_gk = o_gq + GQA_HEADS * GQA_HEAD_DIM
    o_gv = o_gk + GQA_KV_HEADS * GQA_HEAD_DIM
    axial = np.concatenate([_swap_halves(GQA_HEAD_DIM // 2),
                            GQA_HEAD_DIM // 2 + _swap_halves(GQA_HEAD_DIM // 2)])
    idx = [np.arange(0, o_kr)]
    idx += [pad(MLA_NOPE), o_kr + np.arange(MLA_ROPE), pad(LANE - MLA_NOPE - MLA_ROPE)]
    idx += [pad(MLA_NOPE), o_kr + _swap_halves(MLA_ROPE), pad(LANE - MLA_NOPE - MLA_ROPE)]
    assert 2 * GQA_HEAD_DIM == LANE and GQA_KV_HEADS == 2
    for h in range(GQA_HEADS):
        idx += [o_gq + h * GQA_HEAD_DIM + np.arange(GQA_HEAD_DIM)]
    for h in range(GQA_HEADS):
        idx += [o_gq + h * GQA_HEAD_DIM + axial]
    for h in range(GQA_KV_HEADS):
        idx += [o_gk + h * GQA_HEAD_DIM + np.arange(GQA_HEAD_DIM)]
    for h in range(GQA_KV_HEADS):
        idx += [o_gk + h * GQA_HEAD_DIM + axial]
    idx += [o_gv + np.arange(GQA_KV_HEADS * GQA_HEAD_DIM)]
    idx = np.concatenate(idx)
    assert idx.shape[0] == _C_END
    w_in2 = _take_cols(w_in.astype(BF16), idx)

    hd = MLA_NOPE + MLA_ROPE
    ia, ib = [], []
    for h in range(MLA_HEADS):
        ia += [h * hd + np.arange(hd), pad(LANE - hd)]
        ib += [pad(MLA_NOPE), h * hd + MLA_NOPE + _swap_halves(MLA_ROPE), pad(LANE - hd)]
    w_uq2 = _take_cols(w_uq.astype(BF16), np.concatenate(ia + ib))

    kvd = MLA_NOPE + MLA_V
    ik, iv = [], []
    for h in range(MLA_HEADS):
        ik += [h * kvd + np.arange(MLA_NOPE), pad(LANE - MLA_NOPE)]
        iv += [h * kvd + MLA_NOPE + np.arange(MLA_V)]
    w_ukv2 = _take_cols(w_ukv.astype(BF16), np.concatenate(ik + iv))

    def gain_pair(g):
        ga = jnp.concatenate([g, g])[None, :]
        gb = jnp.tile(g[jnp.asarray(axial)], 2)[None, :]
        return ga, gb

    gq_a, gq_b = gain_pair(gq_norm)
    gk_a, gk_b = gain_pair(gk_norm)

    w_out2 = w_out.astype(BF16)

    return dict(w_in=w_in2, q_norm=q_norm[None, :], w_uq=w_uq2, kv_norm=kv_norm[None, :],
                w_ukv=w_ukv2, gq_a=gq_a, gq_b=gq_b, gk_a=gk_a, gk_b=gk_b,
                rope_expand=_rope_expand_matrix()), w_out2


def _rope_tables(pos, row, row_repeat, col, col_tile):
    half = MLA_ROPE // 2
    inv = 1.0 / (ROPE_THETA ** (jnp.arange(half, dtype=F32) / half))
    n = pos.shape[0]

    def cs(p):
        ang = p.astype(F32)[None, :] * inv[:, None]
        return jnp.cos(ang), jnp.sin(ang)

    by_row = tuple(jnp.repeat(x, row_repeat, axis=1) for x in cs(row))
    by_col = tuple(jnp.tile(x, (1, col_tile)) for x in cs(col))
    vals = jnp.concatenate(cs(pos) + by_row + by_col + (jnp.zeros((LANE - 6 * half, n), F32),), axis=0).T
    hi = vals.astype(BF16)
    rest = vals - hi.astype(F32)
    mid = rest.astype(BF16)
    lo = (rest - mid.astype(F32)).astype(BF16)
    return jnp.stack([hi, mid, lo])


def _rope_expand_matrix():
    half = MLA_ROPE // 2
    c1, s1, cr, sr, cc, sc = range(6)
    tail = LANE - MLA_NOPE - MLA_ROPE
    blank = [(None, 0)]
    layout = (blank * (MLA_NOPE // half) + [(c1, 1), (c1, 1)] + blank * (tail // half)
              + blank * (MLA_NOPE // half) + [(s1, -1), (s1, 1)] + blank * (tail // half)
              + [(cr, 1), (cr, 1), (cc, 1), (cc, 1)] * (LANE // GQA_HEAD_DIM)
              + [(sr, -1), (sr, 1), (sc, -1), (sc, 1)] * (LANE // GQA_HEAD_DIM))
    expand = np.zeros((LANE, len(layout) * half), np.float32)
    for blk, (src, sign) in enumerate(layout):
        if src is not None:
            expand[src * half + np.arange(half), blk * half + np.arange(half)] = sign
    return jnp.asarray(np.tile(expand, (3, 1)), BF16)


def _na_bias_tables(rpb, meta_bias):
    c_idx = np.arange(GRID_W)
    c_start = np.clip(c_idx - NA_WIN_C // 2, 0, GRID_W - NA_WIN_C)
    col_mask = (c_idx[None, :] >= c_start[:, None]) & (c_idx[None, :] < c_start[:, None] + NA_WIN_C)
    col_off = np.clip(c_idx[None, :] - c_idx[:, None] + NA_WIN_C - 1, 0, 2 * NA_WIN_C - 2)
    hp = NA_HEADS // 2
    n_off = 2 * NA_WIN_C - 1
    select = np.zeros((2, n_off, GRID_W, 2, GRID_W), np.float32)
    kc_g, c_g = np.meshgrid(c_idx, c_idx, indexing="ij")
    for half in range(2):
        select[half, col_off[c_g, kc_g], kc_g, half, c_g] = 1.0
    rows = rpb.reshape(hp, 2, NA_MASKED, n_off).transpose(0, 2, 1, 3).reshape(hp * NA_MASKED, 2 * n_off)
    t = jnp.dot(rows, jnp.asarray(select.reshape(2 * n_off, GRID_W * LANE)),
                precision=lax.Precision.HIGHEST) * LOG2E
    t = t.reshape(hp, NA_MASKED, GRID_W, LANE)
    keep = np.tile(col_mask.T, (1, 2))
    t = jnp.where(jnp.asarray(keep)[None, None], t, NEG_INF)
    bias = jnp.concatenate([t, jnp.full_like(t[:, :1], NEG_INF)], axis=1)
    mbl = meta_bias * LOG2E
    mb_t = jnp.repeat(mbl.reshape(hp, 2, N_META).transpose(0, 2, 1), GRID_W, axis=2)
    mb = jnp.pad(mbl, ((0, 0), (0, LANE - N_META)))
    return bias, mb_t, mb


def _pad_meta(x, n_seq):
    c = x.shape[1]
    return jnp.pad(x.reshape(n_seq, N_META, c), ((0, 0), (0, LANE - N_META), (0, 0)))


def kernel(x_prompt, x_sample, meta, norm_gains, ffn1_w_gate, ffn1_w_up, ffn1_w_down, ffn2_w_gate, ffn2_w_up, ffn2_w_down, attn_w_in, mla_q_norm, mla_w_uq, mla_kv_norm, mla_w_ukv, gqa_q_norm, gqa_k_norm, attn_w_out, na_w_qkv, na_rpb, na_meta_bias, na_w_out):
    bp, sp, _ = x_prompt.shape
    bs, ss, _ = x_sample.shape
    n_seq = bp + bs
    depth = norm_gains.shape[0]
    groups = [(bp, sp, 0), (bs, ss, bp)]

    n_meta = n_seq * N_META
    meta_rows = -(-n_meta // LANE) * LANE
    pad_rows = lambda x: jnp.pad(x, ((0, meta_rows - x.shape[0]), (0, 0)))
    h_tok = [x_prompt.reshape(bp * sp, D_MODEL), x_sample.reshape(bs * ss, D_MODEL)]
    h_meta = pad_rows(jnp.tile(meta.astype(F32), (n_seq, 1)))

    smax = max(sp, ss)
    tab_tok = _rope_tables(jnp.arange(smax) + N_META, jnp.arange(smax // GRID_W), GRID_W,
                           jnp.arange(GRID_W), smax // GRID_W)
    tab_meta = _rope_tables(jnp.arange(meta_rows) % N_META, jnp.full((1,), -1), meta_rows,
                            jnp.arange(N_META), meta_rows // N_META)

    w1 = (ffn1_w_gate.astype(BF16), ffn1_w_up.astype(BF16), ffn1_w_down.astype(BF16))
    w2 = (ffn2_w_gate.astype(BF16), ffn2_w_up.astype(BF16), ffn2_w_down.astype(BF16))

    for i in range(depth):
        gains = jnp.pad(norm_gains[i], ((0, 2), (0, 0)))
        j = i // 2
        h_tok = [_ffn1(h, gains, *w1, i) for h in h_tok]
        h_meta = _ffn1(h_meta, gains, *w1, i)
        if i % 2 == 0:
            w, w_out = _dense_weights(attn_w_in[j], mla_q_norm[j], mla_w_uq[j], mla_kv_norm[j],
                                      mla_w_ukv[j], gqa_q_norm[j], gqa_k_norm[j], attn_w_out[j])
            qkv_tok = [_proj_dense(h, gains, w, tab_tok, s) for h, (_, s, _) in zip(h_tok, groups)]
            qmt, km, vmt = _proj_dense(h_meta, gains, w, tab_meta, 0)
            kmp = _pad_meta(km[:n_meta], n_seq)
            vmt = vmt.transpose(1, 0, 2).reshape(V_ROWS, meta_rows)
            vmtp = vmt[:, :n_meta].reshape(V_ROWS, n_seq, N_META).transpose(1, 0, 2)
            vmtp = jnp.pad(vmtp, ((0, 0), (0, 0), (0, LANE - N_META)))
            qmt = qmt.transpose(1, 0, 2).reshape(HEAD_SLOTS * LANE, meta_rows)
            qmtp = qmt[:, :n_meta].reshape(HEAD_SLOTS * LANE, n_seq, N_META).transpose(1, 0, 2)
            qmtp = jnp.pad(qmtp, ((0, 0), (0, 0), (0, LANE - N_META)))
            o_tok, o_meta = [], []
            for (qt, k, vt), (nb, s, b0) in zip(qkv_tok, groups):
                o_tok.append(_dense_attn(qt, k, vt, kmp, vmtp, n_seq=nb, seq=s, nq=qt.shape[0] // nb,
                                         q_base=0, meta_base=b0))
                om = _dense_attn(qmtp, k, vt, kmp, vmtp, n_seq=nb, seq=s, nq=1, q_base=b0, meta_base=b0)
                o_meta.append(om.reshape(nb, LANE, D_MODEL)[:, :N_META].reshape(nb * N_META, D_MODEL))
            o_meta = pad_rows(jnp.concatenate(o_meta, axis=0))
        else:
            w_qkv = na_w_qkv[j].astype(BF16)
            w_out = na_w_out[j].astype(BF16)
            bias, mb_t, mb = _na_bias_tables(na_rpb[j], na_meta_bias[j])
            qkv_tok = [_proj_na(h, gains, w_qkv, True) for h in h_tok]
            qm, km, vm = _proj_na(h_meta, gains, w_qkv, False)
            kmp, vmp = _pad_meta(km[:n_meta], n_seq), _pad_meta(vm[:n_meta], n_seq)
            km16 = km[:n_meta].reshape(n_seq, N_META, NA_HEADS * NA_HEAD_DIM)
            vmtp = vmp.transpose(0, 2, 1)
            o_tok = [_na_attn(q, k, vt, km16, vmtp, bias, mb_t, n_seq=nb, seq=s, meta_base=b0)
                     for (q, k, vt), (nb, s, b0) in zip(qkv_tok, groups)]
            o_meta = pad_rows(_na_meta(qm[:n_meta], kmp, vmp, mb))
        h_tok = [_mix_ffn2(h, o, w_out, gains, *w2, i) for h, o in zip(h_tok, o_tok)]
        h_meta = _mix_ffn2(h_meta, o_meta, w_out, gains, *w2, i)

    return (h_tok[0].reshape(bp, sp, D_MODEL), h_tok[1].reshape(bs, ss, D_MODEL))
```

```python
import functools
import math

import jax
import jax.numpy as jnp
import numpy as np
from jax import lax
from jax.experimental import pallas as pl
from jax.experimental.pallas import tpu as pltpu

F32 = jnp.float32
BF16 = jnp.bfloat16

D_MODEL = 1024
N_META = 16
GRID_W = 64
D_FF = 2816
EPS = 1e-6
NEG_INF = -1e30
LOG2E = math.log2(math.e)

MLA_HEADS = 8
MLA_Q_LORA = 256
MLA_KV_LORA = 128
MLA_NOPE = 64
MLA_ROPE = 32
MLA_V = 64
GQA_HEADS = 8
GQA_KV_HEADS = 2
GQA_HEAD_DIM = 64
ROPE_THETA = 10000.0
NA_HEADS = 16
NA_HEAD_DIM = 64
NA_WIN_R = 8
NA_WIN_C = 16

LANE = 128
HEAD_SLOTS = MLA_HEADS + GQA_HEADS
K_SLOTS = MLA_HEADS + 1
V_ROWS = (MLA_HEADS + GQA_KV_HEADS) * MLA_V
SUBLANE = 8
DENSE_LOOKAHEAD, DENSE_SLOTS = 2, 4
NA_LOOKAHEAD, NA_SLOTS = 2, 4
VMEM_LIMIT = 56 * 1024 * 1024
ROW_TILE = 512
FFN_ROW_TILE = 1024
FFN_CHAINS = 4
DENSE_SUBTILES = 8
NA_ROWS_PER_STEP = 16

_C_CQ = 0
_C_CKV = _C_CQ + MLA_Q_LORA
_C_KRA = _C_CKV + MLA_KV_LORA
_C_KRB = _C_KRA + LANE
_C_GQA = _C_KRB + LANE
_C_GQB = _C_GQA + GQA_HEADS // 2 * LANE
_C_GKA = _C_GQB + GQA_HEADS // 2 * LANE
_C_GKB = _C_GKA + LANE
_C_GV = _C_GKB + LANE
_C_END = _C_GV + LANE


def _const_spec(shape):
    nd = len(shape)
    return pl.BlockSpec(shape, lambda *_: (0,) * nd, pipeline_mode=pl.Buffered(1))


def _rms(x, g):
    ms = jnp.mean(x * x, axis=-1, keepdims=True)
    return x * lax.rsqrt(ms + EPS) * g


def _dot(a, b):
    return jnp.dot(a, b, preferred_element_type=F32)


def _dot_nt(a, b):
    return lax.dot_general(a, b, (((1,), (1,)), ((), ())), preferred_element_type=F32)


def _row_tile(rows, want):
    t = min(rows, want)
    while rows % t:
        t //= 2
    return t


def _ffn_rows(h, g_ref, pre, post, wg_ref, wu_ref, wd_ref):
    xn = _rms(h, g_ref[pre:pre + 1, :]).astype(BF16)
    gate = _dot(xn, wg_ref[...])
    up = _dot(xn, wu_ref[...])
    act = (gate * jax.nn.sigmoid(gate) * up).astype(BF16)
    y = _dot(act, wd_ref[...])
    return h + 0.5 * _rms(y, g_ref[post:post + 1, :])


def _ffn_body(h, g_ref, pre, post, wg_ref, wu_ref, wd_ref):
    part = h.shape[0] // FFN_CHAINS
    return jnp.concatenate([_ffn_rows(h[i * part:(i + 1) * part], g_ref, pre, post, wg_ref, wu_ref, wd_ref)
                            for i in range(FFN_CHAINS)], axis=0)


def _ffn1_kernel(h_ref, g_ref, wg_ref, wu_ref, wd_ref, out_ref):
    out_ref[...] = _ffn_body(h_ref[...], g_ref, 0, 1, wg_ref, wu_ref, wd_ref)


def _mix_ffn2_kernel(h_ref, o_ref, wo_ref, g_ref, wg_ref, wu_ref, wd_ref, out_ref):
    mixed = _dot(o_ref[...], wo_ref[...])
    h = h_ref[...] + _rms(mixed, g_ref[3:4, :])
    out_ref[...] = _ffn_body(h, g_ref, 4, 5, wg_ref, wu_ref, wd_ref)


def _layer_spec(w, layer):
    return pl.BlockSpec((None,) + w.shape[1:], lambda *_: (layer, 0, 0), pipeline_mode=pl.Buffered(1))


def _ffn1(h, gains, wg, wu, wd, layer):
    rows = h.shape[0]
    tm = _row_tile(rows, FFN_ROW_TILE)
    return pl.pallas_call(
        _ffn1_kernel,
        grid=(rows // tm,),
        in_specs=[
            pl.BlockSpec((tm, D_MODEL), lambda i: (i, 0)),
            _const_spec(gains.shape),
            _layer_spec(wg, layer), _layer_spec(wu, layer), _layer_spec(wd, layer),
        ],
        out_specs=pl.BlockSpec((tm, D_MODEL), lambda i: (i, 0)),
        out_shape=jax.ShapeDtypeStruct((rows, D_MODEL), F32),
        compiler_params=pltpu.CompilerParams(
            dimension_semantics=("arbitrary",), vmem_limit_bytes=VMEM_LIMIT),
        name="ffn1",
    )(h, gains, wg, wu, wd)


def _mix_ffn2(h, o, wo, gains, wg, wu, wd, layer):
    rows = h.shape[0]
    tm = _row_tile(rows, FFN_ROW_TILE)
    return pl.pallas_call(
        _mix_ffn2_kernel,
        grid=(rows // tm,),
        in_specs=[
            pl.BlockSpec((tm, D_MODEL), lambda i: (i, 0)),
            pl.BlockSpec((tm, o.shape[1]), lambda i: (i, 0)),
            _const_spec(wo.shape),
            _const_spec(gains.shape),
            _layer_spec(wg, layer), _layer_spec(wu, layer), _layer_spec(wd, layer),
        ],
        out_specs=pl.BlockSpec((tm, D_MODEL), lambda i: (i, 0)),
        out_shape=jax.ShapeDtypeStruct((rows, D_MODEL), F32),
        compiler_params=pltpu.CompilerParams(
            dimension_semantics=("arbitrary",), vmem_limit_bytes=VMEM_LIMIT),
        name="mix_ffn2",
    )(h, o, wo, gains, wg, wu, wd)


def _proj_dense_kernel(h_ref, g_ref, win_ref, qn_ref, wuq_ref, kvn_ref, wukv_ref,
                       gqa_ref, gqb_ref, gka_ref, gkb_ref, exp_ref, tab_ref, qt_ref, k_ref, vt_ref):
    a = _rms(h_ref[...], g_ref[2:3, :]).astype(BF16)
    proj = _dot(a, win_ref[...])
    pieces = jnp.concatenate([tab_ref[i] for i in range(tab_ref.shape[0])], axis=1)
    tab = _dot(pieces, exp_ref[...])
    cos_k, sin_k = tab[:, 0:LANE], tab[:, LANE:2 * LANE]
    cos_g, sin_g = tab[:, 2 * LANE:3 * LANE], tab[:, 3 * LANE:4 * LANE]
    qs = (MLA_NOPE + MLA_ROPE) ** -0.5 * LOG2E
    lane = lax.broadcasted_iota(jnp.int32, cos_k.shape, 1)
    cos_q = jnp.where(lane < MLA_NOPE, qs, cos_k * qs)
    sin_q = sin_k * qs

    cqn = _rms(proj[:, _C_CQ:_C_CQ + MLA_Q_LORA], qn_ref[...]).astype(BF16)
    qab = _dot(cqn, wuq_ref[...])
    nq = MLA_HEADS * LANE
    for h in range(MLA_HEADS):
        qa = qab[:, h * LANE:(h + 1) * LANE]
        qb = qab[:, nq + h * LANE:nq + (h + 1) * LANE]
        qt_ref[0, h * LANE:(h + 1) * LANE, :] = (qa * cos_q + qb * sin_q).T.astype(BF16)

    ckvn = _rms(proj[:, _C_CKV:_C_CKV + MLA_KV_LORA], kvn_ref[...]).astype(BF16)
    kv = _dot(ckvn, wukv_ref[...])
    k_rope = (proj[:, _C_KRA:_C_KRA + LANE] * cos_k + proj[:, _C_KRB:_C_KRB + LANE] * sin_k)
    for h in range(MLA_HEADS):
        k_ref[:, h * LANE:(h + 1) * LANE] = (kv[:, h * LANE:(h + 1) * LANE] + k_rope).astype(BF16)
    nv = MLA_HEADS * MLA_V

    low = lane < GQA_HEAD_DIM

    def normed_rotary(xa, xb, cos, sin):
        sq = xa * xa
        ss_lo = jnp.sum(jnp.where(low, sq, 0.0), axis=-1, keepdims=True)
        ss_hi = jnp.sum(jnp.where(low, 0.0, sq), axis=-1, keepdims=True)
        r = lax.rsqrt(jnp.where(low, ss_lo, ss_hi) * (1.0 / GQA_HEAD_DIM) + EPS)
        return (xa * cos + xb * sin) * r

    gq_scale = GQA_HEAD_DIM ** -0.5 * LOG2E
    cq_g = cos_g * (gqa_ref[...] * gq_scale)
    sq_g = sin_g * (gqb_ref[...] * gq_scale)
    per_kv = GQA_HEADS // GQA_KV_HEADS
    zeros_t = jnp.zeros((LANE - GQA_HEAD_DIM, qt_ref.shape[2]), F32)
    for j in range(GQA_HEADS // 2):
        xa = proj[:, _C_GQA + j * LANE:_C_GQA + (j + 1) * LANE]
        xb = proj[:, _C_GQB + j * LANE:_C_GQB + (j + 1) * LANE]
        y_t = normed_rotary(xa, xb, cq_g, sq_g).T
        for half in range(2):
            h = 2 * j + half
            q_t = y_t[half * GQA_HEAD_DIM:(half + 1) * GQA_HEAD_DIM]
            rows = [q_t, zeros_t] if h // per_kv == 0 else [zeros_t, q_t]
            qt_ref[0, nq + h * LANE:nq + (h + 1) * LANE, :] = jnp.concatenate(rows, axis=0).astype(BF16)
    xa = proj[:, _C_GKA:_C_GKA + LANE]
    xb = proj[:, _C_GKB:_C_GKB + LANE]
    k_ref[:, nq:nq + LANE] = normed_rotary(xa, xb, cos_g * gka_ref[...], sin_g * gkb_ref[...]).astype(BF16)
    v = jnp.concatenate([kv[:, nq:nq + nv], proj[:, _C_GV:_C_GV + LANE]], axis=1)
    vt_ref[0] = v.T.astype(BF16)


def _proj_dense(h, gains, w, tab, seq):
    rows = h.shape[0]
    tm = _row_tile(seq if seq else rows, ROW_TILE)
    nblk = (seq // tm) if seq else 1
    consts = [w["w_in"], w["q_norm"], w["w_uq"], w["kv_norm"], w["w_ukv"],
              w["gq_a"], w["gq_b"], w["gk_a"], w["gk_b"], w["rope_expand"]]
    return pl.pallas_call(
        _proj_dense_kernel,
        grid=(rows // tm,),
        in_specs=[pl.BlockSpec((tm, D_MODEL), lambda i: (i, 0)), _const_spec(gains.shape)]
        + [_const_spec(c.shape) for c in consts]
        + [pl.BlockSpec((tab.shape[0], tm, LANE), lambda i: (0, i % nblk, 0))],
        out_specs=[
            pl.BlockSpec((1, HEAD_SLOTS * LANE, tm), lambda i: (i, 0, 0)),
            pl.BlockSpec((tm, K_SLOTS * LANE), lambda i: (i, 0)),
            pl.BlockSpec((1, V_ROWS, tm), lambda i: (i, 0, 0)),
        ],
        out_shape=[
            jax.ShapeDtypeStruct((rows // tm, HEAD_SLOTS * LANE, tm), BF16),
            jax.ShapeDtypeStruct((rows, K_SLOTS * LANE), BF16),
            jax.ShapeDtypeStruct((rows // tm, V_ROWS, tm), BF16),
        ],
        compiler_params=pltpu.CompilerParams(
            dimension_semantics=("arbitrary",), vmem_limit_bytes=VMEM_LIMIT),
        name="proj_dense",
    )(h, gains, *consts, tab)


def _head_slots(h):
    if h < MLA_HEADS:
        return h, h
    kvh = (h - MLA_HEADS) // (GQA_HEADS // GQA_KV_HEADS)
    return MLA_HEADS, MLA_HEADS + kvh


def _sublane_bcast_max(x):
    return jnp.broadcast_to(jnp.max(x, axis=0, keepdims=True), x.shape)


def _dense_attn_kernel(qt_ref, k_ref, vt_ref, km_ref, vmt_ref, o_ref,
                       m_ref, smax_ref, acc_ref, s_ref, sm_ref):
    kv = pl.program_id(2)
    tq = qt_ref.shape[1]
    n_sub, _, tk = vt_ref.shape
    hd = MLA_V
    acc_rows = acc_ref.shape[1]

    def with_ones(vt):
        return jnp.concatenate([vt, jnp.ones((acc_rows - hd, vt.shape[1]), BF16)], axis=0)

    @pl.when(kv == 0)
    def _():
        for h in range(HEAD_SLOTS):
            ks, _ = _head_slots(h)
            qt = qt_ref[h * LANE:(h + 1) * LANE, :]
            sm_ref[h] = _dot(km_ref[0:N_META, ks * LANE:(ks + 1) * LANE], qt)
        zeros = jnp.zeros((LANE - N_META, tq), F32)
        for h in range(HEAD_SLOTS):
            _, vh = _head_slots(h)
            s3 = sm_ref[h].reshape(N_META // SUBLANE, SUBLANE, tq)
            m = _sublane_bcast_max(jnp.max(s3, axis=0))
            p3 = jnp.exp2(s3 - m[None])
            m_ref[h] = m
            p = jnp.concatenate([p3.reshape(N_META, tq), zeros], axis=0).astype(BF16)
            acc_ref[h] = _dot(with_ones(vmt_ref[vh * hd:(vh + 1) * hd, :]), p)

    n_slots = s_ref.shape[0]

    def scores(t, h):
        ks, _ = _head_slots(h)
        k0 = pl.multiple_of(t * tk, tk)
        k = k_ref[pl.ds(k0, tk), ks * LANE:(ks + 1) * LANE]
        s = _dot(k, qt_ref[h * LANE:(h + 1) * LANE, :])
        s_ref[h % n_slots] = s
        smax_ref[h] = jnp.max(s.reshape(tk // SUBLANE, SUBLANE, tq), axis=0)

    def softmax_pv(t, h):
        _, vh = _head_slots(h)
        m_prev = m_ref[h]
        m_new = jnp.maximum(m_prev, _sublane_bcast_max(smax_ref[h]))
        alpha = jnp.exp2(m_prev - m_new)
        s3 = s_ref[h % n_slots].reshape(tk // SUBLANE, SUBLANE, tq)
        p = jnp.exp2(s3 - m_new[None]).reshape(tk, tq).astype(BF16)
        pv = _dot(with_ones(vt_ref[t, vh * hd:(vh + 1) * hd, :]), p)
        acc = acc_ref[h].reshape(acc_rows // SUBLANE, SUBLANE, tq) * alpha[None]
        acc_ref[h] = acc.reshape(acc_rows, tq) + pv
        m_ref[h] = m_new

    for h in range(DENSE_LOOKAHEAD):
        scores(0, h)

    def sub_tile(t, carry):
        t_next = jnp.minimum(t + 1, n_sub - 1)
        for h in range(HEAD_SLOTS):
            ahead = h + DENSE_LOOKAHEAD
            if ahead < HEAD_SLOTS:
                scores(t, ahead)
            else:
                scores(t_next, ahead - HEAD_SLOTS)
            softmax_pv(t, h)
        return carry

    lax.fori_loop(0, n_sub, sub_tile, 0, unroll=2)

    @pl.when(kv == pl.num_programs(2) - 1)
    def _():
        for j in range(HEAD_SLOTS // 2):
            outs = []
            for h in (2 * j, 2 * j + 1):
                outs.append(acc_ref[h, 0:hd, :] / acc_ref[h, hd:hd + 1, :])
            o_t = jnp.concatenate(outs, axis=0)
            o_ref[:, j * LANE:(j + 1) * LANE] = o_t.T.astype(BF16)


def _dense_attn(qt, k, vt, km, vmt, *, n_seq, seq, nq, q_base, meta_base):
    tq = qt.shape[2]
    tk = vt.shape[2]
    n_sub = _row_tile(seq // tk, DENSE_SUBTILES)
    nk = seq // (tk * n_sub)
    out_rows = n_seq * nq * tq
    return pl.pallas_call(
        _dense_attn_kernel,
        grid=(n_seq, nq, nk),
        in_specs=[
            pl.BlockSpec((None, HEAD_SLOTS * LANE, tq), lambda b, i, j: (q_base + b * nq + i, 0, 0)),
            pl.BlockSpec((n_sub * tk, K_SLOTS * LANE), lambda b, i, j: (b * nk + j, 0)),
            pl.BlockSpec((n_sub, V_ROWS, tk), lambda b, i, j: (b * nk + j, 0, 0)),
            pl.BlockSpec((None, LANE, K_SLOTS * LANE), lambda b, i, j: (meta_base + b, 0, 0)),
            pl.BlockSpec((None, V_ROWS, LANE), lambda b, i, j: (meta_base + b, 0, 0)),
        ],
        out_specs=pl.BlockSpec((tq, D_MODEL), lambda b, i, j: (b * nq + i, 0)),
        out_shape=jax.ShapeDtypeStruct((out_rows, D_MODEL), BF16),
        scratch_shapes=[
            pltpu.VMEM((HEAD_SLOTS, SUBLANE, tq), F32),
            pltpu.VMEM((HEAD_SLOTS, SUBLANE, tq), F32),
            pltpu.VMEM((HEAD_SLOTS, MLA_V + 2 * SUBLANE, tq), F32),
            pltpu.VMEM((DENSE_SLOTS, tk, tq), F32),
            pltpu.VMEM((HEAD_SLOTS, N_META, tq), F32),
        ],
        compiler_params=pltpu.CompilerParams(
            dimension_semantics=("arbitrary", "arbitrary", "arbitrary"),
            vmem_limit_bytes=VMEM_LIMIT),
        name="dense_attn",
    )(qt, k, vt, km, vmt)


def _proj_na_kernel(h_ref, g_ref, w_ref, q_ref, k_ref, v_ref, *, transpose_v):
    a = _rms(h_ref[...], g_ref[2:3, :]).astype(BF16)
    qkv = _dot(a, w_ref[...])
    n = NA_HEADS * NA_HEAD_DIM
    q_ref[...] = (qkv[:, 0:n] * (NA_HEAD_DIM ** -0.5 * LOG2E)).astype(BF16)
    k_ref[...] = qkv[:, n:2 * n].astype(BF16)
    v = qkv[:, 2 * n:3 * n]
    if transpose_v:
        vt = v.T.astype(BF16)
        for t in range(v_ref.shape[0]):
            v_ref[t] = vt[:, t * LANE:(t + 1) * LANE]
    else:
        v_ref[...] = v.astype(BF16)


def _proj_na(h, gains, w, transpose_v):
    rows = h.shape[0]
    tm = _row_tile(rows, ROW_TILE)
    n = NA_HEADS * NA_HEAD_DIM
    if transpose_v:
        v_spec = pl.BlockSpec((tm // LANE, n, LANE), lambda i: (i, 0, 0))
        v_shape = jax.ShapeDtypeStruct((rows // LANE, n, LANE), BF16)
    else:
        v_spec = pl.BlockSpec((tm, n), lambda i: (i, 0))
        v_shape = jax.ShapeDtypeStruct((rows, n), BF16)
    return pl.pallas_call(
        functools.partial(_proj_na_kernel, transpose_v=transpose_v),
        grid=(rows // tm,),
        in_specs=[pl.BlockSpec((tm, D_MODEL), lambda i: (i, 0)), _const_spec(gains.shape),
                  _const_spec(w.shape)],
        out_specs=[pl.BlockSpec((tm, n), lambda i: (i, 0))] * 2 + [v_spec],
        out_shape=[jax.ShapeDtypeStruct((rows, n), BF16)] * 2 + [v_shape],
        compiler_params=pltpu.CompilerParams(
            dimension_semantics=("arbitrary",), vmem_limit_bytes=VMEM_LIMIT),
        name="proj_na",
    )(h, gains, w)


NA_SPAN_R = NA_WIN_R + 2
NA_MASKED = 2 * NA_WIN_R - 1


def _na_kernel(q_ref, k_ref, vt_ref, km_ref, vmt_ref, bias_ref, mb_ref, o_ref, s_ref,
               *, rows, rows_per_step):
    step = pl.program_id(1)
    n_pairs = rows_per_step // 2
    n_hp = NA_HEADS // 2
    span = NA_SPAN_R * GRID_W
    lane = lax.broadcasted_iota(jnp.int32, (GRID_W, LANE), 1)
    first = lane < (LANE // 2)
    zeros_m = jnp.zeros((LANE - N_META, 2 * LANE), F32)
    ones_v = jnp.ones((2 * SUBLANE, span + LANE), BF16)

    def geometry(rp):
        ra = step * rows_per_step + 2 * rp
        rs = [jnp.clip(ra + x - NA_WIN_R // 2, 0, rows - NA_WIN_R) for x in range(2)]
        ws = jnp.minimum((rs[0] // 2) * 2, rows - NA_SPAN_R)
        return ra, rs, ws

    n_slots = s_ref.shape[0]

    def scores(rp, hp):
        slot = hp % n_slots
        _, _, ws = geometry(rp)
        cols = slice(hp * LANE, (hp + 1) * LANE)
        parts = []
        for x in range(2):
            q0 = pl.multiple_of((2 * rp + x) * GRID_W, GRID_W)
            qx = q_ref[pl.ds(q0, GRID_W), cols]
            parts += [jnp.where(first, qx, jnp.zeros_like(qx)), jnp.where(first, jnp.zeros_like(qx), qx)]
        qblk = jnp.concatenate(parts, axis=0)
        k0 = pl.multiple_of(ws * GRID_W, 2 * GRID_W)
        s_ref[slot, 0:span, :] = _dot_nt(k_ref[pl.ds(k0, span), cols], qblk)
        s_ref[slot, span:span + N_META, :] = _dot_nt(km_ref[:, cols], qblk)

    for hp in range(NA_LOOKAHEAD):
        scores(0, hp)

    def row_pair(rp, carry):
        ra, rs, ws = geometry(rp)
        idx = []
        for jj in range(NA_SPAN_R):
            kr = ws + jj
            idx.append([jnp.where((kr >= rs[x]) & (kr < rs[x] + NA_WIN_R),
                                  kr - (ra + x) + NA_WIN_R - 1, NA_MASKED) for x in range(2)])
        t0 = ws // 2
        for hp in range(n_hp):
            slot = hp % n_slots
            ahead = hp + NA_LOOKAHEAD
            if ahead < n_hp:
                scores(rp, ahead)
            else:
                scores(jnp.minimum(rp + 1, n_pairs - 1), ahead - n_hp)
            cols = slice(hp * LANE, (hp + 1) * LANE)
            b = jnp.concatenate(
                [jnp.concatenate([bias_ref[hp, idx[jj][0]], bias_ref[hp, idx[jj][1]]], axis=1)
                 for jj in range(NA_SPAN_R)], axis=0)
            s = s_ref[slot, 0:span, :] + b
            mb = mb_ref[hp]
            sm = s_ref[slot, span:span + N_META, :] + jnp.concatenate([mb, mb], axis=1)
            s3 = s.reshape(span // SUBLANE, SUBLANE, 2 * LANE)
            sm3 = sm.reshape(N_META // SUBLANE, SUBLANE, 2 * LANE)
            m = _sublane_bcast_max(jnp.maximum(jnp.max(s3, axis=0), jnp.max(sm3, axis=0)))
            p = jnp.exp2(s3 - m[None]).reshape(span, 2 * LANE).astype(BF16)
            pm3 = jnp.exp2(sm3 - m[None])
            pm = jnp.concatenate([pm3.reshape(N_META, 2 * LANE), zeros_m], axis=0).astype(BF16)
            v_all = jnp.concatenate([vt_ref[t0 + t, cols, :] for t in range(span // LANE)]
                                    + [vmt_ref[cols, :]], axis=1)
            o_t = _dot(jnp.concatenate([v_all, ones_v], axis=0),
                       jnp.concatenate([p, pm], axis=0))
            o_t = o_t[0:LANE] / o_t[LANE:LANE + 1]
            for x in range(2):
                blk = o_t[:, x * LANE:(x + 1) * LANE].T
                q0 = pl.multiple_of((2 * rp + x) * GRID_W, GRID_W)
                o_ref[pl.ds(q0, GRID_W), cols] = jnp.where(
                    first, blk[0:GRID_W], blk[GRID_W:2 * GRID_W]).astype(BF16)
        return carry

    lax.fori_loop(0, n_pairs, row_pair, 0, unroll=4)


def _na_attn(q, k, vt, km, vmt, bias, mb, *, n_seq, seq, meta_base):
    rows = seq // GRID_W
    assert rows >= NA_SPAN_R and rows % 2 == 0
    rps = _row_tile(rows, NA_ROWS_PER_STEP)
    nsteps = rows // rps
    n = NA_HEADS * NA_HEAD_DIM
    span = NA_SPAN_R * GRID_W
    return pl.pallas_call(
        functools.partial(_na_kernel, rows=rows, rows_per_step=rps),
        grid=(n_seq, nsteps),
        in_specs=[
            pl.BlockSpec((rps * GRID_W, n), lambda b, i: (b * nsteps + i, 0)),
            pl.BlockSpec((seq, n), lambda b, i: (b, 0), pipeline_mode=pl.Buffered(1)),
            pl.BlockSpec((seq // LANE, n, LANE), lambda b, i: (b, 0, 0), pipeline_mode=pl.Buffered(1)),
            pl.BlockSpec((None, N_META, n), lambda b, i: (meta_base + b, 0, 0)),
            pl.BlockSpec((None, n, LANE), lambda b, i: (meta_base + b, 0, 0)),
            _const_spec(bias.shape),
            _const_spec(mb.shape),
        ],
        out_specs=pl.BlockSpec((rps * GRID_W, n), lambda b, i: (b * nsteps + i, 0)),
        out_shape=jax.ShapeDtypeStruct((n_seq * seq, n), BF16),
        scratch_shapes=[pltpu.VMEM((NA_SLOTS, span + N_META, 2 * LANE), F32)],
        compiler_params=pltpu.CompilerParams(
            dimension_semantics=("arbitrary", "arbitrary"), vmem_limit_bytes=VMEM_LIMIT),
        name="na_attn",
    )(q, k, vt, km, vmt, bias, mb)


def _na_meta_kernel(q_ref, km_ref, vm_ref, mb_ref, o_ref):
    lane = lax.broadcasted_iota(jnp.int32, (N_META, LANE), 1)
    first = lane < (LANE // 2)
    for j in range(NA_HEADS // 2):
        cols = slice(j * LANE, (j + 1) * LANE)
        qp = q_ref[:, cols]
        km = km_ref[:, cols]
        vm = vm_ref[:, cols]
        outs = []
        for half in range(2):
            h = 2 * j + half
            qh = jnp.where(first if half == 0 else jnp.logical_not(first), qp, jnp.zeros_like(qp))
            sm = _dot_nt(qh, km)
            sm = jnp.where(lane < N_META, sm + mb_ref[h:h + 1, :], NEG_INF)
            m = jnp.max(sm, axis=-1, keepdims=True)
            pm = jnp.exp2(sm - m)
            l = jnp.sum(pm, axis=-1, keepdims=True)
            outs.append(_dot(pm.astype(BF16), vm) / l)
        o_ref[:, cols] = jnp.where(first, outs[0], outs[1]).astype(BF16)


def _na_meta(qm, km, vm, mb):
    n_seq = km.shape[0]
    n = NA_HEADS * NA_HEAD_DIM
    return pl.pallas_call(
        _na_meta_kernel,
        grid=(n_seq,),
        in_specs=[
            pl.BlockSpec((N_META, n), lambda b: (b, 0)),
            pl.BlockSpec((None, LANE, n), lambda b: (b, 0, 0)),
            pl.BlockSpec((None, LANE, n), lambda b: (b, 0, 0)),
            _const_spec(mb.shape),
        ],
        out_specs=pl.BlockSpec((N_META, n), lambda b: (b, 0)),
        out_shape=jax.ShapeDtypeStruct((n_seq * N_META, n), BF16),
        compiler_params=pltpu.CompilerParams(dimension_semantics=("arbitrary",)),
        name="na_meta",
    )(qm, km, vm, mb)


def _take_cols(w, idx):
    idx = np.asarray(idx)
    neg = idx < 0
    same_run = np.where(neg[1:] | neg[:-1], neg[1:] & neg[:-1], np.diff(idx) == 1)
    breaks = np.flatnonzero(~same_run) + 1
    parts = []
    for run in np.split(idx, breaks):
        if run[0] < 0:
            parts.append(jnp.zeros((w.shape[0], len(run)), w.dtype))
        else:
            parts.append(w[:, int(run[0]):int(run[-1]) + 1])
    return jnp.concatenate(parts, axis=1)


def _swap_halves(n):
    half = n // 2
    return np.concatenate([np.arange(half, n), np.arange(0, half)])


def _dense_weights(w_in, q_norm, w_uq, kv_norm, w_ukv, gq_norm, gk_norm, w_out):
    pad = lambda k: -np.ones(k, np.int64)
    o_kr = MLA_Q_LORA + MLA_KV_LORA
    o_gq = o_kr + MLA_ROPE
    o_gk = o_gq + GQA_HEADS * GQA_HEAD_DIM
    o_gv = o_gk + GQA_KV_HEADS * GQA_HEAD_DIM
    axial = np.concatenate([_swap_halves(GQA_HEAD_DIM // 2),
                            GQA_HEAD_DIM // 2 + _swap_halves(GQA_HEAD_DIM // 2)])
    idx = [np.arange(0, o_kr)]
    idx += [pad(MLA_NOPE), o_kr + np.arange(MLA_ROPE), pad(LANE - MLA_NOPE - MLA_ROPE)]
    idx += [pad(MLA_NOPE), o_kr + _swap_halves(MLA_ROPE), pad(LANE - MLA_NOPE - MLA_ROPE)]
    assert 2 * GQA_HEAD_DIM == LANE and GQA_KV_HEADS == 2
    for h in range(GQA_HEADS):
        idx += [o_gq + h * GQA_HEAD_DIM + np.arange(GQA_HEAD_DIM)]
    for h in range(GQA_HEADS):
        idx += [o_gq + h * GQA_HEAD_DIM + axial]
    for h in range(GQA_KV_HEADS):
        idx += [o_gk + h * GQA_HEAD_DIM + np.arange(GQA_HEAD_DIM)]
    for h in range(GQA_KV_HEADS):
        idx += [o_gk + h * GQA_HEAD_DIM + axial]
    idx += [o_gv + np.arange(GQA_KV_HEADS * GQA_HEAD_DIM)]
    idx = np.concatenate(idx)
    assert idx.shape[0] == _C_END
    w_in2 = _take_cols(w_in.astype(BF16), idx)

    hd = MLA_NOPE + MLA_ROPE
    ia, ib = [], []
    for h in range(MLA_HEADS):
        ia += [h * hd + np.arange(hd), pad(LANE - hd)]
        ib += [pad(MLA_NOPE), h * hd + MLA_NOPE + _swap_halves(MLA_ROPE), pad(LANE - hd)]
    w_uq2 = _take_cols(w_uq.astype(BF16), np.concatenate(ia + ib))

    kvd = MLA_NOPE + MLA_V
    ik, iv = [], []
    for h in range(MLA_HEADS):
        ik += [h * kvd + np.arange(MLA_NOPE), pad(LANE - MLA_NOPE)]
        iv += [h * kvd + MLA_NOPE + np.arange(MLA_V)]
    w_ukv2 = _take_cols(w_ukv.astype(BF16), np.concatenate(ik + iv))

    def gain_pair(g):
        ga = jnp.concatenate([g, g])[None, :]
        gb = jnp.tile(g[jnp.asarray(axial)], 2)[None, :]
        return ga, gb

    gq_a, gq_b = gain_pair(gq_norm)
    gk_a, gk_b = gain_pair(gk_norm)

    w_out2 = w_out.astype(BF16)

    return dict(w_in=w_in2, q_norm=q_norm[None, :], w_uq=w_uq2, kv_norm=kv_norm[None, :],
                w_ukv=w_ukv2, gq_a=gq_a, gq_b=gq_b, gk_a=gk_a, gk_b=gk_b,
                rope_expand=_rope_expand_matrix()), w_out2


def _rope_tables(pos, row, row_repeat, col, col_tile):
    half = MLA_ROPE // 2
    inv = 1.0 / (ROPE_THETA ** (jnp.arange(half, dtype=F32) / half))
    n = pos.shape[0]

    def cs(p):
        ang = p.astype(F32)[None, :] * inv[:, None]
        return jnp.cos(ang), jnp.sin(ang)

    by_row = tuple(jnp.repeat(x, row_repeat, axis=1) for x in cs(row))
    by_col = tuple(jnp.tile(x, (1, col_tile)) for x in cs(col))
    vals = jnp.concatenate(cs(pos) + by_row + by_col + (jnp.zeros((LANE - 6 * half, n), F32),), axis=0).T
    hi = vals.astype(BF16)
    rest = vals - hi.astype(F32)
    mid = rest.astype(BF16)
    lo = (rest - mid.astype(F32)).astype(BF16)
    return jnp.stack([hi, mid, lo])


def _rope_expand_matrix():
    half = MLA_ROPE // 2
    c1, s1, cr, sr, cc, sc = range(6)
    tail = LANE - MLA_NOPE - MLA_ROPE
    blank = [(None, 0)]
    layout = (blank * (MLA_NOPE // half) + [(c1, 1), (c1, 1)] + blank * (tail // half)
              + blank * (MLA_NOPE // half) + [(s1, -1), (s1, 1)] + blank * (tail // half)
              + [(cr, 1), (cr, 1), (cc, 1), (cc, 1)] * (LANE // GQA_HEAD_DIM)
              + [(sr, -1), (sr, 1), (sc, -1), (sc, 1)] * (LANE // GQA_HEAD_DIM))
    expand = np.zeros((LANE, len(layout) * half), np.float32)
    for blk, (src, sign) in enumerate(layout):
        if src is not None:
            expand[src * half + np.arange(half), blk * half + np.arange(half)] = sign
    return jnp.asarray(np.tile(expand, (3, 1)), BF16)


def _na_bias_tables(rpb, meta_bias):
    c_idx = np.arange(GRID_W)
    c_start = np.clip(c_idx - NA_WIN_C // 2, 0, GRID_W - NA_WIN_C)
    col_mask = (c_idx[None, :] >= c_start[:, None]) & (c_idx[None, :] < c_start[:, None] + NA_WIN_C)
    col_off = np.clip(c_idx[None, :] - c_idx[:, None] + NA_WIN_C - 1, 0, 2 * NA_WIN_C - 2)
    hp = NA_HEADS // 2
    n_off = 2 * NA_WIN_C - 1
    select = np.zeros((2, n_off, GRID_W, 2, GRID_W), np.float32)
    kc_g, c_g = np.meshgrid(c_idx, c_idx, indexing="ij")
    for half in range(2):
        select[half, col_off[c_g, kc_g], kc_g, half, c_g] = 1.0
    rows = rpb.reshape(hp, 2, NA_MASKED, n_off).transpose(0, 2, 1, 3).reshape(hp * NA_MASKED, 2 * n_off)
    t = jnp.dot(rows, jnp.asarray(select.reshape(2 * n_off, GRID_W * LANE)),
                precision=lax.Precision.HIGHEST) * LOG2E
    t = t.reshape(hp, NA_MASKED, GRID_W, LANE)
    keep = np.tile(col_mask.T, (1, 2))
    t = jnp.where(jnp.asarray(keep)[None, None], t, NEG_INF)
    bias = jnp.concatenate([t, jnp.full_like(t[:, :1], NEG_INF)], axis=1)
    mbl = meta_bias * LOG2E
    mb_t = jnp.repeat(mbl.reshape(hp, 2, N_META).transpose(0, 2, 1), GRID_W, axis=2)
    mb = jnp.pad(mbl, ((0, 0), (0, LANE - N_META)))
    return bias, mb_t, mb


def _pad_meta(x, n_seq):
    c = x.shape[1]
    return jnp.pad(x.reshape(n_seq, N_META, c), ((0, 0), (0, LANE - N_META), (0, 0)))


def kernel(x_prompt, x_sample, meta, norm_gains, ffn1_w_gate, ffn1_w_up, ffn1_w_down, ffn2_w_gate, ffn2_w_up, ffn2_w_down, attn_w_in, mla_q_norm, mla_w_uq, mla_kv_norm, mla_w_ukv, gqa_q_norm, gqa_k_norm, attn_w_out, na_w_qkv, na_rpb, na_meta_bias, na_w_out):
    bp, sp, _ = x_prompt.shape
    bs, ss, _ = x_sample.shape
    n_seq = bp + bs
    depth = norm_gains.shape[0]
    groups = [(bp, sp, 0), (bs, ss, bp)]

    n_meta = n_seq * N_META
    meta_rows = -(-n_meta // LANE) * LANE
    pad_rows = lambda x: jnp.pad(x, ((0, meta_rows - x.shape[0]), (0, 0)))
    h_tok = [x_prompt.reshape(bp * sp, D_MODEL), x_sample.reshape(bs * ss, D_MODEL)]
    h_meta = pad_rows(jnp.tile(meta.astype(F32), (n_seq, 1)))

    smax = max(sp, ss)
    tab_tok = _rope_tables(jnp.arange(smax) + N_META, jnp.arange(smax // GRID_W), GRID_W,
                           jnp.arange(GRID_W), smax // GRID_W)
    tab_meta = _rope_tables(jnp.arange(meta_rows) % N_META, jnp.full((1,), -1), meta_rows,
                            jnp.arange(N_META), meta_rows // N_META)

    w1 = (ffn1_w_gate.astype(BF16), ffn1_w_up.astype(BF16), ffn1_w_down.astype(BF16))
    w2 = (ffn2_w_gate.astype(BF16), ffn2_w_up.astype(BF16), ffn2_w_down.astype(BF16))

    for i in range(depth):
        gains = jnp.pad(norm_gains[i], ((0, 2), (0, 0)))
        j = i // 2
        h_tok = [_ffn1(h, gains, *w1, i) for h in h_tok]
        h_meta = _ffn1(h_meta, gains, *w1, i)
        if i % 2 == 0:
            w, w_out = _dense_weights(attn_w_in[j], mla_q_norm[j], mla_w_uq[j], mla_kv_norm[j],
                                      mla_w_ukv[j], gqa_q_norm[j], gqa_k_norm[j], attn_w_out[j])
            qkv_tok = [_proj_dense(h, gains, w, tab_tok, s) for h, (_, s, _) in zip(h_tok, groups)]
            qmt, km, vmt = _proj_dense(h_meta, gains, w, tab_meta, 0)
            kmp = _pad_meta(km[:n_meta], n_seq)
            vmt = vmt.transpose(1, 0, 2).reshape(V_ROWS, meta_rows)
            vmtp = vmt[:, :n_meta].reshape(V_ROWS, n_seq, N_META).transpose(1, 0, 2)
            vmtp = jnp.pad(vmtp, ((0, 0), (0, 0), (0, LANE - N_META)))
            qmt = qmt.transpose(1, 0, 2).reshape(HEAD_SLOTS * LANE, meta_rows)
            qmtp = qmt[:, :n_meta].reshape(HEAD_SLOTS * LANE, n_seq, N_META).transpose(1, 0, 2)
            qmtp = jnp.pad(qmtp, ((0, 0), (0, 0), (0, LANE - N_META)))
            o_tok, o_meta = [], []
            for (qt, k, vt), (nb, s, b0) in zip(qkv_tok, groups):
                o_tok.append(_dense_attn(qt, k, vt, kmp, vmtp, n_seq=nb, seq=s, nq=qt.shape[0] // nb,
                                         q_base=0, meta_base=b0))
                om = _dense_attn(qmtp, k, vt, kmp, vmtp, n_seq=nb, seq=s, nq=1, q_base=b0, meta_base=b0)
                o_meta.append(om.reshape(nb, LANE, D_MODEL)[:, :N_META].reshape(nb * N_META, D_MODEL))
            o_meta = pad_rows(jnp.concatenate(o_meta, axis=0))
        else:
            w_qkv = na_w_qkv[j].astype(BF16)
            w_out = na_w_out[j].astype(BF16)
            bias, mb_t, mb = _na_bias_tables(na_rpb[j], na_meta_bias[j])
            qkv_tok = [_proj_na(h, gains, w_qkv, True) for h in h_tok]
            qm, km, vm = _proj_na(h_meta, gains, w_qkv, False)
            kmp, vmp = _pad_meta(km[:n_meta], n_seq), _pad_meta(vm[:n_meta], n_seq)
            km16 = km[:n_meta].reshape(n_seq, N_META, NA_HEADS * NA_HEAD_DIM)
            vmtp = vmp.transpose(0, 2, 1)
            o_tok = [_na_attn(q, k, vt, km16, vmtp, bias, mb_t, n_seq=nb, seq=s, meta_base=b0)
                     for (q, k, vt), (nb, s, b0) in zip(qkv_tok, groups)]
            o_meta = pad_rows(_na_meta(qm[:n_meta], kmp, vmp, mb))
        h_tok = [_mix_ffn2(h, o, w_out, gains, *w2, i) for h, o in zip(h_tok, o_tok)]
        h_meta = _mix_ffn2(h_meta, o_meta, w_out, gains, *w2, i)

    return (h_tok[0].reshape(bp, sp, D_MODEL), h_tok[1].reshape(bs, ss, D_MODEL))
```

```python
import functools
import math

import jax
import jax.numpy as jnp
import numpy as np
from jax import lax
from jax.experimental import pallas as pl
from jax.experimental.pallas import tpu as pltpu

F32 = jnp.float32
BF16 = jnp.bfloat16

D_MODEL = 1024
N_META = 16
GRID_W = 64
D_FF = 2816
EPS = 1e-6
NEG_INF = -1e30
LOG2E = math.log2(math.e)

MLA_HEADS = 8
MLA_Q_LORA = 256
MLA_KV_LORA = 128
MLA_NOPE = 64
MLA_ROPE = 32
MLA_V = 64
GQA_HEADS = 8
GQA_KV_HEADS = 2
GQA_HEAD_DIM = 64
ROPE_THETA = 10000.0
NA_HEADS = 16
NA_HEAD_DIM = 64
NA_WIN_R = 8
NA_WIN_C = 16

LANE = 128
HEAD_SLOTS = MLA_HEADS + GQA_HEADS
K_SLOTS = MLA_HEADS + 1
V_ROWS = (MLA_HEADS + GQA_KV_HEADS) * MLA_V
SUBLANE = 8
DENSE_LOOKAHEAD, DENSE_SLOTS = 2, 4
NA_LOOKAHEAD, NA_SLOTS = 2, 4
VMEM_LIMIT = 56 * 1024 * 1024
ROW_TILE = 512
PROJ_CHAINS = 2
FFN_ROW_TILE = 1024
FFN_CHAINS = 4
DENSE_SUBTILES = 8
NA_ROWS_PER_STEP = 16

_C_CQ = 0
_C_CKV = _C_CQ + MLA_Q_LORA
_C_KRA = _C_CKV + MLA_KV_LORA
_C_KRB = _C_KRA + LANE
_C_GQA = _C_KRB + LANE
_C_GQB = _C_GQA + GQA_HEADS // 2 * LANE
_C_GKA = _C_GQB + GQA_HEADS // 2 * LANE
_C_GKB = _C_GKA + LANE
_C_GV = _C_GKB + LANE
_C_END = _C_GV + LANE


def _const_spec(shape):
    nd = len(shape)
    return pl.BlockSpec(shape, lambda *_: (0,) * nd, pipeline_mode=pl.Buffered(1))


def _rms(x, g):
    ms = jnp.mean(x * x, axis=-1, keepdims=True)
    return x * lax.rsqrt(ms + EPS) * g


def _dot(a, b):
    return jnp.dot(a, b, preferred_element_type=F32)


def _dot_nt(a, b):
    return lax.dot_general(a, b, (((1,), (1,)), ((), ())), preferred_element_type=F32)


def _row_tile(rows, want):
    t = min(rows, want)
    while rows % t:
        t //= 2
    return t


def _ffn_rows(h, g_ref, pre, post, wg_ref, wu_ref, wd_ref):
    xn = _rms(h, g_ref[pre:pre + 1, :]).astype(BF16)
    gate = _dot(xn, wg_ref[...])
    up = _dot(xn, wu_ref[...])
    act = (gate * jax.nn.sigmoid(gate) * up).astype(BF16)
    y = _dot(act, wd_ref[...])
    return h + 0.5 * _rms(y, g_ref[post:post + 1, :])


def _ffn_body(h, g_ref, pre, post, wg_ref, wu_ref, wd_ref):
    part = h.shape[0] // FFN_CHAINS
    return jnp.concatenate([_ffn_rows(h[i * part:(i + 1) * part], g_ref, pre, post, wg_ref, wu_ref, wd_ref)
                            for i in range(FFN_CHAINS)], axis=0)


def _ffn1_kernel(h_ref, g_ref, wg_ref, wu_ref, wd_ref, out_ref):
    out_ref[...] = _ffn_body(h_ref[...], g_ref, 0, 1, wg_ref, wu_ref, wd_ref)


def _mix_ffn2_kernel(h_ref, o_ref, wo_ref, g_ref, wg_ref, wu_ref, wd_ref, out_ref):
    mixed = _dot(o_ref[...], wo_ref[...])
    h = h_ref[...] + _rms(mixed, g_ref[3:4, :])
    out_ref[...] = _ffn_body(h, g_ref, 4, 5, wg_ref, wu_ref, wd_ref)


def _layer_spec(w, layer):
    return pl.BlockSpec((None,) + w.shape[1:], lambda *_: (layer, 0, 0), pipeline_mode=pl.Buffered(1))


def _ffn1(h, gains, wg, wu, wd, layer):
    rows = h.shape[0]
    tm = _row_tile(rows, FFN_ROW_TILE)
    return pl.pallas_call(
        _ffn1_kernel,
        grid=(rows // tm,),
        in_specs=[
            pl.BlockSpec((tm, D_MODEL), lambda i: (i, 0)),
            _const_spec(gains.shape),
            _layer_spec(wg, layer), _layer_spec(wu, layer), _layer_spec(wd, layer),
        ],
        out_specs=pl.BlockSpec((tm, D_MODEL), lambda i: (i, 0)),
        out_shape=jax.ShapeDtypeStruct((rows, D_MODEL), F32),
        compiler_params=pltpu.CompilerParams(
            dimension_semantics=("arbitrary",), vmem_limit_bytes=VMEM_LIMIT),
        name="ffn1",
    )(h, gains, wg, wu, wd)


def _mix_ffn2(h, o, wo, gains, wg, wu, wd, layer):
    rows = h.shape[0]
    tm = _row_tile(rows, FFN_ROW_TILE)
    return pl.pallas_call(
        _mix_ffn2_kernel,
        grid=(rows // tm,),
        in_specs=[
            pl.BlockSpec((tm, D_MODEL), lambda i: (i, 0)),
            pl.BlockSpec((tm, o.shape[1]), lambda i: (i, 0)),
            _const_spec(wo.shape),
            _const_spec(gains.shape),
            _layer_spec(wg, layer), _layer_spec(wu, layer), _layer_spec(wd, layer),
        ],
        out_specs=pl.BlockSpec((tm, D_MODEL), lambda i: (i, 0)),
        out_shape=jax.ShapeDtypeStruct((rows, D_MODEL), F32),
        compiler_params=pltpu.CompilerParams(
            dimension_semantics=("arbitrary",), vmem_limit_bytes=VMEM_LIMIT),
        name="mix_ffn2",
    )(h, o, wo, gains, wg, wu, wd)


def _proj_dense_kernel(h_ref, g_ref, win_ref, qn_ref, wuq_ref, kvn_ref, wukv_ref,
                       gqa_ref, gqb_ref, gka_ref, gkb_ref, exp_ref, tab_ref, qt_ref, k_ref, vt_ref):
    a = _rms(h_ref[...], g_ref[2:3, :]).astype(BF16)
    proj = _dot(a, win_ref[...])
    pieces = jnp.concatenate([tab_ref[i] for i in range(tab_ref.shape[0])], axis=1)
    tab = _dot(pieces, exp_ref[...])
    cos_k, sin_k = tab[:, 0:LANE], tab[:, LANE:2 * LANE]
    cos_g, sin_g = tab[:, 2 * LANE:3 * LANE], tab[:, 3 * LANE:4 * LANE]
    qs = (MLA_NOPE + MLA_ROPE) ** -0.5 * LOG2E
    lane = lax.broadcasted_iota(jnp.int32, cos_k.shape, 1)
    cos_q = jnp.where(lane < MLA_NOPE, qs, cos_k * qs)
    sin_q = sin_k * qs

    cqn = _rms(proj[:, _C_CQ:_C_CQ + MLA_Q_LORA], qn_ref[...]).astype(BF16)
    qab = _dot(cqn, wuq_ref[...])
    nq = MLA_HEADS * LANE
    for h in range(MLA_HEADS):
        qa = qab[:, h * LANE:(h + 1) * LANE]
        qb = qab[:, nq + h * LANE:nq + (h + 1) * LANE]
        qt_ref[0, h * LANE:(h + 1) * LANE, :] = (qa * cos_q + qb * sin_q).T.astype(BF16)

    ckvn = _rms(proj[:, _C_CKV:_C_CKV + MLA_KV_LORA], kvn_ref[...]).astype(BF16)
    kv = _dot(ckvn, wukv_ref[...])
    k_rope = (proj[:, _C_KRA:_C_KRA + LANE] * cos_k + proj[:, _C_KRB:_C_KRB + LANE] * sin_k)
    for h in range(MLA_HEADS):
        k_ref[:, h * LANE:(h + 1) * LANE] = (kv[:, h * LANE:(h + 1) * LANE] + k_rope).astype(BF16)
    nv = MLA_HEADS * MLA_V

    low = lane < GQA_HEAD_DIM

    def normed_rotary(xa, xb, cos, sin):
        sq = xa * xa
        ss_lo = jnp.sum(jnp.where(low, sq, 0.0), axis=-1, keepdims=True)
        ss_hi = jnp.sum(jnp.where(low, 0.0, sq), axis=-1, keepdims=True)
        r = lax.rsqrt(jnp.where(low, ss_lo, ss_hi) * (1.0 / GQA_HEAD_DIM) + EPS)
        return (xa * cos + xb * sin) * r

    gq_scale = GQA_HEAD_DIM ** -0.5 * LOG2E
    cq_g = cos_g * (gqa_ref[...] * gq_scale)
    sq_g = sin_g * (gqb_ref[...] * gq_scale)
    per_kv = GQA_HEADS // GQA_KV_HEADS
    zeros_t = jnp.zeros((LANE - GQA_HEAD_DIM, qt_ref.shape[2]), F32)
    for j in range(GQA_HEADS // 2):
        xa = proj[:, _C_GQA + j * LANE:_C_GQA + (j + 1) * LANE]
        xb = proj[:, _C_GQB + j * LANE:_C_GQB + (j + 1) * LANE]
        y_t = normed_rotary(xa, xb, cq_g, sq_g).T
        for half in range(2):
            h = 2 * j + half
            q_t = y_t[half * GQA_HEAD_DIM:(half + 1) * GQA_HEAD_DIM]
            rows = [q_t, zeros_t] if h // per_kv == 0 else [zeros_t, q_t]
            qt_ref[0, nq + h * LANE:nq + (h + 1) * LANE, :] = jnp.concatenate(rows, axis=0).astype(BF16)
    xa = proj[:, _C_GKA:_C_GKA + LANE]
    xb = proj[:, _C_GKB:_C_GKB + LANE]
    k_ref[:, nq:nq + LANE] = normed_rotary(xa, xb, cos_g * gka_ref[...], sin_g * gkb_ref[...]).astype(BF16)
    v = jnp.concatenate([kv[:, nq:nq + nv], proj[:, _C_GV:_C_GV + LANE]], axis=1)
    vt_ref[0] = v.T.astype(BF16)


def _proj_dense(h, gains, w, tab, seq):
    rows = h.shape[0]
    tm = _row_tile(seq if seq else rows, ROW_TILE)
    nblk = (seq // tm) if seq else 1
    consts = [w["w_in"], w["q_norm"], w["w_uq"], w["kv_norm"], w["w_ukv"],
              w["gq_a"], w["gq_b"], w["gk_a"], w["gk_b"], w["rope_expand"]]
    return pl.pallas_call(
        _proj_dense_kernel,
        grid=(rows // tm,),
        in_specs=[pl.BlockSpec((tm, D_MODEL), lambda i: (i, 0)), _const_spec(gains.shape)]
        + [_const_spec(c.shape) for c in consts]
        + [pl.BlockSpec((tab.shape[0], tm, LANE), lambda i: (0, i % nblk, 0))],
        out_specs=[
            pl.BlockSpec((1, HEAD_SLOTS * LANE, tm), lambda i: (i, 0, 0)),
            pl.BlockSpec((tm, K_SLOTS * LANE), lambda i: (i, 0)),
            pl.BlockSpec((1, V_ROWS, tm), lambda i: (i, 0, 0)),
        ],
        out_shape=[
            jax.ShapeDtypeStruct((rows // tm, HEAD_SLOTS * LANE, tm), BF16),
            jax.ShapeDtypeStruct((rows, K_SLOTS * LANE), BF16),
            jax.ShapeDtypeStruct((rows // tm, V_ROWS, tm), BF16),
        ],
        compiler_params=pltpu.CompilerParams(
            dimension_semantics=("arbitrary",), vmem_limit_bytes=VMEM_LIMIT),
        name="proj_dense",
    )(h, gains, *consts, tab)


def _head_slots(h):
    if h < MLA_HEADS:
        return h, h
    kvh = (h - MLA_HEADS) // (GQA_HEADS // GQA_KV_HEADS)
    return MLA_HEADS, MLA_HEADS + kvh


def _sublane_bcast_max(x):
    return jnp.broadcast_to(jnp.max(x, axis=0, keepdims=True), x.shape)


def _dense_attn_kernel(qt_ref, k_ref, vt_ref, km_ref, vmt_ref, o_ref,
                       m_ref, smax_ref, acc_ref, s_ref, sm_ref):
    kv = pl.program_id(2)
    tq = qt_ref.shape[1]
    n_sub, _, tk = vt_ref.shape
    hd = MLA_V
    acc_rows = acc_ref.shape[1]

    def with_ones(vt):
        return jnp.concatenate([vt, jnp.ones((acc_rows - hd, vt.shape[1]), BF16)], axis=0)

    @pl.when(kv == 0)
    def _():
        for h in range(HEAD_SLOTS):
            ks, _ = _head_slots(h)
            qt = qt_ref[h * LANE:(h + 1) * LANE, :]
            sm_ref[h] = _dot(km_ref[0:N_META, ks * LANE:(ks + 1) * LANE], qt)
        zeros = jnp.zeros((LANE - N_META, tq), F32)
        for h in range(HEAD_SLOTS):
            _, vh = _head_slots(h)
            s3 = sm_ref[h].reshape(N_META // SUBLANE, SUBLANE, tq)
            m = _sublane_bcast_max(jnp.max(s3, axis=0))
            p3 = jnp.exp2(s3 - m[None])
            m_ref[h] = m
            p = jnp.concatenate([p3.reshape(N_META, tq), zeros], axis=0).astype(BF16)
            acc_ref[h] = _dot(with_ones(vmt_ref[vh * hd:(vh + 1) * hd, :]), p)

    n_slots = s_ref.shape[0]

    def scores(t, h):
        ks, _ = _head_slots(h)
        k0 = pl.multiple_of(t * tk, tk)
        k = k_ref[pl.ds(k0, tk), ks * LANE:(ks + 1) * LANE]
        s = _dot(k, qt_ref[h * LANE:(h + 1) * LANE, :])
        s_ref[h % n_slots] = s
        smax_ref[h] = jnp.max(s.reshape(tk // SUBLANE, SUBLANE, tq), axis=0)

    def softmax_pv(t, h):
        _, vh = _head_slots(h)
        m_prev = m_ref[h]
        m_new = jnp.maximum(m_prev, _sublane_bcast_max(smax_ref[h]))
        alpha = jnp.exp2(m_prev - m_new)
        s3 = s_ref[h % n_slots].reshape(tk // SUBLANE, SUBLANE, tq)
        p = jnp.exp2(s3 - m_new[None]).reshape(tk, tq).astype(BF16)
        pv = _dot(with_ones(vt_ref[t, vh * hd:(vh + 1) * hd, :]), p)
        acc = acc_ref[h].reshape(acc_rows // SUBLANE, SUBLANE, tq) * alpha[None]
        acc_ref[h] = acc.reshape(acc_rows, tq) + pv
        m_ref[h] = m_new

    for h in range(DENSE_LOOKAHEAD):
        scores(0, h)

    def sub_tile(t, carry):
        t_next = jnp.minimum(t + 1, n_sub - 1)
        for h in range(HEAD_SLOTS):
            ahead = h + DENSE_LOOKAHEAD
            if ahead < HEAD_SLOTS:
                scores(t, ahead)
            else:
                scores(t_next, ahead - HEAD_SLOTS)
            softmax_pv(t, h)
        return carry

    lax.fori_loop(0, n_sub, sub_tile, 0, unroll=2)

    @pl.when(kv == pl.num_programs(2) - 1)
    def _():
        for j in range(HEAD_SLOTS // 2):
            outs = []
            for h in (2 * j, 2 * j + 1):
                outs.append(acc_ref[h, 0:hd, :] / acc_ref[h, hd:hd + 1, :])
            o_t = jnp.concatenate(outs, axis=0)
            o_ref[:, j * LANE:(j + 1) * LANE] = o_t.T.astype(BF16)


def _dense_attn(qt, k, vt, km, vmt, *, n_seq, seq, nq, q_base, meta_base):
    tq = qt.shape[2]
    tk = vt.shape[2]
    n_sub = _row_tile(seq // tk, DENSE_SUBTILES)
    nk = seq // (tk * n_sub)
    out_rows = n_seq * nq * tq
    return pl.pallas_call(
        _dense_attn_kernel,
        grid=(n_seq, nq, nk),
        in_specs=[
            pl.BlockSpec((None, HEAD_SLOTS * LANE, tq), lambda b, i, j: (q_base + b * nq + i, 0, 0)),
            pl.BlockSpec((n_sub * tk, K_SLOTS * LANE), lambda b, i, j: (b * nk + j, 0)),
            pl.BlockSpec((n_sub, V_ROWS, tk), lambda b, i, j: (b * nk + j, 0, 0)),
            pl.BlockSpec((None, LANE, K_SLOTS * LANE), lambda b, i, j: (meta_base + b, 0, 0)),
            pl.BlockSpec((None, V_ROWS, LANE), lambda b, i, j: (meta_base + b, 0, 0)),
        ],
        out_specs=pl.BlockSpec((tq, D_MODEL), lambda b, i, j: (b * nq + i, 0)),
        out_shape=jax.ShapeDtypeStruct((out_rows, D_MODEL), BF16),
        scratch_shapes=[
            pltpu.VMEM((HEAD_SLOTS, SUBLANE, tq), F32),
            pltpu.VMEM((HEAD_SLOTS, SUBLANE, tq), F32),
            pltpu.VMEM((HEAD_SLOTS, MLA_V + 2 * SUBLANE, tq), F32),
            pltpu.VMEM((DENSE_SLOTS, tk, tq), F32),
            pltpu.VMEM((HEAD_SLOTS, N_META, tq), F32),
        ],
        compiler_params=pltpu.CompilerParams(
            dimension_semantics=("arbitrary", "arbitrary", "arbitrary"),
            vmem_limit_bytes=VMEM_LIMIT),
        name="dense_attn",
    )(qt, k, vt, km, vmt)


def _proj_na_kernel(h_ref, g_ref, w_ref, q_ref, k_ref, v_ref, *, transpose_v):
    n = NA_HEADS * NA_HEAD_DIM
    rows = h_ref.shape[0]
    part = rows // PROJ_CHAINS if rows % (PROJ_CHAINS * LANE) == 0 else rows
    for c in range(rows // part):
        r = slice(c * part, (c + 1) * part)
        a = _rms(h_ref[r, :], g_ref[2:3, :]).astype(BF16)
        qkv = _dot(a, w_ref[...])
        q_ref[r, :] = (qkv[:, 0:n] * (NA_HEAD_DIM ** -0.5 * LOG2E)).astype(BF16)
        k_ref[r, :] = qkv[:, n:2 * n].astype(BF16)
        v = qkv[:, 2 * n:3 * n]
        if transpose_v:
            vt = v.T.astype(BF16)
            for t in range(part // LANE):
                v_ref[c * (part // LANE) + t] = vt[:, t * LANE:(t + 1) * LANE]
        else:
            v_ref[r, :] = v.astype(BF16)


def _proj_na(h, gains, w, transpose_v):
    rows = h.shape[0]
    tm = _row_tile(rows, ROW_TILE)
    n = NA_HEADS * NA_HEAD_DIM
    if transpose_v:
        v_spec = pl.BlockSpec((tm // LANE, n, LANE), lambda i: (i, 0, 0))
        v_shape = jax.ShapeDtypeStruct((rows // LANE, n, LANE), BF16)
    else:
        v_spec = pl.BlockSpec((tm, n), lambda i: (i, 0))
        v_shape = jax.ShapeDtypeStruct((rows, n), BF16)
    return pl.pallas_call(
        functools.partial(_proj_na_kernel, transpose_v=transpose_v),
        grid=(rows // tm,),
        in_specs=[pl.BlockSpec((tm, D_MODEL), lambda i: (i, 0)), _const_spec(gains.shape),
                  _const_spec(w.shape)],
        out_specs=[pl.BlockSpec((tm, n), lambda i: (i, 0))] * 2 + [v_spec],
        out_shape=[jax.ShapeDtypeStruct((rows, n), BF16)] * 2 + [v_shape],
        compiler_params=pltpu.CompilerParams(
            dimension_semantics=("arbitrary",), vmem_limit_bytes=VMEM_LIMIT),
        name="proj_na",
    )(h, gains, w)


NA_SPAN_R = NA_WIN_R + 2
NA_MASKED = 2 * NA_WIN_R - 1


def _na_kernel(q_ref, k_ref, vt_ref, km_ref, vmt_ref, bias_ref, mb_ref, o_ref, s_ref,
               *, rows, rows_per_step):
    step = pl.program_id(1)
    n_pairs = rows_per_step // 2
    n_hp = NA_HEADS // 2
    span = NA_SPAN_R * GRID_W
    lane = lax.broadcasted_iota(jnp.int32, (GRID_W, LANE), 1)
    first = lane < (LANE // 2)
    zeros_m = jnp.zeros((LANE - N_META, 2 * LANE), F32)
    ones_v = jnp.ones((2 * SUBLANE, span + LANE), BF16)

    def geometry(rp):
        ra = step * rows_per_step + 2 * rp
        rs = [jnp.clip(ra + x - NA_WIN_R // 2, 0, rows - NA_WIN_R) for x in range(2)]
        ws = jnp.minimum((rs[0] // 2) * 2, rows - NA_SPAN_R)
        return ra, rs, ws

    n_slots = s_ref.shape[0]

    def scores(rp, hp):
        slot = hp % n_slots
        _, _, ws = geometry(rp)
        cols = slice(hp * LANE, (hp + 1) * LANE)
        parts = []
        for x in range(2):
            q0 = pl.multiple_of((2 * rp + x) * GRID_W, GRID_W)
            qx = q_ref[pl.ds(q0, GRID_W), cols]
            parts += [jnp.where(first, qx, jnp.zeros_like(qx)), jnp.where(first, jnp.zeros_like(qx), qx)]
        qblk = jnp.concatenate(parts, axis=0)
        k0 = pl.multiple_of(ws * GRID_W, 2 * GRID_W)
        s_ref[slot, 0:span, :] = _dot_nt(k_ref[pl.ds(k0, span), cols], qblk)
        s_ref[slot, span:span + N_META, :] = _dot_nt(km_ref[:, cols], qblk)

    for hp in range(NA_LOOKAHEAD):
        scores(0, hp)

    def row_pair(rp, carry):
        ra, rs, ws = geometry(rp)
        idx = []
        for jj in range(NA_SPAN_R):
            kr = ws + jj
            idx.append([jnp.where((kr >= rs[x]) & (kr < rs[x] + NA_WIN_R),
                                  kr - (ra + x) + NA_WIN_R - 1, NA_MASKED) for x in range(2)])
        t0 = ws // 2
        for hp in range(n_hp):
            slot = hp % n_slots
            ahead = hp + NA_LOOKAHEAD
            if ahead < n_hp:
                scores(rp, ahead)
            else:
                scores(jnp.minimum(rp + 1, n_pairs - 1), ahead - n_hp)
            cols = slice(hp * LANE, (hp + 1) * LANE)
            b = jnp.concatenate(
                [jnp.concatenate([bias_ref[hp, idx[jj][0]], bias_ref[hp, idx[jj][1]]], axis=1)
                 for jj in range(NA_SPAN_R)], axis=0)
            s = s_ref[slot, 0:span, :] + b
            mb = mb_ref[hp]
            sm = s_ref[slot, span:span + N_META, :] + jnp.concatenate([mb, mb], axis=1)
            s3 = s.reshape(span // SUBLANE, SUBLANE, 2 * LANE)
            sm3 = sm.reshape(N_META // SUBLANE, SUBLANE, 2 * LANE)
            m = _sublane_bcast_max(jnp.maximum(jnp.max(s3, axis=0), jnp.max(sm3, axis=0)))
            p = jnp.exp2(s3 - m[None]).reshape(span, 2 * LANE).astype(BF16)
            pm3 = jnp.exp2(sm3 - m[None])
            pm = jnp.concatenate([pm3.reshape(N_META, 2 * LANE), zeros_m], axis=0).astype(BF16)
            v_all = jnp.concatenate([vt_ref[t0 + t, cols, :] for t in range(span // LANE)]
                                    + [vmt_ref[cols, :]], axis=1)
            o_t = _dot(jnp.concatenate([v_all, ones_v], axis=0),
                       jnp.concatenate([p, pm], axis=0))
            o_t = o_t[0:LANE] / o_t[LANE:LANE + 1]
            for x in range(2):
                blk = o_t[:, x * LANE:(x + 1) * LANE].T
                q0 = pl.multiple_of((2 * rp + x) * GRID_W, GRID_W)
                o_ref[pl.ds(q0, GRID_W), cols] = jnp.where(
                    first, blk[0:GRID_W], blk[GRID_W:2 * GRID_W]).astype(BF16)
        return carry

    lax.fori_loop(0, n_pairs, row_pair, 0, unroll=4)


def _na_attn(q, k, vt, km, vmt, bias, mb, *, n_seq, seq, meta_base):
    rows = seq // GRID_W
    assert rows >= NA_SPAN_R and rows % 2 == 0
    rps = _row_tile(rows, NA_ROWS_PER_STEP)
    nsteps = rows // rps
    n = NA_HEADS * NA_HEAD_DIM
    span = NA_SPAN_R * GRID_W
    return pl.pallas_call(
        functools.partial(_na_kernel, rows=rows, rows_per_step=rps),
        grid=(n_seq, nsteps),
        in_specs=[
            pl.BlockSpec((rps * GRID_W, n), lambda b, i: (b * nsteps + i, 0)),
            pl.BlockSpec((seq, n), lambda b, i: (b, 0), pipeline_mode=pl.Buffered(1)),
            pl.BlockSpec((seq // LANE, n, LANE), lambda b, i: (b, 0, 0), pipeline_mode=pl.Buffered(1)),
            pl.BlockSpec((None, N_META, n), lambda b, i: (meta_base + b, 0, 0)),
            pl.BlockSpec((None, n, LANE), lambda b, i: (meta_base + b, 0, 0)),
            _const_spec(bias.shape),
            _const_spec(mb.shape),
        ],
        out_specs=pl.BlockSpec((rps * GRID_W, n), lambda b, i: (b * nsteps + i, 0)),
        out_shape=jax.ShapeDtypeStruct((n_seq * seq, n), BF16),
        scratch_shapes=[pltpu.VMEM((NA_SLOTS, span + N_META, 2 * LANE), F32)],
        compiler_params=pltpu.CompilerParams(
            dimension_semantics=("arbitrary", "arbitrary"), vmem_limit_bytes=VMEM_LIMIT),
        name="na_attn",
    )(q, k, vt, km, vmt, bias, mb)


def _na_meta_kernel(q_ref, km_ref, vm_ref, mb_ref, o_ref):
    lane = lax.broadcasted_iota(jnp.int32, (N_META, LANE), 1)
    first = lane < (LANE // 2)
    for j in range(NA_HEADS // 2):
        cols = slice(j * LANE, (j + 1) * LANE)
        qp = q_ref[:, cols]
        km = km_ref[:, cols]
        vm = vm_ref[:, cols]
        outs = []
        for half in range(2):
            h = 2 * j + half
            qh = jnp.where(first if half == 0 else jnp.logical_not(first), qp, jnp.zeros_like(qp))
            sm = _dot_nt(qh, km)
            sm = jnp.where(lane < N_META, sm + mb_ref[h:h + 1, :], NEG_INF)
            m = jnp.max(sm, axis=-1, keepdims=True)
            pm = jnp.exp2(sm - m)
            l = jnp.sum(pm, axis=-1, keepdims=True)
            outs.append(_dot(pm.astype(BF16), vm) / l)
        o_ref[:, cols] = jnp.where(first, outs[0], outs[1]).astype(BF16)


def _na_meta(qm, km, vm, mb):
    n_seq = km.shape[0]
    n = NA_HEADS * NA_HEAD_DIM
    return pl.pallas_call(
        _na_meta_kernel,
        grid=(n_seq,),
        in_specs=[
            pl.BlockSpec((N_META, n), lambda b: (b, 0)),
            pl.BlockSpec((None, LANE, n), lambda b: (b, 0, 0)),
            pl.BlockSpec((None, LANE, n), lambda b: (b, 0, 0)),
            _const_spec(mb.shape),
        ],
        out_specs=pl.BlockSpec((N_META, n), lambda b: (b, 0)),
        out_shape=jax.ShapeDtypeStruct((n_seq * N_META, n), BF16),
        compiler_params=pltpu.CompilerParams(dimension_semantics=("arbitrary",)),
        name="na_meta",
    )(qm, km, vm, mb)


def _take_cols(w, idx):
    idx = np.asarray(idx)
    neg = idx < 0
    same_run = np.where(neg[1:] | neg[:-1], neg[1:] & neg[:-1], np.diff(idx) == 1)
    breaks = np.flatnonzero(~same_run) + 1
    parts = []
    for run in np.split(idx, breaks):
        if run[0] < 0:
            parts.append(jnp.zeros((w.shape[0], len(run)), w.dtype))
        else:
            parts.append(w[:, int(run[0]):int(run[-1]) + 1])
    return jnp.concatenate(parts, axis=1)


def _swap_halves(n):
    half = n // 2
    return np.concatenate([np.arange(half, n), np.arange(0, half)])


def _dense_weights(w_in, q_norm, w_uq, kv_norm, w_ukv, gq_norm, gk_norm, w_out):
    pad = lambda k: -np.ones(k, np.int64)
    o_kr = MLA_Q_LORA + MLA_KV_LORA
    o_gq = o_kr + MLA_ROPE
    o_gk = o_gq + GQA_HEADS * GQA_HEAD_DIM
    o_gv = o_gk + GQA_KV_HEADS * GQA_HEAD_DIM
    axial = np.concatenate([_swap_halves(GQA_HEAD_DIM // 2),
                            GQA_HEAD_DIM // 2 + _swap_halves(GQA_HEAD_DIM // 2)])
    idx = [np.arange(0, o_kr)]
    idx += [pad(MLA_NOPE), o_kr + np.arange(MLA_ROPE), pad(LANE - MLA_NOPE - MLA_ROPE)]
    idx += [pad(MLA_NOPE), o_kr + _swap_halves(MLA_ROPE), pad(LANE - MLA_NOPE - MLA_ROPE)]
    assert 2 * GQA_HEAD_DIM == LANE and GQA_KV_HEADS == 2
    for h in range(GQA_HEADS):
        idx += [o_gq + h * GQA_HEAD_DIM + np.arange(GQA_HEAD_DIM)]
    for h in range(GQA_HEADS):
        idx += [o_gq + h * GQA_HEAD_DIM + axial]
    for h in range(GQA_KV_HEADS):
        idx += [o_gk + h * GQA_HEAD_DIM + np.arange(GQA_HEAD_DIM)]
    for h in range(GQA_KV_HEADS):
        idx += [o_gk + h * GQA_HEAD_DIM + axial]
    idx += [o_gv + np.arange(GQA_KV_HEADS * GQA_HEAD_DIM)]
    idx = np.concatenate(idx)
    assert idx.shape[0] == _C_END
    w_in2 = _take_cols(w_in.astype(BF16), idx)

    hd = MLA_NOPE + MLA_ROPE
    ia, ib = [], []
    for h in range(MLA_HEADS):
        ia += [h * hd + np.arange(hd), pad(LANE - hd)]
        ib += [pad(MLA_NOPE), h * hd + MLA_NOPE + _swap_halves(MLA_ROPE), pad(LANE - hd)]
    w_uq2 = _take_cols(w_uq.astype(BF16), np.concatenate(ia + ib))

    kvd = MLA_NOPE + MLA_V
    ik, iv = [], []
    for h in range(MLA_HEADS):
        ik += [h * kvd + np.arange(MLA_NOPE), pad(LANE - MLA_NOPE)]
        iv += [h * kvd + MLA_NOPE + np.arange(MLA_V)]
    w_ukv2 = _take_cols(w_ukv.astype(BF16), np.concatenate(ik + iv))

    def gain_pair(g):
        ga = jnp.concatenate([g, g])[None, :]
        gb = jnp.tile(g[jnp.asarray(axial)], 2)[None, :]
        return ga, gb

    gq_a, gq_b = gain_pair(gq_norm)
    gk_a, gk_b = gain_pair(gk_norm)

    w_out2 = w_out.astype(BF16)

    return dict(w_in=w_in2, q_norm=q_norm[None, :], w_uq=w_uq2, kv_norm=kv_norm[None, :],
                w_ukv=w_ukv2, gq_a=gq_a, gq_b=gq_b, gk_a=gk_a, gk_b=gk_b,
                rope_expand=_rope_expand_matrix()), w_out2


def _rope_tables(pos, row, row_repeat, col, col_tile):
    half = MLA_ROPE // 2
    inv = 1.0 / (ROPE_THETA ** (jnp.arange(half, dtype=F32) / half))
    n = pos.shape[0]

    def cs(p):
        ang = p.astype(F32)[None, :] * inv[:, None]
        return jnp.cos(ang), jnp.sin(ang)

    by_row = tuple(jnp.repeat(x, row_repeat, axis=1) for x in cs(row))
    by_col = tuple(jnp.tile(x, (1, col_tile)) for x in cs(col))
    vals = jnp.concatenate(cs(pos) + by_row + by_col + (jnp.zeros((LANE - 6 * half, n), F32),), axis=0).T
    hi = vals.astype(BF16)
    rest = vals - hi.astype(F32)
    mid = rest.astype(BF16)
    lo = (rest - mid.astype(F32)).astype(BF16)
    return jnp.stack([hi, mid, lo])


def _rope_expand_matrix():
    half = MLA_ROPE // 2
    c1, s1, cr, sr, cc, sc = range(6)
    tail = LANE - MLA_NOPE - MLA_ROPE
    blank = [(None, 0)]
    layout = (blank * (MLA_NOPE // half) + [(c1, 1), (c1, 1)] + blank * (tail // half)
              + blank * (MLA_NOPE // half) + [(s1, -1), (s1, 1)] + blank * (tail // half)
              + [(cr, 1), (cr, 1), (cc, 1), (cc, 1)] * (LANE // GQA_HEAD_DIM)
              + [(sr, -1), (sr, 1), (sc, -1), (sc, 1)] * (LANE // GQA_HEAD_DIM))
    expand = np.zeros((LANE, len(layout) * half), np.float32)
    for blk, (src, sign) in enumerate(layout):
        if src is not None:
            expand[src * half + np.arange(half), blk * half + np.arange(half)] = sign
    return jnp.asarray(np.tile(expand, (3, 1)), BF16)


def _na_bias_tables(rpb, meta_bias):
    c_idx = np.arange(GRID_W)
    c_start = np.clip(c_idx - NA_WIN_C // 2, 0, GRID_W - NA_WIN_C)
    col_mask = (c_idx[None, :] >= c_start[:, None]) & (c_idx[None, :] < c_start[:, None] + NA_WIN_C)
    col_off = np.clip(c_idx[None, :] - c_idx[:, None] + NA_WIN_C - 1, 0, 2 * NA_WIN_C - 2)
    hp = NA_HEADS // 2
    n_off = 2 * NA_WIN_C - 1
    select = np.zeros((2, n_off, GRID_W, 2, GRID_W), np.float32)
    kc_g, c_g = np.meshgrid(c_idx, c_idx, indexing="ij")
    for half in range(2):
        select[half, col_off[c_g, kc_g], kc_g, half, c_g] = 1.0
    rows = rpb.reshape(hp, 2, NA_MASKED, n_off).transpose(0, 2, 1, 3).reshape(hp * NA_MASKED, 2 * n_off)
    t = jnp.dot(rows, jnp.asarray(select.reshape(2 * n_off, GRID_W * LANE)),
                precision=lax.Precision.HIGHEST) * LOG2E
    t = t.reshape(hp, NA_MASKED, GRID_W, LANE)
    keep = np.tile(col_mask.T, (1, 2))
    t = jnp.where(jnp.asarray(keep)[None, None], t, NEG_INF)
    bias = jnp.concatenate([t, jnp.full_like(t[:, :1], NEG_INF)], axis=1)
    mbl = meta_bias * LOG2E
    mb_t = jnp.repeat(mbl.reshape(hp, 2, N_META).transpose(0, 2, 1), GRID_W, axis=2)
    mb = jnp.pad(mbl, ((0, 0), (0, LANE - N_META)))
    return bias, mb_t, mb


def _pad_meta(x, n_seq):
    c = x.shape[1]
    return jnp.pad(x.reshape(n_seq, N_META, c), ((0, 0), (0, LANE - N_META), (0, 0)))


def kernel(x_prompt, x_sample, meta, norm_gains, ffn1_w_gate, ffn1_w_up, ffn1_w_down, ffn2_w_gate, ffn2_w_up, ffn2_w_down, attn_w_in, mla_q_norm, mla_w_uq, mla_kv_norm, mla_w_ukv, gqa_q_norm, gqa_k_norm, attn_w_out, na_w_qkv, na_rpb, na_meta_bias, na_w_out):
    bp, sp, _ = x_prompt.shape
    bs, ss, _ = x_sample.shape
    n_seq = bp + bs
    depth = norm_gains.shape[0]
    groups = [(bp, sp, 0), (bs, ss, bp)]

    n_meta = n_seq * N_META
    meta_rows = -(-n_meta // LANE) * LANE
    pad_rows = lambda x: jnp.pad(x, ((0, meta_rows - x.shape[0]), (0, 0)))
    h_tok = [x_prompt.reshape(bp * sp, D_MODEL), x_sample.reshape(bs * ss, D_MODEL)]
    h_meta = pad_rows(jnp.tile(meta.astype(F32), (n_seq, 1)))

    smax = max(sp, ss)
    tab_tok = _rope_tables(jnp.arange(smax) + N_META, jnp.arange(smax // GRID_W), GRID_W,
                           jnp.arange(GRID_W), smax // GRID_W)
    tab_meta = _rope_tables(jnp.arange(meta_rows) % N_META, jnp.full((1,), -1), meta_rows,
                            jnp.arange(N_META), meta_rows // N_META)

    w1 = (ffn1_w_gate.astype(BF16), ffn1_w_up.astype(BF16), ffn1_w_down.astype(BF16))
    w2 = (ffn2_w_gate.astype(BF16), ffn2_w_up.astype(BF16), ffn2_w_down.astype(BF16))

    for i in range(depth):
        gains = jnp.pad(norm_gains[i], ((0, 2), (0, 0)))
        j = i // 2
        h_tok = [_ffn1(h, gains, *w1, i) for h in h_tok]
        h_meta = _ffn1(h_meta, gains, *w1, i)
        if i % 2 == 0:
            w, w_out = _dense_weights(attn_w_in[j], mla_q_norm[j], mla_w_uq[j], mla_kv_norm[j],
                                      mla_w_ukv[j], gqa_q_norm[j], gqa_k_norm[j], attn_w_out[j])
            qkv_tok = [_proj_dense(h, gains, w, tab_tok, s) for h, (_, s, _) in zip(h_tok, groups)]
            qmt, km, vmt = _proj_dense(h_meta, gains, w, tab_meta, 0)
            kmp = _pad_meta(km[:n_meta], n_seq)
            vmt = vmt.transpose(1, 0, 2).reshape(V_ROWS, meta_rows)
            vmtp = vmt[:, :n_meta].reshape(V_ROWS, n_seq, N_META).transpose(1, 0, 2)
            vmtp = jnp.pad(vmtp, ((0, 0), (0, 0), (0, LANE - N_META)))
            qmt = qmt.transpose(1, 0, 2).reshape(HEAD_SLOTS * LANE, meta_rows)
            qmtp = qmt[:, :n_meta].reshape(HEAD_SLOTS * LANE, n_seq, N_META).transpose(1, 0, 2)
            qmtp = jnp.pad(qmtp, ((0, 0), (0, 0), (0, LANE - N_META)))
            o_tok, o_meta = [], []
            for (qt, k, vt), (nb, s, b0) in zip(qkv_tok, groups):
                o_tok.append(_dense_attn(qt, k, vt, kmp, vmtp, n_seq=nb, seq=s, nq=qt.shape[0] // nb,
                                         q_base=0, meta_base=b0))
                om = _dense_attn(qmtp, k, vt, kmp, vmtp, n_seq=nb, seq=s, nq=1, q_base=b0, meta_base=b0)
                o_meta.append(om.reshape(nb, LANE, D_MODEL)[:, :N_META].reshape(nb * N_META, D_MODEL))
            o_meta = pad_rows(jnp.concatenate(o_meta, axis=0))
        else:
            w_qkv = na_w_qkv[j].astype(BF16)
            w_out = na_w_out[j].astype(BF16)
            bias, mb_t, mb = _na_bias_tables(na_rpb[j], na_meta_bias[j])
            qkv_tok = [_proj_na(h, gains, w_qkv, True) for h in h_tok]
            qm, km, vm = _proj_na(h_meta, gains, w_qkv, False)
            kmp, vmp = _pad_meta(km[:n_meta], n_seq), _pad_meta(vm[:n_meta], n_seq)
            km16 = km[:n_meta].reshape(n_seq, N_META, NA_HEADS * NA_HEAD_DIM)
            vmtp = vmp.transpose(0, 2, 1)
            o_tok = [_na_attn(q, k, vt, km16, vmtp, bias, mb_t, n_seq=nb, seq=s, meta_base=b0)
                     for (q, k, vt), (nb, s, b0) in zip(qkv_tok, groups)]
            o_meta = pad_rows(_na_meta(qm[:n_meta], kmp, vmp, mb))
        h_tok = [_mix_ffn2(h, o, w_out, gains, *w2, i) for h, o in zip(h_tok, o_tok)]
        h_meta = _mix_ffn2(h_meta, o_meta, w_out, gains, *w2, i)

    return (h_tok[0].reshape(bp, sp, D_MODEL), h_tok[1].reshape(bs, ss, D_MODEL))
```

```python
import functools
import math

import jax
import jax.numpy as jnp
import numpy as np
from jax import lax
from jax.experimental import pallas as pl
from jax.experimental.pallas import tpu as pltpu

F32 = jnp.float32
BF16 = jnp.bfloat16

D_MODEL = 1024
N_META = 16
GRID_W = 64
D_FF = 2816
EPS = 1e-6
NEG_INF = -1e30
LOG2E = math.log2(math.e)

MLA_HEADS = 8
MLA_Q_LORA = 256
MLA_KV_LORA = 128
MLA_NOPE = 64
MLA_ROPE = 32
MLA_V = 64
GQA_HEADS = 8
GQA_KV_HEADS = 2
GQA_HEAD_DIM = 64
ROPE_THETA = 10000.0
NA_HEADS = 16
NA_HEAD_DIM = 64
NA_WIN_R = 8
NA_WIN_C = 16

LANE = 128
HEAD_SLOTS = MLA_HEADS + GQA_HEADS
K_SLOTS = MLA_HEADS + 1
V_ROWS = (MLA_HEADS + GQA_KV_HEADS) * MLA_V
SUBLANE = 8
DENSE_LOOKAHEAD, DENSE_SLOTS = 2, 4
NA_LOOKAHEAD, NA_SLOTS = 2, 4
VMEM_LIMIT = 56 * 1024 * 1024
ROW_TILE = 512
PROJ_CHAINS = 2
FFN_ROW_TILE = 1024
FFN_CHAINS = 4
DENSE_SUBTILES = 8
NA_ROWS_PER_STEP = 16

_C_CQ = 0
_C_CKV = _C_CQ + MLA_Q_LORA
_C_KRA = _C_CKV + MLA_KV_LORA
_C_KRB = _C_KRA + LANE
_C_GQA = _C_KRB + LANE
_C_GQB = _C_GQA + GQA_HEADS // 2 * LANE
_C_GKA = _C_GQB + GQA_HEADS // 2 * LANE
_C_GKB = _C_GKA + LANE
_C_GV = _C_GKB + LANE
_C_END = _C_GV + LANE


def _const_spec(shape):
    nd = len(shape)
    return pl.BlockSpec(shape, lambda *_: (0,) * nd, pipeline_mode=pl.Buffered(1))


def _rms(x, g):
    ms = jnp.mean(x * x, axis=-1, keepdims=True)
    return x * lax.rsqrt(ms + EPS) * g


def _dot(a, b):
    return jnp.dot(a, b, preferred_element_type=F32)


def _dot_nt(a, b):
    return lax.dot_general(a, b, (((1,), (1,)), ((), ())), preferred_element_type=F32)


def _row_tile(rows, want):
    t = min(rows, want)
    while rows % t:
        t //= 2
    return t


def _ffn_rows(h, g_ref, pre, post, wg_ref, wu_ref, wd_ref):
    xn = _rms(h, g_ref[pre:pre + 1, :]).astype(BF16)
    gate = _dot(xn, wg_ref[...])
    up = _dot(xn, wu_ref[...])
    act = (gate * jax.nn.sigmoid(gate) * up).astype(BF16)
    y = _dot(act, wd_ref[...])
    return h + 0.5 * _rms(y, g_ref[post:post + 1, :])


def _ffn_body(h, g_ref, pre, post, wg_ref, wu_ref, wd_ref):
    part = max(h.shape[0] // FFN_CHAINS, FFN_ROW_TILE // FFN_CHAINS)
    return jnp.concatenate([_ffn_rows(h[i * part:(i + 1) * part], g_ref, pre, post, wg_ref, wu_ref, wd_ref)
                            for i in range(h.shape[0] // part)], axis=0)


def _ffn1_kernel(h_ref, g_ref, wg_ref, wu_ref, wd_ref, out_ref):
    out_ref[...] = _ffn_body(h_ref[...], g_ref, 0, 1, wg_ref, wu_ref, wd_ref)


def _mix_ffn2_kernel(h_ref, o_ref, wo_ref, g_ref, wg_ref, wu_ref, wd_ref, out_ref):
    mixed = _dot(o_ref[...], wo_ref[...])
    h = h_ref[...] + _rms(mixed, g_ref[3:4, :])
    out_ref[...] = _ffn_body(h, g_ref, 4, 5, wg_ref, wu_ref, wd_ref)


def _layer_spec(w, layer):
    return pl.BlockSpec((None,) + w.shape[1:], lambda *_: (layer, 0, 0), pipeline_mode=pl.Buffered(1))


def _ffn1(h, gains, wg, wu, wd, layer):
    rows = h.shape[0]
    tm = _row_tile(rows, FFN_ROW_TILE)
    return pl.pallas_call(
        _ffn1_kernel,
        grid=(rows // tm,),
        in_specs=[
            pl.BlockSpec((tm, D_MODEL), lambda i: (i, 0)),
            _const_spec(gains.shape),
            _layer_spec(wg, layer), _layer_spec(wu, layer), _layer_spec(wd, layer),
        ],
        out_specs=pl.BlockSpec((tm, D_MODEL), lambda i: (i, 0)),
        out_shape=jax.ShapeDtypeStruct((rows, D_MODEL), F32),
        compiler_params=pltpu.CompilerParams(
            dimension_semantics=("arbitrary",), vmem_limit_bytes=VMEM_LIMIT),
        name="ffn1",
    )(h, gains, wg, wu, wd)


def _mix_ffn2(h, o, wo, gains, wg, wu, wd, layer):
    rows = h.shape[0]
    tm = _row_tile(rows, FFN_ROW_TILE)
    return pl.pallas_call(
        _mix_ffn2_kernel,
        grid=(rows // tm,),
        in_specs=[
            pl.BlockSpec((tm, D_MODEL), lambda i: (i, 0)),
            pl.BlockSpec((tm, o.shape[1]), lambda i: (i, 0)),
            _const_spec(wo.shape),
            _const_spec(gains.shape),
            _layer_spec(wg, layer), _layer_spec(wu, layer), _layer_spec(wd, layer),
        ],
        out_specs=pl.BlockSpec((tm, D_MODEL), lambda i: (i, 0)),
        out_shape=jax.ShapeDtypeStruct((rows, D_MODEL), F32),
        compiler_params=pltpu.CompilerParams(
            dimension_semantics=("arbitrary",), vmem_limit_bytes=VMEM_LIMIT),
        name="mix_ffn2",
    )(h, o, wo, gains, wg, wu, wd)


def _proj_dense_kernel(h_ref, g_ref, win_ref, qn_ref, wuq_ref, kvn_ref, wukv_ref,
                       gqa_ref, gqb_ref, gka_ref, gkb_ref, exp_ref, tab_ref, qt_ref, k_ref, vt_ref):
    a = _rms(h_ref[...], g_ref[2:3, :]).astype(BF16)
    proj = _dot(a, win_ref[...])
    pieces = jnp.concatenate([tab_ref[i] for i in range(tab_ref.shape[0])], axis=1)
    tab = _dot(pieces, exp_ref[...])
    cos_k, sin_k = tab[:, 0:LANE], tab[:, LANE:2 * LANE]
    cos_g, sin_g = tab[:, 2 * LANE:3 * LANE], tab[:, 3 * LANE:4 * LANE]
    qs = (MLA_NOPE + MLA_ROPE) ** -0.5 * LOG2E
    lane = lax.broadcasted_iota(jnp.int32, cos_k.shape, 1)
    cos_q = jnp.where(lane < MLA_NOPE, qs, cos_k * qs)
    sin_q = sin_k * qs

    cqn = _rms(proj[:, _C_CQ:_C_CQ + MLA_Q_LORA], qn_ref[...]).astype(BF16)
    qab = _dot(cqn, wuq_ref[...])
    nq = MLA_HEADS * LANE
    for h in range(MLA_HEADS):
        qa = qab[:, h * LANE:(h + 1) * LANE]
        qb = qab[:, nq + h * LANE:nq + (h + 1) * LANE]
        qt_ref[0, h * LANE:(h + 1) * LANE, :] = (qa * cos_q + qb * sin_q).T.astype(BF16)

    ckvn = _rms(proj[:, _C_CKV:_C_CKV + MLA_KV_LORA], kvn_ref[...]).astype(BF16)
    kv = _dot(ckvn, wukv_ref[...])
    k_rope = (proj[:, _C_KRA:_C_KRA + LANE] * cos_k + proj[:, _C_KRB:_C_KRB + LANE] * sin_k)
    for h in range(MLA_HEADS):
        k_ref[:, h * LANE:(h + 1) * LANE] = (kv[:, h * LANE:(h + 1) * LANE] + k_rope).astype(BF16)
    nv = MLA_HEADS * MLA_V

    low = lane < GQA_HEAD_DIM

    def normed_rotary(xa, xb, cos, sin):
        sq = xa * xa
        ss_lo = jnp.sum(jnp.where(low, sq, 0.0), axis=-1, keepdims=True)
        ss_hi = jnp.sum(jnp.where(low, 0.0, sq), axis=-1, keepdims=True)
        r = lax.rsqrt(jnp.where(low, ss_lo, ss_hi) * (1.0 / GQA_HEAD_DIM) + EPS)
        return (xa * cos + xb * sin) * r

    gq_scale = GQA_HEAD_DIM ** -0.5 * LOG2E
    cq_g = cos_g * (gqa_ref[...] * gq_scale)
    sq_g = sin_g * (gqb_ref[...] * gq_scale)
    per_kv = GQA_HEADS // GQA_KV_HEADS
    zeros_t = jnp.zeros((LANE - GQA_HEAD_DIM, qt_ref.shape[2]), F32)
    for j in range(GQA_HEADS // 2):
        xa = proj[:, _C_GQA + j * LANE:_C_GQA + (j + 1) * LANE]
        xb = proj[:, _C_GQB + j * LANE:_C_GQB + (j + 1) * LANE]
        y_t = normed_rotary(xa, xb, cq_g, sq_g).T
        for half in range(2):
            h = 2 * j + half
            q_t = y_t[half * GQA_HEAD_DIM:(half + 1) * GQA_HEAD_DIM]
            rows = [q_t, zeros_t] if h // per_kv == 0 else [zeros_t, q_t]
            qt_ref[0, nq + h * LANE:nq + (h + 1) * LANE, :] = jnp.concatenate(rows, axis=0).astype(BF16)
    xa = proj[:, _C_GKA:_C_GKA + LANE]
    xb = proj[:, _C_GKB:_C_GKB + LANE]
    k_ref[:, nq:nq + LANE] = normed_rotary(xa, xb, cos_g * gka_ref[...], sin_g * gkb_ref[...]).astype(BF16)
    v = jnp.concatenate([kv[:, nq:nq + nv], proj[:, _C_GV:_C_GV + LANE]], axis=1)
    vt_ref[0] = v.T.astype(BF16)


def _proj_dense(h, gains, w, tab, seq):
    rows = h.shape[0]
    tm = _row_tile(seq if seq else rows, ROW_TILE)
    nblk = (seq // tm) if seq else 1
    consts = [w["w_in"], w["q_norm"], w["w_uq"], w["kv_norm"], w["w_ukv"],
              w["gq_a"], w["gq_b"], w["gk_a"], w["gk_b"], w["rope_expand"]]
    return pl.pallas_call(
        _proj_dense_kernel,
        grid=(rows // tm,),
        in_specs=[pl.BlockSpec((tm, D_MODEL), lambda i: (i, 0)), _const_spec(gains.shape)]
        + [_const_spec(c.shape) for c in consts]
        + [pl.BlockSpec((tab.shape[0], tm, LANE), lambda i: (0, i % nblk, 0))],
        out_specs=[
            pl.BlockSpec((1, HEAD_SLOTS * LANE, tm), lambda i: (i, 0, 0)),
            pl.BlockSpec((tm, K_SLOTS * LANE), lambda i: (i, 0)),
            pl.BlockSpec((1, V_ROWS, tm), lambda i: (i, 0, 0)),
        ],
        out_shape=[
            jax.ShapeDtypeStruct((rows // tm, HEAD_SLOTS * LANE, tm), BF16),
            jax.ShapeDtypeStruct((rows, K_SLOTS * LANE), BF16),
            jax.ShapeDtypeStruct((rows // tm, V_ROWS, tm), BF16),
        ],
        compiler_params=pltpu.CompilerParams(
            dimension_semantics=("arbitrary",), vmem_limit_bytes=VMEM_LIMIT),
        name="proj_dense",
    )(h, gains, *consts, tab)


def _head_slots(h):
    if h < MLA_HEADS:
        return h, h
    kvh = (h - MLA_HEADS) // (GQA_HEADS // GQA_KV_HEADS)
    return MLA_HEADS, MLA_HEADS + kvh


def _sublane_bcast_max(x):
    return jnp.broadcast_to(jnp.max(x, axis=0, keepdims=True), x.shape)


def _dense_attn_kernel(qt_ref, k_ref, vt_ref, km_ref, vmt_ref, o_ref,
                       m_ref, smax_ref, acc_ref, s_ref, sm_ref):
    kv = pl.program_id(2)
    tq = qt_ref.shape[1]
    n_sub, _, tk = vt_ref.shape
    hd = MLA_V
    acc_rows = acc_ref.shape[1]

    def with_ones(vt):
        return jnp.concatenate([vt, jnp.ones((acc_rows - hd, vt.shape[1]), BF16)], axis=0)

    @pl.when(kv == 0)
    def _():
        for h in range(HEAD_SLOTS):
            ks, _ = _head_slots(h)
            qt = qt_ref[h * LANE:(h + 1) * LANE, :]
            sm_ref[h] = _dot(km_ref[0:N_META, ks * LANE:(ks + 1) * LANE], qt)
        zeros = jnp.zeros((LANE - N_META, tq), F32)
        for h in range(HEAD_SLOTS):
            _, vh = _head_slots(h)
            s3 = sm_ref[h].reshape(N_META // SUBLANE, SUBLANE, tq)
            m = _sublane_bcast_max(jnp.max(s3, axis=0))
            p3 = jnp.exp2(s3 - m[None])
            m_ref[h] = m
            p = jnp.concatenate([p3.reshape(N_META, tq), zeros], axis=0).astype(BF16)
            acc_ref[h] = _dot(with_ones(vmt_ref[vh * hd:(vh + 1) * hd, :]), p)

    n_slots = s_ref.shape[0]

    def scores(t, h):
        ks, _ = _head_slots(h)
        k0 = pl.multiple_of(t * tk, tk)
        k = k_ref[pl.ds(k0, tk), ks * LANE:(ks + 1) * LANE]
        s = _dot(k, qt_ref[h * LANE:(h + 1) * LANE, :])
        s_ref[h % n_slots] = s
        smax_ref[h] = jnp.max(s.reshape(tk // SUBLANE, SUBLANE, tq), axis=0)

    def softmax_pv(t, h):
        _, vh = _head_slots(h)
        m_prev = m_ref[h]
        m_new = jnp.maximum(m_prev, _sublane_bcast_max(smax_ref[h]))
        alpha = jnp.exp2(m_prev - m_new)
        s3 = s_ref[h % n_slots].reshape(tk // SUBLANE, SUBLANE, tq)
        p = jnp.exp2(s3 - m_new[None]).reshape(tk, tq).astype(BF16)
        pv = _dot(with_ones(vt_ref[t, vh * hd:(vh + 1) * hd, :]), p)
        acc = acc_ref[h].reshape(acc_rows // SUBLANE, SUBLANE, tq) * alpha[None]
        acc_ref[h] = acc.reshape(acc_rows, tq) + pv
        m_ref[h] = m_new

    for h in range(DENSE_LOOKAHEAD):
        scores(0, h)

    def sub_tile(t, carry):
        t_next = jnp.minimum(t + 1, n_sub - 1)
        for h in range(HEAD_SLOTS):
            ahead = h + DENSE_LOOKAHEAD
            if ahead < HEAD_SLOTS:
                scores(t, ahead)
            else:
                scores(t_next, ahead - HEAD_SLOTS)
            softmax_pv(t, h)
        return carry

    lax.fori_loop(0, n_sub, sub_tile, 0, unroll=2)

    @pl.when(kv == pl.num_programs(2) - 1)
    def _():
        for j in range(HEAD_SLOTS // 2):
            outs = []
            for h in (2 * j, 2 * j + 1):
                outs.append(acc_ref[h, 0:hd, :] / acc_ref[h, hd:hd + 1, :])
            o_t = jnp.concatenate(outs, axis=0)
            o_ref[:, j * LANE:(j + 1) * LANE] = o_t.T.astype(BF16)


def _dense_attn(qt, k, vt, km, vmt, *, n_seq, seq, nq, q_base, meta_base):
    tq = qt.shape[2]
    tk = vt.shape[2]
    n_sub = _row_tile(seq // tk, DENSE_SUBTILES)
    nk = seq // (tk * n_sub)
    out_rows = n_seq * nq * tq
    return pl.pallas_call(
        _dense_attn_kernel,
        grid=(n_seq, nq, nk),
        in_specs=[
            pl.BlockSpec((None, HEAD_SLOTS * LANE, tq), lambda b, i, j: (q_base + b * nq + i, 0, 0)),
            pl.BlockSpec((n_sub * tk, K_SLOTS * LANE), lambda b, i, j: (b * nk + j, 0)),
            pl.BlockSpec((n_sub, V_ROWS, tk), lambda b, i, j: (b * nk + j, 0, 0)),
            pl.BlockSpec((None, LANE, K_SLOTS * LANE), lambda b, i, j: (meta_base + b, 0, 0)),
            pl.BlockSpec((None, V_ROWS, LANE), lambda b, i, j: (meta_base + b, 0, 0)),
        ],
        out_specs=pl.BlockSpec((tq, D_MODEL), lambda b, i, j: (b * nq + i, 0)),
        out_shape=jax.ShapeDtypeStruct((out_rows, D_MODEL), BF16),
        scratch_shapes=[
            pltpu.VMEM((HEAD_SLOTS, SUBLANE, tq), F32),
            pltpu.VMEM((HEAD_SLOTS, SUBLANE, tq), F32),
            pltpu.VMEM((HEAD_SLOTS, MLA_V + 2 * SUBLANE, tq), F32),
            pltpu.VMEM((DENSE_SLOTS, tk, tq), F32),
            pltpu.VMEM((HEAD_SLOTS, N_META, tq), F32),
        ],
        compiler_params=pltpu.CompilerParams(
            dimension_semantics=("arbitrary", "arbitrary", "arbitrary"),
            vmem_limit_bytes=VMEM_LIMIT),
        name="dense_attn",
    )(qt, k, vt, km, vmt)


def _proj_na_kernel(h_ref, g_ref, w_ref, q_ref, k_ref, v_ref, *, transpose_v):
    n = NA_HEADS * NA_HEAD_DIM
    rows = h_ref.shape[0]
    part = rows // PROJ_CHAINS if rows % (PROJ_CHAINS * LANE) == 0 else rows
    for c in range(rows // part):
        r = slice(c * part, (c + 1) * part)
        a = _rms(h_ref[r, :], g_ref[2:3, :]).astype(BF16)
        qkv = _dot(a, w_ref[...])
        q_ref[r, :] = (qkv[:, 0:n] * (NA_HEAD_DIM ** -0.5 * LOG2E)).astype(BF16)
        k_ref[r, :] = qkv[:, n:2 * n].astype(BF16)
        v = qkv[:, 2 * n:3 * n]
        if transpose_v:
            vt = v.T.astype(BF16)
            for t in range(part // LANE):
                v_ref[c * (part // LANE) + t] = vt[:, t * LANE:(t + 1) * LANE]
        else:
            v_ref[r, :] = v.astype(BF16)


def _proj_na(h, gains, w, transpose_v):
    rows = h.shape[0]
    tm = _row_tile(rows, ROW_TILE)
    n = NA_HEADS * NA_HEAD_DIM
    if transpose_v:
        v_spec = pl.BlockSpec((tm // LANE, n, LANE), lambda i: (i, 0, 0))
        v_shape = jax.ShapeDtypeStruct((rows // LANE, n, LANE), BF16)
    else:
        v_spec = pl.BlockSpec((tm, n), lambda i: (i, 0))
        v_shape = jax.ShapeDtypeStruct((rows, n), BF16)
    return pl.pallas_call(
        functools.partial(_proj_na_kernel, transpose_v=transpose_v),
        grid=(rows // tm,),
        in_specs=[pl.BlockSpec((tm, D_MODEL), lambda i: (i, 0)), _const_spec(gains.shape),
                  _const_spec(w.shape)],
        out_specs=[pl.BlockSpec((tm, n), lambda i: (i, 0))] * 2 + [v_spec],
        out_shape=[jax.ShapeDtypeStruct((rows, n), BF16)] * 2 + [v_shape],
        compiler_params=pltpu.CompilerParams(
            dimension_semantics=("arbitrary",), vmem_limit_bytes=VMEM_LIMIT),
        name="proj_na",
    )(h, gains, w)


NA_SPAN_R = NA_WIN_R + 2
NA_MASKED = 2 * NA_WIN_R - 1


def _na_kernel(q_ref, k_ref, vt_ref, km_ref, vmt_ref, bias_ref, mb_ref, o_ref, s_ref,
               *, rows, rows_per_step):
    step = pl.program_id(1)
    n_pairs = rows_per_step // 2
    n_hp = NA_HEADS // 2
    span = NA_SPAN_R * GRID_W
    lane = lax.broadcasted_iota(jnp.int32, (GRID_W, LANE), 1)
    first = lane < (LANE // 2)
    zeros_m = jnp.zeros((LANE - N_META, 2 * LANE), F32)
    ones_v = jnp.ones((2 * SUBLANE, span + LANE), BF16)

    def geometry(rp):
        ra = step * rows_per_step + 2 * rp
        rs = [jnp.clip(ra + x - NA_WIN_R // 2, 0, rows - NA_WIN_R) for x in range(2)]
        ws = jnp.minimum((rs[0] // 2) * 2, rows - NA_SPAN_R)
        return ra, rs, ws

    n_slots = s_ref.shape[0]

    def scores(rp, hp):
        slot = hp % n_slots
        _, _, ws = geometry(rp)
        cols = slice(hp * LANE, (hp + 1) * LANE)
        parts = []
        for x in range(2):
            q0 = pl.multiple_of((2 * rp + x) * GRID_W, GRID_W)
            qx = q_ref[pl.ds(q0, GRID_W), cols]
            parts += [jnp.where(first, qx, jnp.zeros_like(qx)), jnp.where(first, jnp.zeros_like(qx), qx)]
        qblk = jnp.concatenate(parts, axis=0)
        k0 = pl.multiple_of(ws * GRID_W, 2 * GRID_W)
        s_ref[slot, 0:span, :] = _dot_nt(k_ref[pl.ds(k0, span), cols], qblk)
        s_ref[slot, span:span + N_META, :] = _dot_nt(km_ref[:, cols], qblk)

    for hp in range(NA_LOOKAHEAD):
        scores(0, hp)

    def row_pair(rp, carry):
        ra, rs, ws = geometry(rp)
        idx = []
        for jj in range(NA_SPAN_R):
            kr = ws + jj
            idx.append([jnp.where((kr >= rs[x]) & (kr < rs[x] + NA_WIN_R),
                                  kr - (ra + x) + NA_WIN_R - 1, NA_MASKED) for x in range(2)])
        t0 = ws // 2
        for hp in range(n_hp):
            slot = hp % n_slots
            ahead = hp + NA_LOOKAHEAD
            if ahead < n_hp:
                scores(rp, ahead)
            else:
                scores(jnp.minimum(rp + 1, n_pairs - 1), ahead - n_hp)
            cols = slice(hp * LANE, (hp + 1) * LANE)
            b = jnp.concatenate(
                [jnp.concatenate([bias_ref[hp, idx[jj][0]], bias_ref[hp, idx[jj][1]]], axis=1)
                 for jj in range(NA_SPAN_R)], axis=0)
            s = s_ref[slot, 0:span, :] + b
            mb = mb_ref[hp]
            sm = s_ref[slot, span:span + N_META, :] + jnp.concatenate([mb, mb], axis=1)
            s3 = s.reshape(span // SUBLANE, SUBLANE, 2 * LANE)
            sm3 = sm.reshape(N_META // SUBLANE, SUBLANE, 2 * LANE)
            m = _sublane_bcast_max(jnp.maximum(jnp.max(s3, axis=0), jnp.max(sm3, axis=0)))
            p = jnp.exp2(s3 - m[None]).reshape(span, 2 * LANE).astype(BF16)
            pm3 = jnp.exp2(sm3 - m[None])
            pm = jnp.concatenate([pm3.reshape(N_META, 2 * LANE), zeros_m], axis=0).astype(BF16)
            v_all = jnp.concatenate([vt_ref[t0 + t, cols, :] for t in range(span // LANE)]
                                    + [vmt_ref[cols, :]], axis=1)
            o_t = _dot(jnp.concatenate([v_all, ones_v], axis=0),
                       jnp.concatenate([p, pm], axis=0))
            o_t = o_t[0:LANE] / o_t[LANE:LANE + 1]
            for x in range(2):
                blk = o_t[:, x * LANE:(x + 1) * LANE].T
                q0 = pl.multiple_of((2 * rp + x) * GRID_W, GRID_W)
                o_ref[pl.ds(q0, GRID_W), cols] = jnp.where(
                    first, blk[0:GRID_W], blk[GRID_W:2 * GRID_W]).astype(BF16)
        return carry

    lax.fori_loop(0, n_pairs, row_pair, 0, unroll=4)


def _na_attn(q, k, vt, km, vmt, bias, mb, *, n_seq, seq, meta_base):
    rows = seq // GRID_W
    assert rows >= NA_SPAN_R and rows % 2 == 0
    rps = _row_tile(rows, NA_ROWS_PER_STEP)
    nsteps = rows // rps
    n = NA_HEADS * NA_HEAD_DIM
    span = NA_SPAN_R * GRID_W
    return pl.pallas_call(
        functools.partial(_na_kernel, rows=rows, rows_per_step=rps),
        grid=(n_seq, nsteps),
        in_specs=[
            pl.BlockSpec((rps * GRID_W, n), lambda b, i: (b * nsteps + i, 0)),
            pl.BlockSpec((seq, n), lambda b, i: (b, 0), pipeline_mode=pl.Buffered(1)),
            pl.BlockSpec((seq // LANE, n, LANE), lambda b, i: (b, 0, 0), pipeline_mode=pl.Buffered(1)),
            pl.BlockSpec((None, N_META, n), lambda b, i: (meta_base + b, 0, 0)),
            pl.BlockSpec((None, n, LANE), lambda b, i: (meta_base + b, 0, 0)),
            _const_spec(bias.shape),
            _const_spec(mb.shape),
        ],
        out_specs=pl.BlockSpec((rps * GRID_W, n), lambda b, i: (b * nsteps + i, 0)),
        out_shape=jax.ShapeDtypeStruct((n_seq * seq, n), BF16),
        scratch_shapes=[pltpu.VMEM((NA_SLOTS, span + N_META, 2 * LANE), F32)],
        compiler_params=pltpu.CompilerParams(
            dimension_semantics=("arbitrary", "arbitrary"), vmem_limit_bytes=VMEM_LIMIT),
        name="na_attn",
    )(q, k, vt, km, vmt, bias, mb)


def _na_meta_kernel(q_ref, km_ref, vm_ref, mb_ref, o_ref):
    lane = lax.broadcasted_iota(jnp.int32, (N_META, LANE), 1)
    first = lane < (LANE // 2)
    for j in range(NA_HEADS // 2):
        cols = slice(j * LANE, (j + 1) * LANE)
        qp = q_ref[:, cols]
        km = km_ref[:, cols]
        vm = vm_ref[:, cols]
        outs = []
        for half in range(2):
            h = 2 * j + half
            qh = jnp.where(first if half == 0 else jnp.logical_not(first), qp, jnp.zeros_like(qp))
            sm = _dot_nt(qh, km)
            sm = jnp.where(lane < N_META, sm + mb_ref[h:h + 1, :], NEG_INF)
            m = jnp.max(sm, axis=-1, keepdims=True)
            pm = jnp.exp2(sm - m)
            l = jnp.sum(pm, axis=-1, keepdims=True)
            outs.append(_dot(pm.astype(BF16), vm) / l)
        o_ref[:, cols] = jnp.where(first, outs[0], outs[1]).astype(BF16)


def _na_meta(qm, km, vm, mb):
    n_seq = km.shape[0]
    n = NA_HEADS * NA_HEAD_DIM
    return pl.pallas_call(
        _na_meta_kernel,
        grid=(n_seq,),
        in_specs=[
            pl.BlockSpec((N_META, n), lambda b: (b, 0)),
            pl.BlockSpec((None, LANE, n), lambda b: (b, 0, 0)),
            pl.BlockSpec((None, LANE, n), lambda b: (b, 0, 0)),
            _const_spec(mb.shape),
        ],
        out_specs=pl.BlockSpec((N_META, n), lambda b: (b, 0)),
        out_shape=jax.ShapeDtypeStruct((n_seq * N_META, n), BF16),
        compiler_params=pltpu.CompilerParams(dimension_semantics=("arbitrary",)),
        name="na_meta",
    )(qm, km, vm, mb)


def _take_cols(w, idx):
    idx = np.asarray(idx)
    neg = idx < 0
    same_run = np.where(neg[1:] | neg[:-1], neg[1:] & neg[:-1], np.diff(idx) == 1)
    breaks = np.flatnonzero(~same_run) + 1
    parts = []
    for run in np.split(idx, breaks):
        if run[0] < 0:
            parts.append(jnp.zeros((w.shape[0], len(run)), w.dtype))
        else:
            parts.append(w[:, int(run[0]):int(run[-1]) + 1])
    return jnp.concatenate(parts, axis=1)


def _swap_halves(n):
    half = n // 2
    return np.concatenate([np.arange(half, n), np.arange(0, half)])


def _dense_weights(w_in, q_norm, w_uq, kv_norm, w_ukv, gq_norm, gk_norm, w_out):
    pad = lambda k: -np.ones(k, np.int64)
    o_kr = MLA_Q_LORA + MLA_KV_LORA
    o_gq = o_kr + MLA_ROPE
    o_gk = o_gq + GQA_HEADS * GQA_HEAD_DIM
    o_gv = o_gk + GQA_KV_HEADS * GQA_HEAD_DIM
    axial = np.concatenate([_swap_halves(GQA_HEAD_DIM // 2),
                            GQA_HEAD_DIM // 2 + _swap_halves(GQA_HEAD_DIM // 2)])
    idx = [np.arange(0, o_kr)]
    idx += [pad(MLA_NOPE), o_kr + np.arange(MLA_ROPE), pad(LANE - MLA_NOPE - MLA_ROPE)]
    idx += [pad(MLA_NOPE), o_kr + _swap_halves(MLA_ROPE), pad(LANE - MLA_NOPE - MLA_ROPE)]
    assert 2 * GQA_HEAD_DIM == LANE and GQA_KV_HEADS == 2
    for h in range(GQA_HEADS):
        idx += [o_gq + h * GQA_HEAD_DIM + np.arange(GQA_HEAD_DIM)]
    for h in range(GQA_HEADS):
        idx += [o_gq + h * GQA_HEAD_DIM + axial]
    for h in range(GQA_KV_HEADS):
        idx += [o_gk + h * GQA_HEAD_DIM + np.arange(GQA_HEAD_DIM)]
    for h in range(GQA_KV_HEADS):
        idx += [o_gk + h * GQA_HEAD_DIM + axial]
    idx += [o_gv + np.arange(GQA_KV_HEADS * GQA_HEAD_DIM)]
    idx = np.concatenate(idx)
    assert idx.shape[0] == _C_END
    w_in2 = _take_cols(w_in.astype(BF16), idx)

    hd = MLA_NOPE + MLA_ROPE
    ia, ib = [], []
    for h in range(MLA_HEADS):
        ia += [h * hd + np.arange(hd), pad(LANE - hd)]
        ib += [pad(MLA_NOPE), h * hd + MLA_NOPE + _swap_halves(MLA_ROPE), pad(LANE - hd)]
    w_uq2 = _take_cols(w_uq.astype(BF16), np.concatenate(ia + ib))

    kvd = MLA_NOPE + MLA_V
    ik, iv = [], []
    for h in range(MLA_HEADS):
        ik += [h * kvd + np.arange(MLA_NOPE), pad(LANE - MLA_NOPE)]
        iv += [h * kvd + MLA_NOPE + np.arange(MLA_V)]
    w_ukv2 = _take_cols(w_ukv.astype(BF16), np.concatenate(ik + iv))

    def gain_pair(g):
        ga = jnp.concatenate([g, g])[None, :]
        gb = jnp.tile(g[jnp.asarray(axial)], 2)[None, :]
        return ga, gb

    gq_a, gq_b = gain_pair(gq_norm)
    gk_a, gk_b = gain_pair(gk_norm)

    w_out2 = w_out.astype(BF16)

    return dict(w_in=w_in2, q_norm=q_norm[None, :], w_uq=w_uq2, kv_norm=kv_norm[None, :],
                w_ukv=w_ukv2, gq_a=gq_a, gq_b=gq_b, gk_a=gk_a, gk_b=gk_b,
                rope_expand=_rope_expand_matrix()), w_out2


def _rope_tables(pos, row, row_repeat, col, col_tile):
    half = MLA_ROPE // 2
    inv = 1.0 / (ROPE_THETA ** (jnp.arange(half, dtype=F32) / half))
    n = pos.shape[0]

    def cs(p):
        ang = p.astype(F32)[None, :] * inv[:, None]
        return jnp.cos(ang), jnp.sin(ang)

    by_row = tuple(jnp.repeat(x, row_repeat, axis=1) for x in cs(row))
    by_col = tuple(jnp.tile(x, (1, col_tile)) for x in cs(col))
    vals = jnp.concatenate(cs(pos) + by_row + by_col + (jnp.zeros((LANE - 6 * half, n), F32),), axis=0).T
    hi = vals.astype(BF16)
    rest = vals - hi.astype(F32)
    mid = rest.astype(BF16)
    lo = (rest - mid.astype(F32)).astype(BF16)
    return jnp.stack([hi, mid, lo])


def _rope_expand_matrix():
    half = MLA_ROPE // 2
    c1, s1, cr, sr, cc, sc = range(6)
    tail = LANE - MLA_NOPE - MLA_ROPE
    blank = [(None, 0)]
    layout = (blank * (MLA_NOPE // half) + [(c1, 1), (c1, 1)] + blank * (tail // half)
              + blank * (MLA_NOPE // half) + [(s1, -1), (s1, 1)] + blank * (tail // half)
              + [(cr, 1), (cr, 1), (cc, 1), (cc, 1)] * (LANE // GQA_HEAD_DIM)
              + [(sr, -1), (sr, 1), (sc, -1), (sc, 1)] * (LANE // GQA_HEAD_DIM))
    expand = np.zeros((LANE, len(layout) * half), np.float32)
    for blk, (src, sign) in enumerate(layout):
        if src is not None:
            expand[src * half + np.arange(half), blk * half + np.arange(half)] = sign
    return jnp.asarray(np.tile(expand, (3, 1)), BF16)


def _na_bias_tables(rpb, meta_bias):
    c_idx = np.arange(GRID_W)
    c_start = np.clip(c_idx - NA_WIN_C // 2, 0, GRID_W - NA_WIN_C)
    col_mask = (c_idx[None, :] >= c_start[:, None]) & (c_idx[None, :] < c_start[:, None] + NA_WIN_C)
    col_off = np.clip(c_idx[None, :] - c_idx[:, None] + NA_WIN_C - 1, 0, 2 * NA_WIN_C - 2)
    hp = NA_HEADS // 2
    n_off = 2 * NA_WIN_C - 1
    select = np.zeros((2, n_off, GRID_W, 2, GRID_W), np.float32)
    kc_g, c_g = np.meshgrid(c_idx, c_idx, indexing="ij")
    for half in range(2):
        select[half, col_off[c_g, kc_g], kc_g, half, c_g] = 1.0
    rows = rpb.reshape(hp, 2, NA_MASKED, n_off).transpose(0, 2, 1, 3).reshape(hp * NA_MASKED, 2 * n_off)
    t = jnp.dot(rows, jnp.asarray(select.reshape(2 * n_off, GRID_W * LANE)),
                precision=lax.Precision.HIGHEST) * LOG2E
    t = t.reshape(hp, NA_MASKED, GRID_W, LANE)
    keep = np.tile(col_mask.T, (1, 2))
    t = jnp.where(jnp.asarray(keep)[None, None], t, NEG_INF)
    bias = jnp.concatenate([t, jnp.full_like(t[:, :1], NEG_INF)], axis=1)
    mbl = meta_bias * LOG2E
    mb_t = jnp.repeat(mbl.reshape(hp, 2, N_META).transpose(0, 2, 1), GRID_W, axis=2)
    mb = jnp.pad(mbl, ((0, 0), (0, LANE - N_META)))
    return bias, mb_t, mb


def _pad_meta(x, n_seq):
    c = x.shape[1]
    return jnp.pad(x.reshape(n_seq, N_META, c), ((0, 0), (0, LANE - N_META), (0, 0)))


def kernel(x_prompt, x_sample, meta, norm_gains, ffn1_w_gate, ffn1_w_up, ffn1_w_down, ffn2_w_gate, ffn2_w_up, ffn2_w_down, attn_w_in, mla_q_norm, mla_w_uq, mla_kv_norm, mla_w_ukv, gqa_q_norm, gqa_k_norm, attn_w_out, na_w_qkv, na_rpb, na_meta_bias, na_w_out):
    bp, sp, _ = x_prompt.shape
    bs, ss, _ = x_sample.shape
    n_seq = bp + bs
    depth = norm_gains.shape[0]
    groups = [(bp, sp, 0), (bs, ss, bp)]

    n_meta = n_seq * N_META
    meta_rows = -(-n_meta // LANE) * LANE
    pad_rows = lambda x: jnp.pad(x, ((0, meta_rows - x.shape[0]), (0, 0)))
    h_tok = [x_prompt.reshape(bp * sp, D_MODEL), x_sample.reshape(bs * ss, D_MODEL)]
    h_meta = pad_rows(jnp.tile(meta.astype(F32), (n_seq, 1)))

    smax = max(sp, ss)
    tab_tok = _rope_tables(jnp.arange(smax) + N_META, jnp.arange(smax // GRID_W), GRID_W,
                           jnp.arange(GRID_W), smax // GRID_W)
    tab_meta = _rope_tables(jnp.arange(meta_rows) % N_META, jnp.full((1,), -1), meta_rows,
                            jnp.arange(N_META), meta_rows // N_META)

    w1 = (ffn1_w_gate.astype(BF16), ffn1_w_up.astype(BF16), ffn1_w_down.astype(BF16))
    w2 = (ffn2_w_gate.astype(BF16), ffn2_w_up.astype(BF16), ffn2_w_down.astype(BF16))

    for i in range(depth):
        gains = jnp.pad(norm_gains[i], ((0, 2), (0, 0)))
        j = i // 2
        h_tok = [_ffn1(h, gains, *w1, i) for h in h_tok]
        h_meta = _ffn1(h_meta, gains, *w1, i)
        if i % 2 == 0:
            w, w_out = _dense_weights(attn_w_in[j], mla_q_norm[j], mla_w_uq[j], mla_kv_norm[j],
                                      mla_w_ukv[j], gqa_q_norm[j], gqa_k_norm[j], attn_w_out[j])
            qkv_tok = [_proj_dense(h, gains, w, tab_tok, s) for h, (_, s, _) in zip(h_tok, groups)]
            qmt, km, vmt = _proj_dense(h_meta, gains, w, tab_meta, 0)
            kmp = _pad_meta(km[:n_meta], n_seq)
            vmt = vmt.transpose(1, 0, 2).reshape(V_ROWS, meta_rows)
            vmtp = vmt[:, :n_meta].reshape(V_ROWS, n_seq, N_META).transpose(1, 0, 2)
            vmtp = jnp.pad(vmtp, ((0, 0), (0, 0), (0, LANE - N_META)))
            qmt = qmt.transpose(1, 0, 2).reshape(HEAD_SLOTS * LANE, meta_rows)
            qmtp = qmt[:, :n_meta].reshape(HEAD_SLOTS * LANE, n_seq, N_META).transpose(1, 0, 2)
            qmtp = jnp.pad(qmtp, ((0, 0), (0, 0), (0, LANE - N_META)))
            o_tok, o_meta = [], []
            for (qt, k, vt), (nb, s, b0) in zip(qkv_tok, groups):
                o_tok.append(_dense_attn(qt, k, vt, kmp, vmtp, n_seq=nb, seq=s, nq=qt.shape[0] // nb,
                                         q_base=0, meta_base=b0))
                om = _dense_attn(qmtp, k, vt, kmp, vmtp, n_seq=nb, seq=s, nq=1, q_base=b0, meta_base=b0)
                o_meta.append(om.reshape(nb, LANE, D_MODEL)[:, :N_META].reshape(nb * N_META, D_MODEL))
            o_meta = pad_rows(jnp.concatenate(o_meta, axis=0))
        else:
            w_qkv = na_w_qkv[j].astype(BF16)
            w_out = na_w_out[j].astype(BF16)
            bias, mb_t, mb = _na_bias_tables(na_rpb[j], na_meta_bias[j])
            qkv_tok = [_proj_na(h, gains, w_qkv, True) for h in h_tok]
            qm, km, vm = _proj_na(h_meta, gains, w_qkv, False)
            kmp, vmp = _pad_meta(km[:n_meta], n_seq), _pad_meta(vm[:n_meta], n_seq)
            km16 = km[:n_meta].reshape(n_seq, N_META, NA_HEADS * NA_HEAD_DIM)
            vmtp = vmp.transpose(0, 2, 1)
            o_tok = [_na_attn(q, k, vt, km16, vmtp, bias, mb_t, n_seq=nb, seq=s, meta_base=b0)
                     for (q, k, vt), (nb, s, b0) in zip(qkv_tok, groups)]
            o_meta = pad_rows(_na_meta(qm[:n_meta], kmp, vmp, mb))
        h_tok = [_mix_ffn2(h, o, w_out, gains, *w2, i) for h, o in zip(h_tok, o_tok)]
        h_meta = _mix_ffn2(h_meta, o_meta, w_out, gains, *w2, i)

    return (h_tok[0].reshape(bp, sp, D_MODEL), h_tok[1].reshape(bs, ss, D_MODEL))
```
